```python
import math
import jax, jax.numpy as jnp
from jax import lax
import numpy as np

D_MODEL = 1024
BATCH = 8
SEQ = 4096
DEPTH = 1

N_HEADS_DIFF = 8
HEAD_DIM_DIFF = 64
N_HEADS_DSA = 8
HEAD_DIM_DSA = 128
KV_LATENT = 256
N_HEADS_IDX = 16
HEAD_DIM_IDX = 64
TOPK_MAX = 256
N_BUCKETS = 32
MAX_DISTANCE = 128
N_EXPERTS = 32
TOP_K_EXPERTS = 4
D_EXPERT = 1024
SWIGLU_LIMIT = 7.0
SWIGLU_ALPHA = 1.702
MOE_BLOCK = 128
Q_BLOCK = 128
EPS = 1e-6

W_DIFF_Q = N_HEADS_DIFF * 2 * HEAD_DIM_DIFF
W_DIFF_K = N_HEADS_DIFF * 2 * HEAD_DIM_DIFF
W_DIFF_V = N_HEADS_DIFF * 2 * HEAD_DIM_DIFF
W_DSA_Q = N_HEADS_DSA * HEAD_DIM_DSA
W_IDX_Q = N_HEADS_IDX * HEAD_DIM_IDX
W_DIFF_OUT = N_HEADS_DIFF * 2 * HEAD_DIM_DIFF
W_DSA_OUT = N_HEADS_DSA * HEAD_DIM_DSA
IN_SPLIT_SIZES = (W_DIFF_Q, W_DIFF_K, W_DIFF_V, W_DSA_Q, KV_LATENT, W_IDX_Q, HEAD_DIM_IDX, N_HEADS_IDX, D_MODEL, D_MODEL)
IN_WIDTH = 7 * 1024 + KV_LATENT + HEAD_DIM_IDX + N_HEADS_IDX

kernel_name = "hybrid_diffattn_dsa_gated_moe"


def rmsnorm(x, g):
    xf = x.astype(jnp.float32)
    y = xf * lax.rsqrt(jnp.mean(xf * xf, axis=-1, keepdims=True) + EPS)
    return (y * g.astype(jnp.float32)).astype(x.dtype)


def t5_bucket(dist):
    n = jnp.maximum(dist, 0)
    max_exact = N_BUCKETS // 2
    nf = jnp.maximum(n, 1).astype(jnp.float32)
    large = max_exact + (jnp.log(nf / max_exact) / math.log(MAX_DISTANCE / max_exact)
                         * (N_BUCKETS - max_exact)).astype(jnp.int32)
    large = jnp.minimum(large, N_BUCKETS - 1)
    return jnp.where(n < max_exact, n, large)


def split_cols(p):
    outs, off = [], 0
    for sz in IN_SPLIT_SIZES:
        outs.append(p[..., off:off + sz])
        off += sz
    return outs


def diff_attention(q, k, v, bias_tab, lam, lam_init, subln_g):
    B, S = q.shape[0], q.shape[1]
    H, d = N_HEADS_DIFF, HEAD_DIM_DIFF
    nblk = S // Q_BLOCK
    q = q * (d ** -0.5)
    qb = q.reshape(B, nblk, Q_BLOCK, H, 2, d).swapaxes(0, 1)
    kpos = jnp.arange(S, dtype=jnp.int32)

    def one_block(args):
        qi, i = args
        qpos = i * Q_BLOCK + jnp.arange(Q_BLOCK, dtype=jnp.int32)
        dist = qpos[:, None] - kpos[None, :]
        bias = bias_tab[t5_bucket(dist)].astype(jnp.float32).transpose(2, 0, 1)
        s = jnp.einsum('bqhcd,bkhcd->bhcqk', qi, k).astype(jnp.float32) + bias[None, :, None]
        s = jnp.where((dist >= 0)[None, None, None], s, -jnp.inf)
        p = jax.nn.softmax(s, axis=-1)
        a = p[:, :, 0] - lam * p[:, :, 1]
        return jnp.einsum('bhqk,bkhe->bqhe', a.astype(v.dtype), v)

    o = lax.map(one_block, (qb, jnp.arange(nblk, dtype=jnp.int32)))
    o = o.swapaxes(0, 1).reshape(B, S, H, 2 * d)
    o = rmsnorm(o, subln_g) * (1.0 - lam_init)
    return o.reshape(B, S, H * 2 * d)


def dsa_attention(q, c, q_idx, k_idx, w_idx, bias_tab, w_uk, w_uv):
    B, S = q.shape[0], q.shape[1]
    topk = min(TOPK_MAX, S // 4)
    nblk = S // Q_BLOCK
    blk = lambda a: a.reshape((B, nblk, Q_BLOCK) + a.shape[2:]).swapaxes(0, 1)
    w_idx = w_idx * (N_HEADS_IDX ** -0.5) * (HEAD_DIM_IDX ** -0.5)
    kpos = jnp.arange(S, dtype=jnp.int32)
    scale = HEAD_DIM_DSA ** -0.5

    def one_block(args):
        qi, qii, wi, i = args
        qpos = i * Q_BLOCK + jnp.arange(Q_BLOCK, dtype=jnp.int32)
        causal = kpos[None, :] <= qpos[:, None]
        dots = jnp.einsum('bqhd,bsd->bqhs', qii, k_idx)
        score = jnp.einsum('bqhs,bqh->bqs', jax.nn.relu(dots), wi).astype(jnp.float32)
        score = jnp.where(causal[None], score, -jnp.inf)
        _, idx = lax.top_k(score, topk)
        valid = idx <= qpos[None, :, None]
        sel = jax.vmap(lambda cb, ib: cb[ib])(c, idx)
        bias = bias_tab[t5_bucket(qpos[None, :, None] - idx)].astype(jnp.float32)
        ql = jnp.einsum('bqhd,hcd->bqhc', qi, w_uk)
        s = jnp.einsum('bqhc,bqkc->bqhk', ql, sel).astype(jnp.float32) * scale + bias.transpose(0, 1, 3, 2)
        s = jnp.where(valid[:, :, None, :], s, -jnp.inf)
        p = jax.nn.softmax(s, axis=-1).astype(c.dtype)
        ol = jnp.einsum('bqhk,bqkc->bqhc', p, sel)
        return jnp.einsum('bqhc,hcd->bqhd', ol, w_uv)

    o = lax.map(one_block, (blk(q), blk(q_idx), blk(w_idx), jnp.arange(nblk, dtype=jnp.int32)))
    return o.swapaxes(0, 1).reshape(B, S, N_HEADS_DSA * HEAD_DIM_DSA)


def moe(h, w_router, b_router, w_gu, b_gu, w_down, b_down):
    B, S, D = h.shape
    N = B * S
    E, K = N_EXPERTS, TOP_K_EXPERTS
    xf = h.reshape(N, D)
    logits = (xf @ w_router + b_router).astype(jnp.float32)
    top_vals, top_idx = lax.top_k(logits, K)
    gates = jax.nn.softmax(top_vals, axis=-1)
    A = N * K
    a_exp = top_idx.reshape(A).astype(jnp.int32)
    a_tok = jnp.repeat(jnp.arange(N, dtype=jnp.int32), K)
    a_w = gates.reshape(A)
    order = jnp.argsort(a_exp)
    s_exp, s_tok, s_w = a_exp[order], a_tok[order], a_w[order]
    counts = jnp.bincount(a_exp, length=E).astype(jnp.int32)
    padded = (counts + MOE_BLOCK - 1) // MOE_BLOCK * MOE_BLOCK
    starts = jnp.cumsum(counts) - counts
    pends = jnp.cumsum(padded)
    pstarts = pends - padded
    dest = pstarts[s_exp] + jnp.arange(A, dtype=jnp.int32) - starts[s_exp]
    nblk = -(-(A + E * (MOE_BLOCK - 1)) // MOE_BLOCK)
    P = nblk * MOE_BLOCK
    row_tok = jnp.full((P,), N, jnp.int32).at[dest].set(s_tok)
    row_w = jnp.zeros((P,), jnp.float32).at[dest].set(s_w)
    blk_exp = jnp.minimum(jnp.searchsorted(pends, jnp.arange(nblk, dtype=jnp.int32) * MOE_BLOCK, side='right'), E - 1)
    x_pad = jnp.concatenate([xf, jnp.zeros((1, D), xf.dtype)], axis=0)
    xb = x_pad[row_tok].reshape(nblk, MOE_BLOCK, D)

    def expert_block(args):
        xi, e = args
        gu = xi @ w_gu[e] + b_gu[e]
        gate = jnp.minimum(gu[:, 0::2], SWIGLU_LIMIT)
        up = jnp.clip(gu[:, 1::2], -SWIGLU_LIMIT, SWIGLU_LIMIT)
        glu = gate * jax.nn.sigmoid(gate * SWIGLU_ALPHA)
        return ((up + 1.0) * glu) @ w_down[e] + b_down[e]

    yb = lax.map(expert_block, (xb, blk_exp)).reshape(P, D)
    out = jnp.zeros((N + 1, D), yb.dtype).at[row_tok].add(yb * row_w[:, None].astype(yb.dtype))
    return out[:N].reshape(B, S, D).astype(h.dtype)


def setup_inputs(seed: int = 0) -> dict:
    key = jax.random.key(seed)
    ks = jax.random.split(key, 24)
    nrm = lambda k, shape, s: jax.random.normal(k, shape, jnp.float32) * s
    gain = lambda k, shape: 1.0 + 0.02 * jax.random.normal(k, shape, jnp.float32)
    Lz = DEPTH
    return {
        "x": nrm(ks[0], (BATCH, SEQ, D_MODEL), 1.0),
        "norm_attn_g": gain(ks[1], (Lz, D_MODEL)),
        "w_in": nrm(ks[2], (Lz, D_MODEL, IN_WIDTH), D_MODEL ** -0.5),
        "rel_bias": nrm(ks[3], (N_BUCKETS, N_HEADS_DIFF + N_HEADS_DSA), 0.5),
        "lam_q1": nrm(ks[4], (Lz, HEAD_DIM_DIFF), 0.1),
        "lam_k1": nrm(ks[5], (Lz, HEAD_DIM_DIFF), 0.1),
        "lam_q2": nrm(ks[6], (Lz, HEAD_DIM_DIFF), 0.1),
        "lam_k2": nrm(ks[7], (Lz, HEAD_DIM_DIFF), 0.1),
        "diff_subln_g": gain(ks[8], (Lz, 2 * HEAD_DIM_DIFF)),
        "kv_norm_g": gain(ks[9], (Lz, KV_LATENT)),
        "w_uk": nrm(ks[10], (Lz, N_HEADS_DSA, KV_LATENT, HEAD_DIM_DSA), KV_LATENT ** -0.5),
        "w_uv": nrm(ks[11], (Lz, N_HEADS_DSA, KV_LATENT, HEAD_DIM_DSA), KV_LATENT ** -0.5),
        "w_branch_diff": nrm(ks[12], (Lz, W_DIFF_OUT, D_MODEL), W_DIFF_OUT ** -0.5),
        "w_branch_dsa": nrm(ks[13], (Lz, W_DSA_OUT, D_MODEL), W_DSA_OUT ** -0.5),
        "w_out": nrm(ks[14], (Lz, D_MODEL, D_MODEL), D_MODEL ** -0.5),
        "norm_ffn_g": gain(ks[15], (Lz, D_MODEL)),
        "w_router": nrm(ks[16], (Lz, D_MODEL, N_EXPERTS), D_MODEL ** -0.5),
        "b_router": nrm(ks[17], (Lz, N_EXPERTS), 0.01),
        "w_gate_up": nrm(ks[18], (Lz, N_EXPERTS, D_MODEL, 2 * D_EXPERT), D_MODEL ** -0.5),
        "b_gate_up": nrm(ks[19], (Lz, N_EXPERTS, 2 * D_EXPERT), 0.01),
        "w_down": nrm(ks[20], (Lz, N_EXPERTS, D_EXPERT, D_MODEL), D_EXPERT ** -0.5),
        "b_down": nrm(ks[21], (Lz, N_EXPERTS, D_MODEL), 0.01),
        "norm_final_g": gain(ks[22], (D_MODEL,)),
    }


def reference(x, norm_attn_g, w_in, rel_bias, lam_q1, lam_k1, lam_q2, lam_k2, diff_subln_g,
              kv_norm_g, w_uk, w_uv, w_branch_diff, w_branch_dsa, w_out, norm_ffn_g,
              w_router, b_router, w_gate_up, b_gate_up, w_down, b_down, norm_final_g):
    B, S, D = x.shape
    h = x
    bias_diff = rel_bias[:, :N_HEADS_DIFF]
    bias_dsa = rel_bias[:, N_HEADS_DIFF:]
    for l in range(DEPTH):
        xn = rmsnorm(h, norm_attn_g[l])
        proj = xn @ w_in[l]
        dq, dk, dv, sq, ckv, iq, ik, iw, ga, gb = split_cols(proj)
        lam_init = 0.8 - 0.6 * math.exp(-0.3 * l)
        lam = (jnp.exp(jnp.sum(lam_q1[l].astype(jnp.float32) * lam_k1[l].astype(jnp.float32)))
               - jnp.exp(jnp.sum(lam_q2[l].astype(jnp.float32) * lam_k2[l].astype(jnp.float32)))
               + lam_init)
        y_diff = diff_attention(
            dq.reshape(B, S, N_HEADS_DIFF, 2, HEAD_DIM_DIFF),
            dk.reshape(B, S, N_HEADS_DIFF, 2, HEAD_DIM_DIFF),
            dv.reshape(B, S, N_HEADS_DIFF, 2 * HEAD_DIM_DIFF),
            bias_diff, lam, lam_init, diff_subln_g[l])
        y_dsa = dsa_attention(
            sq.reshape(B, S, N_HEADS_DSA, HEAD_DIM_DSA),
            rmsnorm(ckv, kv_norm_g[l]),
            iq.reshape(B, S, N_HEADS_IDX, HEAD_DIM_IDX),
            ik, iw, bias_dsa, w_uk[l], w_uv[l])
        merged = (jax.nn.sigmoid(ga) * (y_diff @ w_branch_diff[l])
                  + jax.nn.sigmoid(gb) * (y_dsa @ w_branch_dsa[l]))
        h = h + merged @ w_out[l]
        hn = rmsnorm(h, norm_ffn_g[l])
        h = h + moe(hn, w_router[l], b_router[l], w_gate_up[l], b_gate_up[l], w_down[l], b_down[l])
    return rmsnorm(h, norm_final_g)
```

```python
import functools
import math

import jax
import jax.numpy as jnp
from jax import lax
from jax.experimental import pallas as pl
from jax.experimental.pallas import tpu as pltpu

F32 = jnp.float32
BF16 = jnp.bfloat16
I32 = jnp.int32

D_MODEL = 1024
N_HEADS_DIFF = 8
HEAD_DIM_DIFF = 64
N_HEADS_DSA = 8
HEAD_DIM_DSA = 128
KV_LATENT = 256
N_HEADS_IDX = 16
HEAD_DIM_IDX = 64
TOPK_MAX = 256
N_BUCKETS = 32
MAX_DISTANCE = 128
N_EXPERTS = 32
TOP_K_EXPERTS = 4
D_EXPERT = 1024
SWIGLU_LIMIT = 7.0
SWIGLU_ALPHA = 1.702
EPS = 1e-6

LANES = 128
ATT_BLOCK = 256
MOE_ROWS = 512
PROJ_WIDTH = 7680
VMEM_LIMIT = 48 * 1024 * 1024

COL_DQ, COL_DK, COL_DV, COL_SQ, COL_IQ, COL_GA, COL_GB = (i * 1024 for i in range(7))
COL_CKV = 7168
COL_IK = 7424
COL_IW = 7552

MASK_VALUE = -1e30
M_INIT = -1e29
INT_MIN = -2 ** 31


def _cparams(sem):
    return pltpu.CompilerParams(dimension_semantics=sem, vmem_limit_bytes=VMEM_LIMIT)


def _inproj_kernel(x_ref, g_ref, w_ref, o_ref, xn_ref):
    @pl.when(pl.program_id(1) == 0)
    def _():
        x = x_ref[...]
        ms = jnp.mean(x * x, axis=-1, keepdims=True)
        xn_ref[...] = (x * lax.rsqrt(ms + EPS) * g_ref[...]).astype(BF16)

    o_ref[...] = jnp.dot(xn_ref[...], w_ref[...], preferred_element_type=F32).astype(o_ref.dtype)


def _inproj(x2, g, w, tm, tn):
    n, d = x2.shape
    width = w.shape[1]
    return pl.pallas_call(
        _inproj_kernel,
        out_shape=jax.ShapeDtypeStruct((n, width), BF16),
        grid=(n // tm, width // tn),
        in_specs=[pl.BlockSpec((tm, d), lambda i, j: (i, 0)),
                  pl.BlockSpec((1, d), lambda i, j: (0, 0)),
                  pl.BlockSpec((d, tn), lambda i, j: (0, j))],
        out_specs=pl.BlockSpec((tm, tn), lambda i, j: (i, j)),
        scratch_shapes=[pltpu.VMEM((tm, d), BF16)],
        compiler_params=_cparams(("parallel", "arbitrary")),
        name="inproj",
    )(x2, g, w)


def _softmax_update(s, v, m_ref, l_ref, acc_ref):
    m_prev = m_ref[...]
    m_new = jnp.maximum(m_prev, jnp.max(s, axis=-1, keepdims=True))
    alpha = jnp.exp(m_prev - m_new)
    p = jnp.exp(s - m_new)
    l_ref[...] = alpha * l_ref[...] + jnp.sum(p, axis=-1, keepdims=True)
    acc_ref[...] = alpha * acc_ref[...] + jnp.dot(p.astype(BF16), v, preferred_element_type=F32)
    m_ref[...] = m_new


def _dot_nt(a, b):
    return lax.dot_general(a, b, (((1,), (1,)), ((), ())), preferred_element_type=F32)


def _diff_kernel(lam_ref, q_ref, k_ref, v_ref, bias_ref, g_ref, o_ref, q2_ref, m_ref, l_ref, acc_ref,
                 *, out_scale):
    t = q_ref.shape[0]
    qi = pl.program_id(2)

    q = q_ref[...]
    lane = lax.broadcasted_iota(I32, q.shape, 1)
    zero = jnp.zeros_like(q)
    q2_ref[0:t, :] = jnp.where(lane < HEAD_DIM_DIFF, q, zero)
    q2_ref[t:2 * t, :] = jnp.where(lane >= HEAD_DIM_DIFF, q, zero)
    m_ref[...] = jnp.full(m_ref.shape, M_INIT, F32)
    l_ref[...] = jnp.zeros(l_ref.shape, F32)
    acc_ref[...] = jnp.zeros(acc_ref.shape, F32)

    def chunk(kc, bias):
        off = pl.multiple_of(kc * t, t)
        s = _dot_nt(q2_ref[...], k_ref[pl.ds(off, t), :])
        if bias is not None:
            s = (s.reshape(2, t, t) + bias[None]).reshape(2 * t, t)
        _softmax_update(s, v_ref[pl.ds(off, t), :], m_ref, l_ref, acc_ref)

    def far_body(kc, carry):
        chunk(kc, None)
        return carry

    lax.fori_loop(0, jnp.maximum(qi - 1, 0), far_body, 0)

    @pl.when(qi >= 1)
    def _():
        chunk(qi - 1, bias_ref[1, 0])

    chunk(qi, bias_ref[0, 0])

    o = acc_ref[...] / l_ref[...]
    o = o[0:t, :] - lam_ref[0, 0] * o[t:2 * t, :]
    ms = jnp.mean(o * o, axis=-1, keepdims=True)
    o_ref[...] = (o * lax.rsqrt(ms + EPS) * g_ref[...] * out_scale).astype(o_ref.dtype)


def _diff_attention(proj, lam, bias_tiles, subln_g, batch, seq, out_scale):
    t = ATT_BLOCK
    nq = seq // t
    h = N_HEADS_DIFF
    e = 2 * HEAD_DIM_DIFF
    kernel = functools.partial(_diff_kernel, out_scale=out_scale)
    return pl.pallas_call(
        kernel,
        out_shape=jax.ShapeDtypeStruct((batch * seq, h * e), BF16),
        grid=(batch, h, nq),
        in_specs=[pl.BlockSpec(memory_space=pltpu.SMEM),
                  pl.BlockSpec((t, e), lambda b, hh, qi: (b * nq + qi, COL_DQ // e + hh)),
                  pl.BlockSpec((seq, e), lambda b, hh, qi: (b, COL_DK // e + hh)),
                  pl.BlockSpec((seq, e), lambda b, hh, qi: (b, COL_DV // e + hh)),
                  pl.BlockSpec((2, 1, t, t), lambda b, hh, qi: (0, hh, 0, 0)),
                  pl.BlockSpec((1, e), lambda b, hh, qi: (0, 0))],
        out_specs=pl.BlockSpec((t, e), lambda b, hh, qi: (b * nq + qi, hh)),
        scratch_shapes=[pltpu.VMEM((2 * t, e), BF16),
                        pltpu.VMEM((2 * t, 1), F32),
                        pltpu.VMEM((2 * t, 1), F32),
                        pltpu.VMEM((2 * t, e), F32)],
        compiler_params=_cparams(("parallel", "parallel", "arbitrary")),
        name="diff_attn",
    )(lam, proj, proj, proj, bias_tiles, subln_g)


def _sortable_key(x):
    bits = lax.bitcast_convert_type(x, I32)
    return bits ^ ((bits >> 31) & jnp.int32(0x7FFFFFFF))


def _dsa_kernel(iq_ref, sq_ref, iw_ref, ik_ref, ckv_ref, kvg_ref, wuk_ref, wuv_ref, bias_ref, o_ref,
                c_ref, key_ref, am_ref, qi_ref, wb_ref, ql_ref, m_ref, l_ref, acc_ref, *, topk, scale):
    t = iq_ref.shape[0]
    qi = pl.program_id(1)
    n_chunks = qi + 1
    hi, hb = N_HEADS_IDX, N_HEADS_DSA

    @pl.when(qi == 0)
    def _():
        ckv = ckv_ref[...].astype(F32)
        ms = jnp.mean(ckv * ckv, axis=-1, keepdims=True)
        c_ref[...] = (ckv * lax.rsqrt(ms + EPS) * kvg_ref[...]).astype(BF16)

    lane = lax.broadcasted_iota(I32, (t, LANES), 1)
    for h in range(hi):
        blk = iq_ref[:, (h // 2) * LANES:(h // 2 + 1) * LANES]
        keep = (lane < HEAD_DIM_IDX) if h % 2 == 0 else (lane >= HEAD_DIM_IDX)
        qi_ref[h * t:(h + 1) * t, :] = jnp.where(keep, blk, jnp.zeros_like(blk))
        wb_ref[h] = jnp.broadcast_to(iw_ref[:, h:h + 1].astype(F32), (t, LANES))

    def score_chunk(kc, diag):
        off = pl.multiple_of(kc * t, t)
        d = _dot_nt(qi_ref[...], ik_ref[pl.ds(off, t), :]).reshape(hi, t, t)
        sc = jnp.zeros((t, t), F32)
        for h in range(hi):
            w = wb_ref[h]
            w = jnp.concatenate([w] * (t // LANES), axis=1)
            sc = sc + jnp.maximum(d[h], 0.0) * w
        key = _sortable_key(sc + 0.0)
        if diag:
            row = lax.broadcasted_iota(I32, (t, t), 0)
            col = lax.broadcasted_iota(I32, (t, t), 1)
            key = jnp.where(col <= row, key, jnp.int32(INT_MIN))
        key_ref[:, pl.ds(off, t)] = key

    def score_body(kc, carry):
        score_chunk(kc, False)
        return carry

    lax.fori_loop(0, qi, score_body, 0)
    score_chunk(qi, True)

    def count_ge(cand_b):
        def body(kc, cnt):
            off = pl.multiple_of(kc * t, t)
            k = key_ref[:, pl.ds(off, t)]
            for j in range(t // LANES):
                cnt = cnt + jnp.where(k[:, j * LANES:(j + 1) * LANES] >= cand_b, 1, 0)
            return cnt
        cnt = lax.fori_loop(0, n_chunks, body, jnp.zeros((t, LANES), I32))
        return jnp.sum(cnt, axis=1, keepdims=True)

    def bit_body(i, cur):
        bit = lax.shift_left(jnp.int32(1), 31 - i)
        cand = cur | bit
        cnt = count_ge(jnp.broadcast_to(cand ^ jnp.int32(INT_MIN), (t, LANES)))
        return jnp.where(cnt >= topk, cand, cur)

    cur = lax.fori_loop(0, 32, bit_body, jnp.zeros((t, 1), I32))
    thr = jnp.maximum(cur ^ jnp.int32(INT_MIN), jnp.int32(INT_MIN + 1))
    thr_b = jnp.broadcast_to(thr, (t, LANES))

    def mask_body(kc, carry):
        off = pl.multiple_of(kc * t, t)
        k = key_ref[:, pl.ds(off, t)]
        thr_w = jnp.concatenate([thr_b] * (t // LANES), axis=1)
        am_ref[:, pl.ds(off, t)] = jnp.where(k >= thr_w, 0.0, MASK_VALUE).astype(F32)
        return carry

    lax.fori_loop(0, n_chunks, mask_body, 0)

    for h in range(hb):
        qh = sq_ref[:, h * HEAD_DIM_DSA:(h + 1) * HEAD_DIM_DSA]
        ql = jnp.dot(qh, wuk_ref[h], preferred_element_type=F32) * scale
        ql_ref[h * t:(h + 1) * t, :] = ql.astype(BF16)
    m_ref[...] = jnp.full(m_ref.shape, M_INIT, F32)
    l_ref[...] = jnp.zeros(l_ref.shape, F32)
    acc_ref[...] = jnp.zeros(acc_ref.shape, F32)

    def att_chunk(kc, bias):
        off = pl.multiple_of(kc * t, t)
        c = c_ref[pl.ds(off, t), :]
        s = _dot_nt(ql_ref[...], c).reshape(hb, t, t) + am_ref[:, pl.ds(off, t)][None]
        if bias is not None:
            s = s + bias
        _softmax_update(s.reshape(hb * t, t), c, m_ref, l_ref, acc_ref)

    def att_body(kc, carry):
        att_chunk(kc, None)
        return carry

    lax.fori_loop(0, jnp.maximum(qi - 1, 0), att_body, 0)

    @pl.when(qi >= 1)
    def _():
        att_chunk(qi - 1, bias_ref[1])

    att_chunk(qi, bias_ref[0])

    ol = (acc_ref[...] / l_ref[...]).astype(BF16)
    for h in range(hb):
        o = jnp.dot(ol[h * t:(h + 1) * t, :], wuv_ref[h], preferred_element_type=F32)
        o_ref[:, h * HEAD_DIM_DSA:(h + 1) * HEAD_DIM_DSA] = o.astype(o_ref.dtype)


def _dsa_attention(proj, kv_g, w_ukt, w_uv, bias_tiles, batch, seq, topk):
    t = ATT_BLOCK
    nq = seq // t
    hb, hi = N_HEADS_DSA, N_HEADS_IDX
    width = hb * HEAD_DIM_DSA
    kernel = functools.partial(_dsa_kernel, topk=topk, scale=HEAD_DIM_DSA ** -0.5)
    return pl.pallas_call(
        kernel,
        out_shape=jax.ShapeDtypeStruct((batch * seq, width), BF16),
        grid=(batch, nq),
        in_specs=[pl.BlockSpec((t, 1024), lambda b, qi: (b * nq + qi, COL_IQ // 1024)),
                  pl.BlockSpec((t, 1024), lambda b, qi: (b * nq + qi, COL_SQ // 1024)),
                  pl.BlockSpec((t, LANES), lambda b, qi: (b * nq + qi, COL_IW // LANES)),
                  pl.BlockSpec((seq, LANES), lambda b, qi: (b, COL_IK // LANES)),
                  pl.BlockSpec((seq, KV_LATENT), lambda b, qi: (b, COL_CKV // KV_LATENT)),
                  pl.BlockSpec((1, KV_LATENT), lambda b, qi: (0, 0)),
                  pl.BlockSpec((hb, HEAD_DIM_DSA, KV_LATENT), lambda b, qi: (0, 0, 0)),
                  pl.BlockSpec((hb, KV_LATENT, HEAD_DIM_DSA), lambda b, qi: (0, 0, 0)),
                  pl.BlockSpec((2, hb, t, t), lambda b, qi: (0, 0, 0, 0))],
        out_specs=pl.BlockSpec((t, width), lambda b, qi: (b * nq + qi, 0)),
        scratch_shapes=[pltpu.VMEM((seq, KV_LATENT), BF16),
                        pltpu.VMEM((t, seq), I32),
                        pltpu.VMEM((t, seq), F32),
                        pltpu.VMEM((hi * t, LANES), BF16),
                        pltpu.VMEM((hi, t, LANES), F32),
                        pltpu.VMEM((hb * t, KV_LATENT), BF16),
                        pltpu.VMEM((hb * t, 1), F32),
                        pltpu.VMEM((hb * t, 1), F32),
                        pltpu.VMEM((hb * t, KV_LATENT), F32)],
        compiler_params=_cparams(("parallel", "arbitrary")),
        name="dsa_attn",
    )(proj, proj, proj, proj, proj, kv_g, w_ukt, w_uv, bias_tiles)


def _merge_kernel(x_ref, yd_ref, ys_ref, ga_ref, gb_ref, wd_ref, ws_ref, wo_ref, g_ref, wr_ref, br_ref,
                  h_ref, hn_ref, route_ref):
    bd = jnp.dot(yd_ref[...], wd_ref[...], preferred_element_type=F32)
    bs = jnp.dot(ys_ref[...], ws_ref[...], preferred_element_type=F32)
    merged = (jax.nn.sigmoid(ga_ref[...].astype(F32)) * bd + jax.nn.sigmoid(gb_ref[...].astype(F32)) * bs)
    h = x_ref[...] + jnp.dot(merged.astype(BF16), wo_ref[...], preferred_element_type=F32)
    h_ref[...] = h
    ms = jnp.mean(h * h, axis=-1, keepdims=True)
    hn = h * lax.rsqrt(ms + EPS) * g_ref[...]
    hn_ref[...] = hn.astype(hn_ref.dtype)

    logits = jnp.dot(hn, wr_ref[...], preferred_element_type=F32, precision=lax.Precision.HIGHEST)
    logits = logits + br_ref[...]
    lane = lax.broadcasted_iota(I32, logits.shape, 1)
    vals, ids = [], []
    for _ in range(TOP_K_EXPERTS):
        mx = jnp.max(logits, axis=-1, keepdims=True)
        ix = jnp.min(jnp.where(logits == mx, lane, LANES), axis=-1, keepdims=True)
        vals.append(mx)
        ids.append(ix)
        logits = jnp.where(lane == ix, -jnp.inf, logits)
    es = [jnp.exp(v - vals[0]) for v in vals]
    inv = 1.0 / (es[0] + es[1] + es[2] + es[3])
    route = jnp.zeros(logits.shape, F32)
    for k in range(TOP_K_EXPERTS):
        route = jnp.where(lane == k, es[k] * inv, route)
        route = jnp.where(lane == TOP_K_EXPERTS + k, ids[k].astype(F32), route)
    route_ref[...] = route


def _merge(x2, y_diff, y_dsa, proj, w_bd, w_bs, w_out, g_ffn, w_router, b_router, tm):
    n, d = x2.shape
    row = lambda i: (i, 0)
    const = lambda i: (0, 0)
    return pl.pallas_call(
        _merge_kernel,
        out_shape=(jax.ShapeDtypeStruct((n, d), F32),
                   jax.ShapeDtypeStruct((n, d), BF16),
                   jax.ShapeDtypeStruct((n, LANES), F32)),
        grid=(n // tm,),
        in_specs=[pl.BlockSpec((tm, d), row),
                  pl.BlockSpec((tm, d), row),
                  pl.BlockSpec((tm, d), row),
                  pl.BlockSpec((tm, d), lambda i: (i, COL_GA // 1024)),
                  pl.BlockSpec((tm, d), lambda i: (i, COL_GB // 1024)),
                  pl.BlockSpec((d, d), const),
                  pl.BlockSpec((d, d), const),
                  pl.BlockSpec((d, d), const),
                  pl.BlockSpec((1, d), const),
                  pl.BlockSpec((d, LANES), const),
                  pl.BlockSpec((1, LANES), const)],
        out_specs=(pl.BlockSpec((tm, d), row),
                   pl.BlockSpec((tm, d), row),
                   pl.BlockSpec((tm, LANES), row)),
        compiler_params=_cparams(("parallel",)),
        name="merge_router",
    )(x2, y_diff, y_dsa, proj, proj, w_bd, w_bs, w_out, g_ffn, w_router, b_router)


def _ffn_kernel(be_ref, nu_ref, x_ref, wg_ref, wu_ref, wd_ref, bg_ref, bu_ref, bd_ref, o_ref):
    @pl.when(pl.program_id(0) < nu_ref[0])
    def _():
        x = x_ref[...]
        g = jnp.dot(x, wg_ref[0], preferred_element_type=F32) + bg_ref[0]
        u = jnp.dot(x, wu_ref[0], preferred_element_type=F32) + bu_ref[0]
        gate = jnp.minimum(g, SWIGLU_LIMIT)
        up = jnp.clip(u, -SWIGLU_LIMIT, SWIGLU_LIMIT)
        glu = gate * jax.nn.sigmoid(gate * SWIGLU_ALPHA)
        a = ((up + 1.0) * glu).astype(BF16)
        y = jnp.dot(a, wd_ref[0], preferred_element_type=F32) + bd_ref[0]
        o_ref[...] = y.astype(o_ref.dtype)

    @pl.when(pl.program_id(0) >= nu_ref[0])
    def _():
        o_ref[...] = jnp.zeros(o_ref.shape, o_ref.dtype)


def _expert_ffn(blk_exp, n_used, xs, wg, wu, wd, bg, bu, bd):
    p, d = xs.shape
    f = wg.shape[2]
    nblk = p // MOE_ROWS
    wmap = lambda i, be, nu: (be[i], 0, 0)
    grid_spec = pltpu.PrefetchScalarGridSpec(
        num_scalar_prefetch=2,
        grid=(nblk,),
        in_specs=[pl.BlockSpec((MOE_ROWS, d), lambda i, be, nu: (i, 0)),
                  pl.BlockSpec((1, d, f), wmap),
                  pl.BlockSpec((1, d, f), wmap),
                  pl.BlockSpec((1, f, d), wmap),
                  pl.BlockSpec((1, 1, f), wmap),
                  pl.BlockSpec((1, 1, f), wmap),
                  pl.BlockSpec((1, 1, d), wmap)],
        out_specs=pl.BlockSpec((MOE_ROWS, d), lambda i, be, nu: (i, 0)),
    )
    return pl.pallas_call(
        _ffn_kernel,
        out_shape=jax.ShapeDtypeStruct((p, d), BF16),
        grid_spec=grid_spec,
        compiler_params=_cparams(("arbitrary",)),
        name="expert_ffn",
    )(blk_exp, n_used, xs, wg, wu, wd, bg, bu, bd)


def _combine_kernel(h_ref, y_ref, route_ref, g_ref, o_ref):
    d = h_ref.shape[1]
    h = h_ref[...]
    route = route_ref[...]
    for k in range(TOP_K_EXPERTS):
        h = h + route[:, k:k + 1] * y_ref[:, k * d:(k + 1) * d].astype(F32)
    ms = jnp.mean(h * h, axis=-1, keepdims=True)
    o_ref[...] = h * lax.rsqrt(ms + EPS) * g_ref[...]


def _combine(h1, yg, route, g_final, tm):
    n, d = h1.shape
    return pl.pallas_call(
        _combine_kernel,
        out_shape=jax.ShapeDtypeStruct((n, d), F32),
        grid=(n // tm,),
        in_specs=[pl.BlockSpec((tm, d), lambda i: (i, 0)),
                  pl.BlockSpec((tm, TOP_K_EXPERTS * d), lambda i: (i, 0)),
                  pl.BlockSpec((tm, LANES), lambda i: (i, 0)),
                  pl.BlockSpec((1, d), lambda i: (0, 0))],
        out_specs=pl.BlockSpec((tm, d), lambda i: (i, 0)),
        compiler_params=_cparams(("parallel",)),
        name="combine_norm",
    )(h1, yg, route, g_final)


def _t5_bucket(dist):
    n = jnp.maximum(dist, 0)
    max_exact = N_BUCKETS // 2
    nf = jnp.maximum(n, 1).astype(F32)
    large = max_exact + (jnp.log(nf / max_exact) / math.log(MAX_DISTANCE / max_exact)
                         * (N_BUCKETS - max_exact)).astype(I32)
    large = jnp.minimum(large, N_BUCKETS - 1)
    return jnp.where(n < max_exact, n, large)


def _bias_tiles(bias_tab):
    t = ATT_BLOCK
    r = jnp.arange(t, dtype=I32)[:, None]
    c = jnp.arange(t, dtype=I32)[None, :]
    rel = (bias_tab - bias_tab[N_BUCKETS - 1][None, :]).astype(F32)
    tiles = []
    for delta in (0, t):
        dist = r - c + delta
        b = rel[_t5_bucket(dist)]
        b = jnp.where((dist >= 0)[:, :, None], b, MASK_VALUE)
        tiles.append(b.transpose(2, 0, 1))
    return jnp.stack(tiles)


def _regroup_w_in(w_in):
    sizes = (1024, 1024, 1024, 1024, KV_LATENT, 1024, HEAD_DIM_IDX, N_HEADS_IDX, D_MODEL, D_MODEL)
    parts, off = [], 0
    for sz in sizes:
        parts.append(w_in[:, off:off + sz])
        off += sz
    dq, dk, dv, sq, ckv, iq, ik, iw, ga, gb = parts
    dq = dq * (HEAD_DIM_DIFF ** -0.5)
    iw = iw * ((N_HEADS_IDX ** -0.5) * (HEAD_DIM_IDX ** -0.5))
    pad = jnp.zeros((w_in.shape[0], PROJ_WIDTH - COL_IW - N_HEADS_IDX), w_in.dtype)
    w = jnp.concatenate([dq, dk, dv, sq, iq, ga, gb, ckv, ik, ik, iw, pad], axis=1)
    return w.astype(BF16)


def _route_tables(route, n_tok):
    e, k, bm = N_EXPERTS, TOP_K_EXPERTS, MOE_ROWS
    a = n_tok * k
    a_exp = route[:, k:2 * k].astype(I32).reshape(a)
    onehot = (a_exp[:, None] == jnp.arange(e, dtype=I32)[None, :]).astype(I32)
    csum = jnp.cumsum(onehot, axis=0)
    counts = csum[-1]
    rank = jnp.take_along_axis(csum, a_exp[:, None], axis=1)[:, 0] - 1
    padded = (counts + bm - 1) // bm * bm
    pends = jnp.cumsum(padded)
    pstarts = pends - padded
    dest = pstarts[a_exp] + rank
    nblk = -(-(a + e * (bm - 1)) // bm)
    a_tok = jnp.arange(a, dtype=I32) // k
    row_tok = jnp.zeros((nblk * bm,), I32).at[dest].set(a_tok)
    blk_exp = jnp.minimum(jnp.searchsorted(pends, jnp.arange(nblk, dtype=I32) * bm, side='right'),
                          e - 1).astype(I32)
    n_used = (pends[-1] // bm).astype(I32).reshape(1)
    return row_tok, dest, blk_exp, n_used


def kernel(x, norm_attn_g, w_in, rel_bias, lam_q1, lam_k1, lam_q2, lam_k2, diff_subln_g, kv_norm_g, w_uk, w_uv,
           w_branch_diff, w_branch_dsa, w_out, norm_ffn_g, w_router, b_router, w_gate_up, b_gate_up, w_down,
           b_down, norm_final_g):
    batch, seq, d = x.shape
    n = batch * seq
    assert norm_attn_g.shape[0] == 1, "single-layer kernel"
    assert seq % ATT_BLOCK == 0 and d == D_MODEL
    row_tile = math.gcd(n, 1024)

    x2 = x.reshape(n, d)
    proj = _inproj(x2, norm_attn_g[0].reshape(1, d), _regroup_w_in(w_in[0]), row_tile, 1280)

    lam_init = 0.8 - 0.6 * math.exp(-0.3 * 0)
    lam = (jnp.exp(jnp.sum(lam_q1[0].astype(F32) * lam_k1[0].astype(F32)))
           - jnp.exp(jnp.sum(lam_q2[0].astype(F32) * lam_k2[0].astype(F32))) + lam_init)
    y_diff = _diff_attention(proj, lam.reshape(1, 1).astype(F32), _bias_tiles(rel_bias[:, :N_HEADS_DIFF]),
                             diff_subln_g[0].reshape(1, -1).astype(F32), batch, seq, 1.0 - lam_init)

    y_dsa = _dsa_attention(proj, kv_norm_g[0].reshape(1, -1).astype(F32),
                           w_uk[0].transpose(0, 2, 1).astype(BF16), w_uv[0].astype(BF16),
                           _bias_tiles(rel_bias[:, N_HEADS_DIFF:]), batch, seq, min(TOPK_MAX, seq // 4))

    w_r = jnp.zeros((d, LANES), F32).at[:, :N_EXPERTS].set(w_router[0].astype(F32))
    b_r = jnp.full((1, LANES), MASK_VALUE, F32).at[0, :N_EXPERTS].set(b_router[0].astype(F32))
    h1, hn, route = _merge(x2, y_diff, y_dsa, proj, w_branch_diff[0].astype(BF16), w_branch_dsa[0].astype(BF16),
                           w_out[0].astype(BF16), norm_ffn_g[0].reshape(1, d).astype(F32), w_r, b_r,
                           math.gcd(n, 512))

    row_tok, dest, blk_exp, n_used = _route_tables(route, n)
    xs = jnp.take(hn, row_tok, axis=0)
    ys = _expert_ffn(blk_exp, n_used, xs,
                     w_gate_up[0][:, :, 0::2].astype(BF16), w_gate_up[0][:, :, 1::2].astype(BF16),
                     w_down[0].astype(BF16),
                     b_gate_up[0][:, None, 0::2].astype(F32), b_gate_up[0][:, None, 1::2].astype(F32),
                     b_down[0][:, None, :].astype(F32))
    yg = jnp.take(ys, dest, axis=0).reshape(n, TOP_K_EXPERTS * d)
    out = _combine(h1, yg, route, norm_final_g.reshape(1, d).astype(F32), math.gcd(n, 512))
    return out.reshape(batch, seq, d)
```

```python
import functools
import math

import jax
import jax.numpy as jnp
from jax import lax
from jax.experimental import pallas as pl
from jax.experimental.pallas import tpu as pltpu

F32 = jnp.float32
BF16 = jnp.bfloat16
I32 = jnp.int32

D_MODEL = 1024
N_HEADS_DIFF = 8
HEAD_DIM_DIFF = 64
N_HEADS_DSA = 8
HEAD_DIM_DSA = 128
KV_LATENT = 256
N_HEADS_IDX = 16
HEAD_DIM_IDX = 64
TOPK_MAX = 256
N_BUCKETS = 32
MAX_DISTANCE = 128
N_EXPERTS = 32
TOP_K_EXPERTS = 4
D_EXPERT = 1024
SWIGLU_LIMIT = 7.0
SWIGLU_ALPHA = 1.702
EPS = 1e-6

LANES = 128
DIFF_BLOCK = 512
DSA_BLOCK = 256
KEY_CHUNK = 512
MOE_ROWS = 512
PROJ_WIDTH = 7680
VMEM_LIMIT = 56 * 1024 * 1024

COL_DQ, COL_DK, COL_DV, COL_SQ, COL_IQ, COL_GA, COL_GB = (i * 1024 for i in range(7))
COL_CKV = 7168
COL_IK = 7424
COL_IW = 7552

MASK_VALUE = -1e30
M_INIT = -1e29
INT_MIN = -2 ** 31


def _cparams(sem):
    return pltpu.CompilerParams(dimension_semantics=sem, vmem_limit_bytes=VMEM_LIMIT)


def _inproj_kernel(x_ref, g_ref, w_ref, o_ref, xn_ref):
    @pl.when(pl.program_id(1) == 0)
    def _():
        x = x_ref[...]
        ms = jnp.mean(x * x, axis=-1, keepdims=True)
        xn_ref[...] = (x * lax.rsqrt(ms + EPS) * g_ref[...]).astype(BF16)

    o_ref[...] = jnp.dot(xn_ref[...], w_ref[...], preferred_element_type=F32).astype(o_ref.dtype)


def _inproj(x2, g, w, tm, tn):
    n, d = x2.shape
    width = w.shape[1]
    return pl.pallas_call(
        _inproj_kernel,
        out_shape=jax.ShapeDtypeStruct((n, width), BF16),
        grid=(n // tm, width // tn),
        in_specs=[pl.BlockSpec((tm, d), lambda i, j: (i, 0)),
                  pl.BlockSpec((1, d), lambda i, j: (0, 0)),
                  pl.BlockSpec((d, tn), lambda i, j: (0, j))],
        out_specs=pl.BlockSpec((tm, tn), lambda i, j: (i, j)),
        scratch_shapes=[pltpu.VMEM((tm, d), BF16)],
        compiler_params=_cparams(("parallel", "arbitrary")),
        name="inproj",
    )(x2, g, w)


def _softmax_step(s_ref, tk, v, m_ref, l_ref, acc_ref):
    nl = tk // LANES
    smax = s_ref[:, 0:LANES]
    for j in range(1, nl):
        smax = jnp.maximum(smax, s_ref[:, j * LANES:(j + 1) * LANES])
    m_prev = m_ref[...]
    m_new = jnp.maximum(m_prev, jnp.max(smax, axis=-1, keepdims=True))
    alpha = jnp.exp(m_prev - m_new)
    psum = None
    ps = []
    for j in range(nl):
        pj = jnp.exp(s_ref[:, j * LANES:(j + 1) * LANES] - m_new)
        psum = pj if psum is None else psum + pj
        ps.append(pj.astype(BF16))
    l_ref[...] = alpha * l_ref[...] + psum
    pv = jnp.dot(jnp.concatenate(ps, axis=1), v, preferred_element_type=F32)
    e = acc_ref.shape[1]
    a = alpha if e == LANES else jnp.concatenate([alpha] * (e // LANES), axis=1)
    acc_ref[...] = a * acc_ref[...] + pv
    m_ref[...] = m_new


def _softmax_init(m_ref, l_ref, acc_ref):
    m_ref[...] = jnp.full(m_ref.shape, M_INIT, F32)
    l_ref[...] = jnp.zeros(l_ref.shape, F32)
    acc_ref[...] = jnp.zeros(acc_ref.shape, F32)


def _softmax_result(l_ref, acc_ref):
    return acc_ref[...] * (1.0 / jnp.sum(l_ref[...], axis=-1, keepdims=True))


def _near_bias(s_ref, d0, d1, delta, groups, t, tk):
    for g in range(groups):
        for rb in range(t // LANES):
            for cb in range(tk // LANES):
                bd = delta + rb - cb
                rows = slice(g * t + rb * LANES, g * t + (rb + 1) * LANES)
                cols = slice(cb * LANES, (cb + 1) * LANES)
                if bd == 0:
                    s_ref[rows, cols] = s_ref[rows, cols] + d0(g)
                elif bd == 1:
                    s_ref[rows, cols] = s_ref[rows, cols] + d1(g)
                elif bd < 0:
                    s_ref[rows, cols] = jnp.full((LANES, LANES), MASK_VALUE, F32)


def _dot_nt(a, b):
    return lax.dot_general(a, b, (((1,), (1,)), ((), ())), preferred_element_type=F32)


def _diff_kernel(lam_ref, q_ref, k_ref, v_ref, bias_ref, g_ref, o_ref, q2_ref, s_ref, m_ref, l_ref, acc_ref,
                 *, out_scale):
    t = q_ref.shape[0]
    qi = pl.program_id(2)

    q = q_ref[...]
    lane = lax.broadcasted_iota(I32, q.shape, 1)
    zero = jnp.zeros_like(q)
    q2_ref[0:t, :] = jnp.where(lane < HEAD_DIM_DIFF, q, zero)
    q2_ref[t:2 * t, :] = jnp.where(lane >= HEAD_DIM_DIFF, q, zero)
    _softmax_init(m_ref, l_ref, acc_ref)
    d0 = lambda g: bias_ref[0, 0]
    d1 = lambda g: bias_ref[1, 0]

    def chunk(kc, delta):
        off = pl.multiple_of(kc * t, t)
        s_ref[...] = _dot_nt(q2_ref[...], k_ref[pl.ds(off, t), :])
        if delta is not None:
            _near_bias(s_ref, d0, d1, delta, 2, t, t)
        _softmax_step(s_ref, t, v_ref[pl.ds(off, t), :], m_ref, l_ref, acc_ref)

    def far_body(kc, carry):
        chunk(kc, None)
        return carry

    lax.fori_loop(0, jnp.maximum(qi - 1, 0), far_body, 0)

    @pl.when(qi >= 1)
    def _():
        chunk(qi - 1, t // LANES)

    chunk(qi, 0)

    o = _softmax_result(l_ref, acc_ref)
    o = o[0:t, :] - lam_ref[0, 0] * o[t:2 * t, :]
    ms = jnp.mean(o * o, axis=-1, keepdims=True)
    o_ref[...] = (o * lax.rsqrt(ms + EPS) * g_ref[...] * out_scale).astype(o_ref.dtype)


def _diff_attention(proj, lam, bias_blocks, subln_g, batch, seq, out_scale):
    t = DIFF_BLOCK
    nq = seq // t
    h = N_HEADS_DIFF
    e = 2 * HEAD_DIM_DIFF
    kernel = functools.partial(_diff_kernel, out_scale=out_scale)
    return pl.pallas_call(
        kernel,
        out_shape=jax.ShapeDtypeStruct((batch * seq, h * e), BF16),
        grid=(batch, h, nq),
        in_specs=[pl.BlockSpec(memory_space=pltpu.SMEM),
                  pl.BlockSpec((t, e), lambda b, hh, qi: (b * nq + qi, COL_DQ // e + hh)),
                  pl.BlockSpec((seq, e), lambda b, hh, qi: (b, COL_DK // e + hh)),
                  pl.BlockSpec((seq, e), lambda b, hh, qi: (b, COL_DV // e + hh)),
                  pl.BlockSpec((2, 1, LANES, LANES), lambda b, hh, qi: (0, hh, 0, 0)),
                  pl.BlockSpec((1, e), lambda b, hh, qi: (0, 0))],
        out_specs=pl.BlockSpec((t, e), lambda b, hh, qi: (b * nq + qi, hh)),
        scratch_shapes=[pltpu.VMEM((2 * t, e), BF16),
                        pltpu.VMEM((2 * t, t), F32),
                        pltpu.VMEM((2 * t, LANES), F32),
                        pltpu.VMEM((2 * t, LANES), F32),
                        pltpu.VMEM((2 * t, e), F32)],
        compiler_params=_cparams(("parallel", "parallel", "arbitrary")),
        name="diff_attn",
    )(lam, proj, proj, proj, bias_blocks, subln_g)


def _sortable_key(x):
    bits = lax.bitcast_convert_type(x, I32)
    return bits ^ ((bits >> 31) & jnp.int32(0x7FFFFFFF))


def _dsa_kernel(iq_ref, sq_ref, iw_ref, ik_ref, ckv_ref, kvg_ref, wuk_ref, wuv_ref, bias_ref, o_ref,
                c_ref, key_ref, am_ref, qi_ref, wb_ref, ql_ref, s_ref, m_ref, l_ref, acc_ref, *, topk, scale):
    t = iq_ref.shape[0]
    tk = KEY_CHUNK
    qi = pl.program_id(1)
    n_chunks = qi + 1
    hi, hb = N_HEADS_IDX, N_HEADS_DSA

    @pl.when(qi == 0)
    def _():
        ckv = ckv_ref[...].astype(F32)
        ms = jnp.mean(ckv * ckv, axis=-1, keepdims=True)
        c_ref[...] = (ckv * lax.rsqrt(ms + EPS) * kvg_ref[...]).astype(BF16)

    lane = lax.broadcasted_iota(I32, (t, LANES), 1)
    for h in range(hi):
        blk = iq_ref[:, (h // 2) * LANES:(h // 2 + 1) * LANES]
        keep = (lane < HEAD_DIM_IDX) if h % 2 == 0 else (lane >= HEAD_DIM_IDX)
        qi_ref[h * t:(h + 1) * t, :] = jnp.where(keep, blk, jnp.zeros_like(blk))
        wb_ref[h] = jnp.broadcast_to(iw_ref[:, h:h + 1].astype(F32), (t, LANES))

    def score_chunk(kc, diag):
        off = pl.multiple_of(kc * t, t)
        d = _dot_nt(qi_ref[...], ik_ref[pl.ds(off, t), :]).reshape(hi, t, t)
        sc = jnp.zeros((t, t), F32)
        for h in range(hi):
            w = wb_ref[h]
            w = jnp.concatenate([w] * (t // LANES), axis=1)
            sc = sc + jnp.maximum(d[h], 0.0) * w
        key = _sortable_key(sc + 0.0)
        if diag:
            row = lax.broadcasted_iota(I32, (t, t), 0)
            col = lax.broadcasted_iota(I32, (t, t), 1)
            key = jnp.where(col <= row, key, jnp.int32(INT_MIN))
        key_ref[:, pl.ds(off, t)] = key

    def score_body(kc, carry):
        score_chunk(kc, False)
        return carry

    lax.fori_loop(0, qi, score_body, 0)
    score_chunk(qi, True)

    def count_ge(cand):
        parts = []
        for rg in range(t // LANES):
            rows = slice(rg * LANES, (rg + 1) * LANES)
            cand_b = jnp.broadcast_to(cand[rows], (LANES, LANES))

            def body(kc, cnt, rows=rows, cand_b=cand_b):
                for j in range(t // LANES):
                    off = pl.multiple_of(kc * t + j * LANES, LANES)
                    cnt = cnt + jnp.where(key_ref[rows, pl.ds(off, LANES)] >= cand_b, 1, 0)
                return cnt

            cnt = lax.fori_loop(0, n_chunks, body, jnp.zeros((LANES, LANES), I32))
            parts.append(jnp.sum(cnt, axis=1, keepdims=True))
        return jnp.concatenate(parts, axis=0)

    def bit_body(i, cur):
        bit = lax.shift_left(jnp.int32(1), 31 - i)
        cand = cur | bit
        cnt = count_ge(cand ^ jnp.int32(INT_MIN))
        return jnp.where(cnt >= topk, cand, cur)

    cur = lax.fori_loop(0, 32, bit_body, jnp.zeros((t, 1), I32))
    thr = jnp.maximum(cur ^ jnp.int32(INT_MIN), jnp.int32(INT_MIN + 1))
    thr_b = jnp.broadcast_to(thr, (t, LANES))

    def mask_body(kc, carry):
        off = pl.multiple_of(kc * t, t)
        k = key_ref[:, pl.ds(off, t)]
        thr_w = jnp.concatenate([thr_b] * (t // LANES), axis=1)
        am_ref[:, pl.ds(off, t)] = jnp.where(k >= thr_w, 0.0, MASK_VALUE).astype(F32)
        return carry

    lax.fori_loop(0, n_chunks, mask_body, 0)

    for h in range(hb):
        qh = sq_ref[:, h * HEAD_DIM_DSA:(h + 1) * HEAD_DIM_DSA]
        ql = jnp.dot(qh, wuk_ref[h], preferred_element_type=F32) * scale
        ql_ref[h * t:(h + 1) * t, :] = ql.astype(BF16)
    _softmax_init(m_ref, l_ref, acc_ref)
    d0 = lambda g: bias_ref[0, g]
    d1 = lambda g: bias_ref[1, g]

    def chunk(kc, width, delta):
        off = pl.multiple_of(kc * tk, tk)
        c = c_ref[pl.ds(off, width), :]
        s = _dot_nt(ql_ref[...], c).reshape(hb, t, width) + am_ref[:, pl.ds(off, width)][None]
        s_ref[:, 0:width] = s.reshape(hb * t, width)
        if delta is not None:
            _near_bias(s_ref, d0, d1, delta, hb, t, width)
        _softmax_step(s_ref, width, c, m_ref, l_ref, acc_ref)

    def far_body(kc, carry):
        chunk(kc, tk, None)
        return carry

    lax.fori_loop(0, jnp.maximum((qi - 1) // 2, 0), far_body, 0)
    half = qi // 2

    @pl.when(qi % 2 == 1)
    def _():
        chunk(half, tk, t // LANES)

    @pl.when(qi % 2 == 0)
    def _():
        @pl.when(half >= 1)
        def _():
            chunk(half - 1, tk, tk // LANES)
        chunk(half, t, 0)

    ol = _softmax_result(l_ref, acc_ref).astype(BF16)
    for h in range(hb):
        o = jnp.dot(ol[h * t:(h + 1) * t, :], wuv_ref[h], preferred_element_type=F32)
        o_ref[:, h * HEAD_DIM_DSA:(h + 1) * HEAD_DIM_DSA] = o.astype(o_ref.dtype)


def _dsa_attention(proj, kv_g, w_ukt, w_uv, bias_blocks, batch, seq, topk):
    t = DSA_BLOCK
    nq = seq // t
    hb, hi = N_HEADS_DSA, N_HEADS_IDX
    width = hb * HEAD_DIM_DSA
    kernel = functools.partial(_dsa_kernel, topk=topk, scale=HEAD_DIM_DSA ** -0.5)
    return pl.pallas_call(
        kernel,
        out_shape=jax.ShapeDtypeStruct((batch * seq, width), BF16),
        grid=(batch, nq),
        in_specs=[pl.BlockSpec((t, 1024), lambda b, qi: (b * nq + qi, COL_IQ // 1024)),
                  pl.BlockSpec((t, 1024), lambda b, qi: (b * nq + qi, COL_SQ // 1024)),
                  pl.BlockSpec((t, LANES), lambda b, qi: (b * nq + qi, COL_IW // LANES)),
                  pl.BlockSpec((seq, LANES), lambda b, qi: (b, COL_IK // LANES)),
                  pl.BlockSpec((seq, KV_LATENT), lambda b, qi: (b, COL_CKV // KV_LATENT)),
                  pl.BlockSpec((1, KV_LATENT), lambda b, qi: (0, 0)),
                  pl.BlockSpec((hb, HEAD_DIM_DSA, KV_LATENT), lambda b, qi: (0, 0, 0)),
                  pl.BlockSpec((hb, KV_LATENT, HEAD_DIM_DSA), lambda b, qi: (0, 0, 0)),
                  pl.BlockSpec((2, hb, LANES, LANES), lambda b, qi: (0, 0, 0, 0))],
        out_specs=pl.BlockSpec((t, width), lambda b, qi: (b * nq + qi, 0)),
        scratch_shapes=[pltpu.VMEM((seq, KV_LATENT), BF16),
                        pltpu.VMEM((t, seq), I32),
                        pltpu.VMEM((t, seq), F32),
                        pltpu.VMEM((hi * t, LANES), BF16),
                        pltpu.VMEM((hi, t, LANES), F32),
                        pltpu.VMEM((hb * t, KV_LATENT), BF16),
                        pltpu.VMEM((hb * t, KEY_CHUNK), F32),
                        pltpu.VMEM((hb * t, LANES), F32),
                        pltpu.VMEM((hb * t, LANES), F32),
                        pltpu.VMEM((hb * t, KV_LATENT), F32)],
        compiler_params=_cparams(("parallel", "arbitrary")),
        name="dsa_attn",
    )(proj, proj, proj, proj, proj, kv_g, w_ukt, w_uv, bias_blocks)


def _merge_kernel(x_ref, yd_ref, ys_ref, ga_ref, gb_ref, wd_ref, ws_ref, wo_ref, g_ref, wr_ref, br_ref,
                  h_ref, hn_ref, route_ref):
    bd = jnp.dot(yd_ref[...], wd_ref[...], preferred_element_type=F32)
    bs = jnp.dot(ys_ref[...], ws_ref[...], preferred_element_type=F32)
    merged = (jax.nn.sigmoid(ga_ref[...].astype(F32)) * bd + jax.nn.sigmoid(gb_ref[...].astype(F32)) * bs)
    h = x_ref[...] + jnp.dot(merged.astype(BF16), wo_ref[...], preferred_element_type=F32)
    h_ref[...] = h
    ms = jnp.mean(h * h, axis=-1, keepdims=True)
    hn = h * lax.rsqrt(ms + EPS) * g_ref[...]
    hn_ref[...] = hn.astype(hn_ref.dtype)

    logits = jnp.dot(hn, wr_ref[...], preferred_element_type=F32, precision=lax.Precision.HIGHEST)
    logits = logits + br_ref[...]
    lane = lax.broadcasted_iota(I32, logits.shape, 1)
    vals, ids = [], []
    for _ in range(TOP_K_EXPERTS):
        mx = jnp.max(logits, axis=-1, keepdims=True)
        ix = jnp.min(jnp.where(logits == mx, lane, LANES), axis=-1, keepdims=True)
        vals.append(mx)
        ids.append(ix)
        logits = jnp.where(lane == ix, -jnp.inf, logits)
    es = [jnp.exp(v - vals[0]) for v in vals]
    inv = 1.0 / (es[0] + es[1] + es[2] + es[3])
    route = jnp.zeros(logits.shape, F32)
    for k in range(TOP_K_EXPERTS):
        route = jnp.where(lane == k, es[k] * inv, route)
        route = jnp.where(lane == TOP_K_EXPERTS + k, ids[k].astype(F32), route)
    route_ref[...] = route


def _merge(x2, y_diff, y_dsa, proj, w_bd, w_bs, w_out, g_ffn, w_router, b_router, tm):
    n, d = x2.shape
    row = lambda i: (i, 0)
    const = lambda i: (0, 0)
    return pl.pallas_call(
        _merge_kernel,
        out_shape=(jax.ShapeDtypeStruct((n, d), F32),
                   jax.ShapeDtypeStruct((n, d), BF16),
                   jax.ShapeDtypeStruct((n, LANES), F32)),
        grid=(n // tm,),
        in_specs=[pl.BlockSpec((tm, d), row),
                  pl.BlockSpec((tm, d), row),
                  pl.BlockSpec((tm, d), row),
                  pl.BlockSpec((tm, d), lambda i: (i, COL_GA // 1024)),
                  pl.BlockSpec((tm, d), lambda i: (i, COL_GB // 1024)),
                  pl.BlockSpec((d, d), const),
                  pl.BlockSpec((d, d), const),
                  pl.BlockSpec((d, d), const),
                  pl.BlockSpec((1, d), const),
                  pl.BlockSpec((d, LANES), const),
                  pl.BlockSpec((1, LANES), const)],
        out_specs=(pl.BlockSpec((tm, d), row),
                   pl.BlockSpec((tm, d), row),
                   pl.BlockSpec((tm, LANES), row)),
        compiler_params=_cparams(("parallel",)),
        name="merge_router",
    )(x2, y_diff, y_dsa, proj, proj, w_bd, w_bs, w_out, g_ffn, w_router, b_router)


def _regroup_kernel(w_ref, p_ref, o_ref):
    pw = p_ref.shape[0]
    for j in range(w_ref.shape[2] // pw):
        w = w_ref[0, :, j * pw:(j + 1) * pw].astype(BF16)
        o_ref[0, :, j * pw:(j + 1) * pw] = jnp.dot(w, p_ref[...], preferred_element_type=F32).astype(BF16)


def _regroup_gate_up(w_gu, rows):
    e, d, f2 = w_gu.shape
    pw = 2 * LANES
    src = jnp.arange(pw, dtype=I32)
    dst = (src % 2) * LANES + src // 2
    perm = (dst[:, None] == jnp.arange(pw, dtype=I32)[None, :]).astype(BF16)
    return pl.pallas_call(
        _regroup_kernel,
        out_shape=jax.ShapeDtypeStruct((e, d, f2), BF16),
        grid=(e, d // rows),
        in_specs=[pl.BlockSpec((1, rows, f2), lambda i, j: (i, j, 0)),
                  pl.BlockSpec((pw, pw), lambda i, j: (0, 0))],
        out_specs=pl.BlockSpec((1, rows, f2), lambda i, j: (i, j, 0)),
        compiler_params=_cparams(("parallel", "parallel")),
        name="regroup_gate_up",
    )(w_gu, perm)


def _ffn_kernel(be_ref, nu_ref, x_ref, wgu_ref, wd_ref, bgu_ref, bd_ref, o_ref):
    @pl.when(pl.program_id(0) < nu_ref[0])
    def _():
        gu = jnp.dot(x_ref[...], wgu_ref[0], preferred_element_type=F32) + bgu_ref[0]
        acts = []
        for j in range(gu.shape[1] // (2 * LANES)):
            gate = jnp.minimum(gu[:, 2 * j * LANES:(2 * j + 1) * LANES], SWIGLU_LIMIT)
            up = jnp.clip(gu[:, (2 * j + 1) * LANES:(2 * j + 2) * LANES], -SWIGLU_LIMIT, SWIGLU_LIMIT)
            glu = gate * jax.nn.sigmoid(gate * SWIGLU_ALPHA)
            acts.append(((up + 1.0) * glu).astype(BF16))
        a = jnp.concatenate(acts, axis=1)
        y = jnp.dot(a, wd_ref[0], preferred_element_type=F32) + bd_ref[0]
        o_ref[...] = y.astype(o_ref.dtype)

    @pl.when(pl.program_id(0) >= nu_ref[0])
    def _():
        o_ref[...] = jnp.zeros(o_ref.shape, o_ref.dtype)


def _expert_ffn(blk_exp, n_used, xs, wgu, wd, bgu, bd):
    p, d = xs.shape
    f = wd.shape[1]
    nblk = p // MOE_ROWS
    wmap = lambda i, be, nu: (be[i], 0, 0)
    grid_spec = pltpu.PrefetchScalarGridSpec(
        num_scalar_prefetch=2,
        grid=(nblk,),
        in_specs=[pl.BlockSpec((MOE_ROWS, d), lambda i, be, nu: (i, 0)),
                  pl.BlockSpec((1, d, 2 * f), wmap),
                  pl.BlockSpec((1, f, d), wmap),
                  pl.BlockSpec((1, 1, 2 * f), wmap),
                  pl.BlockSpec((1, 1, d), wmap)],
        out_specs=pl.BlockSpec((MOE_ROWS, d), lambda i, be, nu: (i, 0)),
    )
    return pl.pallas_call(
        _ffn_kernel,
        out_shape=jax.ShapeDtypeStruct((p, d), BF16),
        grid_spec=grid_spec,
        compiler_params=_cparams(("arbitrary",)),
        name="expert_ffn",
    )(blk_exp, n_used, xs, wgu, wd, bgu, bd)


def _combine_kernel(h_ref, y_ref, route_ref, g_ref, o_ref):
    d = h_ref.shape[1]
    h = h_ref[...]
    route = route_ref[...]
    for k in range(TOP_K_EXPERTS):
        h = h + route[:, k:k + 1] * y_ref[:, k * d:(k + 1) * d].astype(F32)
    ms = jnp.mean(h * h, axis=-1, keepdims=True)
    o_ref[...] = h * lax.rsqrt(ms + EPS) * g_ref[...]


def _combine(h1, yg, route, g_final, tm):
    n, d = h1.shape
    return pl.pallas_call(
        _combine_kernel,
        out_shape=jax.ShapeDtypeStruct((n, d), F32),
        grid=(n // tm,),
        in_specs=[pl.BlockSpec((tm, d), lambda i: (i, 0)),
                  pl.BlockSpec((tm, TOP_K_EXPERTS * d), lambda i: (i, 0)),
                  pl.BlockSpec((tm, LANES), lambda i: (i, 0)),
                  pl.BlockSpec((1, d), lambda i: (0, 0))],
        out_specs=pl.BlockSpec((tm, d), lambda i: (i, 0)),
        compiler_params=_cparams(("parallel",)),
        name="combine_norm",
    )(h1, yg, route, g_final)


def _t5_bucket(dist):
    n = jnp.maximum(dist, 0)
    max_exact = N_BUCKETS // 2
    nf = jnp.maximum(n, 1).astype(F32)
    large = max_exact + (jnp.log(nf / max_exact) / math.log(MAX_DISTANCE / max_exact)
                         * (N_BUCKETS - max_exact)).astype(I32)
    large = jnp.minimum(large, N_BUCKETS - 1)
    return jnp.where(n < max_exact, n, large)


def _bias_blocks(bias_tab):
    t = LANES
    assert MAX_DISTANCE <= LANES
    r = jnp.arange(t, dtype=I32)[:, None]
    c = jnp.arange(t, dtype=I32)[None, :]
    rel = (bias_tab - bias_tab[N_BUCKETS - 1][None, :]).astype(F32)
    tiles = []
    for delta in (0, t):
        dist = r - c + delta
        b = rel[_t5_bucket(dist)]
        b = jnp.where((dist >= 0)[:, :, None], b, MASK_VALUE)
        tiles.append(b.transpose(2, 0, 1))
    return jnp.stack(tiles)


def _regroup_w_in(w_in):
    sizes = (1024, 1024, 1024, 1024, KV_LATENT, 1024, HEAD_DIM_IDX, N_HEADS_IDX, D_MODEL, D_MODEL)
    parts, off = [], 0
    for sz in sizes:
        parts.append(w_in[:, off:off + sz])
        off += sz
    dq, dk, dv, sq, ckv, iq, ik, iw, ga, gb = parts
    dq = dq * (HEAD_DIM_DIFF ** -0.5)
    iw = iw * ((N_HEADS_IDX ** -0.5) * (HEAD_DIM_IDX ** -0.5))
    pad = jnp.zeros((w_in.shape[0], PROJ_WIDTH - COL_IW - N_HEADS_IDX), w_in.dtype)
    w = jnp.concatenate([dq, dk, dv, sq, iq, ga, gb, ckv, ik, ik, iw, pad], axis=1)
    return w.astype(BF16)


def _route_tables(route, n_tok):
    e, k, bm = N_EXPERTS, TOP_K_EXPERTS, MOE_ROWS
    a = n_tok * k
    a_exp = route[:, k:2 * k].astype(I32).reshape(a)
    onehot = (a_exp[:, None] == jnp.arange(e, dtype=I32)[None, :]).astype(I32)
    csum = jnp.cumsum(onehot, axis=0)
    counts = csum[-1]
    rank = jnp.take_along_axis(csum, a_exp[:, None], axis=1)[:, 0] - 1
    padded = (counts + bm - 1) // bm * bm
    pends = jnp.cumsum(padded)
    pstarts = pends - padded
    dest = pstarts[a_exp] + rank
    nblk = -(-(a + e * (bm - 1)) // bm)
    a_tok = jnp.arange(a, dtype=I32) // k
    row_tok = jnp.zeros((nblk * bm,), I32).at[dest].set(a_tok)
    blk_exp = jnp.minimum(jnp.searchsorted(pends, jnp.arange(nblk, dtype=I32) * bm, side='right'),
                          e - 1).astype(I32)
    n_used = (pends[-1] // bm).astype(I32).reshape(1)
    return row_tok, dest, blk_exp, n_used


def kernel(x, norm_attn_g, w_in, rel_bias, lam_q1, lam_k1, lam_q2, lam_k2, diff_subln_g, kv_norm_g, w_uk, w_uv,
           w_branch_diff, w_branch_dsa, w_out, norm_ffn_g, w_router, b_router, w_gate_up, b_gate_up, w_down,
           b_down, norm_final_g):
    batch, seq, d = x.shape
    n = batch * seq
    assert norm_attn_g.shape[0] == 1, "single-layer kernel"
    assert seq % DIFF_BLOCK == 0 and seq % KEY_CHUNK == 0 and d == D_MODEL
    row_tile = math.gcd(n, 1024)

    x2 = x.reshape(n, d)
    proj = _inproj(x2, norm_attn_g[0].reshape(1, d), _regroup_w_in(w_in[0]), row_tile, 1280)

    lam_init = 0.8 - 0.6 * math.exp(-0.3 * 0)
    lam = (jnp.exp(jnp.sum(lam_q1[0].astype(F32) * lam_k1[0].astype(F32)))
           - jnp.exp(jnp.sum(lam_q2[0].astype(F32) * lam_k2[0].astype(F32))) + lam_init)
    y_diff = _diff_attention(proj, lam.reshape(1, 1).astype(F32), _bias_blocks(rel_bias[:, :N_HEADS_DIFF]),
                             diff_subln_g[0].reshape(1, -1).astype(F32), batch, seq, 1.0 - lam_init)

    y_dsa = _dsa_attention(proj, kv_norm_g[0].reshape(1, -1).astype(F32),
                           w_uk[0].transpose(0, 2, 1).astype(BF16), w_uv[0].astype(BF16),
                           _bias_blocks(rel_bias[:, N_HEADS_DIFF:]), batch, seq, min(TOPK_MAX, seq // 4))

    w_r = jnp.zeros((d, LANES), F32).at[:, :N_EXPERTS].set(w_router[0].astype(F32))
    b_r = jnp.full((1, LANES), MASK_VALUE, F32).at[0, :N_EXPERTS].set(b_router[0].astype(F32))
    h1, hn, route = _merge(x2, y_diff, y_dsa, proj, w_branch_diff[0].astype(BF16), w_branch_dsa[0].astype(BF16),
                           w_out[0].astype(BF16), norm_ffn_g[0].reshape(1, d).astype(F32), w_r, b_r,
                           math.gcd(n, 512))

    row_tok, dest, blk_exp, n_used = _route_tables(route, n)
    xs = jnp.take(hn, row_tok, axis=0)
    e, f = N_EXPERTS, D_EXPERT
    b_gu = b_gate_up[0].astype(F32).reshape(e, f // LANES, LANES, 2).transpose(0, 1, 3, 2).reshape(e, 1, 2 * f)
    ys = _expert_ffn(blk_exp, n_used, xs, _regroup_gate_up(w_gate_up[0], 512), w_down[0].astype(BF16),
                     b_gu, b_down[0][:, None, :].astype(F32))
    yg = jnp.take(ys, dest, axis=0).reshape(n, TOP_K_EXPERTS * d)
    out = _combine(h1, yg, route, norm_final_g.reshape(1, d).astype(F32), math.gcd(n, 512))
    return out.reshape(batch, seq, d)
```

```python
import functools
import math

import jax
import jax.numpy as jnp
from jax import lax
from jax.experimental import pallas as pl
from jax.experimental.pallas import tpu as pltpu
from jax.experimental.pallas import tpu_sc as plsc

F32 = jnp.float32
BF16 = jnp.bfloat16
I32 = jnp.int32

D_MODEL = 1024
N_HEADS_DIFF = 8
HEAD_DIM_DIFF = 64
N_HEADS_DSA = 8
HEAD_DIM_DSA = 128
KV_LATENT = 256
N_HEADS_IDX = 16
HEAD_DIM_IDX = 64
TOPK_MAX = 256
N_BUCKETS = 32
MAX_DISTANCE = 128
N_EXPERTS = 32
TOP_K_EXPERTS = 4
D_EXPERT = 1024
SWIGLU_LIMIT = 7.0
SWIGLU_ALPHA = 1.702
EPS = 1e-6

LANES = 128
DIFF_BLOCK = 512
DSA_BLOCK = 256
KEY_CHUNK = 512
MOE_ROWS = 512
PROJ_WIDTH = 7680
VMEM_LIMIT = 56 * 1024 * 1024

COL_DQ, COL_DK, COL_DV, COL_SQ, COL_IQ, COL_GA, COL_GB = (i * 1024 for i in range(7))
COL_CKV = 7168
COL_IK = 7424
COL_IW = 7552

MASK_VALUE = -1e30
M_INIT = -1e29
INT_MIN = -2 ** 31


def _cparams(sem):
    return pltpu.CompilerParams(dimension_semantics=sem, vmem_limit_bytes=VMEM_LIMIT)


def _inproj_kernel(x_ref, g_ref, w_ref, o_ref, xn_ref):
    @pl.when(pl.program_id(1) == 0)
    def _():
        x = x_ref[...]
        ms = jnp.mean(x * x, axis=-1, keepdims=True)
        xn_ref[...] = (x * lax.rsqrt(ms + EPS) * g_ref[...]).astype(BF16)

    o_ref[...] = jnp.dot(xn_ref[...], w_ref[...], preferred_element_type=F32).astype(o_ref.dtype)


def _inproj(x2, g, w, tm, tn):
    n, d = x2.shape
    width = w.shape[1]
    return pl.pallas_call(
        _inproj_kernel,
        out_shape=jax.ShapeDtypeStruct((n, width), BF16),
        grid=(n // tm, width // tn),
        in_specs=[pl.BlockSpec((tm, d), lambda i, j: (i, 0)),
                  pl.BlockSpec((1, d), lambda i, j: (0, 0)),
                  pl.BlockSpec((d, tn), lambda i, j: (0, j))],
        out_specs=pl.BlockSpec((tm, tn), lambda i, j: (i, j)),
        scratch_shapes=[pltpu.VMEM((tm, d), BF16)],
        compiler_params=_cparams(("parallel", "arbitrary")),
        name="inproj",
    )(x2, g, w)


def _softmax_step(s_ref, tk, v, m_ref, l_ref, acc_ref):
    nl = tk // LANES
    smax = s_ref[:, 0:LANES]
    for j in range(1, nl):
        smax = jnp.maximum(smax, s_ref[:, j * LANES:(j + 1) * LANES])
    m_prev = m_ref[...]
    m_new = jnp.maximum(m_prev, jnp.max(smax, axis=-1, keepdims=True))
    alpha = jnp.exp(m_prev - m_new)
    psum = None
    ps = []
    for j in range(nl):
        pj = jnp.exp(s_ref[:, j * LANES:(j + 1) * LANES] - m_new)
        psum = pj if psum is None else psum + pj
        ps.append(pj.astype(BF16))
    l_ref[...] = alpha * l_ref[...] + psum
    pv = jnp.dot(jnp.concatenate(ps, axis=1), v, preferred_element_type=F32)
    e = acc_ref.shape[1]
    a = alpha if e == LANES else jnp.concatenate([alpha] * (e // LANES), axis=1)
    acc_ref[...] = a * acc_ref[...] + pv
    m_ref[...] = m_new


def _softmax_init(m_ref, l_ref, acc_ref):
    m_ref[...] = jnp.full(m_ref.shape, M_INIT, F32)
    l_ref[...] = jnp.zeros(l_ref.shape, F32)
    acc_ref[...] = jnp.zeros(acc_ref.shape, F32)


def _softmax_result(l_ref, acc_ref):
    return acc_ref[...] * (1.0 / jnp.sum(l_ref[...], axis=-1, keepdims=True))


def _near_bias(s_ref, d0, d1, delta, groups, t, tk):
    for g in range(groups):
        for rb in range(t // LANES):
            for cb in range(tk // LANES):
                bd = delta + rb - cb
                rows = slice(g * t + rb * LANES, g * t + (rb + 1) * LANES)
                cols = slice(cb * LANES, (cb + 1) * LANES)
                if bd == 0:
                    s_ref[rows, cols] = s_ref[rows, cols] + d0(g)
                elif bd == 1:
                    s_ref[rows, cols] = s_ref[rows, cols] + d1(g)
                elif bd < 0:
                    s_ref[rows, cols] = jnp.full((LANES, LANES), MASK_VALUE, F32)


def _dot_nt(a, b):
    return lax.dot_general(a, b, (((1,), (1,)), ((), ())), preferred_element_type=F32)


def _diff_kernel(lam_ref, q_ref, k_ref, v_ref, bias_ref, g_ref, o_ref, q2_ref, s_ref, m_ref, l_ref, acc_ref,
                 *, out_scale):
    t = q_ref.shape[0]
    qi = pl.program_id(2)

    q = q_ref[...]
    lane = lax.broadcasted_iota(I32, q.shape, 1)
    zero = jnp.zeros_like(q)
    q2_ref[0:t, :] = jnp.where(lane < HEAD_DIM_DIFF, q, zero)
    q2_ref[t:2 * t, :] = jnp.where(lane >= HEAD_DIM_DIFF, q, zero)
    _softmax_init(m_ref, l_ref, acc_ref)
    d0 = lambda g: bias_ref[0, 0]
    d1 = lambda g: bias_ref[1, 0]

    def chunk(kc, delta):
        off = pl.multiple_of(kc * t, t)
        s_ref[...] = _dot_nt(q2_ref[...], k_ref[pl.ds(off, t), :])
        if delta is not None:
            _near_bias(s_ref, d0, d1, delta, 2, t, t)
        _softmax_step(s_ref, t, v_ref[pl.ds(off, t), :], m_ref, l_ref, acc_ref)

    def far_body(kc, carry):
        chunk(kc, None)
        return carry

    lax.fori_loop(0, jnp.maximum(qi - 1, 0), far_body, 0)

    @pl.when(qi >= 1)
    def _():
        chunk(qi - 1, t // LANES)

    chunk(qi, 0)

    o = _softmax_result(l_ref, acc_ref)
    o = o[0:t, :] - lam_ref[0, 0] * o[t:2 * t, :]
    ms = jnp.mean(o * o, axis=-1, keepdims=True)
    o_ref[...] = (o * lax.rsqrt(ms + EPS) * g_ref[...] * out_scale).astype(o_ref.dtype)


def _diff_attention(proj, lam, bias_blocks, subln_g, batch, seq, out_scale):
    t = DIFF_BLOCK
    nq = seq // t
    h = N_HEADS_DIFF
    e = 2 * HEAD_DIM_DIFF
    kernel = functools.partial(_diff_kernel, out_scale=out_scale)
    return pl.pallas_call(
        kernel,
        out_shape=jax.ShapeDtypeStruct((batch * seq, h * e), BF16),
        grid=(batch, h, nq),
        in_specs=[pl.BlockSpec(memory_space=pltpu.SMEM),
                  pl.BlockSpec((t, e), lambda b, hh, qi: (b * nq + qi, COL_DQ // e + hh)),
                  pl.BlockSpec((seq, e), lambda b, hh, qi: (b, COL_DK // e + hh)),
                  pl.BlockSpec((seq, e), lambda b, hh, qi: (b, COL_DV // e + hh)),
                  pl.BlockSpec((2, 1, LANES, LANES), lambda b, hh, qi: (0, hh, 0, 0)),
                  pl.BlockSpec((1, e), lambda b, hh, qi: (0, 0))],
        out_specs=pl.BlockSpec((t, e), lambda b, hh, qi: (b * nq + qi, hh)),
        scratch_shapes=[pltpu.VMEM((2 * t, e), BF16),
                        pltpu.VMEM((2 * t, t), F32),
                        pltpu.VMEM((2 * t, LANES), F32),
                        pltpu.VMEM((2 * t, LANES), F32),
                        pltpu.VMEM((2 * t, e), F32)],
        compiler_params=_cparams(("parallel", "parallel", "arbitrary")),
        name="diff_attn",
    )(lam, proj, proj, proj, bias_blocks, subln_g)


def _sortable_key(x):
    bits = lax.bitcast_convert_type(x, I32)
    return bits ^ ((bits >> 31) & jnp.int32(0x7FFFFFFF))


def _dsa_kernel(iq_ref, sq_ref, iw_ref, ik_ref, ckv_ref, kvg_ref, wuk_ref, wuv_ref, bias_ref, o_ref,
                c_ref, key_ref, am_ref, qi_ref, wb_ref, ql_ref, s_ref, m_ref, l_ref, acc_ref, *, topk, scale):
    t = iq_ref.shape[0]
    tk = KEY_CHUNK
    qi = pl.program_id(1)
    n_chunks = qi + 1
    hi, hb = N_HEADS_IDX, N_HEADS_DSA

    @pl.when(qi == 0)
    def _():
        ckv = ckv_ref[...].astype(F32)
        ms = jnp.mean(ckv * ckv, axis=-1, keepdims=True)
        c_ref[...] = (ckv * lax.rsqrt(ms + EPS) * kvg_ref[...]).astype(BF16)

    lane = lax.broadcasted_iota(I32, (t, LANES), 1)
    for h in range(hi):
        blk = iq_ref[:, (h // 2) * LANES:(h // 2 + 1) * LANES]
        keep = (lane < HEAD_DIM_IDX) if h % 2 == 0 else (lane >= HEAD_DIM_IDX)
        qi_ref[h * t:(h + 1) * t, :] = jnp.where(keep, blk, jnp.zeros_like(blk))
        wb_ref[h] = jnp.broadcast_to(iw_ref[:, h:h + 1].astype(F32), (t, LANES))

    def score_chunk(kc, diag):
        off = pl.multiple_of(kc * t, t)
        d = _dot_nt(qi_ref[...], ik_ref[pl.ds(off, t), :]).reshape(hi, t, t)
        sc = jnp.zeros((t, t), F32)
        for h in range(hi):
            w = wb_ref[h]
            w = jnp.concatenate([w] * (t // LANES), axis=1)
            sc = sc + jnp.maximum(d[h], 0.0) * w
        key = _sortable_key(sc + 0.0)
        if diag:
            row = lax.broadcasted_iota(I32, (t, t), 0)
            col = lax.broadcasted_iota(I32, (t, t), 1)
            key = jnp.where(col <= row, key, jnp.int32(INT_MIN))
        key_ref[:, pl.ds(off, t)] = key

    def score_body(kc, carry):
        score_chunk(kc, False)
        return carry

    lax.fori_loop(0, qi, score_body, 0)
    score_chunk(qi, True)

    @pl.when(qi % 2 == 0)
    def _():
        key_ref[:, pl.ds(pl.multiple_of((qi + 1) * t, t), t)] = jnp.full((t, t), INT_MIN, I32)

    n_steps = (qi + 2) // 2

    ones_l = jnp.ones((LANES, LANES), BF16)

    def bit_body(i, cur_b):
        bit = lax.shift_left(jnp.int32(1), 31 - i)
        cand_b = cur_b | bit
        cand_s = cand_b ^ jnp.int32(INT_MIN)
        parts = []
        for rg in range(t // LANES):
            rows = slice(rg * LANES, (rg + 1) * LANES)
            c_rg = cand_s[rows]

            def body(kc, cnt, rows=rows, c_rg=c_rg):
                for j in range(tk // LANES):
                    off = pl.multiple_of(kc * tk + j * LANES, LANES)
                    cnt = cnt + jnp.where(key_ref[rows, pl.ds(off, LANES)] >= c_rg, 1, 0)
                return cnt

            parts.append(lax.fori_loop(0, n_steps, body, jnp.zeros((LANES, LANES), I32)))
        cnt = jnp.concatenate(parts, axis=0).astype(F32).astype(BF16)
        total = jnp.dot(cnt, ones_l, preferred_element_type=F32)
        return jnp.where(total >= float(topk), cand_b, cur_b)

    cur_b = lax.fori_loop(0, 32, bit_body, jnp.zeros((t, LANES), I32))
    thr_b = jnp.maximum(cur_b ^ jnp.int32(INT_MIN), jnp.int32(INT_MIN + 1))
    thr_w = jnp.concatenate([thr_b] * (tk // LANES), axis=1)

    def mask_body(kc, carry):
        off = pl.multiple_of(kc * tk, tk)
        am_ref[:, pl.ds(off, tk)] = jnp.where(key_ref[:, pl.ds(off, tk)] >= thr_w, 0.0, MASK_VALUE).astype(F32)
        return carry

    lax.fori_loop(0, n_steps, mask_body, 0)

    for h in range(hb):
        qh = sq_ref[:, h * HEAD_DIM_DSA:(h + 1) * HEAD_DIM_DSA]
        ql = jnp.dot(qh, wuk_ref[h], preferred_element_type=F32) * scale
        ql_ref[h * t:(h + 1) * t, :] = ql.astype(BF16)
    _softmax_init(m_ref, l_ref, acc_ref)
    d0 = lambda g: bias_ref[0, g]
    d1 = lambda g: bias_ref[1, g]

    def chunk(kc, width, delta):
        off = pl.multiple_of(kc * tk, tk)
        c = c_ref[pl.ds(off, width), :]
        s = _dot_nt(ql_ref[...], c).reshape(hb, t, width) + am_ref[:, pl.ds(off, width)][None]
        s_ref[:, 0:width] = s.reshape(hb * t, width)
        if delta is not None:
            _near_bias(s_ref, d0, d1, delta, hb, t, width)
        _softmax_step(s_ref, width, c, m_ref, l_ref, acc_ref)

    def far_body(kc, carry):
        chunk(kc, tk, None)
        return carry

    lax.fori_loop(0, jnp.maximum((qi - 1) // 2, 0), far_body, 0)
    half = qi // 2

    @pl.when(qi % 2 == 1)
    def _():
        chunk(half, tk, t // LANES)

    @pl.when(qi % 2 == 0)
    def _():
        @pl.when(half >= 1)
        def _():
            chunk(half - 1, tk, tk // LANES)
        chunk(half, t, 0)

    ol = _softmax_result(l_ref, acc_ref).astype(BF16)
    for h in range(hb):
        o = jnp.dot(ol[h * t:(h + 1) * t, :], wuv_ref[h], preferred_element_type=F32)
        o_ref[:, h * HEAD_DIM_DSA:(h + 1) * HEAD_DIM_DSA] = o.astype(o_ref.dtype)


def _dsa_attention(proj, kv_g, w_ukt, w_uv, bias_blocks, batch, seq, topk):
    t = DSA_BLOCK
    nq = seq // t
    hb, hi = N_HEADS_DSA, N_HEADS_IDX
    width = hb * HEAD_DIM_DSA
    kernel = functools.partial(_dsa_kernel, topk=topk, scale=HEAD_DIM_DSA ** -0.5)
    return pl.pallas_call(
        kernel,
        out_shape=jax.ShapeDtypeStruct((batch * seq, width), BF16),
        grid=(batch, nq),
        in_specs=[pl.BlockSpec((t, 1024), lambda b, qi: (b * nq + qi, COL_IQ // 1024)),
                  pl.BlockSpec((t, 1024), lambda b, qi: (b * nq + qi, COL_SQ // 1024)),
                  pl.BlockSpec((t, LANES), lambda b, qi: (b * nq + qi, COL_IW // LANES)),
                  pl.BlockSpec((seq, LANES), lambda b, qi: (b, COL_IK // LANES)),
                  pl.BlockSpec((seq, KV_LATENT), lambda b, qi: (b, COL_CKV // KV_LATENT)),
                  pl.BlockSpec((1, KV_LATENT), lambda b, qi: (0, 0)),
                  pl.BlockSpec((hb, HEAD_DIM_DSA, KV_LATENT), lambda b, qi: (0, 0, 0)),
                  pl.BlockSpec((hb, KV_LATENT, HEAD_DIM_DSA), lambda b, qi: (0, 0, 0)),
                  pl.BlockSpec((2, hb, LANES, LANES), lambda b, qi: (0, 0, 0, 0))],
        out_specs=pl.BlockSpec((t, width), lambda b, qi: (b * nq + qi, 0)),
        scratch_shapes=[pltpu.VMEM((seq, KV_LATENT), BF16),
                        pltpu.VMEM((t, seq), I32),
                        pltpu.VMEM((t, seq), F32),
                        pltpu.VMEM((hi * t, LANES), BF16),
                        pltpu.VMEM((hi, t, LANES), F32),
                        pltpu.VMEM((hb * t, KV_LATENT), BF16),
                        pltpu.VMEM((hb * t, KEY_CHUNK), F32),
                        pltpu.VMEM((hb * t, LANES), F32),
                        pltpu.VMEM((hb * t, LANES), F32),
                        pltpu.VMEM((hb * t, KV_LATENT), F32)],
        compiler_params=_cparams(("parallel", "arbitrary")),
        name="dsa_attn",
    )(proj, proj, proj, proj, proj, kv_g, w_ukt, w_uv, bias_blocks)


def _pack_halves(x):
    c = x.shape[1] // 2
    lo = lax.bitcast_convert_type(x[:, :c].astype(BF16).astype(F32), I32)
    hi = lax.bitcast_convert_type(x[:, c:].astype(BF16).astype(F32), I32)
    return lax.shift_right_logical(lo, 16) | (hi & jnp.int32(-65536))


def _unpack_halves(w):
    lo = lax.bitcast_convert_type(lax.shift_left(w, 16), F32)
    hi = lax.bitcast_convert_type(w & jnp.int32(-65536), F32)
    return lo, hi


def _merge_kernel(x_ref, yd_ref, ys_ref, ga_ref, gb_ref, wd_ref, ws_ref, wo_ref, g_ref, wr_ref, br_ref,
                  h_ref, hn_ref, route_ref):
    bd = jnp.dot(yd_ref[...], wd_ref[...], preferred_element_type=F32)
    bs = jnp.dot(ys_ref[...], ws_ref[...], preferred_element_type=F32)
    merged = (jax.nn.sigmoid(ga_ref[...].astype(F32)) * bd + jax.nn.sigmoid(gb_ref[...].astype(F32)) * bs)
    h = x_ref[...] + jnp.dot(merged.astype(BF16), wo_ref[...], preferred_element_type=F32)
    h_ref[...] = h
    ms = jnp.mean(h * h, axis=-1, keepdims=True)
    hn = h * lax.rsqrt(ms + EPS) * g_ref[...]
    hn_ref[...] = _pack_halves(hn)

    logits = jnp.dot(hn, wr_ref[...], preferred_element_type=F32, precision=lax.Precision.HIGHEST)
    logits = logits + br_ref[...]
    lane = lax.broadcasted_iota(I32, logits.shape, 1)
    vals, ids = [], []
    for _ in range(TOP_K_EXPERTS):
        mx = jnp.max(logits, axis=-1, keepdims=True)
        ix = jnp.min(jnp.where(logits == mx, lane, LANES), axis=-1, keepdims=True)
        vals.append(mx)
        ids.append(ix)
        logits = jnp.where(lane == ix, -jnp.inf, logits)
    es = [jnp.exp(v - vals[0]) for v in vals]
    inv = 1.0 / (es[0] + es[1] + es[2] + es[3])
    route = jnp.zeros(logits.shape, F32)
    for k in range(TOP_K_EXPERTS):
        route = jnp.where(lane == k, es[k] * inv, route)
        route = jnp.where(lane == TOP_K_EXPERTS + k, ids[k].astype(F32), route)
    route_ref[...] = route


def _merge(x2, y_diff, y_dsa, proj, w_bd, w_bs, w_out, g_ffn, w_router, b_router, tm):
    n, d = x2.shape
    row = lambda i: (i, 0)
    const = lambda i: (0, 0)
    return pl.pallas_call(
        _merge_kernel,
        out_shape=(jax.ShapeDtypeStruct((n, d), F32),
                   jax.ShapeDtypeStruct((n, d // 2), I32),
                   jax.ShapeDtypeStruct((n, LANES), F32)),
        grid=(n // tm,),
        in_specs=[pl.BlockSpec((tm, d), row),
                  pl.BlockSpec((tm, d), row),
                  pl.BlockSpec((tm, d), row),
                  pl.BlockSpec((tm, d), lambda i: (i, COL_GA // 1024)),
                  pl.BlockSpec((tm, d), lambda i: (i, COL_GB // 1024)),
                  pl.BlockSpec((d, d), const),
                  pl.BlockSpec((d, d), const),
                  pl.BlockSpec((d, d), const),
                  pl.BlockSpec((1, d), const),
                  pl.BlockSpec((d, LANES), const),
                  pl.BlockSpec((1, LANES), const)],
        out_specs=(pl.BlockSpec((tm, d), row),
                   pl.BlockSpec((tm, d // 2), row),
                   pl.BlockSpec((tm, LANES), row)),
        compiler_params=_cparams(("parallel",)),
        name="merge_router",
    )(x2, y_diff, y_dsa, proj, proj, w_bd, w_bs, w_out, g_ffn, w_router, b_router)


def _regroup_kernel(w_ref, p_ref, o_ref):
    pw = p_ref.shape[0]
    for j in range(w_ref.shape[2] // pw):
        w = w_ref[0, :, j * pw:(j + 1) * pw].astype(BF16)
        o_ref[0, :, j * pw:(j + 1) * pw] = jnp.dot(w, p_ref[...], preferred_element_type=F32).astype(BF16)


def _regroup_gate_up(w_gu, rows):
    e, d, f2 = w_gu.shape
    pw = 2 * LANES
    src = jnp.arange(pw, dtype=I32)
    dst = (src % 2) * LANES + src // 2
    perm = (dst[:, None] == jnp.arange(pw, dtype=I32)[None, :]).astype(BF16)
    return pl.pallas_call(
        _regroup_kernel,
        out_shape=jax.ShapeDtypeStruct((e, d, f2), BF16),
        grid=(e, d // rows),
        in_specs=[pl.BlockSpec((1, rows, f2), lambda i, j: (i, j, 0)),
                  pl.BlockSpec((pw, pw), lambda i, j: (0, 0))],
        out_specs=pl.BlockSpec((1, rows, f2), lambda i, j: (i, j, 0)),
        compiler_params=_cparams(("parallel", "parallel")),
        name="regroup_gate_up",
    )(w_gu, perm)


def _ffn_kernel(be_ref, nu_ref, x_ref, wgu_ref, wd_ref, bgu_ref, bd_ref, o_ref):
    @pl.when(pl.program_id(0) < nu_ref[0])
    def _():
        x_lo, x_hi = _unpack_halves(x_ref[...])
        half = x_lo.shape[1]
        gu = (jnp.dot(x_lo.astype(BF16), wgu_ref[0, 0:half, :], preferred_element_type=F32)
              + jnp.dot(x_hi.astype(BF16), wgu_ref[0, half:2 * half, :], preferred_element_type=F32)
              + bgu_ref[0])
        acts = []
        for j in range(gu.shape[1] // (2 * LANES)):
            gate = jnp.minimum(gu[:, 2 * j * LANES:(2 * j + 1) * LANES], SWIGLU_LIMIT)
            up = jnp.clip(gu[:, (2 * j + 1) * LANES:(2 * j + 2) * LANES], -SWIGLU_LIMIT, SWIGLU_LIMIT)
            glu = gate * jax.nn.sigmoid(gate * SWIGLU_ALPHA)
            acts.append(((up + 1.0) * glu).astype(BF16))
        a = jnp.concatenate(acts, axis=1)
        y = jnp.dot(a, wd_ref[0], preferred_element_type=F32) + bd_ref[0]
        o_ref[...] = _pack_halves(y)

    @pl.when(pl.program_id(0) >= nu_ref[0])
    def _():
        o_ref[...] = jnp.zeros(o_ref.shape, o_ref.dtype)


def _expert_ffn(blk_exp, n_used, xs, wgu, wd, bgu, bd):
    p, dw = xs.shape
    f, d = wd.shape[1], wd.shape[2]
    nblk = p // MOE_ROWS
    wmap = lambda i, be, nu: (be[i], 0, 0)
    grid_spec = pltpu.PrefetchScalarGridSpec(
        num_scalar_prefetch=2,
        grid=(nblk,),
        in_specs=[pl.BlockSpec((MOE_ROWS, dw), lambda i, be, nu: (i, 0)),
                  pl.BlockSpec((1, d, 2 * f), wmap),
                  pl.BlockSpec((1, f, d), wmap),
                  pl.BlockSpec((1, 1, 2 * f), wmap),
                  pl.BlockSpec((1, 1, d), wmap)],
        out_specs=pl.BlockSpec((MOE_ROWS, dw), lambda i, be, nu: (i, 0)),
    )
    return pl.pallas_call(
        _ffn_kernel,
        out_shape=jax.ShapeDtypeStruct((p, dw), I32),
        grid_spec=grid_spec,
        compiler_params=_cparams(("arbitrary",)),
        name="expert_ffn",
    )(blk_exp, n_used, xs, wgu, wd, bgu, bd)


def _route_kernel(route_ref, dest_ref, cnt_ref, u_ref, carry_ref, pstart_ref, *, block_rows):
    ph, i = pl.program_id(0), pl.program_id(1)
    tm = route_ref.shape[0]

    @pl.when((ph == 0) & (i == 0))
    def _():
        r = lax.broadcasted_iota(I32, (tm, tm), 0)
        c = lax.broadcasted_iota(I32, (tm, tm), 1)
        u_ref[...] = jnp.where(r < c, 1.0, 0.0).astype(BF16)
        carry_ref[...] = jnp.zeros(carry_ref.shape, F32)

    @pl.when((ph == 1) & (i == 0))
    def _():
        counts = carry_ref[...]
        cnt_ref[...] = counts
        padded = jnp.ceil(counts * (1.0 / block_rows)) * block_rows
        r = lax.broadcasted_iota(I32, (LANES, LANES), 0)
        c = lax.broadcasted_iota(I32, (LANES, LANES), 1)
        lower = jnp.where(c < r, 1.0, 0.0).astype(F32)
        pstart_ref[...] = jnp.dot(lower, padded, preferred_element_type=F32, precision=lax.Precision.HIGHEST)
        carry_ref[...] = jnp.zeros(carry_ref.shape, F32)

    rt = route_ref[...].T
    sub = lax.broadcasted_iota(I32, (LANES, tm), 0)
    hits = [sub == rt[TOP_K_EXPERTS + k:TOP_K_EXPERTS + k + 1, :].astype(I32) for k in range(TOP_K_EXPERTS)]
    m = jnp.zeros((LANES, tm), F32)
    for hit in hits:
        m = m + jnp.where(hit, 1.0, 0.0)
    tile_counts = jnp.broadcast_to(jnp.sum(m, axis=1, keepdims=True), (LANES, LANES))

    @pl.when(ph == 0)
    def _():
        dest_ref[...] = jnp.zeros(dest_ref.shape, I32)

    @pl.when(ph == 1)
    def _():
        prefix = jnp.dot(m.astype(BF16), u_ref[...], preferred_element_type=F32)
        rank = prefix + (pstart_ref[:, 0:1] + carry_ref[:, 0:1])
        rows = [jnp.sum(jnp.where(hit, rank, 0.0), axis=0, keepdims=True) for hit in hits]
        rows.append(jnp.zeros((dest_ref.shape[0] - TOP_K_EXPERTS, tm), F32))
        dest_ref[...] = jnp.concatenate(rows, axis=0).astype(I32)

    carry_ref[...] = carry_ref[...] + tile_counts


def _route_rows(route, tm):
    n = route.shape[0]
    nt = n // tm
    kernel = functools.partial(_route_kernel, block_rows=MOE_ROWS)
    return pl.pallas_call(
        kernel,
        out_shape=(jax.ShapeDtypeStruct((8, n + tm), I32), jax.ShapeDtypeStruct((LANES, LANES), F32)),
        grid=(2, nt),
        in_specs=[pl.BlockSpec((tm, LANES), lambda ph, i: (i, 0))],
        out_specs=(pl.BlockSpec((8, tm), lambda ph, i: (0, ph * i + (1 - ph) * nt)),
                   pl.BlockSpec((LANES, LANES), lambda ph, i: (0, 0))),
        scratch_shapes=[pltpu.VMEM((tm, tm), BF16),
                        pltpu.VMEM((LANES, LANES), F32),
                        pltpu.VMEM((LANES, LANES), F32)],
        compiler_params=_cparams(("arbitrary", "arbitrary")),
        name="route_rows",
    )(route)


SC_WINDOW = 128
SC_WORKERS = 32


def _sc_mesh():
    return plsc.VectorSubcoreMesh(core_axis_name="c", subcore_axis_name="s")


def _sc_scatter_rows(src, dest, p):
    n, d = src.shape
    per = n // (SC_WINDOW * SC_WORKERS)

    @pl.kernel(out_type=jax.ShapeDtypeStruct((p, d), src.dtype), mesh=_sc_mesh(),
               scratch_types=[pltpu.VMEM((dest.shape[0], SC_WINDOW), I32), pltpu.VMEM((SC_WINDOW, d), src.dtype)])
    def scatter(src_hbm, idx_hbm, out_hbm, idx_vmem, buf):
        wid = lax.axis_index("c") * (SC_WORKERS // 2) + lax.axis_index("s")

        @pl.loop(0, per)
        def _(j):
            off = (wid * per + j) * SC_WINDOW
            pltpu.sync_copy(idx_hbm.at[:, pl.ds(off, SC_WINDOW)], idx_vmem)
            pltpu.sync_copy(src_hbm.at[pl.ds(off, SC_WINDOW), :], buf)
            for k in range(TOP_K_EXPERTS):
                pltpu.sync_copy(buf, out_hbm.at[idx_vmem.at[k]])

    return scatter(src, dest)


def _sc_gather_rows(src, dest):
    n = dest.shape[1]
    d = src.shape[1]
    per = n // (SC_WINDOW * SC_WORKERS)

    @pl.kernel(out_type=jax.ShapeDtypeStruct((TOP_K_EXPERTS * n, d), src.dtype), mesh=_sc_mesh(),
               scratch_types=[pltpu.VMEM((dest.shape[0], SC_WINDOW), I32), pltpu.VMEM((SC_WINDOW, d), src.dtype)])
    def gather(src_hbm, idx_hbm, out_hbm, idx_vmem, buf):
        wid = lax.axis_index("c") * (SC_WORKERS // 2) + lax.axis_index("s")

        @pl.loop(0, per)
        def _(j):
            off = (wid * per + j) * SC_WINDOW
            pltpu.sync_copy(idx_hbm.at[:, pl.ds(off, SC_WINDOW)], idx_vmem)
            for k in range(TOP_K_EXPERTS):
                pltpu.sync_copy(src_hbm.at[idx_vmem.at[k]], buf)
                pltpu.sync_copy(buf, out_hbm.at[pl.ds(k * n + off, SC_WINDOW), :])

    return gather(src, dest)


def _combine_kernel(h_ref, y_ref, route_ref, g_ref, o_ref):
    half = h_ref.shape[1] // 2
    h_lo, h_hi = h_ref[:, 0:half], h_ref[:, half:2 * half]
    route = route_ref[...]
    for k in range(TOP_K_EXPERTS):
        y_lo, y_hi = _unpack_halves(y_ref[k])
        gate = route[:, k:k + 1]
        h_lo = h_lo + gate * y_lo
        h_hi = h_hi + gate * y_hi
    ms = (jnp.sum(h_lo * h_lo, axis=-1, keepdims=True)
          + jnp.sum(h_hi * h_hi, axis=-1, keepdims=True)) * (1.0 / (2 * half))
    inv = lax.rsqrt(ms + EPS)
    o_ref[:, 0:half] = h_lo * inv * g_ref[:, 0:half]
    o_ref[:, half:2 * half] = h_hi * inv * g_ref[:, half:2 * half]


def _combine(h1, yg, route, g_final, tm):
    n, d = h1.shape
    return pl.pallas_call(
        _combine_kernel,
        out_shape=jax.ShapeDtypeStruct((n, d), F32),
        grid=(n // tm,),
        in_specs=[pl.BlockSpec((tm, d), lambda i: (i, 0)),
                  pl.BlockSpec((TOP_K_EXPERTS, tm, d // 2), lambda i: (0, i, 0)),
                  pl.BlockSpec((tm, LANES), lambda i: (i, 0)),
                  pl.BlockSpec((1, d), lambda i: (0, 0))],
        out_specs=pl.BlockSpec((tm, d), lambda i: (i, 0)),
        compiler_params=_cparams(("parallel",)),
        name="combine_norm",
    )(h1, yg, route, g_final)


def _t5_bucket(dist):
    n = jnp.maximum(dist, 0)
    max_exact = N_BUCKETS // 2
    nf = jnp.maximum(n, 1).astype(F32)
    large = max_exact + (jnp.log(nf / max_exact) / math.log(MAX_DISTANCE / max_exact)
                         * (N_BUCKETS - max_exact)).astype(I32)
    large = jnp.minimum(large, N_BUCKETS - 1)
    return jnp.where(n < max_exact, n, large)


def _bias_blocks(bias_tab):
    t = LANES
    assert MAX_DISTANCE <= LANES
    r = jnp.arange(t, dtype=I32)[:, None]
    c = jnp.arange(t, dtype=I32)[None, :]
    rel = (bias_tab - bias_tab[N_BUCKETS - 1][None, :]).astype(F32)
    tiles = []
    for delta in (0, t):
        dist = r - c + delta
        b = rel[_t5_bucket(dist)]
        b = jnp.where((dist >= 0)[:, :, None], b, MASK_VALUE)
        tiles.append(b.transpose(2, 0, 1))
    return jnp.stack(tiles)


def _regroup_w_in(w_in):
    sizes = (1024, 1024, 1024, 1024, KV_LATENT, 1024, HEAD_DIM_IDX, N_HEADS_IDX, D_MODEL, D_MODEL)
    parts, off = [], 0
    for sz in sizes:
        parts.append(w_in[:, off:off + sz])
        off += sz
    dq, dk, dv, sq, ckv, iq, ik, iw, ga, gb = parts
    dq = dq * (HEAD_DIM_DIFF ** -0.5)
    iw = iw * ((N_HEADS_IDX ** -0.5) * (HEAD_DIM_IDX ** -0.5))
    pad = jnp.zeros((w_in.shape[0], PROJ_WIDTH - COL_IW - N_HEADS_IDX), w_in.dtype)
    w = jnp.concatenate([dq, dk, dv, sq, iq, ga, gb, ckv, ik, ik, iw, pad], axis=1)
    return w.astype(BF16)


def _block_tables(counts, n_assign):
    e, bm = N_EXPERTS, MOE_ROWS
    padded = (counts + bm - 1) // bm * bm
    pends = jnp.cumsum(padded)
    nblk = -(-(n_assign + e * (bm - 1)) // bm)
    blk_exp = jnp.minimum(jnp.searchsorted(pends, jnp.arange(nblk, dtype=I32) * bm, side='right'),
                          e - 1).astype(I32)
    n_used = (pends[-1] // bm).astype(I32).reshape(1)
    return blk_exp, n_used, nblk


def kernel(x, norm_attn_g, w_in, rel_bias, lam_q1, lam_k1, lam_q2, lam_k2, diff_subln_g, kv_norm_g, w_uk, w_uv,
           w_branch_diff, w_branch_dsa, w_out, norm_ffn_g, w_router, b_router, w_gate_up, b_gate_up, w_down,
           b_down, norm_final_g):
    batch, seq, d = x.shape
    n = batch * seq
    assert norm_attn_g.shape[0] == 1, "single-layer kernel"
    assert seq % DIFF_BLOCK == 0 and seq % KEY_CHUNK == 0 and d == D_MODEL
    row_tile = math.gcd(n, 1024)

    x2 = x.reshape(n, d)
    proj = _inproj(x2, norm_attn_g[0].reshape(1, d), _regroup_w_in(w_in[0]), row_tile, 1280)

    lam_init = 0.8 - 0.6 * math.exp(-0.3 * 0)
    lam = (jnp.exp(jnp.sum(lam_q1[0].astype(F32) * lam_k1[0].astype(F32)))
           - jnp.exp(jnp.sum(lam_q2[0].astype(F32) * lam_k2[0].astype(F32))) + lam_init)
    y_diff = _diff_attention(proj, lam.reshape(1, 1).astype(F32), _bias_blocks(rel_bias[:, :N_HEADS_DIFF]),
                             diff_subln_g[0].reshape(1, -1).astype(F32), batch, seq, 1.0 - lam_init)

    y_dsa = _dsa_attention(proj, kv_norm_g[0].reshape(1, -1).astype(F32),
                           w_uk[0].transpose(0, 2, 1).astype(BF16), w_uv[0].astype(BF16),
                           _bias_blocks(rel_bias[:, N_HEADS_DIFF:]), batch, seq, min(TOPK_MAX, seq // 4))

    w_r = jnp.zeros((d, LANES), F32).at[:, :N_EXPERTS].set(w_router[0].astype(F32))
    b_r = jnp.full((1, LANES), MASK_VALUE, F32).at[0, :N_EXPERTS].set(b_router[0].astype(F32))
    h1, hn, route = _merge(x2, y_diff, y_dsa, proj, w_branch_diff[0].astype(BF16), w_branch_dsa[0].astype(BF16),
                           w_out[0].astype(BF16), norm_ffn_g[0].reshape(1, d).astype(F32), w_r, b_r,
                           math.gcd(n, 512))

    dest, counts = _route_rows(route, math.gcd(n, 1024))
    dest = dest[:, :n]
    blk_exp, n_used, nblk = _block_tables(counts[:N_EXPERTS, 0].astype(I32), n * TOP_K_EXPERTS)
    xs = _sc_scatter_rows(hn, dest, nblk * MOE_ROWS)
    e, f = N_EXPERTS, D_EXPERT
    b_gu = b_gate_up[0].astype(F32).reshape(e, f // LANES, LANES, 2).transpose(0, 1, 3, 2).reshape(e, 1, 2 * f)
    ys = _expert_ffn(blk_exp, n_used, xs, _regroup_gate_up(w_gate_up[0], 512), w_down[0].astype(BF16),
                     b_gu, b_down[0][:, None, :].astype(F32))
    yg = _sc_gather_rows(ys, dest).reshape(TOP_K_EXPERTS, n, d // 2)
    out = _combine(h1, yg, route, norm_final_g.reshape(1, d).astype(F32), math.gcd(n, 512))
    return out.reshape(batch, seq, d)
```

```python
import functools
import math

import jax
import jax.numpy as jnp
from jax import lax
from jax.experimental import pallas as pl
from jax.experimental.pallas import tpu as pltpu
from jax.experimental.pallas import tpu_sc as plsc

F32 = jnp.float32
BF16 = jnp.bfloat16
I32 = jnp.int32

D_MODEL = 1024
N_HEADS_DIFF = 8
HEAD_DIM_DIFF = 64
N_HEADS_DSA = 8
HEAD_DIM_DSA = 128
KV_LATENT = 256
N_HEADS_IDX = 16
HEAD_DIM_IDX = 64
TOPK_MAX = 256
N_BUCKETS = 32
MAX_DISTANCE = 128
N_EXPERTS = 32
TOP_K_EXPERTS = 4
D_EXPERT = 1024
SWIGLU_LIMIT = 7.0
SWIGLU_ALPHA = 1.702
EPS = 1e-6

LANES = 128
DIFF_BLOCK = 512
DSA_BLOCK = 256
KEY_CHUNK = 512
MOE_ROWS = 512
PROJ_WIDTH = 7680
VMEM_LIMIT = 56 * 1024 * 1024

COL_DQ, COL_DK, COL_DV, COL_SQ, COL_IQ, COL_GA, COL_GB = (i * 1024 for i in range(7))
COL_CKV = 7168
COL_IK = 7424
COL_IW = 7552

MASK_VALUE = -1e30
M_INIT = -1e29
INT_MIN = -2 ** 31


def _cparams(sem):
    return pltpu.CompilerParams(dimension_semantics=sem, vmem_limit_bytes=VMEM_LIMIT)


def _inproj_kernel(x_ref, g_ref, w_ref, o_ref, xn_ref):
    @pl.when(pl.program_id(1) == 0)
    def _():
        x = x_ref[...]
        ms = jnp.mean(x * x, axis=-1, keepdims=True)
        xn_ref[...] = (x * lax.rsqrt(ms + EPS) * g_ref[...]).astype(BF16)

    o_ref[...] = jnp.dot(xn_ref[...], w_ref[...], preferred_element_type=F32).astype(o_ref.dtype)


def _inproj(x2, g, w, tm, tn):
    n, d = x2.shape
    width = w.shape[1]
    return pl.pallas_call(
        _inproj_kernel,
        out_shape=jax.ShapeDtypeStruct((n, width), BF16),
        grid=(n // tm, width // tn),
        in_specs=[pl.BlockSpec((tm, d), lambda i, j: (i, 0)),
                  pl.BlockSpec((1, d), lambda i, j: (0, 0)),
                  pl.BlockSpec((d, tn), lambda i, j: (0, j))],
        out_specs=pl.BlockSpec((tm, tn), lambda i, j: (i, j)),
        scratch_shapes=[pltpu.VMEM((tm, d), BF16)],
        compiler_params=_cparams(("parallel", "arbitrary")),
        name="inproj",
    )(x2, g, w)


def _softmax_step(s_ref, tk, v, m_ref, l_ref, acc_ref):
    nl = tk // LANES
    smax = s_ref[:, 0:LANES]
    for j in range(1, nl):
        smax = jnp.maximum(smax, s_ref[:, j * LANES:(j + 1) * LANES])
    m_prev = m_ref[...]
    m_new = jnp.maximum(m_prev, jnp.max(smax, axis=-1, keepdims=True))
    alpha = jnp.exp(m_prev - m_new)
    psum = None
    ps = []
    for j in range(nl):
        pj = jnp.exp(s_ref[:, j * LANES:(j + 1) * LANES] - m_new)
        psum = pj if psum is None else psum + pj
        ps.append(pj.astype(BF16))
    l_ref[...] = alpha * l_ref[...] + psum
    pv = jnp.dot(jnp.concatenate(ps, axis=1), v, preferred_element_type=F32)
    e = acc_ref.shape[1]
    a = alpha if e == LANES else jnp.concatenate([alpha] * (e // LANES), axis=1)
    acc_ref[...] = a * acc_ref[...] + pv
    m_ref[...] = m_new


def _softmax_init(m_ref, l_ref, acc_ref):
    m_ref[...] = jnp.full(m_ref.shape, M_INIT, F32)
    l_ref[...] = jnp.zeros(l_ref.shape, F32)
    acc_ref[...] = jnp.zeros(acc_ref.shape, F32)


def _softmax_result(l_ref, acc_ref):
    return acc_ref[...] * (1.0 / jnp.sum(l_ref[...], axis=-1, keepdims=True))


def _near_bias(s_ref, d0, d1, delta, groups, t, tk):
    for g in range(groups):
        for rb in range(t // LANES):
            for cb in range(tk // LANES):
                bd = delta + rb - cb
                rows = slice(g * t + rb * LANES, g * t + (rb + 1) * LANES)
                cols = slice(cb * LANES, (cb + 1) * LANES)
                if bd == 0:
                    s_ref[rows, cols] = s_ref[rows, cols] + d0(g)
                elif bd == 1:
                    s_ref[rows, cols] = s_ref[rows, cols] + d1(g)
                elif bd < 0:
                    s_ref[rows, cols] = jnp.full((LANES, LANES), MASK_VALUE, F32)


def _dot_nt(a, b):
    return lax.dot_general(a, b, (((1,), (1,)), ((), ())), preferred_element_type=F32)


def _diff_kernel(lam_ref, q_ref, k_ref, v_ref, bias_ref, g_ref, o_ref, q2_ref, s_ref, m_ref, l_ref, acc_ref,
                 *, out_scale):
    t = q_ref.shape[0]
    qi = pl.program_id(2)

    q = q_ref[...]
    lane = lax.broadcasted_iota(I32, q.shape, 1)
    zero = jnp.zeros_like(q)
    q2_ref[0:t, :] = jnp.where(lane < HEAD_DIM_DIFF, q, zero)
    q2_ref[t:2 * t, :] = jnp.where(lane >= HEAD_DIM_DIFF, q, zero)
    _softmax_init(m_ref, l_ref, acc_ref)
    d0 = lambda g: bias_ref[0, 0]
    d1 = lambda g: bias_ref[1, 0]

    def chunk(kc, delta):
        off = pl.multiple_of(kc * t, t)
        s_ref[...] = _dot_nt(q2_ref[...], k_ref[pl.ds(off, t), :])
        if delta is not None:
            _near_bias(s_ref, d0, d1, delta, 2, t, t)
        _softmax_step(s_ref, t, v_ref[pl.ds(off, t), :], m_ref, l_ref, acc_ref)

    def far_body(kc, carry):
        chunk(kc, None)
        return carry

    lax.fori_loop(0, jnp.maximum(qi - 1, 0), far_body, 0)

    @pl.when(qi >= 1)
    def _():
        chunk(qi - 1, t // LANES)

    chunk(qi, 0)

    o = _softmax_result(l_ref, acc_ref)
    o = o[0:t, :] - lam_ref[0, 0] * o[t:2 * t, :]
    ms = jnp.mean(o * o, axis=-1, keepdims=True)
    o_ref[...] = (o * lax.rsqrt(ms + EPS) * g_ref[...] * out_scale).astype(o_ref.dtype)


def _diff_attention(proj, lam, bias_blocks, subln_g, batch, seq, out_scale):
    t = DIFF_BLOCK
    nq = seq // t
    h = N_HEADS_DIFF
    e = 2 * HEAD_DIM_DIFF
    kernel = functools.partial(_diff_kernel, out_scale=out_scale)
    return pl.pallas_call(
        kernel,
        out_shape=jax.ShapeDtypeStruct((batch * seq, h * e), BF16),
        grid=(batch, h, nq),
        in_specs=[pl.BlockSpec(memory_space=pltpu.SMEM),
                  pl.BlockSpec((t, e), lambda b, hh, qi: (b * nq + qi, COL_DQ // e + hh)),
                  pl.BlockSpec((seq, e), lambda b, hh, qi: (b, COL_DK // e + hh)),
                  pl.BlockSpec((seq, e), lambda b, hh, qi: (b, COL_DV // e + hh)),
                  pl.BlockSpec((2, 1, LANES, LANES), lambda b, hh, qi: (0, hh, 0, 0)),
                  pl.BlockSpec((1, e), lambda b, hh, qi: (0, 0))],
        out_specs=pl.BlockSpec((t, e), lambda b, hh, qi: (b * nq + qi, hh)),
        scratch_shapes=[pltpu.VMEM((2 * t, e), BF16),
                        pltpu.VMEM((2 * t, t), F32),
                        pltpu.VMEM((2 * t, LANES), F32),
                        pltpu.VMEM((2 * t, LANES), F32),
                        pltpu.VMEM((2 * t, e), F32)],
        compiler_params=_cparams(("parallel", "parallel", "arbitrary")),
        name="diff_attn",
    )(lam, proj, proj, proj, bias_blocks, subln_g)


def _sortable_key(x):
    bits = lax.bitcast_convert_type(x, I32)
    return bits ^ ((bits >> 31) & jnp.int32(0x7FFFFFFF))


def _dsa_kernel(iq_ref, sq_ref, iw_ref, ik_ref, ckv_ref, kvg_ref, wuk_ref, wuv_ref, bias_ref, o_ref,
                c_ref, key_ref, keyt_ref, qi_ref, wb_ref, ql_ref, s_ref, m_ref, l_ref, acc_ref, *, topk, scale):
    t = iq_ref.shape[0]
    tk = KEY_CHUNK
    qi = pl.program_id(1)
    n_chunks = qi + 1
    hi, hb = N_HEADS_IDX, N_HEADS_DSA

    @pl.when(qi == 0)
    def _():
        ckv = ckv_ref[...].astype(F32)
        ms = jnp.mean(ckv * ckv, axis=-1, keepdims=True)
        c_ref[...] = (ckv * lax.rsqrt(ms + EPS) * kvg_ref[...]).astype(BF16)

    lane = lax.broadcasted_iota(I32, (t, LANES), 1)
    for h in range(hi):
        blk = iq_ref[:, (h // 2) * LANES:(h // 2 + 1) * LANES]
        keep = (lane < HEAD_DIM_IDX) if h % 2 == 0 else (lane >= HEAD_DIM_IDX)
        qi_ref[h * t:(h + 1) * t, :] = jnp.where(keep, blk, jnp.zeros_like(blk))
        wb_ref[h] = jnp.broadcast_to(iw_ref[:, h:h + 1].astype(F32), (t, LANES))

    def score_chunk(kc, diag):
        off = pl.multiple_of(kc * t, t)
        d = _dot_nt(qi_ref[...], ik_ref[pl.ds(off, t), :]).reshape(hi, t, t)
        sc = jnp.zeros((t, t), F32)
        for h in range(hi):
            w = wb_ref[h]
            w = jnp.concatenate([w] * (t // LANES), axis=1)
            sc = sc + jnp.maximum(d[h], 0.0) * w
        key = _sortable_key(sc + 0.0)
        if diag:
            row = lax.broadcasted_iota(I32, (t, t), 0)
            col = lax.broadcasted_iota(I32, (t, t), 1)
            key = jnp.where(col <= row, key, jnp.int32(INT_MIN))
        key_ref[:, pl.ds(off, t)] = key
        keyt_ref[pl.ds(off, t), :] = key.T

    def score_body(kc, carry):
        score_chunk(kc, False)
        return carry

    lax.fori_loop(0, qi, score_body, 0)
    score_chunk(qi, True)

    @pl.when(qi % 2 == 0)
    def _():
        off = pl.multiple_of((qi + 1) * t, t)
        key_ref[:, pl.ds(off, t)] = jnp.full((t, t), INT_MIN, I32)
        keyt_ref[pl.ds(off, t), :] = jnp.full((t, t), INT_MIN, I32)

    n_steps = (qi + 2) // 2

    def bit_body(i, cur):
        bit = lax.shift_left(jnp.int32(1), 31 - i)
        cand = cur | bit
        cand_s = (cand ^ jnp.int32(INT_MIN))[None]

        def body(kc, cnt):
            off = pl.multiple_of(kc * tk, tk)
            k = keyt_ref[pl.ds(off, tk), :].reshape(tk // 8, 8, t)
            return cnt + jnp.sum(jnp.where(k >= cand_s, 1, 0), axis=0)

        cnt = lax.fori_loop(0, n_steps, body, jnp.zeros((8, t), I32))
        total = jnp.sum(cnt, axis=0, keepdims=True)
        return jnp.where(total >= topk, cand, cur)

    cur = lax.fori_loop(0, 32, bit_body, jnp.zeros((8, t), I32))
    thr = jnp.maximum(cur ^ jnp.int32(INT_MIN), jnp.int32(INT_MIN + 1))
    thr_b = jnp.broadcast_to(thr[0:1, :], (LANES, t)).T
    thr_w = jnp.concatenate([thr_b] * (tk // LANES), axis=1)

    def mask_body(kc, carry):
        off = pl.multiple_of(kc * tk, tk)
        am = jnp.where(key_ref[:, pl.ds(off, tk)] >= thr_w, 0.0, MASK_VALUE).astype(F32)
        key_ref[:, pl.ds(off, tk)] = lax.bitcast_convert_type(am, I32)
        return carry

    lax.fori_loop(0, n_steps, mask_body, 0)

    for h in range(hb):
        qh = sq_ref[:, h * HEAD_DIM_DSA:(h + 1) * HEAD_DIM_DSA]
        ql = jnp.dot(qh, wuk_ref[h], preferred_element_type=F32) * scale
        ql_ref[h * t:(h + 1) * t, :] = ql.astype(BF16)
    _softmax_init(m_ref, l_ref, acc_ref)
    d0 = lambda g: bias_ref[0, g]
    d1 = lambda g: bias_ref[1, g]

    def chunk(kc, width, delta):
        off = pl.multiple_of(kc * tk, tk)
        c = c_ref[pl.ds(off, width), :]
        am = lax.bitcast_convert_type(key_ref[:, pl.ds(off, width)], F32)
        s = _dot_nt(ql_ref[...], c).reshape(hb, t, width) + am[None]
        s_ref[:, 0:width] = s.reshape(hb * t, width)
        if delta is not None:
            _near_bias(s_ref, d0, d1, delta, hb, t, width)
        _softmax_step(s_ref, width, c, m_ref, l_ref, acc_ref)

    def far_body(kc, carry):
        chunk(kc, tk, None)
        return carry

    lax.fori_loop(0, jnp.maximum((qi - 1) // 2, 0), far_body, 0)
    half = qi // 2

    @pl.when(qi % 2 == 1)
    def _():
        chunk(half, tk, t // LANES)

    @pl.when(qi % 2 == 0)
    def _():
        @pl.when(half >= 1)
        def _():
            chunk(half - 1, tk, tk // LANES)
        chunk(half, t, 0)

    ol = _softmax_result(l_ref, acc_ref).astype(BF16)
    for h in range(hb):
        o = jnp.dot(ol[h * t:(h + 1) * t, :], wuv_ref[h], preferred_element_type=F32)
        o_ref[:, h * HEAD_DIM_DSA:(h + 1) * HEAD_DIM_DSA] = o.astype(o_ref.dtype)


def _dsa_attention(proj, kv_g, w_ukt, w_uv, bias_blocks, batch, seq, topk):
    t = DSA_BLOCK
    nq = seq // t
    hb, hi = N_HEADS_DSA, N_HEADS_IDX
    width = hb * HEAD_DIM_DSA
    kernel = functools.partial(_dsa_kernel, topk=topk, scale=HEAD_DIM_DSA ** -0.5)
    return pl.pallas_call(
        kernel,
        out_shape=jax.ShapeDtypeStruct((batch * seq, width), BF16),
        grid=(batch, nq),
        in_specs=[pl.BlockSpec((t, 1024), lambda b, qi: (b * nq + qi, COL_IQ // 1024)),
                  pl.BlockSpec((t, 1024), lambda b, qi: (b * nq + qi, COL_SQ // 1024)),
                  pl.BlockSpec((t, LANES), lambda b, qi: (b * nq + qi, COL_IW // LANES)),
                  pl.BlockSpec((seq, LANES), lambda b, qi: (b, COL_IK // LANES)),
                  pl.BlockSpec((seq, KV_LATENT), lambda b, qi: (b, COL_CKV // KV_LATENT)),
                  pl.BlockSpec((1, KV_LATENT), lambda b, qi: (0, 0)),
                  pl.BlockSpec((hb, HEAD_DIM_DSA, KV_LATENT), lambda b, qi: (0, 0, 0)),
                  pl.BlockSpec((hb, KV_LATENT, HEAD_DIM_DSA), lambda b, qi: (0, 0, 0)),
                  pl.BlockSpec((2, hb, LANES, LANES), lambda b, qi: (0, 0, 0, 0))],
        out_specs=pl.BlockSpec((t, width), lambda b, qi: (b * nq + qi, 0)),
        scratch_shapes=[pltpu.VMEM((seq, KV_LATENT), BF16),
                        pltpu.VMEM((t, seq), I32),
                        pltpu.VMEM((seq, t), I32),
                        pltpu.VMEM((hi * t, LANES), BF16),
                        pltpu.VMEM((hi, t, LANES), F32),
                        pltpu.VMEM((hb * t, KV_LATENT), BF16),
                        pltpu.VMEM((hb * t, KEY_CHUNK), F32),
                        pltpu.VMEM((hb * t, LANES), F32),
                        pltpu.VMEM((hb * t, LANES), F32),
                        pltpu.VMEM((hb * t, KV_LATENT), F32)],
        compiler_params=_cparams(("parallel", "arbitrary")),
        name="dsa_attn",
    )(proj, proj, proj, proj, proj, kv_g, w_ukt, w_uv, bias_blocks)


def _pack_halves(x):
    c = x.shape[1] // 2
    lo = lax.bitcast_convert_type(x[:, :c].astype(BF16).astype(F32), I32)
    hi = lax.bitcast_convert_type(x[:, c:].astype(BF16).astype(F32), I32)
    return lax.shift_right_logical(lo, 16) | (hi & jnp.int32(-65536))


def _unpack_halves(w):
    lo = lax.bitcast_convert_type(lax.shift_left(w, 16), F32)
    hi = lax.bitcast_convert_type(w & jnp.int32(-65536), F32)
    return lo, hi


def _merge_kernel(x_ref, yd_ref, ys_ref, ga_ref, gb_ref, wd_ref, ws_ref, wo_ref, g_ref, wr_ref, br_ref,
                  h_ref, hn_ref, route_ref):
    bd = jnp.dot(yd_ref[...], wd_ref[...], preferred_element_type=F32)
    bs = jnp.dot(ys_ref[...], ws_ref[...], preferred_element_type=F32)
    merged = (jax.nn.sigmoid(ga_ref[...].astype(F32)) * bd + jax.nn.sigmoid(gb_ref[...].astype(F32)) * bs)
    h = x_ref[...] + jnp.dot(merged.astype(BF16), wo_ref[...], preferred_element_type=F32)
    h_ref[...] = h
    ms = jnp.mean(h * h, axis=-1, keepdims=True)
    hn = h * lax.rsqrt(ms + EPS) * g_ref[...]
    hn_ref[...] = _pack_halves(hn)

    logits = jnp.dot(hn, wr_ref[...], preferred_element_type=F32, precision=lax.Precision.HIGHEST)
    logits = logits + br_ref[...]
    lane = lax.broadcasted_iota(I32, logits.shape, 1)
    vals, ids = [], []
    for _ in range(TOP_K_EXPERTS):
        mx = jnp.max(logits, axis=-1, keepdims=True)
        ix = jnp.min(jnp.where(logits == mx, lane, LANES), axis=-1, keepdims=True)
        vals.append(mx)
        ids.append(ix)
        logits = jnp.where(lane == ix, -jnp.inf, logits)
    es = [jnp.exp(v - vals[0]) for v in vals]
    inv = 1.0 / (es[0] + es[1] + es[2] + es[3])
    route = jnp.zeros(logits.shape, F32)
    for k in range(TOP_K_EXPERTS):
        route = jnp.where(lane == k, es[k] * inv, route)
        route = jnp.where(lane == TOP_K_EXPERTS + k, ids[k].astype(F32), route)
    route_ref[...] = route


def _merge(x2, y_diff, y_dsa, proj, w_bd, w_bs, w_out, g_ffn, w_router, b_router, tm):
    n, d = x2.shape
    row = lambda i: (i, 0)
    const = lambda i: (0, 0)
    return pl.pallas_call(
        _merge_kernel,
        out_shape=(jax.ShapeDtypeStruct((n, d), F32),
                   jax.ShapeDtypeStruct((n, d // 2), I32),
                   jax.ShapeDtypeStruct((n, LANES), F32)),
        grid=(n // tm,),
        in_specs=[pl.BlockSpec((tm, d), row),
                  pl.BlockSpec((tm, d), row),
                  pl.BlockSpec((tm, d), row),
                  pl.BlockSpec((tm, d), lambda i: (i, COL_GA // 1024)),
                  pl.BlockSpec((tm, d), lambda i: (i, COL_GB // 1024)),
                  pl.BlockSpec((d, d), const),
                  pl.BlockSpec((d, d), const),
                  pl.BlockSpec((d, d), const),
                  pl.BlockSpec((1, d), const),
                  pl.BlockSpec((d, LANES), const),
                  pl.BlockSpec((1, LANES), const)],
        out_specs=(pl.BlockSpec((tm, d), row),
                   pl.BlockSpec((tm, d // 2), row),
                   pl.BlockSpec((tm, LANES), row)),
        compiler_params=_cparams(("parallel",)),
        name="merge_router",
    )(x2, y_diff, y_dsa, proj, proj, w_bd, w_bs, w_out, g_ffn, w_router, b_router)


def _regroup_kernel(w_ref, p_ref, o_ref):
    pw = p_ref.shape[0]
    for j in range(w_ref.shape[2] // pw):
        w = w_ref[0, :, j * pw:(j + 1) * pw].astype(BF16)
        o_ref[0, :, j * pw:(j + 1) * pw] = jnp.dot(w, p_ref[...], preferred_element_type=F32).astype(BF16)


def _regroup_gate_up(w_gu, rows):
    e, d, f2 = w_gu.shape
    pw = 2 * LANES
    src = jnp.arange(pw, dtype=I32)
    dst = (src % 2) * LANES + src // 2
    perm = (dst[:, None] == jnp.arange(pw, dtype=I32)[None, :]).astype(BF16)
    return pl.pallas_call(
        _regroup_kernel,
        out_shape=jax.ShapeDtypeStruct((e, d, f2), BF16),
        grid=(e, d // rows),
        in_specs=[pl.BlockSpec((1, rows, f2), lambda i, j: (i, j, 0)),
                  pl.BlockSpec((pw, pw), lambda i, j: (0, 0))],
        out_specs=pl.BlockSpec((1, rows, f2), lambda i, j: (i, j, 0)),
        compiler_params=_cparams(("parallel", "parallel")),
        name="regroup_gate_up",
    )(w_gu, perm)


def _ffn_kernel(be_ref, nu_ref, x_ref, wgu_ref, wd_ref, bgu_ref, bd_ref, o_ref):
    @pl.when(pl.program_id(0) < nu_ref[0])
    def _():
        x_lo, x_hi = _unpack_halves(x_ref[...])
        half = x_lo.shape[1]
        gu = (jnp.dot(x_lo.astype(BF16), wgu_ref[0, 0:half, :], preferred_element_type=F32)
              + jnp.dot(x_hi.astype(BF16), wgu_ref[0, half:2 * half, :], preferred_element_type=F32)
              + bgu_ref[0])
        acts = []
        for j in range(gu.shape[1] // (2 * LANES)):
            gate = jnp.minimum(gu[:, 2 * j * LANES:(2 * j + 1) * LANES], SWIGLU_LIMIT)
            up = jnp.clip(gu[:, (2 * j + 1) * LANES:(2 * j + 2) * LANES], -SWIGLU_LIMIT, SWIGLU_LIMIT)
            glu = gate * jax.nn.sigmoid(gate * SWIGLU_ALPHA)
            acts.append(((up + 1.0) * glu).astype(BF16))
        a = jnp.concatenate(acts, axis=1)
        y = jnp.dot(a, wd_ref[0], preferred_element_type=F32) + bd_ref[0]
        o_ref[...] = _pack_halves(y)

    @pl.when(pl.program_id(0) >= nu_ref[0])
    def _():
        o_ref[...] = jnp.zeros(o_ref.shape, o_ref.dtype)


def _expert_ffn(blk_exp, n_used, xs, wgu, wd, bgu, bd):
    p, dw = xs.shape
    f, d = wd.shape[1], wd.shape[2]
    nblk = p // MOE_ROWS
    wmap = lambda i, be, nu: (be[i], 0, 0)
    grid_spec = pltpu.PrefetchScalarGridSpec(
        num_scalar_prefetch=2,
        grid=(nblk,),
        in_specs=[pl.BlockSpec((MOE_ROWS, dw), lambda i, be, nu: (i, 0)),
                  pl.BlockSpec((1, d, 2 * f), wmap),
                  pl.BlockSpec((1, f, d), wmap),
                  pl.BlockSpec((1, 1, 2 * f), wmap),
                  pl.BlockSpec((1, 1, d), wmap)],
        out_specs=pl.BlockSpec((MOE_ROWS, dw), lambda i, be, nu: (i, 0)),
    )
    return pl.pallas_call(
        _ffn_kernel,
        out_shape=jax.ShapeDtypeStruct((p, dw), I32),
        grid_spec=grid_spec,
        compiler_params=_cparams(("arbitrary",)),
        name="expert_ffn",
    )(blk_exp, n_used, xs, wgu, wd, bgu, bd)


def _route_kernel(route_ref, dest_ref, cnt_ref, u_ref, carry_ref, pstart_ref, *, block_rows):
    ph, i = pl.program_id(0), pl.program_id(1)
    tm = route_ref.shape[0]

    @pl.when((ph == 0) & (i == 0))
    def _():
        r = lax.broadcasted_iota(I32, (tm, tm), 0)
        c = lax.broadcasted_iota(I32, (tm, tm), 1)
        u_ref[...] = jnp.where(r < c, 1.0, 0.0).astype(BF16)
        carry_ref[...] = jnp.zeros(carry_ref.shape, F32)

    @pl.when((ph == 1) & (i == 0))
    def _():
        counts = carry_ref[...]
        cnt_ref[...] = counts
        padded = jnp.ceil(counts * (1.0 / block_rows)) * block_rows
        r = lax.broadcasted_iota(I32, (LANES, LANES), 0)
        c = lax.broadcasted_iota(I32, (LANES, LANES), 1)
        lower = jnp.where(c < r, 1.0, 0.0).astype(F32)
        pstart_ref[...] = jnp.dot(lower, padded, preferred_element_type=F32, precision=lax.Precision.HIGHEST)
        carry_ref[...] = jnp.zeros(carry_ref.shape, F32)

    rt = route_ref[...].T
    sub = lax.broadcasted_iota(I32, (LANES, tm), 0)
    hits = [sub == rt[TOP_K_EXPERTS + k:TOP_K_EXPERTS + k + 1, :].astype(I32) for k in range(TOP_K_EXPERTS)]
    m = jnp.zeros((LANES, tm), F32)
    for hit in hits:
        m = m + jnp.where(hit, 1.0, 0.0)
    tile_counts = jnp.broadcast_to(jnp.sum(m, axis=1, keepdims=True), (LANES, LANES))

    @pl.when(ph == 0)
    def _():
        dest_ref[...] = jnp.zeros(dest_ref.shape, I32)

    @pl.when(ph == 1)
    def _():
        prefix = jnp.dot(m.astype(BF16), u_ref[...], preferred_element_type=F32)
        rank = prefix + (pstart_ref[:, 0:1] + carry_ref[:, 0:1])
        rows = [jnp.sum(jnp.where(hit, rank, 0.0), axis=0, keepdims=True) for hit in hits]
        rows.append(jnp.zeros((dest_ref.shape[0] - TOP_K_EXPERTS, tm), F32))
        dest_ref[...] = jnp.concatenate(rows, axis=0).astype(I32)

    carry_ref[...] = carry_ref[...] + tile_counts


def _route_rows(route, tm):
    n = route.shape[0]
    nt = n // tm
    kernel = functools.partial(_route_kernel, block_rows=MOE_ROWS)
    return pl.pallas_call(
        kernel,
        out_shape=(jax.ShapeDtypeStruct((8, n + tm), I32), jax.ShapeDtypeStruct((LANES, LANES), F32)),
        grid=(2, nt),
        in_specs=[pl.BlockSpec((tm, LANES), lambda ph, i: (i, 0))],
        out_specs=(pl.BlockSpec((8, tm), lambda ph, i: (0, ph * i + (1 - ph) * nt)),
                   pl.BlockSpec((LANES, LANES), lambda ph, i: (0, 0))),
        scratch_shapes=[pltpu.VMEM((tm, tm), BF16),
                        pltpu.VMEM((LANES, LANES), F32),
                        pltpu.VMEM((LANES, LANES), F32)],
        compiler_params=_cparams(("arbitrary", "arbitrary")),
        name="route_rows",
    )(route)


SC_WINDOW = 128
SC_WORKERS = 32


def _sc_mesh():
    return plsc.VectorSubcoreMesh(core_axis_name="c", subcore_axis_name="s")


def _sc_scatter_rows(src, dest, p):
    n, d = src.shape
    per = n // (SC_WINDOW * SC_WORKERS)

    @pl.kernel(out_type=jax.ShapeDtypeStruct((p, d), src.dtype), mesh=_sc_mesh(),
               scratch_types=[pltpu.VMEM((dest.shape[0], SC_WINDOW), I32), pltpu.VMEM((SC_WINDOW, d), src.dtype)])
    def scatter(src_hbm, idx_hbm, out_hbm, idx_vmem, buf):
        wid = lax.axis_index("c") * (SC_WORKERS // 2) + lax.axis_index("s")

        @pl.loop(0, per)
        def _(j):
            off = (wid * per + j) * SC_WINDOW
            pltpu.sync_copy(idx_hbm.at[:, pl.ds(off, SC_WINDOW)], idx_vmem)
            pltpu.sync_copy(src_hbm.at[pl.ds(off, SC_WINDOW), :], buf)
            for k in range(TOP_K_EXPERTS):
                pltpu.sync_copy(buf, out_hbm.at[idx_vmem.at[k]])

    return scatter(src, dest)


def _sc_gather_rows(src, dest):
    n = dest.shape[1]
    d = src.shape[1]
    per = n // (SC_WINDOW * SC_WORKERS)

    @pl.kernel(out_type=jax.ShapeDtypeStruct((TOP_K_EXPERTS * n, d), src.dtype), mesh=_sc_mesh(),
               scratch_types=[pltpu.VMEM((dest.shape[0], SC_WINDOW), I32), pltpu.VMEM((SC_WINDOW, d), src.dtype)])
    def gather(src_hbm, idx_hbm, out_hbm, idx_vmem, buf):
        wid = lax.axis_index("c") * (SC_WORKERS // 2) + lax.axis_index("s")

        @pl.loop(0, per)
        def _(j):
            off = (wid * per + j) * SC_WINDOW
            pltpu.sync_copy(idx_hbm.at[:, pl.ds(off, SC_WINDOW)], idx_vmem)
            for k in range(TOP_K_EXPERTS):
                pltpu.sync_copy(src_hbm.at[idx_vmem.at[k]], buf)
                pltpu.sync_copy(buf, out_hbm.at[pl.ds(k * n + off, SC_WINDOW), :])

    return gather(src, dest)


def _combine_kernel(h_ref, y_ref, route_ref, g_ref, o_ref):
    half = h_ref.shape[1] // 2
    h_lo, h_hi = h_ref[:, 0:half], h_ref[:, half:2 * half]
    route = route_ref[...]
    for k in range(TOP_K_EXPERTS):
        y_lo, y_hi = _unpack_halves(y_ref[k])
        gate = route[:, k:k + 1]
        h_lo = h_lo + gate * y_lo
        h_hi = h_hi + gate * y_hi
    ms = (jnp.sum(h_lo * h_lo, axis=-1, keepdims=True)
          + jnp.sum(h_hi * h_hi, axis=-1, keepdims=True)) * (1.0 / (2 * half))
    inv = lax.rsqrt(ms + EPS)
    o_ref[:, 0:half] = h_lo * inv * g_ref[:, 0:half]
    o_ref[:, half:2 * half] = h_hi * inv * g_ref[:, half:2 * half]


def _combine(h1, yg, route, g_final, tm):
    n, d = h1.shape
    return pl.pallas_call(
        _combine_kernel,
        out_shape=jax.ShapeDtypeStruct((n, d), F32),
        grid=(n // tm,),
        in_specs=[pl.BlockSpec((tm, d), lambda i: (i, 0)),
                  pl.BlockSpec((TOP_K_EXPERTS, tm, d // 2), lambda i: (0, i, 0)),
                  pl.BlockSpec((tm, LANES), lambda i: (i, 0)),
                  pl.BlockSpec((1, d), lambda i: (0, 0))],
        out_specs=pl.BlockSpec((tm, d), lambda i: (i, 0)),
        compiler_params=_cparams(("parallel",)),
        name="combine_norm",
    )(h1, yg, route, g_final)


def _t5_bucket(dist):
    n = jnp.maximum(dist, 0)
    max_exact = N_BUCKETS // 2
    nf = jnp.maximum(n, 1).astype(F32)
    large = max_exact + (jnp.log(nf / max_exact) / math.log(MAX_DISTANCE / max_exact)
                         * (N_BUCKETS - max_exact)).astype(I32)
    large = jnp.minimum(large, N_BUCKETS - 1)
    return jnp.where(n < max_exact, n, large)


def _bias_blocks(bias_tab):
    t = LANES
    assert MAX_DISTANCE <= LANES
    r = jnp.arange(t, dtype=I32)[:, None]
    c = jnp.arange(t, dtype=I32)[None, :]
    rel = (bias_tab - bias_tab[N_BUCKETS - 1][None, :]).astype(F32)
    tiles = []
    buckets = jnp.arange(N_BUCKETS, dtype=I32)[:, None, None]
    for delta in (0, t):
        dist = r - c + delta
        hit = _t5_bucket(dist)[None] == buckets
        b = jnp.sum(jnp.where(hit[:, None], rel[:, :, None, None], 0.0), axis=0)
        tiles.append(jnp.where((dist >= 0)[None], b, MASK_VALUE))
    return jnp.stack(tiles)


def _regroup_w_in(w_in):
    sizes = (1024, 1024, 1024, 1024, KV_LATENT, 1024, HEAD_DIM_IDX, N_HEADS_IDX, D_MODEL, D_MODEL)
    parts, off = [], 0
    for sz in sizes:
        parts.append(w_in[:, off:off + sz])
        off += sz
    dq, dk, dv, sq, ckv, iq, ik, iw, ga, gb = parts
    dq = dq * (HEAD_DIM_DIFF ** -0.5)
    iw = iw * ((N_HEADS_IDX ** -0.5) * (HEAD_DIM_IDX ** -0.5))
    pad = jnp.zeros((w_in.shape[0], PROJ_WIDTH - COL_IW - N_HEADS_IDX), w_in.dtype)
    w = jnp.concatenate([dq, dk, dv, sq, iq, ga, gb, ckv, ik, ik, iw, pad], axis=1)
    return w.astype(BF16)


def _block_tables(counts, n_assign):
    e, bm = N_EXPERTS, MOE_ROWS
    padded = (counts + bm - 1) // bm * bm
    pends = jnp.cumsum(padded)
    nblk = -(-(n_assign + e * (bm - 1)) // bm)
    first_row = jnp.arange(nblk, dtype=I32) * bm
    blk_exp = jnp.minimum(jnp.sum((pends[None, :] <= first_row[:, None]).astype(I32), axis=1), e - 1)
    n_used = (pends[-1] // bm).astype(I32).reshape(1)
    return blk_exp, n_used, nblk


def kernel(x, norm_attn_g, w_in, rel_bias, lam_q1, lam_k1, lam_q2, lam_k2, diff_subln_g, kv_norm_g, w_uk, w_uv,
           w_branch_diff, w_branch_dsa, w_out, norm_ffn_g, w_router, b_router, w_gate_up, b_gate_up, w_down,
           b_down, norm_final_g):
    batch, seq, d = x.shape
    n = batch * seq
    assert norm_attn_g.shape[0] == 1, "single-layer kernel"
    assert seq % DIFF_BLOCK == 0 and seq % KEY_CHUNK == 0 and d == D_MODEL
    row_tile = math.gcd(n, 1024)

    x2 = x.reshape(n, d)
    proj = _inproj(x2, norm_attn_g[0].reshape(1, d), _regroup_w_in(w_in[0]), row_tile, 1280)

    lam_init = 0.8 - 0.6 * math.exp(-0.3 * 0)
    lam = (jnp.exp(jnp.sum(lam_q1[0].astype(F32) * lam_k1[0].astype(F32)))
           - jnp.exp(jnp.sum(lam_q2[0].astype(F32) * lam_k2[0].astype(F32))) + lam_init)
    y_diff = _diff_attention(proj, lam.reshape(1, 1).astype(F32), _bias_blocks(rel_bias[:, :N_HEADS_DIFF]),
                             diff_subln_g[0].reshape(1, -1).astype(F32), batch, seq, 1.0 - lam_init)

    y_dsa = _dsa_attention(proj, kv_norm_g[0].reshape(1, -1).astype(F32),
                           w_uk[0].transpose(0, 2, 1).astype(BF16), w_uv[0].astype(BF16),
                           _bias_blocks(rel_bias[:, N_HEADS_DIFF:]), batch, seq, min(TOPK_MAX, seq // 4))

    w_r = jnp.zeros((d, LANES), F32).at[:, :N_EXPERTS].set(w_router[0].astype(F32))
    b_r = jnp.full((1, LANES), MASK_VALUE, F32).at[0, :N_EXPERTS].set(b_router[0].astype(F32))
    h1, hn, route = _merge(x2, y_diff, y_dsa, proj, w_branch_diff[0].astype(BF16), w_branch_dsa[0].astype(BF16),
                           w_out[0].astype(BF16), norm_ffn_g[0].reshape(1, d).astype(F32), w_r, b_r,
                           math.gcd(n, 512))

    dest, counts = _route_rows(route, math.gcd(n, 1024))
    dest = dest[:, :n]
    blk_exp, n_used, nblk = _block_tables(counts[:N_EXPERTS, 0].astype(I32), n * TOP_K_EXPERTS)
    xs = _sc_scatter_rows(hn, dest, nblk * MOE_ROWS)
    e, f = N_EXPERTS, D_EXPERT
    b_gu = b_gate_up[0].astype(F32).reshape(e, f // LANES, LANES, 2).transpose(0, 1, 3, 2).reshape(e, 1, 2 * f)
    ys = _expert_ffn(blk_exp, n_used, xs, _regroup_gate_up(w_gate_up[0], 512), w_down[0].astype(BF16),
                     b_gu, b_down[0][:, None, :].astype(F32))
    yg = _sc_gather_rows(ys, dest).reshape(TOP_K_EXPERTS, n, d // 2)
    out = _combine(h1, yg, route, norm_final_g.reshape(1, d).astype(F32), math.gcd(n, 512))
    return out.reshape(batch, seq, d)
```

```python
import functools
import math

import jax
import jax.numpy as jnp
from jax import lax
from jax.experimental import pallas as pl
from jax.experimental.pallas import tpu as pltpu
from jax.experimental.pallas import tpu_sc as plsc

F32 = jnp.float32
BF16 = jnp.bfloat16
I32 = jnp.int32

D_MODEL = 1024
N_HEADS_DIFF = 8
HEAD_DIM_DIFF = 64
N_HEADS_DSA = 8
HEAD_DIM_DSA = 128
KV_LATENT = 256
N_HEADS_IDX = 16
HEAD_DIM_IDX = 64
TOPK_MAX = 256
N_BUCKETS = 32
MAX_DISTANCE = 128
N_EXPERTS = 32
TOP_K_EXPERTS = 4
D_EXPERT = 1024
SWIGLU_LIMIT = 7.0
SWIGLU_ALPHA = 1.702
EPS = 1e-6

LANES = 128
DIFF_BLOCK = 512
DSA_BLOCK = 256
KEY_CHUNK = 512
MOE_ROWS = 512
PROJ_WIDTH = 7680
VMEM_LIMIT = 56 * 1024 * 1024

COL_DQ, COL_DK, COL_DV, COL_SQ, COL_IQ, COL_GA, COL_GB = (i * 1024 for i in range(7))
COL_CKV = 7168
COL_IK = 7424
COL_IW = 7552

LOG2E = math.log2(math.e)
MASK_VALUE = -1e30
M_INIT = -1e29
INT_MIN = -2 ** 31


def _cparams(sem):
    return pltpu.CompilerParams(dimension_semantics=sem, vmem_limit_bytes=VMEM_LIMIT)


def _inproj_kernel(x_ref, g_ref, w_ref, o_ref, xn_ref):
    @pl.when(pl.program_id(1) == 0)
    def _():
        x = x_ref[...]
        ms = jnp.mean(x * x, axis=-1, keepdims=True)
        xn_ref[...] = (x * lax.rsqrt(ms + EPS) * g_ref[...]).astype(BF16)

    o_ref[...] = jnp.dot(xn_ref[...], w_ref[...], preferred_element_type=F32).astype(o_ref.dtype)


def _inproj(x2, g, w, tm, tn):
    n, d = x2.shape
    width = w.shape[1]
    return pl.pallas_call(
        _inproj_kernel,
        out_shape=jax.ShapeDtypeStruct((n, width), BF16),
        grid=(n // tm, width // tn),
        in_specs=[pl.BlockSpec((tm, d), lambda i, j: (i, 0)),
                  pl.BlockSpec((1, d), lambda i, j: (0, 0)),
                  pl.BlockSpec((d, tn), lambda i, j: (0, j))],
        out_specs=pl.BlockSpec((tm, tn), lambda i, j: (i, j)),
        scratch_shapes=[pltpu.VMEM((tm, d), BF16)],
        compiler_params=_cparams(("parallel", "arbitrary")),
        name="inproj",
    )(x2, g, w)


def _softmax_step(s_ref, tk, v, m_ref, l_ref, acc_ref):
    nl = tk // LANES
    smax = s_ref[:, 0:LANES]
    for j in range(1, nl):
        smax = jnp.maximum(smax, s_ref[:, j * LANES:(j + 1) * LANES])
    m_prev = m_ref[...]
    m_new = jnp.maximum(m_prev, jnp.max(smax, axis=-1, keepdims=True))
    alpha = jnp.exp2(m_prev - m_new)
    psum = None
    ps = []
    for j in range(nl):
        pj = jnp.exp2(s_ref[:, j * LANES:(j + 1) * LANES] - m_new)
        psum = pj if psum is None else psum + pj
        ps.append(pj.astype(BF16))
    l_ref[...] = alpha * l_ref[...] + psum
    pv = _dot_split(jnp.concatenate(ps, axis=1), v)
    e = acc_ref.shape[1]
    a = alpha if e == LANES else jnp.concatenate([alpha] * (e // LANES), axis=1)
    acc_ref[...] = a * acc_ref[...] + pv
    m_ref[...] = m_new


def _softmax_init(m_ref, l_ref, acc_ref):
    m_ref[...] = jnp.full(m_ref.shape, M_INIT, F32)
    l_ref[...] = jnp.zeros(l_ref.shape, F32)
    acc_ref[...] = jnp.zeros(acc_ref.shape, F32)


def _softmax_result(l_ref, acc_ref):
    return acc_ref[...] * (1.0 / jnp.sum(l_ref[...], axis=-1, keepdims=True))


def _near_bias(s_ref, d0, d1, delta, groups, t, tk):
    for g in range(groups):
        for rb in range(t // LANES):
            for cb in range(tk // LANES):
                bd = delta + rb - cb
                rows = slice(g * t + rb * LANES, g * t + (rb + 1) * LANES)
                cols = slice(cb * LANES, (cb + 1) * LANES)
                if bd == 0:
                    s_ref[rows, cols] = s_ref[rows, cols] + d0(g)
                elif bd == 1:
                    s_ref[rows, cols] = s_ref[rows, cols] + d1(g)
                elif bd < 0:
                    s_ref[rows, cols] = jnp.full((LANES, LANES), MASK_VALUE, F32)


def _pipelined_far(n_far, issue, finish):
    odd = n_far % 2

    @pl.when(odd == 1)
    def _():
        issue(0, 1)
        issue(1, 0)
        finish(0, 1)

    @pl.when(odd == 0)
    def _():
        issue(0, 0)

    def body(j, carry):
        a = odd + 2 * j
        issue(a + 1, 1)
        finish(a, 0)
        issue(a + 2, 0)
        finish(a + 1, 1)
        return carry

    lax.fori_loop(0, (n_far - odd) // 2, body, 0)


def _dot_nt(a, b):
    h = a.shape[0] // 2
    dn = (((1,), (1,)), ((), ()))
    return jnp.concatenate([lax.dot_general(a[:h], b, dn, preferred_element_type=F32),
                            lax.dot_general(a[h:], b, dn, preferred_element_type=F32)], axis=0)


def _dot_split(a, b):
    h = a.shape[0] // 2
    return jnp.concatenate([jnp.dot(a[:h], b, preferred_element_type=F32),
                            jnp.dot(a[h:], b, preferred_element_type=F32)], axis=0)


def _diff_kernel(lam_ref, q_ref, k_ref, v_ref, bias_ref, g_ref, o_ref, q2_ref, s0_ref, s1_ref, m_ref, l_ref, acc_ref,
                 *, out_scale):
    t = q_ref.shape[0]
    qi = pl.program_id(2)

    q = q_ref[...]
    lane = lax.broadcasted_iota(I32, q.shape, 1)
    zero = jnp.zeros_like(q)
    q2_ref[0:t, :] = jnp.where(lane < HEAD_DIM_DIFF, q, zero)
    q2_ref[t:2 * t, :] = jnp.where(lane >= HEAD_DIM_DIFF, q, zero)
    _softmax_init(m_ref, l_ref, acc_ref)
    d0 = lambda g: bias_ref[0, 0]
    d1 = lambda g: bias_ref[1, 0]

    s_refs = (s0_ref, s1_ref)

    def issue(kc, slot):
        off = pl.multiple_of(kc * t, t)
        s_refs[slot][...] = _dot_nt(q2_ref[...], k_ref[pl.ds(off, t), :])

    def finish(kc, slot, delta=None):
        if delta is not None:
            _near_bias(s_refs[slot], d0, d1, delta, 2, t, t)
        off = pl.multiple_of(kc * t, t)
        _softmax_step(s_refs[slot], t, v_ref[pl.ds(off, t), :], m_ref, l_ref, acc_ref)

    _pipelined_far(jnp.maximum(qi - 1, 0), issue, finish)

    @pl.when(qi >= 1)
    def _():
        issue(qi, 1)
        finish(qi - 1, 0, t // LANES)
        finish(qi, 1, 0)

    @pl.when(qi == 0)
    def _():
        finish(0, 0, 0)

    o = _softmax_result(l_ref, acc_ref)
    o = o[0:t, :] - lam_ref[0, 0] * o[t:2 * t, :]
    ms = jnp.mean(o * o, axis=-1, keepdims=True)
    o_ref[...] = (o * lax.rsqrt(ms + EPS) * g_ref[...] * out_scale).astype(o_ref.dtype)


def _diff_attention(proj, lam, bias_blocks, subln_g, batch, seq, out_scale):
    t = DIFF_BLOCK
    nq = seq // t
    h = N_HEADS_DIFF
    e = 2 * HEAD_DIM_DIFF
    kernel = functools.partial(_diff_kernel, out_scale=out_scale)
    return pl.pallas_call(
        kernel,
        out_shape=jax.ShapeDtypeStruct((batch * seq, h * e), BF16),
        grid=(batch, h, nq),
        in_specs=[pl.BlockSpec(memory_space=pltpu.SMEM),
                  pl.BlockSpec((t, e), lambda b, hh, qi: (b * nq + qi, COL_DQ // e + hh)),
                  pl.BlockSpec((seq, e), lambda b, hh, qi: (b, COL_DK // e + hh)),
                  pl.BlockSpec((seq, e), lambda b, hh, qi: (b, COL_DV // e + hh)),
                  pl.BlockSpec((2, 1, LANES, LANES), lambda b, hh, qi: (0, hh, 0, 0)),
                  pl.BlockSpec((1, e), lambda b, hh, qi: (0, 0))],
        out_specs=pl.BlockSpec((t, e), lambda b, hh, qi: (b * nq + qi, hh)),
        scratch_shapes=[pltpu.VMEM((2 * t, e), BF16),
                        pltpu.VMEM((2 * t, t), F32),
                        pltpu.VMEM((2 * t, t), F32),
                        pltpu.VMEM((2 * t, LANES), F32),
                        pltpu.VMEM((2 * t, LANES), F32),
                        pltpu.VMEM((2 * t, e), F32)],
        compiler_params=_cparams(("parallel", "parallel", "arbitrary")),
        name="diff_attn",
    )(lam, proj, proj, proj, bias_blocks, subln_g)


def _sortable_key(x):
    bits = lax.bitcast_convert_type(x, I32)
    return bits ^ ((bits >> 31) & jnp.int32(0x7FFFFFFF))


def _dsa_kernel(iq_ref, sq_ref, iw_ref, ik_ref, ckv_ref, kvg_ref, wuk_ref, wuv_ref, bias_ref, o_ref,
                c_ref, key_ref, keyt_ref, qi_ref, wb_ref, ql_ref, s0_ref, s1_ref, m_ref, l_ref, acc_ref, *, topk, scale):
    t = iq_ref.shape[0]
    tk = KEY_CHUNK
    qi = pl.program_id(1)
    n_chunks = qi + 1
    hi, hb = N_HEADS_IDX, N_HEADS_DSA

    @pl.when(qi == 0)
    def _():
        ckv = ckv_ref[...].astype(F32)
        ms = jnp.mean(ckv * ckv, axis=-1, keepdims=True)
        c_ref[...] = (ckv * lax.rsqrt(ms + EPS) * kvg_ref[...]).astype(BF16)

    lane = lax.broadcasted_iota(I32, (t, LANES), 1)
    for h in range(hi):
        blk = iq_ref[:, (h // 2) * LANES:(h // 2 + 1) * LANES]
        keep = (lane < HEAD_DIM_IDX) if h % 2 == 0 else (lane >= HEAD_DIM_IDX)
        qi_ref[h * t:(h + 1) * t, :] = jnp.where(keep, blk, jnp.zeros_like(blk))
        wb_ref[h] = jnp.broadcast_to(iw_ref[:, h:h + 1].astype(F32), (t, LANES))

    def score_chunk(kc, diag):
        off = pl.multiple_of(kc * t, t)
        d = _dot_nt(qi_ref[...], ik_ref[pl.ds(off, t), :]).reshape(hi, t, t)
        sc = jnp.zeros((t, t), F32)
        for h in range(hi):
            w = wb_ref[h]
            w = jnp.concatenate([w] * (t // LANES), axis=1)
            sc = sc + jnp.maximum(d[h], 0.0) * w
        key = _sortable_key(sc + 0.0)
        if diag:
            row = lax.broadcasted_iota(I32, (t, t), 0)
            col = lax.broadcasted_iota(I32, (t, t), 1)
            key = jnp.where(col <= row, key, jnp.int32(INT_MIN))
        key_ref[:, pl.ds(off, t)] = key
        keyt_ref[pl.ds(off, t), :] = key.T

    def score_body(kc, carry):
        score_chunk(kc, False)
        return carry

    lax.fori_loop(0, qi, score_body, 0)
    score_chunk(qi, True)

    @pl.when(qi % 2 == 0)
    def _():
        off = pl.multiple_of((qi + 1) * t, t)
        key_ref[:, pl.ds(off, t)] = jnp.full((t, t), INT_MIN, I32)
        keyt_ref[pl.ds(off, t), :] = jnp.full((t, t), INT_MIN, I32)

    n_steps = (qi + 2) // 2

    def bit_body(i, cur):
        bit = lax.shift_left(jnp.int32(1), 31 - i)
        cand = cur | bit
        cand_s = (cand ^ jnp.int32(INT_MIN))[None]

        def body(kc, cnt):
            off = pl.multiple_of(kc * tk, tk)
            k = keyt_ref[pl.ds(off, tk), :].reshape(tk // 8, 8, t)
            return cnt + jnp.sum(jnp.where(k >= cand_s, 1, 0), axis=0)

        cnt = lax.fori_loop(0, n_steps, body, jnp.zeros((8, t), I32))
        total = jnp.sum(cnt, axis=0, keepdims=True)
        return jnp.where(total >= topk, cand, cur)

    cur = lax.fori_loop(0, 32, bit_body, jnp.zeros((8, t), I32))
    thr = jnp.maximum(cur ^ jnp.int32(INT_MIN), jnp.int32(INT_MIN + 1))
    thr_b = jnp.broadcast_to(thr[0:1, :], (LANES, t)).T
    thr_w = jnp.concatenate([thr_b] * (tk // LANES), axis=1)

    def mask_body(kc, carry):
        off = pl.multiple_of(kc * tk, tk)
        am = jnp.where(key_ref[:, pl.ds(off, tk)] >= thr_w, 0.0, MASK_VALUE).astype(F32)
        key_ref[:, pl.ds(off, tk)] = lax.bitcast_convert_type(am, I32)
        return carry

    lax.fori_loop(0, n_steps, mask_body, 0)

    for h in range(hb):
        qh = sq_ref[:, h * HEAD_DIM_DSA:(h + 1) * HEAD_DIM_DSA]
        ql = jnp.dot(qh, wuk_ref[h], preferred_element_type=F32) * scale
        ql_ref[h * t:(h + 1) * t, :] = ql.astype(BF16)
    _softmax_init(m_ref, l_ref, acc_ref)
    d0 = lambda g: bias_ref[0, g]
    d1 = lambda g: bias_ref[1, g]

    s_refs = (s0_ref, s1_ref)

    def issue(kc, slot):
        off = pl.multiple_of(kc * tk, tk)
        am = lax.bitcast_convert_type(key_ref[:, pl.ds(off, tk)], F32)
        s = _dot_nt(ql_ref[...], c_ref[pl.ds(off, tk), :]).reshape(hb, t, tk) + am[None]
        s_refs[slot][...] = s.reshape(hb * t, tk)

    def finish(kc, slot, width=tk, delta=None):
        if delta is not None:
            _near_bias(s_refs[slot], d0, d1, delta, hb, t, width)
        off = pl.multiple_of(kc * tk, tk)
        _softmax_step(s_refs[slot], width, c_ref[pl.ds(off, width), :], m_ref, l_ref, acc_ref)

    _pipelined_far(jnp.maximum((qi - 1) // 2, 0), issue, finish)
    half = qi // 2

    @pl.when(qi % 2 == 1)
    def _():
        finish(half, 0, tk, t // LANES)

    @pl.when((qi % 2 == 0) & (half >= 1))
    def _():
        issue(half, 1)
        finish(half - 1, 0, tk, tk // LANES)
        finish(half, 1, t, 0)

    @pl.when(qi == 0)
    def _():
        finish(0, 0, t, 0)

    ol = _softmax_result(l_ref, acc_ref).astype(BF16)
    for h in range(hb):
        o = jnp.dot(ol[h * t:(h + 1) * t, :], wuv_ref[h], preferred_element_type=F32)
        o_ref[:, h * HEAD_DIM_DSA:(h + 1) * HEAD_DIM_DSA] = o.astype(o_ref.dtype)


def _dsa_attention(proj, kv_g, w_ukt, w_uv, bias_blocks, batch, seq, topk):
    t = DSA_BLOCK
    nq = seq // t
    hb, hi = N_HEADS_DSA, N_HEADS_IDX
    width = hb * HEAD_DIM_DSA
    kernel = functools.partial(_dsa_kernel, topk=topk, scale=HEAD_DIM_DSA ** -0.5 * LOG2E)
    return pl.pallas_call(
        kernel,
        out_shape=jax.ShapeDtypeStruct((batch * seq, width), BF16),
        grid=(batch, nq),
        in_specs=[pl.BlockSpec((t, 1024), lambda b, qi: (b * nq + qi, COL_IQ // 1024)),
                  pl.BlockSpec((t, 1024), lambda b, qi: (b * nq + qi, COL_SQ // 1024)),
                  pl.BlockSpec((t, LANES), lambda b, qi: (b * nq + qi, COL_IW // LANES)),
                  pl.BlockSpec((seq, LANES), lambda b, qi: (b, COL_IK // LANES)),
                  pl.BlockSpec((seq, KV_LATENT), lambda b, qi: (b, COL_CKV // KV_LATENT)),
                  pl.BlockSpec((1, KV_LATENT), lambda b, qi: (0, 0)),
                  pl.BlockSpec((hb, HEAD_DIM_DSA, KV_LATENT), lambda b, qi: (0, 0, 0)),
                  pl.BlockSpec((hb, KV_LATENT, HEAD_DIM_DSA), lambda b, qi: (0, 0, 0)),
                  pl.BlockSpec((2, hb, LANES, LANES), lambda b, qi: (0, 0, 0, 0))],
        out_specs=pl.BlockSpec((t, width), lambda b, qi: (b * nq + qi, 0)),
        scratch_shapes=[pltpu.VMEM((seq, KV_LATENT), BF16),
                        pltpu.VMEM((t, seq), I32),
                        pltpu.VMEM((seq, t), I32),
                        pltpu.VMEM((hi * t, LANES), BF16),
                        pltpu.VMEM((hi, t, LANES), F32),
                        pltpu.VMEM((hb * t, KV_LATENT), BF16),
                        pltpu.VMEM((hb * t, KEY_CHUNK), F32),
                        pltpu.VMEM((hb * t, KEY_CHUNK), F32),
                        pltpu.VMEM((hb * t, LANES), F32),
                        pltpu.VMEM((hb * t, LANES), F32),
                        pltpu.VMEM((hb * t, KV_LATENT), F32)],
        compiler_params=_cparams(("parallel", "arbitrary")),
        name="dsa_attn",
    )(proj, proj, proj, proj, proj, kv_g, w_ukt, w_uv, bias_blocks)


def _pack_halves(x):
    c = x.shape[1] // 2
    lo = lax.bitcast_convert_type(x[:, :c].astype(BF16).astype(F32), I32)
    hi = lax.bitcast_convert_type(x[:, c:].astype(BF16).astype(F32), I32)
    return lax.shift_right_logical(lo, 16) | (hi & jnp.int32(-65536))


def _unpack_halves(w):
    lo = lax.bitcast_convert_type(lax.shift_left(w, 16), F32)
    hi = lax.bitcast_convert_type(w & jnp.int32(-65536), F32)
    return lo, hi


def _merge_kernel(x_ref, yd_ref, ys_ref, ga_ref, gb_ref, wd_ref, ws_ref, wo_ref, g_ref, wr_ref, br_ref,
                  h_ref, hn_ref, route_ref):
    bd = jnp.dot(yd_ref[...], wd_ref[...], preferred_element_type=F32)
    bs = jnp.dot(ys_ref[...], ws_ref[...], preferred_element_type=F32)
    merged = (jax.nn.sigmoid(ga_ref[...].astype(F32)) * bd + jax.nn.sigmoid(gb_ref[...].astype(F32)) * bs)
    h = x_ref[...] + jnp.dot(merged.astype(BF16), wo_ref[...], preferred_element_type=F32)
    h_ref[...] = h
    ms = jnp.mean(h * h, axis=-1, keepdims=True)
    hn = h * lax.rsqrt(ms + EPS) * g_ref[...]
    hn_ref[...] = _pack_halves(hn)

    logits = jnp.dot(hn, wr_ref[...], preferred_element_type=F32, precision=lax.Precision.HIGHEST)
    logits = logits + br_ref[...]
    lane = lax.broadcasted_iota(I32, logits.shape, 1)
    vals, ids = [], []
    for _ in range(TOP_K_EXPERTS):
        mx = jnp.max(logits, axis=-1, keepdims=True)
        ix = jnp.min(jnp.where(logits == mx, lane, LANES), axis=-1, keepdims=True)
        vals.append(mx)
        ids.append(ix)
        logits = jnp.where(lane == ix, -jnp.inf, logits)
    es = [jnp.exp(v - vals[0]) for v in vals]
    inv = 1.0 / (es[0] + es[1] + es[2] + es[3])
    route = jnp.zeros(logits.shape, F32)
    for k in range(TOP_K_EXPERTS):
        route = jnp.where(lane == k, es[k] * inv, route)
        route = jnp.where(lane == TOP_K_EXPERTS + k, ids[k].astype(F32), route)
    route_ref[...] = route


def _merge(x2, y_diff, y_dsa, proj, w_bd, w_bs, w_out, g_ffn, w_router, b_router, tm):
    n, d = x2.shape
    row = lambda i: (i, 0)
    const = lambda i: (0, 0)
    return pl.pallas_call(
        _merge_kernel,
        out_shape=(jax.ShapeDtypeStruct((n, d), F32),
                   jax.ShapeDtypeStruct((n, d // 2), I32),
                   jax.ShapeDtypeStruct((n, LANES), F32)),
        grid=(n // tm,),
        in_specs=[pl.BlockSpec((tm, d), row),
                  pl.BlockSpec((tm, d), row),
                  pl.BlockSpec((tm, d), row),
                  pl.BlockSpec((tm, d), lambda i: (i, COL_GA // 1024)),
                  pl.BlockSpec((tm, d), lambda i: (i, COL_GB // 1024)),
                  pl.BlockSpec((d, d), const),
                  pl.BlockSpec((d, d), const),
                  pl.BlockSpec((d, d), const),
                  pl.BlockSpec((1, d), const),
                  pl.BlockSpec((d, LANES), const),
                  pl.BlockSpec((1, LANES), const)],
        out_specs=(pl.BlockSpec((tm, d), row),
                   pl.BlockSpec((tm, d // 2), row),
                   pl.BlockSpec((tm, LANES), row)),
        compiler_params=_cparams(("parallel",)),
        name="merge_router",
    )(x2, y_diff, y_dsa, proj, proj, w_bd, w_bs, w_out, g_ffn, w_router, b_router)


def _regroup_kernel(w_ref, p_ref, o_ref):
    pw = p_ref.shape[0]
    for j in range(w_ref.shape[2] // pw):
        w = w_ref[0, :, j * pw:(j + 1) * pw].astype(BF16)
        o_ref[0, :, j * pw:(j + 1) * pw] = jnp.dot(w, p_ref[...], preferred_element_type=F32).astype(BF16)


def _regroup_gate_up(w_gu, rows):
    e, d, f2 = w_gu.shape
    pw = 2 * LANES
    src = jnp.arange(pw, dtype=I32)
    dst = (src % 2) * LANES + src // 2
    perm = (dst[:, None] == jnp.arange(pw, dtype=I32)[None, :]).astype(BF16)
    return pl.pallas_call(
        _regroup_kernel,
        out_shape=jax.ShapeDtypeStruct((e, d, f2), BF16),
        grid=(e, d // rows),
        in_specs=[pl.BlockSpec((1, rows, f2), lambda i, j: (i, j, 0)),
                  pl.BlockSpec((pw, pw), lambda i, j: (0, 0))],
        out_specs=pl.BlockSpec((1, rows, f2), lambda i, j: (i, j, 0)),
        compiler_params=_cparams(("parallel", "parallel")),
        name="regroup_gate_up",
    )(w_gu, perm)


def _ffn_kernel(be_ref, nu_ref, x_ref, wgu_ref, wd_ref, bgu_ref, bd_ref, o_ref):
    @pl.when(pl.program_id(0) < nu_ref[0])
    def _():
        x_lo, x_hi = _unpack_halves(x_ref[...])
        half = x_lo.shape[1]
        gu = (jnp.dot(x_lo.astype(BF16), wgu_ref[0, 0:half, :], preferred_element_type=F32)
              + jnp.dot(x_hi.astype(BF16), wgu_ref[0, half:2 * half, :], preferred_element_type=F32)
              + bgu_ref[0])
        acts = []
        for j in range(gu.shape[1] // (2 * LANES)):
            gate = jnp.minimum(gu[:, 2 * j * LANES:(2 * j + 1) * LANES], SWIGLU_LIMIT)
            up = jnp.clip(gu[:, (2 * j + 1) * LANES:(2 * j + 2) * LANES], -SWIGLU_LIMIT, SWIGLU_LIMIT)
            glu = gate * jax.nn.sigmoid(gate * SWIGLU_ALPHA)
            acts.append(((up + 1.0) * glu).astype(BF16))
        a = jnp.concatenate(acts, axis=1)
        y = jnp.dot(a, wd_ref[0], preferred_element_type=F32) + bd_ref[0]
        o_ref[...] = _pack_halves(y)

    @pl.when(pl.program_id(0) >= nu_ref[0])
    def _():
        o_ref[...] = jnp.zeros(o_ref.shape, o_ref.dtype)


def _expert_ffn(blk_exp, n_used, xs, wgu, wd, bgu, bd):
    p, dw = xs.shape
    f, d = wd.shape[1], wd.shape[2]
    nblk = p // MOE_ROWS
    wmap = lambda i, be, nu: (be[i], 0, 0)
    grid_spec = pltpu.PrefetchScalarGridSpec(
        num_scalar_prefetch=2,
        grid=(nblk,),
        in_specs=[pl.BlockSpec((MOE_ROWS, dw), lambda i, be, nu: (i, 0)),
                  pl.BlockSpec((1, d, 2 * f), wmap),
                  pl.BlockSpec((1, f, d), wmap),
                  pl.BlockSpec((1, 1, 2 * f), wmap),
                  pl.BlockSpec((1, 1, d), wmap)],
        out_specs=pl.BlockSpec((MOE_ROWS, dw), lambda i, be, nu: (i, 0)),
    )
    return pl.pallas_call(
        _ffn_kernel,
        out_shape=jax.ShapeDtypeStruct((p, dw), I32),
        grid_spec=grid_spec,
        compiler_params=_cparams(("arbitrary",)),
        name="expert_ffn",
    )(blk_exp, n_used, xs, wgu, wd, bgu, bd)


def _route_kernel(route_ref, dest_ref, cnt_ref, u_ref, carry_ref, pstart_ref, *, block_rows):
    ph, i = pl.program_id(0), pl.program_id(1)
    tm = route_ref.shape[0]

    @pl.when((ph == 0) & (i == 0))
    def _():
        r = lax.broadcasted_iota(I32, (tm, tm), 0)
        c = lax.broadcasted_iota(I32, (tm, tm), 1)
        u_ref[...] = jnp.where(r < c, 1.0, 0.0).astype(BF16)
        carry_ref[...] = jnp.zeros(carry_ref.shape, F32)

    @pl.when((ph == 1) & (i == 0))
    def _():
        counts = carry_ref[...]
        cnt_ref[...] = counts
        padded = jnp.ceil(counts * (1.0 / block_rows)) * block_rows
        r = lax.broadcasted_iota(I32, (LANES, LANES), 0)
        c = lax.broadcasted_iota(I32, (LANES, LANES), 1)
        lower = jnp.where(c < r, 1.0, 0.0).astype(F32)
        pstart_ref[...] = jnp.dot(lower, padded, preferred_element_type=F32, precision=lax.Precision.HIGHEST)
        carry_ref[...] = jnp.zeros(carry_ref.shape, F32)

    rt = route_ref[...].T
    sub = lax.broadcasted_iota(I32, (LANES, tm), 0)
    hits = [sub == rt[TOP_K_EXPERTS + k:TOP_K_EXPERTS + k + 1, :].astype(I32) for k in range(TOP_K_EXPERTS)]
    m = jnp.zeros((LANES, tm), F32)
    for hit in hits:
        m = m + jnp.where(hit, 1.0, 0.0)
    tile_counts = jnp.broadcast_to(jnp.sum(m, axis=1, keepdims=True), (LANES, LANES))

    @pl.when(ph == 0)
    def _():
        dest_ref[...] = jnp.zeros(dest_ref.shape, I32)

    @pl.when(ph == 1)
    def _():
        prefix = jnp.dot(m.astype(BF16), u_ref[...], preferred_element_type=F32)
        rank = prefix + (pstart_ref[:, 0:1] + carry_ref[:, 0:1])
        rows = [jnp.sum(jnp.where(hit, rank, 0.0), axis=0, keepdims=True) for hit in hits]
        rows.append(jnp.zeros((dest_ref.shape[0] - TOP_K_EXPERTS, tm), F32))
        dest_ref[...] = jnp.concatenate(rows, axis=0).astype(I32)

    carry_ref[...] = carry_ref[...] + tile_counts


def _route_rows(route, tm):
    n = route.shape[0]
    nt = n // tm
    kernel = functools.partial(_route_kernel, block_rows=MOE_ROWS)
    return pl.pallas_call(
        kernel,
        out_shape=(jax.ShapeDtypeStruct((8, n + tm), I32), jax.ShapeDtypeStruct((LANES, LANES), F32)),
        grid=(2, nt),
        in_specs=[pl.BlockSpec((tm, LANES), lambda ph, i: (i, 0))],
        out_specs=(pl.BlockSpec((8, tm), lambda ph, i: (0, ph * i + (1 - ph) * nt)),
                   pl.BlockSpec((LANES, LANES), lambda ph, i: (0, 0))),
        scratch_shapes=[pltpu.VMEM((tm, tm), BF16),
                        pltpu.VMEM((LANES, LANES), F32),
                        pltpu.VMEM((LANES, LANES), F32)],
        compiler_params=_cparams(("arbitrary", "arbitrary")),
        name="route_rows",
    )(route)


SC_WINDOW = 128
SC_WORKERS = 32


def _sc_mesh():
    return plsc.VectorSubcoreMesh(core_axis_name="c", subcore_axis_name="s")


def _sc_scatter_rows(src, dest, p):
    n, d = src.shape
    per = n // (SC_WINDOW * SC_WORKERS)

    @pl.kernel(out_type=jax.ShapeDtypeStruct((p, d), src.dtype), mesh=_sc_mesh(),
               scratch_types=[pltpu.VMEM((dest.shape[0], SC_WINDOW), I32), pltpu.VMEM((SC_WINDOW, d), src.dtype)])
    def scatter(src_hbm, idx_hbm, out_hbm, idx_vmem, buf):
        wid = lax.axis_index("c") * (SC_WORKERS // 2) + lax.axis_index("s")

        @pl.loop(0, per)
        def _(j):
            off = (wid * per + j) * SC_WINDOW
            pltpu.sync_copy(idx_hbm.at[:, pl.ds(off, SC_WINDOW)], idx_vmem)
            pltpu.sync_copy(src_hbm.at[pl.ds(off, SC_WINDOW), :], buf)
            for k in range(TOP_K_EXPERTS):
                pltpu.sync_copy(buf, out_hbm.at[idx_vmem.at[k]])

    return scatter(src, dest)


def _sc_gather_rows(src, dest):
    n = dest.shape[1]
    d = src.shape[1]
    per = n // (SC_WINDOW * SC_WORKERS)

    @pl.kernel(out_type=jax.ShapeDtypeStruct((TOP_K_EXPERTS * n, d), src.dtype), mesh=_sc_mesh(),
               scratch_types=[pltpu.VMEM((dest.shape[0], SC_WINDOW), I32), pltpu.VMEM((SC_WINDOW, d), src.dtype)])
    def gather(src_hbm, idx_hbm, out_hbm, idx_vmem, buf):
        wid = lax.axis_index("c") * (SC_WORKERS // 2) + lax.axis_index("s")

        @pl.loop(0, per)
        def _(j):
            off = (wid * per + j) * SC_WINDOW
            pltpu.sync_copy(idx_hbm.at[:, pl.ds(off, SC_WINDOW)], idx_vmem)
            for k in range(TOP_K_EXPERTS):
                pltpu.sync_copy(src_hbm.at[idx_vmem.at[k]], buf)
                pltpu.sync_copy(buf, out_hbm.at[pl.ds(k * n + off, SC_WINDOW), :])

    return gather(src, dest)


def _combine_kernel(h_ref, y_ref, route_ref, g_ref, o_ref):
    half = h_ref.shape[1] // 2
    h_lo, h_hi = h_ref[:, 0:half], h_ref[:, half:2 * half]
    route = route_ref[...]
    for k in range(TOP_K_EXPERTS):
        y_lo, y_hi = _unpack_halves(y_ref[k])
        gate = route[:, k:k + 1]
        h_lo = h_lo + gate * y_lo
        h_hi = h_hi + gate * y_hi
    ms = (jnp.sum(h_lo * h_lo, axis=-1, keepdims=True)
          + jnp.sum(h_hi * h_hi, axis=-1, keepdims=True)) * (1.0 / (2 * half))
    inv = lax.rsqrt(ms + EPS)
    o_ref[:, 0:half] = h_lo * inv * g_ref[:, 0:half]
    o_ref[:, half:2 * half] = h_hi * inv * g_ref[:, half:2 * half]


def _combine(h1, yg, route, g_final, tm):
    n, d = h1.shape
    return pl.pallas_call(
        _combine_kernel,
        out_shape=jax.ShapeDtypeStruct((n, d), F32),
        grid=(n // tm,),
        in_specs=[pl.BlockSpec((tm, d), lambda i: (i, 0)),
                  pl.BlockSpec((TOP_K_EXPERTS, tm, d // 2), lambda i: (0, i, 0)),
                  pl.BlockSpec((tm, LANES), lambda i: (i, 0)),
                  pl.BlockSpec((1, d), lambda i: (0, 0))],
        out_specs=pl.BlockSpec((tm, d), lambda i: (i, 0)),
        compiler_params=_cparams(("parallel",)),
        name="combine_norm",
    )(h1, yg, route, g_final)


def _t5_bucket(dist):
    n = jnp.maximum(dist, 0)
    max_exact = N_BUCKETS // 2
    nf = jnp.maximum(n, 1).astype(F32)
    large = max_exact + (jnp.log(nf / max_exact) / math.log(MAX_DISTANCE / max_exact)
                         * (N_BUCKETS - max_exact)).astype(I32)
    large = jnp.minimum(large, N_BUCKETS - 1)
    return jnp.where(n < max_exact, n, large)


def _bias_blocks(bias_tab):
    t = LANES
    assert MAX_DISTANCE <= LANES
    r = jnp.arange(t, dtype=I32)[:, None]
    c = jnp.arange(t, dtype=I32)[None, :]
    rel = (bias_tab - bias_tab[N_BUCKETS - 1][None, :]).astype(F32)
    tiles = []
    buckets = jnp.arange(N_BUCKETS, dtype=I32)[:, None, None]
    for delta in (0, t):
        dist = r - c + delta
        hit = _t5_bucket(dist)[None] == buckets
        b = jnp.sum(jnp.where(hit[:, None], rel[:, :, None, None], 0.0), axis=0)
        tiles.append(jnp.where((dist >= 0)[None], b * LOG2E, MASK_VALUE))
    return jnp.stack(tiles)


def _regroup_w_in(w_in):
    sizes = (1024, 1024, 1024, 1024, KV_LATENT, 1024, HEAD_DIM_IDX, N_HEADS_IDX, D_MODEL, D_MODEL)
    parts, off = [], 0
    for sz in sizes:
        parts.append(w_in[:, off:off + sz])
        off += sz
    dq, dk, dv, sq, ckv, iq, ik, iw, ga, gb = parts
    dq = dq * (HEAD_DIM_DIFF ** -0.5 * LOG2E)
    iw = iw * ((N_HEADS_IDX ** -0.5) * (HEAD_DIM_IDX ** -0.5))
    pad = jnp.zeros((w_in.shape[0], PROJ_WIDTH - COL_IW - N_HEADS_IDX), w_in.dtype)
    w = jnp.concatenate([dq, dk, dv, sq, iq, ga, gb, ckv, ik, ik, iw, pad], axis=1)
    return w.astype(BF16)


def _block_tables(counts, n_assign):
    e, bm = N_EXPERTS, MOE_ROWS
    padded = (counts + bm - 1) // bm * bm
    pends = jnp.cumsum(padded)
    nblk = -(-(n_assign + e * (bm - 1)) // bm)
    first_row = jnp.arange(nblk, dtype=I32) * bm
    blk_exp = jnp.minimum(jnp.sum((pends[None, :] <= first_row[:, None]).astype(I32), axis=1), e - 1)
    n_used = (pends[-1] // bm).astype(I32).reshape(1)
    return blk_exp, n_used, nblk


def kernel(x, norm_attn_g, w_in, rel_bias, lam_q1, lam_k1, lam_q2, lam_k2, diff_subln_g, kv_norm_g, w_uk, w_uv,
           w_branch_diff, w_branch_dsa, w_out, norm_ffn_g, w_router, b_router, w_gate_up, b_gate_up, w_down,
           b_down, norm_final_g):
    batch, seq, d = x.shape
    n = batch * seq
    assert norm_attn_g.shape[0] == 1, "single-layer kernel"
    assert seq % DIFF_BLOCK == 0 and seq % KEY_CHUNK == 0 and d == D_MODEL
    row_tile = math.gcd(n, 1024)

    x2 = x.reshape(n, d)
    proj = _inproj(x2, norm_attn_g[0].reshape(1, d), _regroup_w_in(w_in[0]), row_tile, 1280)

    lam_init = 0.8 - 0.6 * math.exp(-0.3 * 0)
    lam = (jnp.exp(jnp.sum(lam_q1[0].astype(F32) * lam_k1[0].astype(F32)))
           - jnp.exp(jnp.sum(lam_q2[0].astype(F32) * lam_k2[0].astype(F32))) + lam_init)
    y_diff = _diff_attention(proj, lam.reshape(1, 1).astype(F32), _bias_blocks(rel_bias[:, :N_HEADS_DIFF]),
                             diff_subln_g[0].reshape(1, -1).astype(F32), batch, seq, 1.0 - lam_init)

    y_dsa = _dsa_attention(proj, kv_norm_g[0].reshape(1, -1).astype(F32),
                           w_uk[0].transpose(0, 2, 1).astype(BF16), w_uv[0].astype(BF16),
                           _bias_blocks(rel_bias[:, N_HEADS_DIFF:]), batch, seq, min(TOPK_MAX, seq // 4))

    w_r = jnp.zeros((d, LANES), F32).at[:, :N_EXPERTS].set(w_router[0].astype(F32))
    b_r = jnp.full((1, LANES), MASK_VALUE, F32).at[0, :N_EXPERTS].set(b_router[0].astype(F32))
    h1, hn, route = _merge(x2, y_diff, y_dsa, proj, w_branch_diff[0].astype(BF16), w_branch_dsa[0].astype(BF16),
                           w_out[0].astype(BF16), norm_ffn_g[0].reshape(1, d).astype(F32), w_r, b_r,
                           math.gcd(n, 512))

    dest, counts = _route_rows(route, math.gcd(n, 1024))
    dest = dest[:, :n]
    blk_exp, n_used, nblk = _block_tables(counts[:N_EXPERTS, 0].astype(I32), n * TOP_K_EXPERTS)
    xs = _sc_scatter_rows(hn, dest, nblk * MOE_ROWS)
    e, f = N_EXPERTS, D_EXPERT
    b_gu = b_gate_up[0].astype(F32).reshape(e, f // LANES, LANES, 2).transpose(0, 1, 3, 2).reshape(e, 1, 2 * f)
    ys = _expert_ffn(blk_exp, n_used, xs, _regroup_gate_up(w_gate_up[0], 512), w_down[0].astype(BF16),
                     b_gu, b_down[0][:, None, :].astype(F32))
    yg = _sc_gather_rows(ys, dest).reshape(TOP_K_EXPERTS, n, d // 2)
    out = _combine(h1, yg, route, norm_final_g.reshape(1, d).astype(F32), math.gcd(n, 512))
    return out.reshape(batch, seq, d)
```

```python
import functools
import math

import jax
import jax.numpy as jnp
from jax import lax
from jax.experimental import pallas as pl
from jax.experimental.pallas import tpu as pltpu
from jax.experimental.pallas import tpu_sc as plsc

F32 = jnp.float32
BF16 = jnp.bfloat16
I32 = jnp.int32

D_MODEL = 1024
N_HEADS_DIFF = 8
HEAD_DIM_DIFF = 64
N_HEADS_DSA = 8
HEAD_DIM_DSA = 128
KV_LATENT = 256
N_HEADS_IDX = 16
HEAD_DIM_IDX = 64
TOPK_MAX = 256
N_BUCKETS = 32
MAX_DISTANCE = 128
N_EXPERTS = 32
TOP_K_EXPERTS = 4
D_EXPERT = 1024
SWIGLU_LIMIT = 7.0
SWIGLU_ALPHA = 1.702
EPS = 1e-6

LANES = 128
DIFF_BLOCK = 512
DSA_BLOCK = 256
KEY_CHUNK = 512
MOE_ROWS = 512
PROJ_WIDTH = 7680
VMEM_LIMIT = 56 * 1024 * 1024

COL_DQ, COL_DK, COL_DV, COL_SQ, COL_IQ, COL_GA, COL_GB = (i * 1024 for i in range(7))
COL_CKV = 7168
COL_IK = 7424
COL_IW = 7552

LOG2E = math.log2(math.e)
MASK_VALUE = -1e30
M_INIT = -1e29
INT_MIN = -2 ** 31


def _cparams(sem):
    return pltpu.CompilerParams(dimension_semantics=sem, vmem_limit_bytes=VMEM_LIMIT)


def _inproj_kernel(x_ref, g_ref, w_ref, o_ref, xn_ref):
    @pl.when(pl.program_id(1) == 0)
    def _():
        x = x_ref[...]
        ms = jnp.mean(x * x, axis=-1, keepdims=True)
        xn_ref[...] = (x * lax.rsqrt(ms + EPS) * g_ref[...]).astype(BF16)

    o_ref[...] = jnp.dot(xn_ref[...], w_ref[...], preferred_element_type=F32).astype(o_ref.dtype)


def _inproj(x2, g, w, tm, tn):
    n, d = x2.shape
    width = w.shape[1]
    return pl.pallas_call(
        _inproj_kernel,
        out_shape=jax.ShapeDtypeStruct((n, width), BF16),
        grid=(n // tm, width // tn),
        in_specs=[pl.BlockSpec((tm, d), lambda i, j: (i, 0)),
                  pl.BlockSpec((1, d), lambda i, j: (0, 0)),
                  pl.BlockSpec((d, tn), lambda i, j: (0, j))],
        out_specs=pl.BlockSpec((tm, tn), lambda i, j: (i, j)),
        scratch_shapes=[pltpu.VMEM((tm, d), BF16)],
        compiler_params=_cparams(("parallel", "arbitrary")),
        name="inproj",
    )(x2, g, w)


def _lane_chunk_max(s):
    smax = s[:, 0:LANES]
    for j in range(1, s.shape[1] // LANES):
        smax = jnp.maximum(smax, s[:, j * LANES:(j + 1) * LANES])
    return smax


def _softmax_step(s_ref, tk, v, m_ref, l_ref, acc_ref, smax=None):
    nl = tk // LANES
    if smax is None:
        smax = _lane_chunk_max(s_ref[:, 0:tk])
    m_prev = m_ref[...]
    m_new = jnp.maximum(m_prev, jnp.max(smax, axis=-1, keepdims=True))
    alpha = jnp.exp2(m_prev - m_new)
    psum = None
    ps = []
    for j in range(nl):
        pj = jnp.exp2(s_ref[:, j * LANES:(j + 1) * LANES] - m_new)
        psum = pj if psum is None else psum + pj
        ps.append(pj.astype(BF16))
    l_ref[...] = alpha * l_ref[...] + psum
    pv = _dot_split(jnp.concatenate(ps, axis=1), v)
    e = acc_ref.shape[1]
    a = alpha if e == LANES else jnp.concatenate([alpha] * (e // LANES), axis=1)
    acc_ref[...] = a * acc_ref[...] + pv
    m_ref[...] = m_new


def _softmax_init(m_ref, l_ref, acc_ref):
    m_ref[...] = jnp.full(m_ref.shape, M_INIT, F32)
    l_ref[...] = jnp.zeros(l_ref.shape, F32)
    acc_ref[...] = jnp.zeros(acc_ref.shape, F32)


def _softmax_result(l_ref, acc_ref):
    return acc_ref[...] * (1.0 / jnp.sum(l_ref[...], axis=-1, keepdims=True))


def _near_bias(s_ref, d0, d1, delta, groups, t, tk):
    for g in range(groups):
        for rb in range(t // LANES):
            for cb in range(tk // LANES):
                bd = delta + rb - cb
                rows = slice(g * t + rb * LANES, g * t + (rb + 1) * LANES)
                cols = slice(cb * LANES, (cb + 1) * LANES)
                if bd == 0:
                    s_ref[rows, cols] = s_ref[rows, cols] + d0(g)
                elif bd == 1:
                    s_ref[rows, cols] = s_ref[rows, cols] + d1(g)
                elif bd < 0:
                    s_ref[rows, cols] = jnp.full((LANES, LANES), MASK_VALUE, F32)


def _pipelined_far(n_far, issue, finish):
    odd = n_far % 2

    @pl.when(odd == 1)
    def _():
        issue(0, 1)
        issue(1, 0)
        finish(0, 1)

    @pl.when(odd == 0)
    def _():
        issue(0, 0)

    def body(j, carry):
        a = odd + 2 * j
        issue(a + 1, 1)
        finish(a, 0)
        issue(a + 2, 0)
        finish(a + 1, 1)
        return carry

    lax.fori_loop(0, (n_far - odd) // 2, body, 0)


def _dot_nt(a, b):
    h = a.shape[0] // 2
    dn = (((1,), (1,)), ((), ()))
    return jnp.concatenate([lax.dot_general(a[:h], b, dn, preferred_element_type=F32),
                            lax.dot_general(a[h:], b, dn, preferred_element_type=F32)], axis=0)


def _dot_split(a, b):
    h = a.shape[0] // 2
    return jnp.concatenate([jnp.dot(a[:h], b, preferred_element_type=F32),
                            jnp.dot(a[h:], b, preferred_element_type=F32)], axis=0)


def _diff_kernel(lam_ref, q_ref, k_ref, v_ref, bias_ref, g_ref, o_ref, q2_ref, s_ref, m_ref, l_ref, acc_ref,
                 *, out_scale):
    t = q_ref.shape[0]
    qi = pl.program_id(2)

    q = q_ref[...]
    lane = lax.broadcasted_iota(I32, q.shape, 1)
    zero = jnp.zeros_like(q)
    q2_ref[0:t, :] = jnp.where(lane < HEAD_DIM_DIFF, q, zero)
    q2_ref[t:2 * t, :] = jnp.where(lane >= HEAD_DIM_DIFF, q, zero)
    _softmax_init(m_ref, l_ref, acc_ref)
    d0 = lambda g: bias_ref[0, 0]
    d1 = lambda g: bias_ref[1, 0]

    def chunk(kc, delta):
        off = pl.multiple_of(kc * t, t)
        s = _dot_nt(q2_ref[...], k_ref[pl.ds(off, t), :])
        s_ref[...] = s
        smax = None
        if delta is None:
            smax = _lane_chunk_max(s)
        else:
            _near_bias(s_ref, d0, d1, delta, 2, t, t)
        _softmax_step(s_ref, t, v_ref[pl.ds(off, t), :], m_ref, l_ref, acc_ref, smax)

    def far_body(kc, carry):
        chunk(kc, None)
        return carry

    lax.fori_loop(0, jnp.maximum(qi - 1, 0), far_body, 0)

    @pl.when(qi >= 1)
    def _():
        chunk(qi - 1, t // LANES)

    chunk(qi, 0)

    o = _softmax_result(l_ref, acc_ref)
    o = o[0:t, :] - lam_ref[0, 0] * o[t:2 * t, :]
    ms = jnp.mean(o * o, axis=-1, keepdims=True)
    o_ref[...] = (o * lax.rsqrt(ms + EPS) * g_ref[...] * out_scale).astype(o_ref.dtype)


def _diff_attention(proj, lam, bias_blocks, subln_g, batch, seq, out_scale):
    t = DIFF_BLOCK
    nq = seq // t
    h = N_HEADS_DIFF
    e = 2 * HEAD_DIM_DIFF
    kernel = functools.partial(_diff_kernel, out_scale=out_scale)
    return pl.pallas_call(
        kernel,
        out_shape=jax.ShapeDtypeStruct((batch * seq, h * e), BF16),
        grid=(batch, h, nq),
        in_specs=[pl.BlockSpec(memory_space=pltpu.SMEM),
                  pl.BlockSpec((t, e), lambda b, hh, qi: (b * nq + qi, COL_DQ // e + hh)),
                  pl.BlockSpec((seq, e), lambda b, hh, qi: (b, COL_DK // e + hh)),
                  pl.BlockSpec((seq, e), lambda b, hh, qi: (b, COL_DV // e + hh)),
                  pl.BlockSpec((2, 1, LANES, LANES), lambda b, hh, qi: (0, hh, 0, 0)),
                  pl.BlockSpec((1, e), lambda b, hh, qi: (0, 0))],
        out_specs=pl.BlockSpec((t, e), lambda b, hh, qi: (b * nq + qi, hh)),
        scratch_shapes=[pltpu.VMEM((2 * t, e), BF16),
                        pltpu.VMEM((2 * t, t), F32),
                        pltpu.VMEM((2 * t, LANES), F32),
                        pltpu.VMEM((2 * t, LANES), F32),
                        pltpu.VMEM((2 * t, e), F32)],
        compiler_params=_cparams(("parallel", "parallel", "arbitrary")),
        name="diff_attn",
    )(lam, proj, proj, proj, bias_blocks, subln_g)


def _sortable_key(x):
    bits = lax.bitcast_convert_type(x, I32)
    return bits ^ ((bits >> 31) & jnp.int32(0x7FFFFFFF))


def _dsa_kernel(iq_ref, sq_ref, iw_ref, ik_ref, ckv_ref, kvg_ref, wuk_ref, wuv_ref, bias_ref, o_ref,
                c_ref, key_ref, keyt_ref, qi_ref, wb_ref, ql_ref, s0_ref, s1_ref, smax_ref, m_ref, l_ref, acc_ref, *, topk, scale):
    t = iq_ref.shape[0]
    tk = KEY_CHUNK
    qi = pl.program_id(1)
    n_chunks = qi + 1
    hi, hb = N_HEADS_IDX, N_HEADS_DSA

    @pl.when(qi == 0)
    def _():
        ckv = ckv_ref[...].astype(F32)
        ms = jnp.mean(ckv * ckv, axis=-1, keepdims=True)
        c_ref[...] = (ckv * lax.rsqrt(ms + EPS) * kvg_ref[...]).astype(BF16)

    lane = lax.broadcasted_iota(I32, (t, LANES), 1)
    for h in range(hi):
        blk = iq_ref[:, (h // 2) * LANES:(h // 2 + 1) * LANES]
        keep = (lane < HEAD_DIM_IDX) if h % 2 == 0 else (lane >= HEAD_DIM_IDX)
        qi_ref[h * t:(h + 1) * t, :] = jnp.where(keep, blk, jnp.zeros_like(blk))
        wb_ref[h] = jnp.broadcast_to(iw_ref[:, h:h + 1].astype(F32), (t, LANES))

    def score_chunk(kc, diag):
        off = pl.multiple_of(kc * t, t)
        d = _dot_nt(qi_ref[...], ik_ref[pl.ds(off, t), :]).reshape(hi, t, t)
        sc = jnp.zeros((t, t), F32)
        for h in range(hi):
            w = wb_ref[h]
            w = jnp.concatenate([w] * (t // LANES), axis=1)
            sc = sc + jnp.maximum(d[h], 0.0) * w
        key = _sortable_key(sc + 0.0)
        if diag:
            row = lax.broadcasted_iota(I32, (t, t), 0)
            col = lax.broadcasted_iota(I32, (t, t), 1)
            key = jnp.where(col <= row, key, jnp.int32(INT_MIN))
        key_ref[:, pl.ds(off, t)] = key
        keyt_ref[pl.ds(off, t), :] = key.T

    def score_body(kc, carry):
        score_chunk(kc, False)
        return carry

    lax.fori_loop(0, qi, score_body, 0)
    score_chunk(qi, True)

    @pl.when(qi % 2 == 0)
    def _():
        off = pl.multiple_of((qi + 1) * t, t)
        key_ref[:, pl.ds(off, t)] = jnp.full((t, t), INT_MIN, I32)
        keyt_ref[pl.ds(off, t), :] = jnp.full((t, t), INT_MIN, I32)

    n_steps = (qi + 2) // 2

    def bit_body(i, cur):
        bit = lax.shift_left(jnp.int32(1), 31 - i)
        cand = cur | bit
        cand_s = (cand ^ jnp.int32(INT_MIN))[None]

        def body(kc, cnt):
            off = pl.multiple_of(kc * tk, tk)
            k = keyt_ref[pl.ds(off, tk), :].reshape(tk // 8, 8, t)
            return cnt + jnp.sum(jnp.where(k >= cand_s, 1, 0), axis=0)

        cnt = lax.fori_loop(0, n_steps, body, jnp.zeros((8, t), I32))
        total = jnp.sum(cnt, axis=0, keepdims=True)
        return jnp.where(total >= topk, cand, cur)

    cur = lax.fori_loop(0, 32, bit_body, jnp.zeros((8, t), I32))
    thr = jnp.maximum(cur ^ jnp.int32(INT_MIN), jnp.int32(INT_MIN + 1))
    thr_b = jnp.broadcast_to(thr[0:1, :], (LANES, t)).T
    thr_w = jnp.concatenate([thr_b] * (tk // LANES), axis=1)

    def mask_body(kc, carry):
        off = pl.multiple_of(kc * tk, tk)
        am = jnp.where(key_ref[:, pl.ds(off, tk)] >= thr_w, 0.0, MASK_VALUE).astype(F32)
        key_ref[:, pl.ds(off, tk)] = lax.bitcast_convert_type(am, I32)
        return carry

    lax.fori_loop(0, n_steps, mask_body, 0)

    for h in range(hb):
        qh = sq_ref[:, h * HEAD_DIM_DSA:(h + 1) * HEAD_DIM_DSA]
        ql = jnp.dot(qh, wuk_ref[h], preferred_element_type=F32) * scale
        ql_ref[h * t:(h + 1) * t, :] = ql.astype(BF16)
    _softmax_init(m_ref, l_ref, acc_ref)
    d0 = lambda g: bias_ref[0, g]
    d1 = lambda g: bias_ref[1, g]

    s_refs = (s0_ref, s1_ref)

    def issue(kc, slot):
        off = pl.multiple_of(kc * tk, tk)
        am = lax.bitcast_convert_type(key_ref[:, pl.ds(off, tk)], F32)
        s = (_dot_nt(ql_ref[...], c_ref[pl.ds(off, tk), :]).reshape(hb, t, tk) + am[None]).reshape(hb * t, tk)
        s_refs[slot][...] = s
        smax_ref[slot] = _lane_chunk_max(s)

    def finish(kc, slot, width=tk, delta=None):
        smax = None
        if delta is None:
            smax = smax_ref[slot]
        else:
            _near_bias(s_refs[slot], d0, d1, delta, hb, t, width)
        off = pl.multiple_of(kc * tk, tk)
        _softmax_step(s_refs[slot], width, c_ref[pl.ds(off, width), :], m_ref, l_ref, acc_ref, smax)

    _pipelined_far(jnp.maximum((qi - 1) // 2, 0), issue, finish)
    half = qi // 2

    @pl.when(qi % 2 == 1)
    def _():
        finish(half, 0, tk, t // LANES)

    @pl.when((qi % 2 == 0) & (half >= 1))
    def _():
        issue(half, 1)
        finish(half - 1, 0, tk, tk // LANES)
        finish(half, 1, t, 0)

    @pl.when(qi == 0)
    def _():
        finish(0, 0, t, 0)

    ol = _softmax_result(l_ref, acc_ref).astype(BF16)
    for h in range(hb):
        o = jnp.dot(ol[h * t:(h + 1) * t, :], wuv_ref[h], preferred_element_type=F32)
        o_ref[:, h * HEAD_DIM_DSA:(h + 1) * HEAD_DIM_DSA] = o.astype(o_ref.dtype)


def _dsa_attention(proj, kv_g, w_ukt, w_uv, bias_blocks, batch, seq, topk):
    t = DSA_BLOCK
    nq = seq // t
    hb, hi = N_HEADS_DSA, N_HEADS_IDX
    width = hb * HEAD_DIM_DSA
    kernel = functools.partial(_dsa_kernel, topk=topk, scale=HEAD_DIM_DSA ** -0.5 * LOG2E)
    return pl.pallas_call(
        kernel,
        out_shape=jax.ShapeDtypeStruct((batch * seq, width), BF16),
        grid=(batch, nq),
        in_specs=[pl.BlockSpec((t, 1024), lambda b, qi: (b * nq + qi, COL_IQ // 1024)),
                  pl.BlockSpec((t, 1024), lambda b, qi: (b * nq + qi, COL_SQ // 1024)),
                  pl.BlockSpec((t, LANES), lambda b, qi: (b * nq + qi, COL_IW // LANES)),
                  pl.BlockSpec((seq, LANES), lambda b, qi: (b, COL_IK // LANES)),
                  pl.BlockSpec((seq, KV_LATENT), lambda b, qi: (b, COL_CKV // KV_LATENT)),
                  pl.BlockSpec((1, KV_LATENT), lambda b, qi: (0, 0)),
                  pl.BlockSpec((hb, HEAD_DIM_DSA, KV_LATENT), lambda b, qi: (0, 0, 0)),
                  pl.BlockSpec((hb, KV_LATENT, HEAD_DIM_DSA), lambda b, qi: (0, 0, 0)),
                  pl.BlockSpec((2, hb, LANES, LANES), lambda b, qi: (0, 0, 0, 0))],
        out_specs=pl.BlockSpec((t, width), lambda b, qi: (b * nq + qi, 0)),
        scratch_shapes=[pltpu.VMEM((seq, KV_LATENT), BF16),
                        pltpu.VMEM((t, seq), I32),
                        pltpu.VMEM((seq, t), I32),
                        pltpu.VMEM((hi * t, LANES), BF16),
                        pltpu.VMEM((hi, t, LANES), F32),
                        pltpu.VMEM((hb * t, KV_LATENT), BF16),
                        pltpu.VMEM((hb * t, KEY_CHUNK), F32),
                        pltpu.VMEM((hb * t, KEY_CHUNK), F32),
                        pltpu.VMEM((2, hb * t, LANES), F32),
                        pltpu.VMEM((hb * t, LANES), F32),
                        pltpu.VMEM((hb * t, LANES), F32),
                        pltpu.VMEM((hb * t, KV_LATENT), F32)],
        compiler_params=_cparams(("parallel", "arbitrary")),
        name="dsa_attn",
    )(proj, proj, proj, proj, proj, kv_g, w_ukt, w_uv, bias_blocks)


def _pack_halves(x):
    c = x.shape[1] // 2
    lo = lax.bitcast_convert_type(x[:, :c].astype(BF16).astype(F32), I32)
    hi = lax.bitcast_convert_type(x[:, c:].astype(BF16).astype(F32), I32)
    return lax.shift_right_logical(lo, 16) | (hi & jnp.int32(-65536))


def _unpack_halves(w):
    lo = lax.bitcast_convert_type(lax.shift_left(w, 16), F32)
    hi = lax.bitcast_convert_type(w & jnp.int32(-65536), F32)
    return lo, hi


def _merge_kernel(x_ref, yd_ref, ys_ref, ga_ref, gb_ref, wd_ref, ws_ref, wo_ref, g_ref, wr_ref, br_ref,
                  h_ref, hn_ref, route_ref):
    bd = jnp.dot(yd_ref[...], wd_ref[...], preferred_element_type=F32)
    bs = jnp.dot(ys_ref[...], ws_ref[...], preferred_element_type=F32)
    merged = (jax.nn.sigmoid(ga_ref[...].astype(F32)) * bd + jax.nn.sigmoid(gb_ref[...].astype(F32)) * bs)
    h = x_ref[...] + jnp.dot(merged.astype(BF16), wo_ref[...], preferred_element_type=F32)
    h_ref[...] = h
    ms = jnp.mean(h * h, axis=-1, keepdims=True)
    hn = h * lax.rsqrt(ms + EPS) * g_ref[...]
    hn_ref[...] = _pack_halves(hn)

    logits = jnp.dot(hn, wr_ref[...], preferred_element_type=F32, precision=lax.Precision.HIGHEST)
    logits = logits + br_ref[...]
    lane = lax.broadcasted_iota(I32, logits.shape, 1)
    vals, ids = [], []
    for _ in range(TOP_K_EXPERTS):
        mx = jnp.max(logits, axis=-1, keepdims=True)
        ix = jnp.min(jnp.where(logits == mx, lane, LANES), axis=-1, keepdims=True)
        vals.append(mx)
        ids.append(ix)
        logits = jnp.where(lane == ix, -jnp.inf, logits)
    es = [jnp.exp(v - vals[0]) for v in vals]
    inv = 1.0 / (es[0] + es[1] + es[2] + es[3])
    route = jnp.zeros(logits.shape, F32)
    for k in range(TOP_K_EXPERTS):
        route = jnp.where(lane == k, es[k] * inv, route)
        route = jnp.where(lane == TOP_K_EXPERTS + k, ids[k].astype(F32), route)
    route_ref[...] = route


def _merge(x2, y_diff, y_dsa, proj, w_bd, w_bs, w_out, g_ffn, w_router, b_router, tm):
    n, d = x2.shape
    row = lambda i: (i, 0)
    const = lambda i: (0, 0)
    return pl.pallas_call(
        _merge_kernel,
        out_shape=(jax.ShapeDtypeStruct((n, d), F32),
                   jax.ShapeDtypeStruct((n, d // 2), I32),
                   jax.ShapeDtypeStruct((n, LANES), F32)),
        grid=(n // tm,),
        in_specs=[pl.BlockSpec((tm, d), row),
                  pl.BlockSpec((tm, d), row),
                  pl.BlockSpec((tm, d), row),
                  pl.BlockSpec((tm, d), lambda i: (i, COL_GA // 1024)),
                  pl.BlockSpec((tm, d), lambda i: (i, COL_GB // 1024)),
                  pl.BlockSpec((d, d), const),
                  pl.BlockSpec((d, d), const),
                  pl.BlockSpec((d, d), const),
                  pl.BlockSpec((1, d), const),
                  pl.BlockSpec((d, LANES), const),
                  pl.BlockSpec((1, LANES), const)],
        out_specs=(pl.BlockSpec((tm, d), row),
                   pl.BlockSpec((tm, d // 2), row),
                   pl.BlockSpec((tm, LANES), row)),
        compiler_params=_cparams(("parallel",)),
        name="merge_router",
    )(x2, y_diff, y_dsa, proj, proj, w_bd, w_bs, w_out, g_ffn, w_router, b_router)


def _regroup_kernel(w_ref, p_ref, o_ref):
    pw = p_ref.shape[0]
    for j in range(w_ref.shape[2] // pw):
        w = w_ref[0, :, j * pw:(j + 1) * pw].astype(BF16)
        o_ref[0, :, j * pw:(j + 1) * pw] = jnp.dot(w, p_ref[...], preferred_element_type=F32).astype(BF16)


def _regroup_gate_up(w_gu, rows):
    e, d, f2 = w_gu.shape
    pw = 2 * LANES
    src = jnp.arange(pw, dtype=I32)
    dst = (src % 2) * LANES + src // 2
    perm = (dst[:, None] == jnp.arange(pw, dtype=I32)[None, :]).astype(BF16)
    return pl.pallas_call(
        _regroup_kernel,
        out_shape=jax.ShapeDtypeStruct((e, d, f2), BF16),
        grid=(e, d // rows),
        in_specs=[pl.BlockSpec((1, rows, f2), lambda i, j: (i, j, 0)),
                  pl.BlockSpec((pw, pw), lambda i, j: (0, 0))],
        out_specs=pl.BlockSpec((1, rows, f2), lambda i, j: (i, j, 0)),
        compiler_params=_cparams(("parallel", "parallel")),
        name="regroup_gate_up",
    )(w_gu, perm)


def _ffn_kernel(be_ref, nu_ref, x_ref, wgu_ref, wd_ref, bgu_ref, bd_ref, o_ref):
    @pl.when(pl.program_id(0) < nu_ref[0])
    def _():
        x_lo, x_hi = _unpack_halves(x_ref[...])
        half = x_lo.shape[1]
        gu = (jnp.dot(x_lo.astype(BF16), wgu_ref[0, 0:half, :], preferred_element_type=F32)
              + jnp.dot(x_hi.astype(BF16), wgu_ref[0, half:2 * half, :], preferred_element_type=F32)
              + bgu_ref[0])
        acts = []
        for j in range(gu.shape[1] // (2 * LANES)):
            gate = jnp.minimum(gu[:, 2 * j * LANES:(2 * j + 1) * LANES], SWIGLU_LIMIT)
            up = jnp.clip(gu[:, (2 * j + 1) * LANES:(2 * j + 2) * LANES], -SWIGLU_LIMIT, SWIGLU_LIMIT)
            glu = gate * jax.nn.sigmoid(gate * SWIGLU_ALPHA)
            acts.append(((up + 1.0) * glu).astype(BF16))
        a = jnp.concatenate(acts, axis=1)
        y = jnp.dot(a, wd_ref[0], preferred_element_type=F32) + bd_ref[0]
        o_ref[...] = _pack_halves(y)

    @pl.when(pl.program_id(0) >= nu_ref[0])
    def _():
        o_ref[...] = jnp.zeros(o_ref.shape, o_ref.dtype)


def _expert_ffn(blk_exp, n_used, xs, wgu, wd, bgu, bd):
    p, dw = xs.shape
    f, d = wd.shape[1], wd.shape[2]
    nblk = p // MOE_ROWS
    wmap = lambda i, be, nu: (be[i], 0, 0)
    grid_spec = pltpu.PrefetchScalarGridSpec(
        num_scalar_prefetch=2,
        grid=(nblk,),
        in_specs=[pl.BlockSpec((MOE_ROWS, dw), lambda i, be, nu: (i, 0)),
                  pl.BlockSpec((1, d, 2 * f), wmap),
                  pl.BlockSpec((1, f, d), wmap),
                  pl.BlockSpec((1, 1, 2 * f), wmap),
                  pl.BlockSpec((1, 1, d), wmap)],
        out_specs=pl.BlockSpec((MOE_ROWS, dw), lambda i, be, nu: (i, 0)),
    )
    return pl.pallas_call(
        _ffn_kernel,
        out_shape=jax.ShapeDtypeStruct((p, dw), I32),
        grid_spec=grid_spec,
        compiler_params=_cparams(("arbitrary",)),
        name="expert_ffn",
    )(blk_exp, n_used, xs, wgu, wd, bgu, bd)


def _route_kernel(route_ref, dest_ref, cnt_ref, u_ref, carry_ref, pstart_ref, *, block_rows):
    ph, i = pl.program_id(0), pl.program_id(1)
    tm = route_ref.shape[0]

    @pl.when((ph == 0) & (i == 0))
    def _():
        r = lax.broadcasted_iota(I32, (tm, tm), 0)
        c = lax.broadcasted_iota(I32, (tm, tm), 1)
        u_ref[...] = jnp.where(r < c, 1.0, 0.0).astype(BF16)
        carry_ref[...] = jnp.zeros(carry_ref.shape, F32)

    @pl.when((ph == 1) & (i == 0))
    def _():
        counts = carry_ref[...]
        cnt_ref[...] = counts
        padded = jnp.ceil(counts * (1.0 / block_rows)) * block_rows
        r = lax.broadcasted_iota(I32, (LANES, LANES), 0)
        c = lax.broadcasted_iota(I32, (LANES, LANES), 1)
        lower = jnp.where(c < r, 1.0, 0.0).astype(F32)
        pstart_ref[...] = jnp.dot(lower, padded, preferred_element_type=F32, precision=lax.Precision.HIGHEST)
        carry_ref[...] = jnp.zeros(carry_ref.shape, F32)

    rt = route_ref[...].T
    sub = lax.broadcasted_iota(I32, (LANES, tm), 0)
    hits = [sub == rt[TOP_K_EXPERTS + k:TOP_K_EXPERTS + k + 1, :].astype(I32) for k in range(TOP_K_EXPERTS)]
    m = jnp.zeros((LANES, tm), F32)
    for hit in hits:
        m = m + jnp.where(hit, 1.0, 0.0)
    tile_counts = jnp.broadcast_to(jnp.sum(m, axis=1, keepdims=True), (LANES, LANES))

    @pl.when(ph == 0)
    def _():
        dest_ref[...] = jnp.zeros(dest_ref.shape, I32)

    @pl.when(ph == 1)
    def _():
        prefix = jnp.dot(m.astype(BF16), u_ref[...], preferred_element_type=F32)
        rank = prefix + (pstart_ref[:, 0:1] + carry_ref[:, 0:1])
        rows = [jnp.sum(jnp.where(hit, rank, 0.0), axis=0, keepdims=True) for hit in hits]
        rows.append(jnp.zeros((dest_ref.shape[0] - TOP_K_EXPERTS, tm), F32))
        dest_ref[...] = jnp.concatenate(rows, axis=0).astype(I32)

    carry_ref[...] = carry_ref[...] + tile_counts


def _route_rows(route, tm):
    n = route.shape[0]
    nt = n // tm
    kernel = functools.partial(_route_kernel, block_rows=MOE_ROWS)
    return pl.pallas_call(
        kernel,
        out_shape=(jax.ShapeDtypeStruct((8, n + tm), I32), jax.ShapeDtypeStruct((LANES, LANES), F32)),
        grid=(2, nt),
        in_specs=[pl.BlockSpec((tm, LANES), lambda ph, i: (i, 0))],
        out_specs=(pl.BlockSpec((8, tm), lambda ph, i: (0, ph * i + (1 - ph) * nt)),
                   pl.BlockSpec((LANES, LANES), lambda ph, i: (0, 0))),
        scratch_shapes=[pltpu.VMEM((tm, tm), BF16),
                        pltpu.VMEM((LANES, LANES), F32),
                        pltpu.VMEM((LANES, LANES), F32)],
        compiler_params=_cparams(("arbitrary", "arbitrary")),
        name="route_rows",
    )(route)


SC_WINDOW = 128
SC_WORKERS = 32


def _sc_mesh():
    return plsc.VectorSubcoreMesh(core_axis_name="c", subcore_axis_name="s")


def _sc_scatter_rows(src, dest, p):
    n, d = src.shape
    per = n // (SC_WINDOW * SC_WORKERS)

    @pl.kernel(out_type=jax.ShapeDtypeStruct((p, d), src.dtype), mesh=_sc_mesh(),
               scratch_types=[pltpu.VMEM((dest.shape[0], SC_WINDOW), I32), pltpu.VMEM((SC_WINDOW, d), src.dtype)])
    def scatter(src_hbm, idx_hbm, out_hbm, idx_vmem, buf):
        wid = lax.axis_index("c") * (SC_WORKERS // 2) + lax.axis_index("s")

        @pl.loop(0, per)
        def _(j):
            off = (wid * per + j) * SC_WINDOW
            pltpu.sync_copy(idx_hbm.at[:, pl.ds(off, SC_WINDOW)], idx_vmem)
            pltpu.sync_copy(src_hbm.at[pl.ds(off, SC_WINDOW), :], buf)
            for k in range(TOP_K_EXPERTS):
                pltpu.sync_copy(buf, out_hbm.at[idx_vmem.at[k]])

    return scatter(src, dest)


def _sc_gather_rows(src, dest):
    n = dest.shape[1]
    d = src.shape[1]
    per = n // (SC_WINDOW * SC_WORKERS)

    @pl.kernel(out_type=jax.ShapeDtypeStruct((TOP_K_EXPERTS * n, d), src.dtype), mesh=_sc_mesh(),
               scratch_types=[pltpu.VMEM((dest.shape[0], SC_WINDOW), I32), pltpu.VMEM((SC_WINDOW, d), src.dtype)])
    def gather(src_hbm, idx_hbm, out_hbm, idx_vmem, buf):
        wid = lax.axis_index("c") * (SC_WORKERS // 2) + lax.axis_index("s")

        @pl.loop(0, per)
        def _(j):
            off = (wid * per + j) * SC_WINDOW
            pltpu.sync_copy(idx_hbm.at[:, pl.ds(off, SC_WINDOW)], idx_vmem)
            for k in range(TOP_K_EXPERTS):
                pltpu.sync_copy(src_hbm.at[idx_vmem.at[k]], buf)
                pltpu.sync_copy(buf, out_hbm.at[pl.ds(k * n + off, SC_WINDOW), :])

    return gather(src, dest)


def _combine_kernel(h_ref, y_ref, route_ref, g_ref, o_ref):
    half = h_ref.shape[1] // 2
    h_lo, h_hi = h_ref[:, 0:half], h_ref[:, half:2 * half]
    route = route_ref[...]
    for k in range(TOP_K_EXPERTS):
        y_lo, y_hi = _unpack_halves(y_ref[k])
        gate = route[:, k:k + 1]
        h_lo = h_lo + gate * y_lo
        h_hi = h_hi + gate * y_hi
    ms = (jnp.sum(h_lo * h_lo, axis=-1, keepdims=True)
          + jnp.sum(h_hi * h_hi, axis=-1, keepdims=True)) * (1.0 / (2 * half))
    inv = lax.rsqrt(ms + EPS)
    o_ref[:, 0:half] = h_lo * inv * g_ref[:, 0:half]
    o_ref[:, half:2 * half] = h_hi * inv * g_ref[:, half:2 * half]


def _combine(h1, yg, route, g_final, tm):
    n, d = h1.shape
    return pl.pallas_call(
        _combine_kernel,
        out_shape=jax.ShapeDtypeStruct((n, d), F32),
        grid=(n // tm,),
        in_specs=[pl.BlockSpec((tm, d), lambda i: (i, 0)),
                  pl.BlockSpec((TOP_K_EXPERTS, tm, d // 2), lambda i: (0, i, 0)),
                  pl.BlockSpec((tm, LANES), lambda i: (i, 0)),
                  pl.BlockSpec((1, d), lambda i: (0, 0))],
        out_specs=pl.BlockSpec((tm, d), lambda i: (i, 0)),
        compiler_params=_cparams(("parallel",)),
        name="combine_norm",
    )(h1, yg, route, g_final)


def _t5_bucket(dist):
    n = jnp.maximum(dist, 0)
    max_exact = N_BUCKETS // 2
    nf = jnp.maximum(n, 1).astype(F32)
    large = max_exact + (jnp.log(nf / max_exact) / math.log(MAX_DISTANCE / max_exact)
                         * (N_BUCKETS - max_exact)).astype(I32)
    large = jnp.minimum(large, N_BUCKETS - 1)
    return jnp.where(n < max_exact, n, large)


def _bias_blocks(bias_tab):
    t = LANES
    assert MAX_DISTANCE <= LANES
    r = jnp.arange(t, dtype=I32)[:, None]
    c = jnp.arange(t, dtype=I32)[None, :]
    rel = (bias_tab - bias_tab[N_BUCKETS - 1][None, :]).astype(F32)
    tiles = []
    buckets = jnp.arange(N_BUCKETS, dtype=I32)[:, None, None]
    for delta in (0, t):
        dist = r - c + delta
        hit = _t5_bucket(dist)[None] == buckets
        b = jnp.sum(jnp.where(hit[:, None], rel[:, :, None, None], 0.0), axis=0)
        tiles.append(jnp.where((dist >= 0)[None], b * LOG2E, MASK_VALUE))
    return jnp.stack(tiles)


def _regroup_w_in(w_in):
    sizes = (1024, 1024, 1024, 1024, KV_LATENT, 1024, HEAD_DIM_IDX, N_HEADS_IDX, D_MODEL, D_MODEL)
    parts, off = [], 0
    for sz in sizes:
        parts.append(w_in[:, off:off + sz])
        off += sz
    dq, dk, dv, sq, ckv, iq, ik, iw, ga, gb = parts
    dq = dq * (HEAD_DIM_DIFF ** -0.5 * LOG2E)
    iw = iw * ((N_HEADS_IDX ** -0.5) * (HEAD_DIM_IDX ** -0.5))
    pad = jnp.zeros((w_in.shape[0], PROJ_WIDTH - COL_IW - N_HEADS_IDX), w_in.dtype)
    w = jnp.concatenate([dq, dk, dv, sq, iq, ga, gb, ckv, ik, ik, iw, pad], axis=1)
    return w.astype(BF16)


def _block_tables(counts, n_assign):
    e, bm = N_EXPERTS, MOE_ROWS
    padded = (counts + bm - 1) // bm * bm
    pends = jnp.cumsum(padded)
    nblk = -(-(n_assign + e * (bm - 1)) // bm)
    first_row = jnp.arange(nblk, dtype=I32) * bm
    blk_exp = jnp.minimum(jnp.sum((pends[None, :] <= first_row[:, None]).astype(I32), axis=1), e - 1)
    n_used = (pends[-1] // bm).astype(I32).reshape(1)
    return blk_exp, n_used, nblk


def kernel(x, norm_attn_g, w_in, rel_bias, lam_q1, lam_k1, lam_q2, lam_k2, diff_subln_g, kv_norm_g, w_uk, w_uv,
           w_branch_diff, w_branch_dsa, w_out, norm_ffn_g, w_router, b_router, w_gate_up, b_gate_up, w_down,
           b_down, norm_final_g):
    batch, seq, d = x.shape
    n = batch * seq
    assert norm_attn_g.shape[0] == 1, "single-layer kernel"
    assert seq % DIFF_BLOCK == 0 and seq % KEY_CHUNK == 0 and d == D_MODEL
    row_tile = math.gcd(n, 1024)

    x2 = x.reshape(n, d)
    proj = _inproj(x2, norm_attn_g[0].reshape(1, d), _regroup_w_in(w_in[0]), row_tile, 1280)

    lam_init = 0.8 - 0.6 * math.exp(-0.3 * 0)
    lam = (jnp.exp(jnp.sum(lam_q1[0].astype(F32) * lam_k1[0].astype(F32)))
           - jnp.exp(jnp.sum(lam_q2[0].astype(F32) * lam_k2[0].astype(F32))) + lam_init)
    y_diff = _diff_attention(proj, lam.reshape(1, 1).astype(F32), _bias_blocks(rel_bias[:, :N_HEADS_DIFF]),
                             diff_subln_g[0].reshape(1, -1).astype(F32), batch, seq, 1.0 - lam_init)

    y_dsa = _dsa_attention(proj, kv_norm_g[0].reshape(1, -1).astype(F32),
                           w_uk[0].transpose(0, 2, 1).astype(BF16), w_uv[0].astype(BF16),
                           _bias_blocks(rel_bias[:, N_HEADS_DIFF:]), batch, seq, min(TOPK_MAX, seq // 4))

    w_r = jnp.zeros((d, LANES), F32).at[:, :N_EXPERTS].set(w_router[0].astype(F32))
    b_r = jnp.full((1, LANES), MASK_VALUE, F32).at[0, :N_EXPERTS].set(b_router[0].astype(F32))
    h1, hn, route = _merge(x2, y_diff, y_dsa, proj, w_branch_diff[0].astype(BF16), w_branch_dsa[0].astype(BF16),
                           w_out[0].astype(BF16), norm_ffn_g[0].reshape(1, d).astype(F32), w_r, b_r,
                           math.gcd(n, 512))

    dest, counts = _route_rows(route, math.gcd(n, 1024))
    dest = dest[:, :n]
    blk_exp, n_used, nblk = _block_tables(counts[:N_EXPERTS, 0].astype(I32), n * TOP_K_EXPERTS)
    xs = _sc_scatter_rows(hn, dest, nblk * MOE_ROWS)
    e, f = N_EXPERTS, D_EXPERT
    b_gu = b_gate_up[0].astype(F32).reshape(e, f // LANES, LANES, 2).transpose(0, 1, 3, 2).reshape(e, 1, 2 * f)
    ys = _expert_ffn(blk_exp, n_used, xs, _regroup_gate_up(w_gate_up[0], 512), w_down[0].astype(BF16),
                     b_gu, b_down[0][:, None, :].astype(F32))
    yg = _sc_gather_rows(ys, dest).reshape(TOP_K_EXPERTS, n, d // 2)
    out = _combine(h1, yg, route, norm_final_g.reshape(1, d).astype(F32), math.gcd(n, 512))
    return out.reshape(batch, seq, d)
```

```python
import functools
import math

import jax
import jax.numpy as jnp
from jax import lax
from jax.experimental import pallas as pl
from jax.experimental.pallas import tpu as pltpu
from jax.experimental.pallas import tpu_sc as plsc

F32 = jnp.float32
BF16 = jnp.bfloat16
I32 = jnp.int32

D_MODEL = 1024
N_HEADS_DIFF = 8
HEAD_DIM_DIFF = 64
N_HEADS_DSA = 8
HEAD_DIM_DSA = 128
KV_LATENT = 256
N_HEADS_IDX = 16
HEAD_DIM_IDX = 64
TOPK_MAX = 256
N_BUCKETS = 32
MAX_DISTANCE = 128
N_EXPERTS = 32
TOP_K_EXPERTS = 4
D_EXPERT = 1024
SWIGLU_LIMIT = 7.0
SWIGLU_ALPHA = 1.702
EPS = 1e-6

LANES = 128
DIFF_BLOCK = 512
DSA_BLOCK = 256
KEY_CHUNK = 512
MOE_ROWS = 512
PROJ_WIDTH = 7680
VMEM_LIMIT = 56 * 1024 * 1024

COL_DQ, COL_DK, COL_DV, COL_SQ, COL_IQ, COL_GA, COL_GB = (i * 1024 for i in range(7))
COL_CKV = 7168
COL_IK = 7424
COL_IW = 7552

LOG2E = math.log2(math.e)
MASK_VALUE = -1e30
M_INIT = -1e29
INT_MIN = -2 ** 31


def _cparams(sem):
    return pltpu.CompilerParams(dimension_semantics=sem, vmem_limit_bytes=VMEM_LIMIT)


def _inproj_kernel(x_ref, g_ref, w_ref, o_ref, xn_ref):
    @pl.when(pl.program_id(1) == 0)
    def _():
        x = x_ref[...]
        ms = jnp.mean(x * x, axis=-1, keepdims=True)
        xn_ref[...] = (x * lax.rsqrt(ms + EPS) * g_ref[...]).astype(BF16)

    o_ref[...] = jnp.dot(xn_ref[...], w_ref[...], preferred_element_type=F32).astype(o_ref.dtype)


def _inproj(x2, g, w, tm, tn):
    n, d = x2.shape
    width = w.shape[1]
    return pl.pallas_call(
        _inproj_kernel,
        out_shape=jax.ShapeDtypeStruct((n, width), BF16),
        grid=(n // tm, width // tn),
        in_specs=[pl.BlockSpec((tm, d), lambda i, j: (i, 0)),
                  pl.BlockSpec((1, d), lambda i, j: (0, 0)),
                  pl.BlockSpec((d, tn), lambda i, j: (0, j))],
        out_specs=pl.BlockSpec((tm, tn), lambda i, j: (i, j)),
        scratch_shapes=[pltpu.VMEM((tm, d), BF16)],
        compiler_params=_cparams(("parallel", "arbitrary")),
        name="inproj",
    )(x2, g, w)


def _lane_chunk_max(s):
    smax = s[:, 0:LANES]
    for j in range(1, s.shape[1] // LANES):
        smax = jnp.maximum(smax, s[:, j * LANES:(j + 1) * LANES])
    return smax


def _softmax_step(s_ref, tk, v, m_ref, l_ref, acc_ref, smax=None):
    nl = tk // LANES
    if smax is None:
        smax = _lane_chunk_max(s_ref[:, 0:tk])
    m_prev = m_ref[...]
    m_new = jnp.maximum(m_prev, jnp.max(smax, axis=-1, keepdims=True))
    alpha = jnp.exp2(m_prev - m_new)
    psum = None
    ps = []
    for j in range(nl):
        pj = jnp.exp2(s_ref[:, j * LANES:(j + 1) * LANES] - m_new)
        psum = pj if psum is None else psum + pj
        ps.append(pj.astype(BF16))
    l_ref[...] = alpha * l_ref[...] + psum
    pv = _dot_split(jnp.concatenate(ps, axis=1), v)
    e = acc_ref.shape[1]
    a = alpha if e == LANES else jnp.concatenate([alpha] * (e // LANES), axis=1)
    acc_ref[...] = a * acc_ref[...] + pv
    m_ref[...] = m_new


def _softmax_init(m_ref, l_ref, acc_ref):
    m_ref[...] = jnp.full(m_ref.shape, M_INIT, F32)
    l_ref[...] = jnp.zeros(l_ref.shape, F32)
    acc_ref[...] = jnp.zeros(acc_ref.shape, F32)


def _softmax_result(l_ref, acc_ref):
    return acc_ref[...] * (1.0 / jnp.sum(l_ref[...], axis=-1, keepdims=True))


def _near_bias(s_ref, d0, d1, delta, groups, t, tk):
    for g in range(groups):
        for rb in range(t // LANES):
            for cb in range(tk // LANES):
                bd = delta + rb - cb
                rows = slice(g * t + rb * LANES, g * t + (rb + 1) * LANES)
                cols = slice(cb * LANES, (cb + 1) * LANES)
                if bd == 0:
                    s_ref[rows, cols] = s_ref[rows, cols] + d0(g)
                elif bd == 1:
                    s_ref[rows, cols] = s_ref[rows, cols] + d1(g)
                elif bd < 0:
                    s_ref[rows, cols] = jnp.full((LANES, LANES), MASK_VALUE, F32)


def _pipelined_far(n_far, issue, finish):
    odd = n_far % 2

    @pl.when(odd == 1)
    def _():
        issue(0, 1)
        issue(1, 0)
        finish(0, 1)

    @pl.when(odd == 0)
    def _():
        issue(0, 0)

    def body(j, carry):
        a = odd + 2 * j
        issue(a + 1, 1)
        finish(a, 0)
        issue(a + 2, 0)
        finish(a + 1, 1)
        return carry

    lax.fori_loop(0, (n_far - odd) // 2, body, 0)


def _dot_nt(a, b):
    h = a.shape[0] // 2
    dn = (((1,), (1,)), ((), ()))
    return jnp.concatenate([lax.dot_general(a[:h], b, dn, preferred_element_type=F32),
                            lax.dot_general(a[h:], b, dn, preferred_element_type=F32)], axis=0)


def _dot_split(a, b):
    h = a.shape[0] // 2
    return jnp.concatenate([jnp.dot(a[:h], b, preferred_element_type=F32),
                            jnp.dot(a[h:], b, preferred_element_type=F32)], axis=0)


def _diff_kernel(lam_ref, q_ref, k_ref, v_ref, bias_ref, g_ref, o_ref, q2_ref, s_ref, m_ref, l_ref, acc_ref,
                 *, out_scale):
    t = q_ref.shape[0]
    qi = pl.program_id(2)

    q = q_ref[...]
    lane = lax.broadcasted_iota(I32, q.shape, 1)
    zero = jnp.zeros_like(q)
    q2_ref[0:t, :] = jnp.where(lane < HEAD_DIM_DIFF, q, zero)
    q2_ref[t:2 * t, :] = jnp.where(lane >= HEAD_DIM_DIFF, q, zero)
    _softmax_init(m_ref, l_ref, acc_ref)
    d0 = lambda g: bias_ref[0, 0]
    d1 = lambda g: bias_ref[1, 0]

    def chunk(kc, delta):
        off = pl.multiple_of(kc * t, t)
        s = _dot_nt(q2_ref[...], k_ref[pl.ds(off, t), :])
        s_ref[...] = s
        smax = None
        if delta is None:
            smax = _lane_chunk_max(s)
        else:
            _near_bias(s_ref, d0, d1, delta, 2, t, t)
        _softmax_step(s_ref, t, v_ref[pl.ds(off, t), :], m_ref, l_ref, acc_ref, smax)

    def far_body(kc, carry):
        chunk(kc, None)
        return carry

    lax.fori_loop(0, jnp.maximum(qi - 1, 0), far_body, 0)

    @pl.when(qi >= 1)
    def _():
        chunk(qi - 1, t // LANES)

    chunk(qi, 0)

    o = _softmax_result(l_ref, acc_ref)
    o = o[0:t, :] - lam_ref[0, 0] * o[t:2 * t, :]
    ms = jnp.mean(o * o, axis=-1, keepdims=True)
    o_ref[...] = (o * lax.rsqrt(ms + EPS) * g_ref[...] * out_scale).astype(o_ref.dtype)


def _diff_attention(proj, lam, bias_blocks, subln_g, batch, seq, out_scale):
    t = DIFF_BLOCK
    nq = seq // t
    h = N_HEADS_DIFF
    e = 2 * HEAD_DIM_DIFF
    kernel = functools.partial(_diff_kernel, out_scale=out_scale)
    return pl.pallas_call(
        kernel,
        out_shape=jax.ShapeDtypeStruct((batch * seq, h * e), BF16),
        grid=(batch, h, nq),
        in_specs=[pl.BlockSpec(memory_space=pltpu.SMEM),
                  pl.BlockSpec((t, e), lambda b, hh, qi: (b * nq + qi, COL_DQ // e + hh)),
                  pl.BlockSpec((seq, e), lambda b, hh, qi: (b, COL_DK // e + hh)),
                  pl.BlockSpec((seq, e), lambda b, hh, qi: (b, COL_DV // e + hh)),
                  pl.BlockSpec((2, 1, LANES, LANES), lambda b, hh, qi: (0, hh, 0, 0)),
                  pl.BlockSpec((1, e), lambda b, hh, qi: (0, 0))],
        out_specs=pl.BlockSpec((t, e), lambda b, hh, qi: (b * nq + qi, hh)),
        scratch_shapes=[pltpu.VMEM((2 * t, e), BF16),
                        pltpu.VMEM((2 * t, t), F32),
                        pltpu.VMEM((2 * t, LANES), F32),
                        pltpu.VMEM((2 * t, LANES), F32),
                        pltpu.VMEM((2 * t, e), F32)],
        compiler_params=_cparams(("parallel", "parallel", "arbitrary")),
        name="diff_attn",
    )(lam, proj, proj, proj, bias_blocks, subln_g)


def _sortable_key(x):
    bits = lax.bitcast_convert_type(x, I32)
    return bits ^ ((bits >> 31) & jnp.int32(0x7FFFFFFF))


def _dsa_kernel(iq_ref, sq_ref, iw_ref, ik_ref, ckv_ref, kvg_ref, wuk_ref, wuv_ref, bias_ref, o_ref,
                c_ref, key_ref, keyt_ref, qi_ref, wb_ref, ql_ref, s0_ref, s1_ref, smax_ref, m_ref, l_ref, acc_ref, *, topk, scale):
    t = iq_ref.shape[0]
    tk = KEY_CHUNK
    qi = pl.program_id(1)
    n_chunks = qi + 1
    hi, hb = N_HEADS_IDX, N_HEADS_DSA

    @pl.when(qi == 0)
    def _():
        ckv = ckv_ref[...].astype(F32)
        ms = jnp.mean(ckv * ckv, axis=-1, keepdims=True)
        c_ref[...] = (ckv * lax.rsqrt(ms + EPS) * kvg_ref[...]).astype(BF16)

    lane = lax.broadcasted_iota(I32, (t, LANES), 1)
    for h in range(hi):
        blk = iq_ref[:, (h // 2) * LANES:(h // 2 + 1) * LANES]
        keep = (lane < HEAD_DIM_IDX) if h % 2 == 0 else (lane >= HEAD_DIM_IDX)
        qi_ref[h * t:(h + 1) * t, :] = jnp.where(keep, blk, jnp.zeros_like(blk))
        wb_ref[h] = jnp.broadcast_to(iw_ref[:, h:h + 1].astype(F32), (t, LANES))

    def score_chunk(kc, diag):
        off = pl.multiple_of(kc * t, t)
        d = _dot_nt(qi_ref[...], ik_ref[pl.ds(off, t), :]).reshape(hi, t, t)
        sc = jnp.zeros((t, t), F32)
        for h in range(hi):
            w = wb_ref[h]
            w = jnp.concatenate([w] * (t // LANES), axis=1)
            sc = sc + jnp.maximum(d[h], 0.0) * w
        key = _sortable_key(sc + 0.0)
        if diag:
            row = lax.broadcasted_iota(I32, (t, t), 0)
            col = lax.broadcasted_iota(I32, (t, t), 1)
            key = jnp.where(col <= row, key, jnp.int32(INT_MIN))
        key_ref[:, pl.ds(off, t)] = key
        keyt_ref[pl.ds(off, t), :] = key.T

    def score_body(kc, carry):
        score_chunk(kc, False)
        return carry

    lax.fori_loop(0, qi, score_body, 0)
    score_chunk(qi, True)

    @pl.when(qi % 2 == 0)
    def _():
        off = pl.multiple_of((qi + 1) * t, t)
        key_ref[:, pl.ds(off, t)] = jnp.full((t, t), INT_MIN, I32)
        keyt_ref[pl.ds(off, t), :] = jnp.full((t, t), INT_MIN, I32)

    n_steps = (qi + 2) // 2

    def bit_body(i, cur):
        bit = lax.shift_left(jnp.int32(1), 31 - i)
        cand = cur | bit
        cand_s = (cand ^ jnp.int32(INT_MIN))[None]

        def body(kc, cnt):
            off = pl.multiple_of(kc * tk, tk)
            k = keyt_ref[pl.ds(off, tk), :].reshape(tk // 8, 8, t)
            return cnt + jnp.sum(jnp.where(k >= cand_s, 1, 0), axis=0)

        cnt = lax.fori_loop(0, n_steps, body, jnp.zeros((8, t), I32))
        total = jnp.sum(cnt, axis=0, keepdims=True)
        return jnp.where(total >= topk, cand, cur)

    cur = lax.fori_loop(0, 32, bit_body, jnp.zeros((8, t), I32))
    thr = jnp.maximum(cur ^ jnp.int32(INT_MIN), jnp.int32(INT_MIN + 1))
    thr_b = jnp.broadcast_to(thr[0:1, :], (LANES, t)).T
    thr_w = jnp.concatenate([thr_b] * (tk // LANES), axis=1)

    def mask_body(kc, carry):
        off = pl.multiple_of(kc * tk, tk)
        am = jnp.where(key_ref[:, pl.ds(off, tk)] >= thr_w, 0.0, MASK_VALUE).astype(F32)
        key_ref[:, pl.ds(off, tk)] = lax.bitcast_convert_type(am, I32)
        return carry

    lax.fori_loop(0, n_steps, mask_body, 0)

    for h in range(hb):
        qh = sq_ref[:, h * HEAD_DIM_DSA:(h + 1) * HEAD_DIM_DSA]
        ql = jnp.dot(qh, wuk_ref[h], preferred_element_type=F32) * scale
        ql_ref[h * t:(h + 1) * t, :] = ql.astype(BF16)
    _softmax_init(m_ref, l_ref, acc_ref)
    d0 = lambda g: bias_ref[0, g]
    d1 = lambda g: bias_ref[1, g]

    s_refs = (s0_ref, s1_ref)

    def issue(kc, slot):
        off = pl.multiple_of(kc * tk, tk)
        am = lax.bitcast_convert_type(key_ref[:, pl.ds(off, tk)], F32)
        s = (_dot_nt(ql_ref[...], c_ref[pl.ds(off, tk), :]).reshape(hb, t, tk) + am[None]).reshape(hb * t, tk)
        s_refs[slot][...] = s
        smax_ref[slot] = _lane_chunk_max(s)

    def finish(kc, slot, width=tk, delta=None):
        smax = None
        if delta is None:
            smax = smax_ref[slot]
        else:
            _near_bias(s_refs[slot], d0, d1, delta, hb, t, width)
        off = pl.multiple_of(kc * tk, tk)
        _softmax_step(s_refs[slot], width, c_ref[pl.ds(off, width), :], m_ref, l_ref, acc_ref, smax)

    _pipelined_far(jnp.maximum((qi - 1) // 2, 0), issue, finish)
    half = qi // 2

    @pl.when(qi % 2 == 1)
    def _():
        finish(half, 0, tk, t // LANES)

    @pl.when((qi % 2 == 0) & (half >= 1))
    def _():
        issue(half, 1)
        finish(half - 1, 0, tk, tk // LANES)
        finish(half, 1, t, 0)

    @pl.when(qi == 0)
    def _():
        finish(0, 0, t, 0)

    ol = _softmax_result(l_ref, acc_ref).astype(BF16)
    for h in range(hb):
        o = jnp.dot(ol[h * t:(h + 1) * t, :], wuv_ref[h], preferred_element_type=F32)
        o_ref[:, h * HEAD_DIM_DSA:(h + 1) * HEAD_DIM_DSA] = o.astype(o_ref.dtype)


def _dsa_attention(proj, kv_g, w_ukt, w_uv, bias_blocks, batch, seq, topk):
    t = DSA_BLOCK
    nq = seq // t
    hb, hi = N_HEADS_DSA, N_HEADS_IDX
    width = hb * HEAD_DIM_DSA
    kernel = functools.partial(_dsa_kernel, topk=topk, scale=HEAD_DIM_DSA ** -0.5 * LOG2E)
    return pl.pallas_call(
        kernel,
        out_shape=jax.ShapeDtypeStruct((batch * seq, width), BF16),
        grid=(batch, nq),
        in_specs=[pl.BlockSpec((t, 1024), lambda b, qi: (b * nq + qi, COL_IQ // 1024)),
                  pl.BlockSpec((t, 1024), lambda b, qi: (b * nq + qi, COL_SQ // 1024)),
                  pl.BlockSpec((t, LANES), lambda b, qi: (b * nq + qi, COL_IW // LANES)),
                  pl.BlockSpec((seq, LANES), lambda b, qi: (b, COL_IK // LANES)),
                  pl.BlockSpec((seq, KV_LATENT), lambda b, qi: (b, COL_CKV // KV_LATENT)),
                  pl.BlockSpec((1, KV_LATENT), lambda b, qi: (0, 0)),
                  pl.BlockSpec((hb, HEAD_DIM_DSA, KV_LATENT), lambda b, qi: (0, 0, 0)),
                  pl.BlockSpec((hb, KV_LATENT, HEAD_DIM_DSA), lambda b, qi: (0, 0, 0)),
                  pl.BlockSpec((2, hb, LANES, LANES), lambda b, qi: (0, 0, 0, 0))],
        out_specs=pl.BlockSpec((t, width), lambda b, qi: (b * nq + qi, 0)),
        scratch_shapes=[pltpu.VMEM((seq, KV_LATENT), BF16),
                        pltpu.VMEM((t, seq), I32),
                        pltpu.VMEM((seq, t), I32),
                        pltpu.VMEM((hi * t, LANES), BF16),
                        pltpu.VMEM((hi, t, LANES), F32),
                        pltpu.VMEM((hb * t, KV_LATENT), BF16),
                        pltpu.VMEM((hb * t, KEY_CHUNK), F32),
                        pltpu.VMEM((hb * t, KEY_CHUNK), F32),
                        pltpu.VMEM((2, hb * t, LANES), F32),
                        pltpu.VMEM((hb * t, LANES), F32),
                        pltpu.VMEM((hb * t, LANES), F32),
                        pltpu.VMEM((hb * t, KV_LATENT), F32)],
        compiler_params=_cparams(("parallel", "arbitrary")),
        name="dsa_attn",
    )(proj, proj, proj, proj, proj, kv_g, w_ukt, w_uv, bias_blocks)


def _pack_halves(x):
    c = x.shape[1] // 2
    lo = lax.bitcast_convert_type(x[:, :c].astype(BF16).astype(F32), I32)
    hi = lax.bitcast_convert_type(x[:, c:].astype(BF16).astype(F32), I32)
    return lax.shift_right_logical(lo, 16) | (hi & jnp.int32(-65536))


def _unpack_halves(w):
    lo = lax.bitcast_convert_type(lax.shift_left(w, 16), F32)
    hi = lax.bitcast_convert_type(w & jnp.int32(-65536), F32)
    return lo, hi


def _merge_kernel(x_ref, yd_ref, ys_ref, ga_ref, gb_ref, wd_ref, ws_ref, wo_ref, g_ref, wrh_ref, wrl_ref, br_ref,
                  h_ref, hn_ref, route_ref):
    bd = _dot_split(yd_ref[...], wd_ref[...])
    bs = _dot_split(ys_ref[...], ws_ref[...])
    merged = (jax.nn.sigmoid(ga_ref[...].astype(F32)) * bd + jax.nn.sigmoid(gb_ref[...].astype(F32)) * bs)
    h = x_ref[...] + _dot_split(merged.astype(BF16), wo_ref[...])
    h_ref[...] = h
    ms = jnp.mean(h * h, axis=-1, keepdims=True)
    hn = h * lax.rsqrt(ms + EPS) * g_ref[...]
    hn_ref[...] = _pack_halves(hn)

    hn_hi = hn.astype(BF16)
    hn_lo = (hn - hn_hi.astype(F32)).astype(BF16)
    logits = (_dot_split(hn_hi, wrh_ref[...]) + _dot_split(hn_lo, wrh_ref[...])
              + _dot_split(hn_hi, wrl_ref[...]))
    logits = logits + br_ref[...]
    lane = lax.broadcasted_iota(I32, logits.shape, 1)
    vals, ids = [], []
    for _ in range(TOP_K_EXPERTS):
        mx = jnp.max(logits, axis=-1, keepdims=True)
        ix = jnp.min(jnp.where(logits == mx, lane, LANES), axis=-1, keepdims=True)
        vals.append(mx)
        ids.append(ix)
        logits = jnp.where(lane == ix, -jnp.inf, logits)
    es = [jnp.exp(v - vals[0]) for v in vals]
    inv = 1.0 / (es[0] + es[1] + es[2] + es[3])
    route = jnp.zeros(logits.shape, F32)
    for k in range(TOP_K_EXPERTS):
        route = jnp.where(lane == k, es[k] * inv, route)
        route = jnp.where(lane == TOP_K_EXPERTS + k, ids[k].astype(F32), route)
    route_ref[...] = route


def _merge(x2, y_diff, y_dsa, proj, w_bd, w_bs, w_out, g_ffn, w_router_hi, w_router_lo, b_router, tm):
    n, d = x2.shape
    row = lambda i: (i, 0)
    const = lambda i: (0, 0)
    return pl.pallas_call(
        _merge_kernel,
        out_shape=(jax.ShapeDtypeStruct((n, d), F32),
                   jax.ShapeDtypeStruct((n, d // 2), I32),
                   jax.ShapeDtypeStruct((n, LANES), F32)),
        grid=(n // tm,),
        in_specs=[pl.BlockSpec((tm, d), row),
                  pl.BlockSpec((tm, d), row),
                  pl.BlockSpec((tm, d), row),
                  pl.BlockSpec((tm, d), lambda i: (i, COL_GA // 1024)),
                  pl.BlockSpec((tm, d), lambda i: (i, COL_GB // 1024)),
                  pl.BlockSpec((d, d), const),
                  pl.BlockSpec((d, d), const),
                  pl.BlockSpec((d, d), const),
                  pl.BlockSpec((1, d), const),
                  pl.BlockSpec((d, LANES), const),
                  pl.BlockSpec((d, LANES), const),
                  pl.BlockSpec((1, LANES), const)],
        out_specs=(pl.BlockSpec((tm, d), row),
                   pl.BlockSpec((tm, d // 2), row),
                   pl.BlockSpec((tm, LANES), row)),
        compiler_params=_cparams(("parallel",)),
        name="merge_router",
    )(x2, y_diff, y_dsa, proj, proj, w_bd, w_bs, w_out, g_ffn, w_router_hi, w_router_lo, b_router)


def _regroup_kernel(w_ref, p_ref, o_ref):
    pw = p_ref.shape[0]
    for j in range(w_ref.shape[2] // pw):
        w = w_ref[0, :, j * pw:(j + 1) * pw].astype(BF16)
        o_ref[0, :, j * pw:(j + 1) * pw] = jnp.dot(w, p_ref[...], preferred_element_type=F32).astype(BF16)


def _regroup_gate_up(w_gu, rows):
    e, d, f2 = w_gu.shape
    pw = 2 * LANES
    src = jnp.arange(pw, dtype=I32)
    dst = (src % 2) * LANES + src // 2
    perm = (dst[:, None] == jnp.arange(pw, dtype=I32)[None, :]).astype(BF16)
    return pl.pallas_call(
        _regroup_kernel,
        out_shape=jax.ShapeDtypeStruct((e, d, f2), BF16),
        grid=(e, d // rows),
        in_specs=[pl.BlockSpec((1, rows, f2), lambda i, j: (i, j, 0)),
                  pl.BlockSpec((pw, pw), lambda i, j: (0, 0))],
        out_specs=pl.BlockSpec((1, rows, f2), lambda i, j: (i, j, 0)),
        compiler_params=_cparams(("parallel", "parallel")),
        name="regroup_gate_up",
    )(w_gu, perm)


def _ffn_kernel(be_ref, nu_ref, x_ref, wgu_ref, wd_ref, bgu_ref, bd_ref, o_ref):
    @pl.when(pl.program_id(0) < nu_ref[0])
    def _():
        x_lo, x_hi = _unpack_halves(x_ref[...])
        half = x_lo.shape[1]
        gu = (jnp.dot(x_lo.astype(BF16), wgu_ref[0, 0:half, :], preferred_element_type=F32)
              + jnp.dot(x_hi.astype(BF16), wgu_ref[0, half:2 * half, :], preferred_element_type=F32)
              + bgu_ref[0])
        acts = []
        for j in range(gu.shape[1] // (2 * LANES)):
            gate = jnp.minimum(gu[:, 2 * j * LANES:(2 * j + 1) * LANES], SWIGLU_LIMIT)
            up = jnp.clip(gu[:, (2 * j + 1) * LANES:(2 * j + 2) * LANES], -SWIGLU_LIMIT, SWIGLU_LIMIT)
            glu = gate * jax.nn.sigmoid(gate * SWIGLU_ALPHA)
            acts.append(((up + 1.0) * glu).astype(BF16))
        a = jnp.concatenate(acts, axis=1)
        y = jnp.dot(a, wd_ref[0], preferred_element_type=F32) + bd_ref[0]
        o_ref[...] = _pack_halves(y)

    @pl.when(pl.program_id(0) >= nu_ref[0])
    def _():
        o_ref[...] = jnp.zeros(o_ref.shape, o_ref.dtype)


def _expert_ffn(blk_exp, n_used, xs, wgu, wd, bgu, bd):
    p, dw = xs.shape
    f, d = wd.shape[1], wd.shape[2]
    nblk = p // MOE_ROWS
    wmap = lambda i, be, nu: (be[i], 0, 0)
    grid_spec = pltpu.PrefetchScalarGridSpec(
        num_scalar_prefetch=2,
        grid=(nblk,),
        in_specs=[pl.BlockSpec((MOE_ROWS, dw), lambda i, be, nu: (i, 0)),
                  pl.BlockSpec((1, d, 2 * f), wmap),
                  pl.BlockSpec((1, f, d), wmap),
                  pl.BlockSpec((1, 1, 2 * f), wmap),
                  pl.BlockSpec((1, 1, d), wmap)],
        out_specs=pl.BlockSpec((MOE_ROWS, dw), lambda i, be, nu: (i, 0)),
    )
    return pl.pallas_call(
        _ffn_kernel,
        out_shape=jax.ShapeDtypeStruct((p, dw), I32),
        grid_spec=grid_spec,
        compiler_params=_cparams(("arbitrary",)),
        name="expert_ffn",
    )(blk_exp, n_used, xs, wgu, wd, bgu, bd)


def _route_kernel(route_ref, dest_ref, cnt_ref, u_ref, carry_ref, pstart_ref, *, block_rows):
    ph, i = pl.program_id(0), pl.program_id(1)
    tm = route_ref.shape[0]

    @pl.when((ph == 0) & (i == 0))
    def _():
        r = lax.broadcasted_iota(I32, (tm, tm), 0)
        c = lax.broadcasted_iota(I32, (tm, tm), 1)
        u_ref[...] = jnp.where(r < c, 1.0, 0.0).astype(BF16)
        carry_ref[...] = jnp.zeros(carry_ref.shape, F32)

    @pl.when((ph == 1) & (i == 0))
    def _():
        counts = carry_ref[...]
        cnt_ref[...] = counts
        padded = jnp.ceil(counts * (1.0 / block_rows)) * block_rows
        r = lax.broadcasted_iota(I32, (LANES, LANES), 0)
        c = lax.broadcasted_iota(I32, (LANES, LANES), 1)
        lower = jnp.where(c < r, 1.0, 0.0).astype(F32)
        pstart_ref[...] = jnp.dot(lower, padded, preferred_element_type=F32, precision=lax.Precision.HIGHEST)
        carry_ref[...] = jnp.zeros(carry_ref.shape, F32)

    rt = route_ref[...].T
    sub = lax.broadcasted_iota(I32, (LANES, tm), 0)
    hits = [sub == rt[TOP_K_EXPERTS + k:TOP_K_EXPERTS + k + 1, :].astype(I32) for k in range(TOP_K_EXPERTS)]
    m = jnp.zeros((LANES, tm), F32)
    for hit in hits:
        m = m + jnp.where(hit, 1.0, 0.0)
    tile_counts = jnp.broadcast_to(jnp.sum(m, axis=1, keepdims=True), (LANES, LANES))

    @pl.when(ph == 0)
    def _():
        dest_ref[...] = jnp.zeros(dest_ref.shape, I32)

    @pl.when(ph == 1)
    def _():
        prefix = jnp.dot(m.astype(BF16), u_ref[...], preferred_element_type=F32)
        rank = prefix + (pstart_ref[:, 0:1] + carry_ref[:, 0:1])
        rows = [jnp.sum(jnp.where(hit, rank, 0.0), axis=0, keepdims=True) for hit in hits]
        rows.append(jnp.zeros((dest_ref.shape[0] - TOP_K_EXPERTS, tm), F32))
        dest_ref[...] = jnp.concatenate(rows, axis=0).astype(I32)

    carry_ref[...] = carry_ref[...] + tile_counts


def _route_rows(route, tm):
    n = route.shape[0]
    nt = n // tm
    kernel = functools.partial(_route_kernel, block_rows=MOE_ROWS)
    return pl.pallas_call(
        kernel,
        out_shape=(jax.ShapeDtypeStruct((8, n + tm), I32), jax.ShapeDtypeStruct((LANES, LANES), F32)),
        grid=(2, nt),
        in_specs=[pl.BlockSpec((tm, LANES), lambda ph, i: (i, 0))],
        out_specs=(pl.BlockSpec((8, tm), lambda ph, i: (0, ph * i + (1 - ph) * nt)),
                   pl.BlockSpec((LANES, LANES), lambda ph, i: (0, 0))),
        scratch_shapes=[pltpu.VMEM((tm, tm), BF16),
                        pltpu.VMEM((LANES, LANES), F32),
                        pltpu.VMEM((LANES, LANES), F32)],
        compiler_params=_cparams(("arbitrary", "arbitrary")),
        name="route_rows",
    )(route)


SC_WINDOW = 128
SC_WORKERS = 32


def _sc_mesh():
    return plsc.VectorSubcoreMesh(core_axis_name="c", subcore_axis_name="s")


def _sc_scatter_rows(src, dest, p):
    n, d = src.shape
    per = n // (SC_WINDOW * SC_WORKERS)

    @pl.kernel(out_type=jax.ShapeDtypeStruct((p, d), src.dtype), mesh=_sc_mesh(),
               scratch_types=[pltpu.VMEM((dest.shape[0], SC_WINDOW), I32), pltpu.VMEM((SC_WINDOW, d), src.dtype)])
    def scatter(src_hbm, idx_hbm, out_hbm, idx_vmem, buf):
        wid = lax.axis_index("c") * (SC_WORKERS // 2) + lax.axis_index("s")

        @pl.loop(0, per)
        def _(j):
            off = (wid * per + j) * SC_WINDOW
            pltpu.sync_copy(idx_hbm.at[:, pl.ds(off, SC_WINDOW)], idx_vmem)
            pltpu.sync_copy(src_hbm.at[pl.ds(off, SC_WINDOW), :], buf)
            for k in range(TOP_K_EXPERTS):
                pltpu.sync_copy(buf, out_hbm.at[idx_vmem.at[k]])

    return scatter(src, dest)


def _sc_gather_rows(src, dest):
    n = dest.shape[1]
    d = src.shape[1]
    per = n // (SC_WINDOW * SC_WORKERS)

    @pl.kernel(out_type=jax.ShapeDtypeStruct((TOP_K_EXPERTS * n, d), src.dtype), mesh=_sc_mesh(),
               scratch_types=[pltpu.VMEM((dest.shape[0], SC_WINDOW), I32), pltpu.VMEM((SC_WINDOW, d), src.dtype)])
    def gather(src_hbm, idx_hbm, out_hbm, idx_vmem, buf):
        wid = lax.axis_index("c") * (SC_WORKERS // 2) + lax.axis_index("s")

        @pl.loop(0, per)
        def _(j):
            off = (wid * per + j) * SC_WINDOW
            pltpu.sync_copy(idx_hbm.at[:, pl.ds(off, SC_WINDOW)], idx_vmem)
            for k in range(TOP_K_EXPERTS):
                pltpu.sync_copy(src_hbm.at[idx_vmem.at[k]], buf)
                pltpu.sync_copy(buf, out_hbm.at[pl.ds(k * n + off, SC_WINDOW), :])

    return gather(src, dest)


def _combine_kernel(h_ref, y_ref, route_ref, g_ref, o_ref):
    half = h_ref.shape[1] // 2
    h_lo, h_hi = h_ref[:, 0:half], h_ref[:, half:2 * half]
    route = route_ref[...]
    for k in range(TOP_K_EXPERTS):
        y_lo, y_hi = _unpack_halves(y_ref[k])
        gate = route[:, k:k + 1]
        h_lo = h_lo + gate * y_lo
        h_hi = h_hi + gate * y_hi
    ms = (jnp.sum(h_lo * h_lo, axis=-1, keepdims=True)
          + jnp.sum(h_hi * h_hi, axis=-1, keepdims=True)) * (1.0 / (2 * half))
    inv = lax.rsqrt(ms + EPS)
    o_ref[:, 0:half] = h_lo * inv * g_ref[:, 0:half]
    o_ref[:, half:2 * half] = h_hi * inv * g_ref[:, half:2 * half]


def _combine(h1, yg, route, g_final, tm):
    n, d = h1.shape
    return pl.pallas_call(
        _combine_kernel,
        out_shape=jax.ShapeDtypeStruct((n, d), F32),
        grid=(n // tm,),
        in_specs=[pl.BlockSpec((tm, d), lambda i: (i, 0)),
                  pl.BlockSpec((TOP_K_EXPERTS, tm, d // 2), lambda i: (0, i, 0)),
                  pl.BlockSpec((tm, LANES), lambda i: (i, 0)),
                  pl.BlockSpec((1, d), lambda i: (0, 0))],
        out_specs=pl.BlockSpec((tm, d), lambda i: (i, 0)),
        compiler_params=_cparams(("parallel",)),
        name="combine_norm",
    )(h1, yg, route, g_final)


def _t5_bucket(dist):
    n = jnp.maximum(dist, 0)
    max_exact = N_BUCKETS // 2
    nf = jnp.maximum(n, 1).astype(F32)
    large = max_exact + (jnp.log(nf / max_exact) / math.log(MAX_DISTANCE / max_exact)
                         * (N_BUCKETS - max_exact)).astype(I32)
    large = jnp.minimum(large, N_BUCKETS - 1)
    return jnp.where(n < max_exact, n, large)


def _bias_blocks(bias_tab):
    t = LANES
    assert MAX_DISTANCE <= LANES
    r = jnp.arange(t, dtype=I32)[:, None]
    c = jnp.arange(t, dtype=I32)[None, :]
    rel = (bias_tab - bias_tab[N_BUCKETS - 1][None, :]).astype(F32)
    tiles = []
    buckets = jnp.arange(N_BUCKETS, dtype=I32)[:, None, None]
    for delta in (0, t):
        dist = r - c + delta
        hit = _t5_bucket(dist)[None] == buckets
        b = jnp.sum(jnp.where(hit[:, None], rel[:, :, None, None], 0.0), axis=0)
        tiles.append(jnp.where((dist >= 0)[None], b * LOG2E, MASK_VALUE))
    return jnp.stack(tiles)


def _regroup_w_in(w_in):
    sizes = (1024, 1024, 1024, 1024, KV_LATENT, 1024, HEAD_DIM_IDX, N_HEADS_IDX, D_MODEL, D_MODEL)
    parts, off = [], 0
    for sz in sizes:
        parts.append(w_in[:, off:off + sz])
        off += sz
    dq, dk, dv, sq, ckv, iq, ik, iw, ga, gb = parts
    dq = dq * (HEAD_DIM_DIFF ** -0.5 * LOG2E)
    iw = iw * ((N_HEADS_IDX ** -0.5) * (HEAD_DIM_IDX ** -0.5))
    pad = jnp.zeros((w_in.shape[0], PROJ_WIDTH - COL_IW - N_HEADS_IDX), w_in.dtype)
    w = jnp.concatenate([dq, dk, dv, sq, iq, ga, gb, ckv, ik, ik, iw, pad], axis=1)
    return w.astype(BF16)


def _block_tables(counts, n_assign):
    e, bm = N_EXPERTS, MOE_ROWS
    padded = (counts + bm - 1) // bm * bm
    pends = jnp.cumsum(padded)
    nblk = -(-(n_assign + e * (bm - 1)) // bm)
    first_row = jnp.arange(nblk, dtype=I32) * bm
    blk_exp = jnp.minimum(jnp.sum((pends[None, :] <= first_row[:, None]).astype(I32), axis=1), e - 1)
    n_used = (pends[-1] // bm).astype(I32).reshape(1)
    return blk_exp, n_used, nblk


def kernel(x, norm_attn_g, w_in, rel_bias, lam_q1, lam_k1, lam_q2, lam_k2, diff_subln_g, kv_norm_g, w_uk, w_uv,
           w_branch_diff, w_branch_dsa, w_out, norm_ffn_g, w_router, b_router, w_gate_up, b_gate_up, w_down,
           b_down, norm_final_g):
    batch, seq, d = x.shape
    n = batch * seq
    assert norm_attn_g.shape[0] == 1, "single-layer kernel"
    assert seq % DIFF_BLOCK == 0 and seq % KEY_CHUNK == 0 and d == D_MODEL
    row_tile = math.gcd(n, 1024)

    x2 = x.reshape(n, d)
    proj = _inproj(x2, norm_attn_g[0].reshape(1, d), _regroup_w_in(w_in[0]), row_tile, 1280)

    lam_init = 0.8 - 0.6 * math.exp(-0.3 * 0)
    lam = (jnp.exp(jnp.sum(lam_q1[0].astype(F32) * lam_k1[0].astype(F32)))
           - jnp.exp(jnp.sum(lam_q2[0].astype(F32) * lam_k2[0].astype(F32))) + lam_init)
    y_diff = _diff_attention(proj, lam.reshape(1, 1).astype(F32), _bias_blocks(rel_bias[:, :N_HEADS_DIFF]),
                             diff_subln_g[0].reshape(1, -1).astype(F32), batch, seq, 1.0 - lam_init)

    y_dsa = _dsa_attention(proj, kv_norm_g[0].reshape(1, -1).astype(F32),
                           w_uk[0].transpose(0, 2, 1).astype(BF16), w_uv[0].astype(BF16),
                           _bias_blocks(rel_bias[:, N_HEADS_DIFF:]), batch, seq, min(TOPK_MAX, seq // 4))

    w_r = jnp.zeros((d, LANES), F32).at[:, :N_EXPERTS].set(w_router[0].astype(F32))
    b_r = jnp.full((1, LANES), MASK_VALUE, F32).at[0, :N_EXPERTS].set(b_router[0].astype(F32))
    h1, hn, route = _merge(x2, y_diff, y_dsa, proj, w_branch_diff[0].astype(BF16), w_branch_dsa[0].astype(BF16),
                           w_out[0].astype(BF16), norm_ffn_g[0].reshape(1, d).astype(F32),
                           w_r.astype(BF16), (w_r - w_r.astype(BF16).astype(F32)).astype(BF16), b_r,
                           math.gcd(n, 512))

    dest, counts = _route_rows(route, math.gcd(n, 1024))
    dest = dest[:, :n]
    blk_exp, n_used, nblk = _block_tables(counts[:N_EXPERTS, 0].astype(I32), n * TOP_K_EXPERTS)
    xs = _sc_scatter_rows(hn, dest, nblk * MOE_ROWS)
    e, f = N_EXPERTS, D_EXPERT
    b_gu = b_gate_up[0].astype(F32).reshape(e, f // LANES, LANES, 2).transpose(0, 1, 3, 2).reshape(e, 1, 2 * f)
    ys = _expert_ffn(blk_exp, n_used, xs, _regroup_gate_up(w_gate_up[0], 512), w_down[0].astype(BF16),
                     b_gu, b_down[0][:, None, :].astype(F32))
    yg = _sc_gather_rows(ys, dest).reshape(TOP_K_EXPERTS, n, d // 2)
    out = _combine(h1, yg, route, norm_final_g.reshape(1, d).astype(F32), math.gcd(n, 512))
    return out.reshape(batch, seq, d)
```

```python
import functools
import math

import jax
import jax.numpy as jnp
from jax import lax
from jax.experimental import pallas as pl
from jax.experimental.pallas import tpu as pltpu
from jax.experimental.pallas import tpu_sc as plsc

F32 = jnp.float32
BF16 = jnp.bfloat16
I32 = jnp.int32

D_MODEL = 1024
N_HEADS_DIFF = 8
HEAD_DIM_DIFF = 64
N_HEADS_DSA = 8
HEAD_DIM_DSA = 128
KV_LATENT = 256
N_HEADS_IDX = 16
HEAD_DIM_IDX = 64
TOPK_MAX = 256
N_BUCKETS = 32
MAX_DISTANCE = 128
N_EXPERTS = 32
TOP_K_EXPERTS = 4
D_EXPERT = 1024
SWIGLU_LIMIT = 7.0
SWIGLU_ALPHA = 1.702
EPS = 1e-6

LANES = 128
DIFF_BLOCK = 512
DIFF_HEADS_PER_STEP = 4
DSA_BLOCK = 256
KEY_CHUNK = 512
MOE_ROWS = 512
PROJ_WIDTH = 7680
VMEM_LIMIT = 56 * 1024 * 1024

COL_DQ, COL_DK, COL_DV, COL_SQ, COL_IQ, COL_GA, COL_GB = (i * 1024 for i in range(7))
COL_CKV = 7168
COL_IK = 7424
COL_IW = 7552

LOG2E = math.log2(math.e)
MASK_VALUE = -1e30
M_INIT = -1e29
INT_MIN = -2 ** 31


def _cparams(sem):
    return pltpu.CompilerParams(dimension_semantics=sem, vmem_limit_bytes=VMEM_LIMIT)


def _inproj_kernel(x_ref, g_ref, w_ref, o_ref, xn_ref):
    @pl.when(pl.program_id(1) == 0)
    def _():
        x = x_ref[...]
        ms = jnp.mean(x * x, axis=-1, keepdims=True)
        xn_ref[...] = (x * lax.rsqrt(ms + EPS) * g_ref[...]).astype(BF16)

    o_ref[...] = jnp.dot(xn_ref[...], w_ref[...], preferred_element_type=F32).astype(o_ref.dtype)


def _inproj(x2, g, w, tm, tn):
    n, d = x2.shape
    width = w.shape[1]
    return pl.pallas_call(
        _inproj_kernel,
        out_shape=jax.ShapeDtypeStruct((n, width), BF16),
        grid=(n // tm, width // tn),
        in_specs=[pl.BlockSpec((tm, d), lambda i, j: (i, 0)),
                  pl.BlockSpec((1, d), lambda i, j: (0, 0)),
                  pl.BlockSpec((d, tn), lambda i, j: (0, j))],
        out_specs=pl.BlockSpec((tm, tn), lambda i, j: (i, j)),
        scratch_shapes=[pltpu.VMEM((tm, d), BF16)],
        compiler_params=_cparams(("parallel", "arbitrary")),
        name="inproj",
    )(x2, g, w)


def _lane_chunk_max(s):
    smax = s[:, 0:LANES]
    for j in range(1, s.shape[1] // LANES):
        smax = jnp.maximum(smax, s[:, j * LANES:(j + 1) * LANES])
    return smax


def _softmax_step(s_ref, tk, v, m_ref, l_ref, acc_ref, smax=None):
    nl = tk // LANES
    if smax is None:
        smax = _lane_chunk_max(s_ref[:, 0:tk])
    m_prev = m_ref[...]
    m_new = jnp.maximum(m_prev, jnp.max(smax, axis=-1, keepdims=True))
    alpha = jnp.exp2(m_prev - m_new)
    psum = None
    ps = []
    for j in range(nl):
        pj = jnp.exp2(s_ref[:, j * LANES:(j + 1) * LANES] - m_new)
        psum = pj if psum is None else psum + pj
        ps.append(pj.astype(BF16))
    l_ref[...] = alpha * l_ref[...] + psum
    pv = _dot_split(jnp.concatenate(ps, axis=1), v)
    e = acc_ref.shape[1]
    a = alpha if e == LANES else jnp.concatenate([alpha] * (e // LANES), axis=1)
    acc_ref[...] = a * acc_ref[...] + pv
    m_ref[...] = m_new


def _softmax_init(m_ref, l_ref, acc_ref):
    m_ref[...] = jnp.full(m_ref.shape, M_INIT, F32)
    l_ref[...] = jnp.zeros(l_ref.shape, F32)
    acc_ref[...] = jnp.zeros(acc_ref.shape, F32)


def _softmax_result(l_ref, acc_ref):
    return acc_ref[...] * (1.0 / jnp.sum(l_ref[...], axis=-1, keepdims=True))


def _near_bias(s_ref, d0, d1, delta, groups, t, tk):
    for g in range(groups):
        for rb in range(t // LANES):
            for cb in range(tk // LANES):
                bd = delta + rb - cb
                rows = slice(g * t + rb * LANES, g * t + (rb + 1) * LANES)
                cols = slice(cb * LANES, (cb + 1) * LANES)
                if bd == 0:
                    s_ref[rows, cols] = s_ref[rows, cols] + d0(g)
                elif bd == 1:
                    s_ref[rows, cols] = s_ref[rows, cols] + d1(g)
                elif bd < 0:
                    s_ref[rows, cols] = jnp.full((LANES, LANES), MASK_VALUE, F32)


def _pipelined_far(n_far, issue, finish):
    odd = n_far % 2

    @pl.when(odd == 1)
    def _():
        issue(0, 1)
        issue(1, 0)
        finish(0, 1)

    @pl.when(odd == 0)
    def _():
        issue(0, 0)

    def body(j, carry):
        a = odd + 2 * j
        issue(a + 1, 1)
        finish(a, 0)
        issue(a + 2, 0)
        finish(a + 1, 1)
        return carry

    lax.fori_loop(0, (n_far - odd) // 2, body, 0)


def _dot_nt(a, b):
    h = a.shape[0] // 2
    dn = (((1,), (1,)), ((), ()))
    return jnp.concatenate([lax.dot_general(a[:h], b, dn, preferred_element_type=F32),
                            lax.dot_general(a[h:], b, dn, preferred_element_type=F32)], axis=0)


def _dot_split(a, b):
    h = a.shape[0] // 2
    return jnp.concatenate([jnp.dot(a[:h], b, preferred_element_type=F32),
                            jnp.dot(a[h:], b, preferred_element_type=F32)], axis=0)


def _diff_kernel(lam_ref, q_ref, k_ref, v_ref, bias_ref, g_ref, o_ref, q2_ref, s_ref, m_ref, l_ref, acc_ref,
                 *, out_scale, heads):
    t = q_ref.shape[0]
    e = 2 * HEAD_DIM_DIFF
    qi = pl.program_id(2)
    cols = [slice(hh * e, (hh + 1) * e) for hh in range(heads)]

    lane = lax.broadcasted_iota(I32, (t, e), 1)
    for hh in range(heads):
        q = q_ref[:, cols[hh]]
        zero = jnp.zeros_like(q)
        q2_ref[hh, 0:t, :] = jnp.where(lane < HEAD_DIM_DIFF, q, zero)
        q2_ref[hh, t:2 * t, :] = jnp.where(lane >= HEAD_DIM_DIFF, q, zero)
        _softmax_init(m_ref.at[hh], l_ref.at[hh], acc_ref.at[hh])

    def chunk(kc, delta):
        off = pl.multiple_of(kc * t, t)
        for hh in range(heads):
            s = _dot_nt(q2_ref[hh], k_ref[pl.ds(off, t), cols[hh]])
            s_ref[hh] = s
            smax = None
            if delta is None:
                smax = _lane_chunk_max(s)
            else:
                _near_bias(s_ref.at[hh], lambda g: bias_ref[0, hh], lambda g: bias_ref[1, hh], delta, 2, t, t)
            _softmax_step(s_ref.at[hh], t, v_ref[pl.ds(off, t), cols[hh]], m_ref.at[hh], l_ref.at[hh],
                          acc_ref.at[hh], smax)

    def far_body(kc, carry):
        chunk(kc, None)
        return carry

    lax.fori_loop(0, jnp.maximum(qi - 1, 0), far_body, 0)

    @pl.when(qi >= 1)
    def _():
        chunk(qi - 1, t // LANES)

    chunk(qi, 0)

    for hh in range(heads):
        o = _softmax_result(l_ref.at[hh], acc_ref.at[hh])
        o = o[0:t, :] - lam_ref[0, 0] * o[t:2 * t, :]
        ms = jnp.mean(o * o, axis=-1, keepdims=True)
        o_ref[:, cols[hh]] = (o * lax.rsqrt(ms + EPS) * g_ref[...] * out_scale).astype(o_ref.dtype)


def _diff_attention(proj, lam, bias_blocks, subln_g, batch, seq, out_scale):
    t = DIFF_BLOCK
    nq = seq // t
    h = N_HEADS_DIFF
    e = 2 * HEAD_DIM_DIFF
    hp = DIFF_HEADS_PER_STEP
    w = hp * e
    kernel = functools.partial(_diff_kernel, out_scale=out_scale, heads=hp)
    return pl.pallas_call(
        kernel,
        out_shape=jax.ShapeDtypeStruct((batch * seq, h * e), BF16),
        grid=(batch, h // hp, nq),
        in_specs=[pl.BlockSpec(memory_space=pltpu.SMEM),
                  pl.BlockSpec((t, w), lambda b, hh, qi: (b * nq + qi, COL_DQ // w + hh)),
                  pl.BlockSpec((seq, w), lambda b, hh, qi: (b, COL_DK // w + hh)),
                  pl.BlockSpec((seq, w), lambda b, hh, qi: (b, COL_DV // w + hh)),
                  pl.BlockSpec((2, hp, LANES, LANES), lambda b, hh, qi: (0, hh, 0, 0)),
                  pl.BlockSpec((1, e), lambda b, hh, qi: (0, 0))],
        out_specs=pl.BlockSpec((t, w), lambda b, hh, qi: (b * nq + qi, hh)),
        scratch_shapes=[pltpu.VMEM((hp, 2 * t, e), BF16),
                        pltpu.VMEM((hp, 2 * t, t), F32),
                        pltpu.VMEM((hp, 2 * t, LANES), F32),
                        pltpu.VMEM((hp, 2 * t, LANES), F32),
                        pltpu.VMEM((hp, 2 * t, e), F32)],
        compiler_params=_cparams(("parallel", "parallel", "arbitrary")),
        name="diff_attn",
    )(lam, proj, proj, proj, bias_blocks, subln_g)


def _sortable_key(x):
    bits = lax.bitcast_convert_type(x, I32)
    return bits ^ ((bits >> 31) & jnp.int32(0x7FFFFFFF))


def _dsa_kernel(iq_ref, sq_ref, iw_ref, ik_ref, ckv_ref, kvg_ref, wuk_ref, wuv_ref, bias_ref, o_ref,
                c_ref, key_ref, keyt_ref, qi_ref, wb_ref, ql_ref, s0_ref, s1_ref, smax_ref, m_ref, l_ref, acc_ref, *, topk, scale):
    t = iq_ref.shape[0]
    tk = KEY_CHUNK
    qi = pl.program_id(1)
    n_chunks = qi + 1
    hi, hb = N_HEADS_IDX, N_HEADS_DSA

    @pl.when(qi == 0)
    def _():
        ckv = ckv_ref[...].astype(F32)
        ms = jnp.mean(ckv * ckv, axis=-1, keepdims=True)
        c_ref[...] = (ckv * lax.rsqrt(ms + EPS) * kvg_ref[...]).astype(BF16)

    lane = lax.broadcasted_iota(I32, (t, LANES), 1)
    for h in range(hi):
        blk = iq_ref[:, (h // 2) * LANES:(h // 2 + 1) * LANES]
        keep = (lane < HEAD_DIM_IDX) if h % 2 == 0 else (lane >= HEAD_DIM_IDX)
        qi_ref[h * t:(h + 1) * t, :] = jnp.where(keep, blk, jnp.zeros_like(blk))
        wb_ref[h] = jnp.broadcast_to(iw_ref[:, h:h + 1].astype(F32), (t, LANES))

    def score_chunk(kc, diag):
        off = pl.multiple_of(kc * t, t)
        d = _dot_nt(qi_ref[...], ik_ref[pl.ds(off, t), :]).reshape(hi, t, t)
        sc = jnp.zeros((t, t), F32)
        for h in range(hi):
            w = wb_ref[h]
            w = jnp.concatenate([w] * (t // LANES), axis=1)
            sc = sc + jnp.maximum(d[h], 0.0) * w
        key = _sortable_key(sc + 0.0)
        if diag:
            row = lax.broadcasted_iota(I32, (t, t), 0)
            col = lax.broadcasted_iota(I32, (t, t), 1)
            key = jnp.where(col <= row, key, jnp.int32(INT_MIN))
        key_ref[:, pl.ds(off, t)] = key
        keyt_ref[pl.ds(off, t), :] = key.T

    def score_body(kc, carry):
        score_chunk(kc, False)
        return carry

    lax.fori_loop(0, qi, score_body, 0)
    score_chunk(qi, True)

    @pl.when(qi % 2 == 0)
    def _():
        off = pl.multiple_of((qi + 1) * t, t)
        key_ref[:, pl.ds(off, t)] = jnp.full((t, t), INT_MIN, I32)
        keyt_ref[pl.ds(off, t), :] = jnp.full((t, t), INT_MIN, I32)

    n_steps = (qi + 2) // 2

    def bit_body(i, cur):
        bit = lax.shift_left(jnp.int32(1), 31 - i)
        cand = cur | bit
        cand_s = (cand ^ jnp.int32(INT_MIN))[None]

        def body(kc, cnt):
            off = pl.multiple_of(kc * tk, tk)
            k = keyt_ref[pl.ds(off, tk), :].reshape(tk // 8, 8, t)
            return cnt + jnp.sum(jnp.where(k >= cand_s, 1, 0), axis=0)

        cnt = lax.fori_loop(0, n_steps, body, jnp.zeros((8, t), I32))
        total = jnp.sum(cnt, axis=0, keepdims=True)
        return jnp.where(total >= topk, cand, cur)

    cur = lax.fori_loop(0, 32, bit_body, jnp.zeros((8, t), I32))
    thr = jnp.maximum(cur ^ jnp.int32(INT_MIN), jnp.int32(INT_MIN + 1))
    thr_b = jnp.broadcast_to(thr[0:1, :], (LANES, t)).T
    thr_w = jnp.concatenate([thr_b] * (tk // LANES), axis=1)

    def mask_body(kc, carry):
        off = pl.multiple_of(kc * tk, tk)
        am = jnp.where(key_ref[:, pl.ds(off, tk)] >= thr_w, 0.0, MASK_VALUE).astype(F32)
        key_ref[:, pl.ds(off, tk)] = lax.bitcast_convert_type(am, I32)
        return carry

    lax.fori_loop(0, n_steps, mask_body, 0)

    for h in range(hb):
        qh = sq_ref[:, h * HEAD_DIM_DSA:(h + 1) * HEAD_DIM_DSA]
        ql = jnp.dot(qh, wuk_ref[h], preferred_element_type=F32) * scale
        ql_ref[h * t:(h + 1) * t, :] = ql.astype(BF16)
    _softmax_init(m_ref, l_ref, acc_ref)
    d0 = lambda g: bias_ref[0, g]
    d1 = lambda g: bias_ref[1, g]

    s_refs = (s0_ref, s1_ref)

    def issue(kc, slot):
        off = pl.multiple_of(kc * tk, tk)
        am = lax.bitcast_convert_type(key_ref[:, pl.ds(off, tk)], F32)
        s = (_dot_nt(ql_ref[...], c_ref[pl.ds(off, tk), :]).reshape(hb, t, tk) + am[None]).reshape(hb * t, tk)
        s_refs[slot][...] = s
        smax_ref[slot] = _lane_chunk_max(s)

    def finish(kc, slot, width=tk, delta=None):
        smax = None
        if delta is None:
            smax = smax_ref[slot]
        else:
            _near_bias(s_refs[slot], d0, d1, delta, hb, t, width)
        off = pl.multiple_of(kc * tk, tk)
        _softmax_step(s_refs[slot], width, c_ref[pl.ds(off, width), :], m_ref, l_ref, acc_ref, smax)

    _pipelined_far(jnp.maximum((qi - 1) // 2, 0), issue, finish)
    half = qi // 2

    @pl.when(qi % 2 == 1)
    def _():
        finish(half, 0, tk, t // LANES)

    @pl.when((qi % 2 == 0) & (half >= 1))
    def _():
        issue(half, 1)
        finish(half - 1, 0, tk, tk // LANES)
        finish(half, 1, t, 0)

    @pl.when(qi == 0)
    def _():
        finish(0, 0, t, 0)

    ol = _softmax_result(l_ref, acc_ref).astype(BF16)
    for h in range(hb):
        o = jnp.dot(ol[h * t:(h + 1) * t, :], wuv_ref[h], preferred_element_type=F32)
        o_ref[:, h * HEAD_DIM_DSA:(h + 1) * HEAD_DIM_DSA] = o.astype(o_ref.dtype)


def _dsa_attention(proj, kv_g, w_ukt, w_uv, bias_blocks, batch, seq, topk):
    t = DSA_BLOCK
    nq = seq // t
    hb, hi = N_HEADS_DSA, N_HEADS_IDX
    width = hb * HEAD_DIM_DSA
    kernel = functools.partial(_dsa_kernel, topk=topk, scale=HEAD_DIM_DSA ** -0.5 * LOG2E)
    return pl.pallas_call(
        kernel,
        out_shape=jax.ShapeDtypeStruct((batch * seq, width), BF16),
        grid=(batch, nq),
        in_specs=[pl.BlockSpec((t, 1024), lambda b, qi: (b * nq + qi, COL_IQ // 1024)),
                  pl.BlockSpec((t, 1024), lambda b, qi: (b * nq + qi, COL_SQ // 1024)),
                  pl.BlockSpec((t, LANES), lambda b, qi: (b * nq + qi, COL_IW // LANES)),
                  pl.BlockSpec((seq, LANES), lambda b, qi: (b, COL_IK // LANES)),
                  pl.BlockSpec((seq, KV_LATENT), lambda b, qi: (b, COL_CKV // KV_LATENT)),
                  pl.BlockSpec((1, KV_LATENT), lambda b, qi: (0, 0)),
                  pl.BlockSpec((hb, HEAD_DIM_DSA, KV_LATENT), lambda b, qi: (0, 0, 0)),
                  pl.BlockSpec((hb, KV_LATENT, HEAD_DIM_DSA), lambda b, qi: (0, 0, 0)),
                  pl.BlockSpec((2, hb, LANES, LANES), lambda b, qi: (0, 0, 0, 0))],
        out_specs=pl.BlockSpec((t, width), lambda b, qi: (b * nq + qi, 0)),
        scratch_shapes=[pltpu.VMEM((seq, KV_LATENT), BF16),
                        pltpu.VMEM((t, seq), I32),
                        pltpu.VMEM((seq, t), I32),
                        pltpu.VMEM((hi * t, LANES), BF16),
                        pltpu.VMEM((hi, t, LANES), F32),
                        pltpu.VMEM((hb * t, KV_LATENT), BF16),
                        pltpu.VMEM((hb * t, KEY_CHUNK), F32),
                        pltpu.VMEM((hb * t, KEY_CHUNK), F32),
                        pltpu.VMEM((2, hb * t, LANES), F32),
                        pltpu.VMEM((hb * t, LANES), F32),
                        pltpu.VMEM((hb * t, LANES), F32),
                        pltpu.VMEM((hb * t, KV_LATENT), F32)],
        compiler_params=_cparams(("parallel", "arbitrary")),
        name="dsa_attn",
    )(proj, proj, proj, proj, proj, kv_g, w_ukt, w_uv, bias_blocks)


def _pack_halves(x):
    c = x.shape[1] // 2
    lo = lax.bitcast_convert_type(x[:, :c].astype(BF16).astype(F32), I32)
    hi = lax.bitcast_convert_type(x[:, c:].astype(BF16).astype(F32), I32)
    return lax.shift_right_logical(lo, 16) | (hi & jnp.int32(-65536))


def _unpack_halves(w):
    lo = lax.bitcast_convert_type(lax.shift_left(w, 16), F32)
    hi = lax.bitcast_convert_type(w & jnp.int32(-65536), F32)
    return lo, hi


def _merge_kernel(x_ref, yd_ref, ys_ref, ga_ref, gb_ref, wd_ref, ws_ref, wo_ref, g_ref, wrh_ref, wrl_ref, br_ref,
                  h_ref, hn_ref, route_ref):
    bd = _dot_split(yd_ref[...], wd_ref[...])
    bs = _dot_split(ys_ref[...], ws_ref[...])
    merged = (jax.nn.sigmoid(ga_ref[...].astype(F32)) * bd + jax.nn.sigmoid(gb_ref[...].astype(F32)) * bs)
    h = x_ref[...] + _dot_split(merged.astype(BF16), wo_ref[...])
    h_ref[...] = h
    ms = jnp.mean(h * h, axis=-1, keepdims=True)
    hn = h * lax.rsqrt(ms + EPS) * g_ref[...]
    hn_ref[...] = _pack_halves(hn)

    hn_hi = hn.astype(BF16)
    hn_lo = (hn - hn_hi.astype(F32)).astype(BF16)
    logits = (_dot_split(hn_hi, wrh_ref[...]) + _dot_split(hn_lo, wrh_ref[...])
              + _dot_split(hn_hi, wrl_ref[...]))
    logits = logits + br_ref[...]
    lane = lax.broadcasted_iota(I32, logits.shape, 1)
    vals, ids = [], []
    for _ in range(TOP_K_EXPERTS):
        mx = jnp.max(logits, axis=-1, keepdims=True)
        ix = jnp.min(jnp.where(logits == mx, lane, LANES), axis=-1, keepdims=True)
        vals.append(mx)
        ids.append(ix)
        logits = jnp.where(lane == ix, -jnp.inf, logits)
    es = [jnp.exp(v - vals[0]) for v in vals]
    inv = 1.0 / (es[0] + es[1] + es[2] + es[3])
    route = jnp.zeros(logits.shape, F32)
    for k in range(TOP_K_EXPERTS):
        route = jnp.where(lane == k, es[k] * inv, route)
        route = jnp.where(lane == TOP_K_EXPERTS + k, ids[k].astype(F32), route)
    route_ref[...] = route


def _merge(x2, y_diff, y_dsa, proj, w_bd, w_bs, w_out, g_ffn, w_router_hi, w_router_lo, b_router, tm):
    n, d = x2.shape
    row = lambda i: (i, 0)
    const = lambda i: (0, 0)
    return pl.pallas_call(
        _merge_kernel,
        out_shape=(jax.ShapeDtypeStruct((n, d), F32),
                   jax.ShapeDtypeStruct((n, d // 2), I32),
                   jax.ShapeDtypeStruct((n, LANES), F32)),
        grid=(n // tm,),
        in_specs=[pl.BlockSpec((tm, d), row),
                  pl.BlockSpec((tm, d), row),
                  pl.BlockSpec((tm, d), row),
                  pl.BlockSpec((tm, d), lambda i: (i, COL_GA // 1024)),
                  pl.BlockSpec((tm, d), lambda i: (i, COL_GB // 1024)),
                  pl.BlockSpec((d, d), const),
                  pl.BlockSpec((d, d), const),
                  pl.BlockSpec((d, d), const),
                  pl.BlockSpec((1, d), const),
                  pl.BlockSpec((d, LANES), const),
                  pl.BlockSpec((d, LANES), const),
                  pl.BlockSpec((1, LANES), const)],
        out_specs=(pl.BlockSpec((tm, d), row),
                   pl.BlockSpec((tm, d // 2), row),
                   pl.BlockSpec((tm, LANES), row)),
        compiler_params=_cparams(("parallel",)),
        name="merge_router",
    )(x2, y_diff, y_dsa, proj, proj, w_bd, w_bs, w_out, g_ffn, w_router_hi, w_router_lo, b_router)


def _regroup_kernel(w_ref, p_ref, o_ref):
    pw = p_ref.shape[0]
    for j in range(w_ref.shape[2] // pw):
        w = w_ref[0, :, j * pw:(j + 1) * pw].astype(BF16)
        o_ref[0, :, j * pw:(j + 1) * pw] = jnp.dot(w, p_ref[...], preferred_element_type=F32).astype(BF16)


def _regroup_gate_up(w_gu, rows):
    e, d, f2 = w_gu.shape
    pw = 2 * LANES
    src = jnp.arange(pw, dtype=I32)
    dst = (src % 2) * LANES + src // 2
    perm = (dst[:, None] == jnp.arange(pw, dtype=I32)[None, :]).astype(BF16)
    return pl.pallas_call(
        _regroup_kernel,
        out_shape=jax.ShapeDtypeStruct((e, d, f2), BF16),
        grid=(e, d // rows),
        in_specs=[pl.BlockSpec((1, rows, f2), lambda i, j: (i, j, 0)),
                  pl.BlockSpec((pw, pw), lambda i, j: (0, 0))],
        out_specs=pl.BlockSpec((1, rows, f2), lambda i, j: (i, j, 0)),
        compiler_params=_cparams(("parallel", "parallel")),
        name="regroup_gate_up",
    )(w_gu, perm)


def _ffn_kernel(be_ref, nu_ref, x_ref, wgu_ref, wd_ref, bgu_ref, bd_ref, o_ref):
    @pl.when(pl.program_id(0) < nu_ref[0])
    def _():
        x_lo, x_hi = _unpack_halves(x_ref[...])
        half = x_lo.shape[1]
        gu = (jnp.dot(x_lo.astype(BF16), wgu_ref[0, 0:half, :], preferred_element_type=F32)
              + jnp.dot(x_hi.astype(BF16), wgu_ref[0, half:2 * half, :], preferred_element_type=F32)
              + bgu_ref[0])
        acts = []
        for j in range(gu.shape[1] // (2 * LANES)):
            gate = jnp.minimum(gu[:, 2 * j * LANES:(2 * j + 1) * LANES], SWIGLU_LIMIT)
            up = jnp.clip(gu[:, (2 * j + 1) * LANES:(2 * j + 2) * LANES], -SWIGLU_LIMIT, SWIGLU_LIMIT)
            glu = gate * jax.nn.sigmoid(gate * SWIGLU_ALPHA)
            acts.append(((up + 1.0) * glu).astype(BF16))
        a = jnp.concatenate(acts, axis=1)
        y = jnp.dot(a, wd_ref[0], preferred_element_type=F32) + bd_ref[0]
        o_ref[...] = _pack_halves(y)

    @pl.when(pl.program_id(0) >= nu_ref[0])
    def _():
        o_ref[...] = jnp.zeros(o_ref.shape, o_ref.dtype)


def _expert_ffn(blk_exp, n_used, xs, wgu, wd, bgu, bd):
    p, dw = xs.shape
    f, d = wd.shape[1], wd.shape[2]
    nblk = p // MOE_ROWS
    wmap = lambda i, be, nu: (be[i], 0, 0)
    grid_spec = pltpu.PrefetchScalarGridSpec(
        num_scalar_prefetch=2,
        grid=(nblk,),
        in_specs=[pl.BlockSpec((MOE_ROWS, dw), lambda i, be, nu: (i, 0)),
                  pl.BlockSpec((1, d, 2 * f), wmap),
                  pl.BlockSpec((1, f, d), wmap),
                  pl.BlockSpec((1, 1, 2 * f), wmap),
                  pl.BlockSpec((1, 1, d), wmap)],
        out_specs=pl.BlockSpec((MOE_ROWS, dw), lambda i, be, nu: (i, 0)),
    )
    return pl.pallas_call(
        _ffn_kernel,
        out_shape=jax.ShapeDtypeStruct((p, dw), I32),
        grid_spec=grid_spec,
        compiler_params=_cparams(("arbitrary",)),
        name="expert_ffn",
    )(blk_exp, n_used, xs, wgu, wd, bgu, bd)


def _route_kernel(route_ref, dest_ref, cnt_ref, u_ref, carry_ref, pstart_ref, *, block_rows):
    ph, i = pl.program_id(0), pl.program_id(1)
    tm = route_ref.shape[0]

    @pl.when((ph == 0) & (i == 0))
    def _():
        r = lax.broadcasted_iota(I32, (tm, tm), 0)
        c = lax.broadcasted_iota(I32, (tm, tm), 1)
        u_ref[...] = jnp.where(r < c, 1.0, 0.0).astype(BF16)
        carry_ref[...] = jnp.zeros(carry_ref.shape, F32)

    @pl.when((ph == 1) & (i == 0))
    def _():
        counts = carry_ref[...]
        cnt_ref[...] = counts
        padded = jnp.ceil(counts * (1.0 / block_rows)) * block_rows
        r = lax.broadcasted_iota(I32, (LANES, LANES), 0)
        c = lax.broadcasted_iota(I32, (LANES, LANES), 1)
        lower = jnp.where(c < r, 1.0, 0.0).astype(F32)
        pstart_ref[...] = jnp.dot(lower, padded, preferred_element_type=F32, precision=lax.Precision.HIGHEST)
        carry_ref[...] = jnp.zeros(carry_ref.shape, F32)

    rt = route_ref[...].T
    sub = lax.broadcasted_iota(I32, (LANES, tm), 0)
    hits = [sub == rt[TOP_K_EXPERTS + k:TOP_K_EXPERTS + k + 1, :].astype(I32) for k in range(TOP_K_EXPERTS)]
    m = jnp.zeros((LANES, tm), F32)
    for hit in hits:
        m = m + jnp.where(hit, 1.0, 0.0)
    tile_counts = jnp.broadcast_to(jnp.sum(m, axis=1, keepdims=True), (LANES, LANES))

    @pl.when(ph == 0)
    def _():
        dest_ref[...] = jnp.zeros(dest_ref.shape, I32)

    @pl.when(ph == 1)
    def _():
        prefix = jnp.dot(m.astype(BF16), u_ref[...], preferred_element_type=F32)
        rank = prefix + (pstart_ref[:, 0:1] + carry_ref[:, 0:1])
        rows = [jnp.sum(jnp.where(hit, rank, 0.0), axis=0, keepdims=True) for hit in hits]
        rows.append(jnp.zeros((dest_ref.shape[0] - TOP_K_EXPERTS, tm), F32))
        dest_ref[...] = jnp.concatenate(rows, axis=0).astype(I32)

    carry_ref[...] = carry_ref[...] + tile_counts


def _route_rows(route, tm):
    n = route.shape[0]
    nt = n // tm
    kernel = functools.partial(_route_kernel, block_rows=MOE_ROWS)
    return pl.pallas_call(
        kernel,
        out_shape=(jax.ShapeDtypeStruct((8, n + tm), I32), jax.ShapeDtypeStruct((LANES, LANES), F32)),
        grid=(2, nt),
        in_specs=[pl.BlockSpec((tm, LANES), lambda ph, i: (i, 0))],
        out_specs=(pl.BlockSpec((8, tm), lambda ph, i: (0, ph * i + (1 - ph) * nt)),
                   pl.BlockSpec((LANES, LANES), lambda ph, i: (0, 0))),
        scratch_shapes=[pltpu.VMEM((tm, tm), BF16),
                        pltpu.VMEM((LANES, LANES), F32),
                        pltpu.VMEM((LANES, LANES), F32)],
        compiler_params=_cparams(("arbitrary", "arbitrary")),
        name="route_rows",
    )(route)


SC_WINDOW = 128
SC_WORKERS = 32


def _sc_mesh():
    return plsc.VectorSubcoreMesh(core_axis_name="c", subcore_axis_name="s")


def _sc_scatter_rows(src, dest, p):
    n, d = src.shape
    per = n // (SC_WINDOW * SC_WORKERS)

    @pl.kernel(out_type=jax.ShapeDtypeStruct((p, d), src.dtype), mesh=_sc_mesh(),
               scratch_types=[pltpu.VMEM((dest.shape[0], SC_WINDOW), I32), pltpu.VMEM((SC_WINDOW, d), src.dtype)])
    def scatter(src_hbm, idx_hbm, out_hbm, idx_vmem, buf):
        wid = lax.axis_index("c") * (SC_WORKERS // 2) + lax.axis_index("s")

        @pl.loop(0, per)
        def _(j):
            off = (wid * per + j) * SC_WINDOW
            pltpu.sync_copy(idx_hbm.at[:, pl.ds(off, SC_WINDOW)], idx_vmem)
            pltpu.sync_copy(src_hbm.at[pl.ds(off, SC_WINDOW), :], buf)
            for k in range(TOP_K_EXPERTS):
                pltpu.sync_copy(buf, out_hbm.at[idx_vmem.at[k]])

    return scatter(src, dest)


def _sc_gather_rows(src, dest):
    n = dest.shape[1]
    d = src.shape[1]
    per = n // (SC_WINDOW * SC_WORKERS)

    @pl.kernel(out_type=jax.ShapeDtypeStruct((TOP_K_EXPERTS * n, d), src.dtype), mesh=_sc_mesh(),
               scratch_types=[pltpu.VMEM((dest.shape[0], SC_WINDOW), I32), pltpu.VMEM((SC_WINDOW, d), src.dtype)])
    def gather(src_hbm, idx_hbm, out_hbm, idx_vmem, buf):
        wid = lax.axis_index("c") * (SC_WORKERS // 2) + lax.axis_index("s")

        @pl.loop(0, per)
        def _(j):
            off = (wid * per + j) * SC_WINDOW
            pltpu.sync_copy(idx_hbm.at[:, pl.ds(off, SC_WINDOW)], idx_vmem)
            for k in range(TOP_K_EXPERTS):
                pltpu.sync_copy(src_hbm.at[idx_vmem.at[k]], buf)
                pltpu.sync_copy(buf, out_hbm.at[pl.ds(k * n + off, SC_WINDOW), :])

    return gather(src, dest)


def _combine_kernel(h_ref, y_ref, route_ref, g_ref, o_ref):
    half = h_ref.shape[1] // 2
    h_lo, h_hi = h_ref[:, 0:half], h_ref[:, half:2 * half]
    route = route_ref[...]
    for k in range(TOP_K_EXPERTS):
        y_lo, y_hi = _unpack_halves(y_ref[k])
        gate = route[:, k:k + 1]
        h_lo = h_lo + gate * y_lo
        h_hi = h_hi + gate * y_hi
    ms = (jnp.sum(h_lo * h_lo, axis=-1, keepdims=True)
          + jnp.sum(h_hi * h_hi, axis=-1, keepdims=True)) * (1.0 / (2 * half))
    inv = lax.rsqrt(ms + EPS)
    o_ref[:, 0:half] = h_lo * inv * g_ref[:, 0:half]
    o_ref[:, half:2 * half] = h_hi * inv * g_ref[:, half:2 * half]


def _combine(h1, yg, route, g_final, tm):
    n, d = h1.shape
    return pl.pallas_call(
        _combine_kernel,
        out_shape=jax.ShapeDtypeStruct((n, d), F32),
        grid=(n // tm,),
        in_specs=[pl.BlockSpec((tm, d), lambda i: (i, 0)),
                  pl.BlockSpec((TOP_K_EXPERTS, tm, d // 2), lambda i: (0, i, 0)),
                  pl.BlockSpec((tm, LANES), lambda i: (i, 0)),
                  pl.BlockSpec((1, d), lambda i: (0, 0))],
        out_specs=pl.BlockSpec((tm, d), lambda i: (i, 0)),
        compiler_params=_cparams(("parallel",)),
        name="combine_norm",
    )(h1, yg, route, g_final)


def _t5_bucket(dist):
    n = jnp.maximum(dist, 0)
    max_exact = N_BUCKETS // 2
    nf = jnp.maximum(n, 1).astype(F32)
    large = max_exact + (jnp.log(nf / max_exact) / math.log(MAX_DISTANCE / max_exact)
                         * (N_BUCKETS - max_exact)).astype(I32)
    large = jnp.minimum(large, N_BUCKETS - 1)
    return jnp.where(n < max_exact, n, large)


def _bias_blocks(bias_tab):
    t = LANES
    assert MAX_DISTANCE <= LANES
    r = jnp.arange(t, dtype=I32)[:, None]
    c = jnp.arange(t, dtype=I32)[None, :]
    rel = (bias_tab - bias_tab[N_BUCKETS - 1][None, :]).astype(F32)
    tiles = []
    buckets = jnp.arange(N_BUCKETS, dtype=I32)[:, None, None]
    for delta in (0, t):
        dist = r - c + delta
        hit = _t5_bucket(dist)[None] == buckets
        b = jnp.sum(jnp.where(hit[:, None], rel[:, :, None, None], 0.0), axis=0)
        tiles.append(jnp.where((dist >= 0)[None], b * LOG2E, MASK_VALUE))
    return jnp.stack(tiles)


def _regroup_w_in(w_in):
    sizes = (1024, 1024, 1024, 1024, KV_LATENT, 1024, HEAD_DIM_IDX, N_HEADS_IDX, D_MODEL, D_MODEL)
    parts, off = [], 0
    for sz in sizes:
        parts.append(w_in[:, off:off + sz])
        off += sz
    dq, dk, dv, sq, ckv, iq, ik, iw, ga, gb = parts
    dq = dq * (HEAD_DIM_DIFF ** -0.5 * LOG2E)
    iw = iw * ((N_HEADS_IDX ** -0.5) * (HEAD_DIM_IDX ** -0.5))
    pad = jnp.zeros((w_in.shape[0], PROJ_WIDTH - COL_IW - N_HEADS_IDX), w_in.dtype)
    w = jnp.concatenate([dq, dk, dv, sq, iq, ga, gb, ckv, ik, ik, iw, pad], axis=1)
    return w.astype(BF16)


def _block_tables(counts, n_assign):
    e, bm = N_EXPERTS, MOE_ROWS
    padded = (counts + bm - 1) // bm * bm
    pends = jnp.cumsum(padded)
    nblk = -(-(n_assign + e * (bm - 1)) // bm)
    first_row = jnp.arange(nblk, dtype=I32) * bm
    blk_exp = jnp.minimum(jnp.sum((pends[None, :] <= first_row[:, None]).astype(I32), axis=1), e - 1)
    n_used = (pends[-1] // bm).astype(I32).reshape(1)
    return blk_exp, n_used, nblk


def kernel(x, norm_attn_g, w_in, rel_bias, lam_q1, lam_k1, lam_q2, lam_k2, diff_subln_g, kv_norm_g, w_uk, w_uv,
           w_branch_diff, w_branch_dsa, w_out, norm_ffn_g, w_router, b_router, w_gate_up, b_gate_up, w_down,
           b_down, norm_final_g):
    batch, seq, d = x.shape
    n = batch * seq
    assert norm_attn_g.shape[0] == 1, "single-layer kernel"
    assert seq % DIFF_BLOCK == 0 and seq % KEY_CHUNK == 0 and d == D_MODEL
    row_tile = math.gcd(n, 1024)

    x2 = x.reshape(n, d)
    proj = _inproj(x2, norm_attn_g[0].reshape(1, d), _regroup_w_in(w_in[0]), row_tile, 1280)

    lam_init = 0.8 - 0.6 * math.exp(-0.3 * 0)
    lam = (jnp.exp(jnp.sum(lam_q1[0].astype(F32) * lam_k1[0].astype(F32)))
           - jnp.exp(jnp.sum(lam_q2[0].astype(F32) * lam_k2[0].astype(F32))) + lam_init)
    y_diff = _diff_attention(proj, lam.reshape(1, 1).astype(F32), _bias_blocks(rel_bias[:, :N_HEADS_DIFF]),
                             diff_subln_g[0].reshape(1, -1).astype(F32), batch, seq, 1.0 - lam_init)

    y_dsa = _dsa_attention(proj, kv_norm_g[0].reshape(1, -1).astype(F32),
                           w_uk[0].transpose(0, 2, 1).astype(BF16), w_uv[0].astype(BF16),
                           _bias_blocks(rel_bias[:, N_HEADS_DIFF:]), batch, seq, min(TOPK_MAX, seq // 4))

    w_r = jnp.zeros((d, LANES), F32).at[:, :N_EXPERTS].set(w_router[0].astype(F32))
    b_r = jnp.full((1, LANES), MASK_VALUE, F32).at[0, :N_EXPERTS].set(b_router[0].astype(F32))
    h1, hn, route = _merge(x2, y_diff, y_dsa, proj, w_branch_diff[0].astype(BF16), w_branch_dsa[0].astype(BF16),
                           w_out[0].astype(BF16), norm_ffn_g[0].reshape(1, d).astype(F32),
                           w_r.astype(BF16), (w_r - w_r.astype(BF16).astype(F32)).astype(BF16), b_r,
                           math.gcd(n, 512))

    dest, counts = _route_rows(route, math.gcd(n, 1024))
    dest = dest[:, :n]
    blk_exp, n_used, nblk = _block_tables(counts[:N_EXPERTS, 0].astype(I32), n * TOP_K_EXPERTS)
    xs = _sc_scatter_rows(hn, dest, nblk * MOE_ROWS)
    e, f = N_EXPERTS, D_EXPERT
    b_gu = b_gate_up[0].astype(F32).reshape(e, f // LANES, LANES, 2).transpose(0, 1, 3, 2).reshape(e, 1, 2 * f)
    ys = _expert_ffn(blk_exp, n_used, xs, _regroup_gate_up(w_gate_up[0], 512), w_down[0].astype(BF16),
                     b_gu, b_down[0][:, None, :].astype(F32))
    yg = _sc_gather_rows(ys, dest).reshape(TOP_K_EXPERTS, n, d // 2)
    out = _combine(h1, yg, route, norm_final_g.reshape(1, d).astype(F32), math.gcd(n, 512))
    return out.reshape(batch, seq, d)
```

```python
import functools
import math

import jax
import jax.numpy as jnp
from jax import lax
from jax.experimental import pallas as pl
from jax.experimental.pallas import tpu as pltpu
from jax.experimental.pallas import tpu_sc as plsc

F32 = jnp.float32
BF16 = jnp.bfloat16
I32 = jnp.int32

D_MODEL = 1024
N_HEADS_DIFF = 8
HEAD_DIM_DIFF = 64
N_HEADS_DSA = 8
HEAD_DIM_DSA = 128
KV_LATENT = 256
N_HEADS_IDX = 16
HEAD_DIM_IDX = 64
TOPK_MAX = 256
N_BUCKETS = 32
MAX_DISTANCE = 128
N_EXPERTS = 32
TOP_K_EXPERTS = 4
D_EXPERT = 1024
SWIGLU_LIMIT = 7.0
SWIGLU_ALPHA = 1.702
EPS = 1e-6

LANES = 128
DIFF_BLOCK = 512
DIFF_HEADS_PER_STEP = 4
DSA_BLOCK = 256
KEY_CHUNK = 512
DSA_STREAMS = 2
MOE_ROWS = 512
PROJ_WIDTH = 7680
VMEM_LIMIT = 56 * 1024 * 1024

COL_DQ, COL_DK, COL_DV, COL_SQ, COL_IQ, COL_GA, COL_GB = (i * 1024 for i in range(7))
COL_CKV = 7168
COL_IK = 7424
COL_IW = 7552

LOG2E = math.log2(math.e)
MASK_VALUE = -1e30
M_INIT = -1e29
INT_MIN = -2 ** 31


def _cparams(sem):
    return pltpu.CompilerParams(dimension_semantics=sem, vmem_limit_bytes=VMEM_LIMIT)


def _inproj_kernel(x_ref, g_ref, w_ref, o_ref, xn_ref):
    @pl.when(pl.program_id(1) == 0)
    def _():
        x = x_ref[...]
        ms = jnp.mean(x * x, axis=-1, keepdims=True)
        xn_ref[...] = (x * lax.rsqrt(ms + EPS) * g_ref[...]).astype(BF16)

    o_ref[...] = jnp.dot(xn_ref[...], w_ref[...], preferred_element_type=F32).astype(o_ref.dtype)


def _inproj(x2, g, w, tm, tn):
    n, d = x2.shape
    width = w.shape[1]
    return pl.pallas_call(
        _inproj_kernel,
        out_shape=jax.ShapeDtypeStruct((n, width), BF16),
        grid=(n // tm, width // tn),
        in_specs=[pl.BlockSpec((tm, d), lambda i, j: (i, 0)),
                  pl.BlockSpec((1, d), lambda i, j: (0, 0)),
                  pl.BlockSpec((d, tn), lambda i, j: (0, j))],
        out_specs=pl.BlockSpec((tm, tn), lambda i, j: (i, j)),
        scratch_shapes=[pltpu.VMEM((tm, d), BF16)],
        compiler_params=_cparams(("parallel", "arbitrary")),
        name="inproj",
    )(x2, g, w)


def _lane_chunk_max(s):
    smax = s[:, 0:LANES]
    for j in range(1, s.shape[1] // LANES):
        smax = jnp.maximum(smax, s[:, j * LANES:(j + 1) * LANES])
    return smax


def _softmax_step(s_ref, tk, v, m_ref, l_ref, acc_ref, smax=None, split=True):
    nl = tk // LANES
    if smax is None:
        smax = _lane_chunk_max(s_ref[:, 0:tk])
    m_prev = m_ref[...]
    m_new = jnp.maximum(m_prev, jnp.max(smax, axis=-1, keepdims=True))
    alpha = jnp.exp2(m_prev - m_new)
    psum = None
    ps = []
    for j in range(nl):
        pj = jnp.exp2(s_ref[:, j * LANES:(j + 1) * LANES] - m_new)
        psum = pj if psum is None else psum + pj
        ps.append(pj.astype(BF16))
    l_ref[...] = alpha * l_ref[...] + psum
    pv = _dot_split(jnp.concatenate(ps, axis=1), v, split)
    e = acc_ref.shape[1]
    a = alpha if e == LANES else jnp.concatenate([alpha] * (e // LANES), axis=1)
    acc_ref[...] = a * acc_ref[...] + pv
    m_ref[...] = m_new


def _softmax_init(m_ref, l_ref, acc_ref):
    m_ref[...] = jnp.full(m_ref.shape, M_INIT, F32)
    l_ref[...] = jnp.zeros(l_ref.shape, F32)
    acc_ref[...] = jnp.zeros(acc_ref.shape, F32)


def _softmax_result(l_ref, acc_ref):
    return acc_ref[...] * (1.0 / jnp.sum(l_ref[...], axis=-1, keepdims=True))


def _near_bias(s_ref, d0, d1, delta, groups, t, tk):
    for g in range(groups):
        for rb in range(t // LANES):
            for cb in range(tk // LANES):
                bd = delta + rb - cb
                rows = slice(g * t + rb * LANES, g * t + (rb + 1) * LANES)
                cols = slice(cb * LANES, (cb + 1) * LANES)
                if bd == 0:
                    s_ref[rows, cols] = s_ref[rows, cols] + d0(g)
                elif bd == 1:
                    s_ref[rows, cols] = s_ref[rows, cols] + d1(g)
                elif bd < 0:
                    s_ref[rows, cols] = jnp.full((LANES, LANES), MASK_VALUE, F32)


def _pipelined_far(n_far, issue, finish):
    odd = n_far % 2

    @pl.when(odd == 1)
    def _():
        issue(0, 1)
        issue(1, 0)
        finish(0, 1)

    @pl.when(odd == 0)
    def _():
        issue(0, 0)

    def body(j, carry):
        a = odd + 2 * j
        issue(a + 1, 1)
        finish(a, 0)
        issue(a + 2, 0)
        finish(a + 1, 1)
        return carry

    lax.fori_loop(0, (n_far - odd) // 2, body, 0)


def _dot_nt(a, b, split=True):
    dn = (((1,), (1,)), ((), ()))
    if not split:
        return lax.dot_general(a, b, dn, preferred_element_type=F32)
    h = a.shape[0] // 2
    return jnp.concatenate([lax.dot_general(a[:h], b, dn, preferred_element_type=F32),
                            lax.dot_general(a[h:], b, dn, preferred_element_type=F32)], axis=0)


def _dot_split(a, b, split=True):
    if not split:
        return jnp.dot(a, b, preferred_element_type=F32)
    h = a.shape[0] // 2
    return jnp.concatenate([jnp.dot(a[:h], b, preferred_element_type=F32),
                            jnp.dot(a[h:], b, preferred_element_type=F32)], axis=0)


def _diff_kernel(lam_ref, q_ref, k_ref, v_ref, bias_ref, g_ref, o_ref, q2_ref, s_ref, m_ref, l_ref, acc_ref,
                 *, out_scale, heads):
    t = q_ref.shape[0]
    e = 2 * HEAD_DIM_DIFF
    qi = pl.program_id(2)
    cols = [slice(hh * e, (hh + 1) * e) for hh in range(heads)]

    lane = lax.broadcasted_iota(I32, (t, e), 1)
    for hh in range(heads):
        q = q_ref[:, cols[hh]]
        zero = jnp.zeros_like(q)
        q2_ref[hh, 0:t, :] = jnp.where(lane < HEAD_DIM_DIFF, q, zero)
        q2_ref[hh, t:2 * t, :] = jnp.where(lane >= HEAD_DIM_DIFF, q, zero)
        _softmax_init(m_ref.at[hh], l_ref.at[hh], acc_ref.at[hh])

    def chunk(kc, delta):
        off = pl.multiple_of(kc * t, t)
        for hh in range(heads):
            s = _dot_nt(q2_ref[hh], k_ref[pl.ds(off, t), cols[hh]])
            s_ref[hh] = s
            smax = None
            if delta is None:
                smax = _lane_chunk_max(s)
            else:
                _near_bias(s_ref.at[hh], lambda g: bias_ref[0, hh], lambda g: bias_ref[1, hh], delta, 2, t, t)
            _softmax_step(s_ref.at[hh], t, v_ref[pl.ds(off, t), cols[hh]], m_ref.at[hh], l_ref.at[hh],
                          acc_ref.at[hh], smax)

    def far_body(kc, carry):
        chunk(kc, None)
        return carry

    lax.fori_loop(0, jnp.maximum(qi - 1, 0), far_body, 0)

    @pl.when(qi >= 1)
    def _():
        chunk(qi - 1, t // LANES)

    chunk(qi, 0)

    for hh in range(heads):
        o = _softmax_result(l_ref.at[hh], acc_ref.at[hh])
        o = o[0:t, :] - lam_ref[0, 0] * o[t:2 * t, :]
        ms = jnp.mean(o * o, axis=-1, keepdims=True)
        o_ref[:, cols[hh]] = (o * lax.rsqrt(ms + EPS) * g_ref[...] * out_scale).astype(o_ref.dtype)


def _diff_attention(proj, lam, bias_blocks, subln_g, batch, seq, out_scale):
    t = DIFF_BLOCK
    nq = seq // t
    h = N_HEADS_DIFF
    e = 2 * HEAD_DIM_DIFF
    hp = DIFF_HEADS_PER_STEP
    w = hp * e
    kernel = functools.partial(_diff_kernel, out_scale=out_scale, heads=hp)
    return pl.pallas_call(
        kernel,
        out_shape=jax.ShapeDtypeStruct((batch * seq, h * e), BF16),
        grid=(batch, h // hp, nq),
        in_specs=[pl.BlockSpec(memory_space=pltpu.SMEM),
                  pl.BlockSpec((t, w), lambda b, hh, qi: (b * nq + qi, COL_DQ // w + hh)),
                  pl.BlockSpec((seq, w), lambda b, hh, qi: (b, COL_DK // w + hh)),
                  pl.BlockSpec((seq, w), lambda b, hh, qi: (b, COL_DV // w + hh)),
                  pl.BlockSpec((2, hp, LANES, LANES), lambda b, hh, qi: (0, hh, 0, 0)),
                  pl.BlockSpec((1, e), lambda b, hh, qi: (0, 0))],
        out_specs=pl.BlockSpec((t, w), lambda b, hh, qi: (b * nq + qi, hh)),
        scratch_shapes=[pltpu.VMEM((hp, 2 * t, e), BF16),
                        pltpu.VMEM((hp, 2 * t, t), F32),
                        pltpu.VMEM((hp, 2 * t, LANES), F32),
                        pltpu.VMEM((hp, 2 * t, LANES), F32),
                        pltpu.VMEM((hp, 2 * t, e), F32)],
        compiler_params=_cparams(("parallel", "parallel", "arbitrary")),
        name="diff_attn",
    )(lam, proj, proj, proj, bias_blocks, subln_g)


def _sortable_key(x):
    bits = lax.bitcast_convert_type(x, I32)
    return bits ^ ((bits >> 31) & jnp.int32(0x7FFFFFFF))


def _dsa_kernel(iq_ref, sq_ref, iw_ref, ik_ref, ckv_ref, kvg_ref, wuk_ref, wuv_ref, bias_ref, o_ref,
                c_ref, key_ref, keyt_ref, qi_ref, wb_ref, ql_ref, s_ref, m_ref, l_ref, acc_ref, *, topk, scale):
    t = iq_ref.shape[0]
    tk = KEY_CHUNK
    qi = pl.program_id(1)
    n_chunks = qi + 1
    hi, hb = N_HEADS_IDX, N_HEADS_DSA

    @pl.when(qi == 0)
    def _():
        ckv = ckv_ref[...].astype(F32)
        ms = jnp.mean(ckv * ckv, axis=-1, keepdims=True)
        c_ref[...] = (ckv * lax.rsqrt(ms + EPS) * kvg_ref[...]).astype(BF16)

    lane = lax.broadcasted_iota(I32, (t, LANES), 1)
    for h in range(hi):
        blk = iq_ref[:, (h // 2) * LANES:(h // 2 + 1) * LANES]
        keep = (lane < HEAD_DIM_IDX) if h % 2 == 0 else (lane >= HEAD_DIM_IDX)
        qi_ref[h * t:(h + 1) * t, :] = jnp.where(keep, blk, jnp.zeros_like(blk))
        wb_ref[h] = jnp.broadcast_to(iw_ref[:, h:h + 1].astype(F32), (t, LANES))

    def score_chunk(kc, diag):
        off = pl.multiple_of(kc * t, t)
        d = _dot_nt(qi_ref[...], ik_ref[pl.ds(off, t), :]).reshape(hi, t, t)
        sc = jnp.zeros((t, t), F32)
        for h in range(hi):
            w = wb_ref[h]
            w = jnp.concatenate([w] * (t // LANES), axis=1)
            sc = sc + jnp.maximum(d[h], 0.0) * w
        key = _sortable_key(sc + 0.0)
        if diag:
            row = lax.broadcasted_iota(I32, (t, t), 0)
            col = lax.broadcasted_iota(I32, (t, t), 1)
            key = jnp.where(col <= row, key, jnp.int32(INT_MIN))
        key_ref[:, pl.ds(off, t)] = key
        keyt_ref[pl.ds(off, t), :] = key.T

    def score_body(kc, carry):
        score_chunk(kc, False)
        return carry

    lax.fori_loop(0, qi, score_body, 0)
    score_chunk(qi, True)

    @pl.when(qi % 2 == 0)
    def _():
        off = pl.multiple_of((qi + 1) * t, t)
        key_ref[:, pl.ds(off, t)] = jnp.full((t, t), INT_MIN, I32)
        keyt_ref[pl.ds(off, t), :] = jnp.full((t, t), INT_MIN, I32)

    n_steps = (qi + 2) // 2

    def bit_body(i, cur):
        bit = lax.shift_left(jnp.int32(1), 31 - i)
        cand = cur | bit
        cand_s = (cand ^ jnp.int32(INT_MIN))[None]

        def body(kc, cnt):
            off = pl.multiple_of(kc * tk, tk)
            k = keyt_ref[pl.ds(off, tk), :].reshape(tk // 8, 8, t)
            return cnt + jnp.sum(jnp.where(k >= cand_s, 1, 0), axis=0)

        cnt = lax.fori_loop(0, n_steps, body, jnp.zeros((8, t), I32))
        total = jnp.sum(cnt, axis=0, keepdims=True)
        return jnp.where(total >= topk, cand, cur)

    cur = lax.fori_loop(0, 32, bit_body, jnp.zeros((8, t), I32))
    thr = jnp.maximum(cur ^ jnp.int32(INT_MIN), jnp.int32(INT_MIN + 1))
    thr_b = jnp.broadcast_to(thr[0:1, :], (LANES, t)).T
    thr_w = jnp.concatenate([thr_b] * (tk // LANES), axis=1)

    def mask_body(kc, carry):
        off = pl.multiple_of(kc * tk, tk)
        am = jnp.where(key_ref[:, pl.ds(off, tk)] >= thr_w, 0.0, MASK_VALUE).astype(F32)
        key_ref[:, pl.ds(off, tk)] = lax.bitcast_convert_type(am, I32)
        return carry

    lax.fori_loop(0, n_steps, mask_body, 0)

    streams = ql_ref.shape[0]
    split = streams == 1
    hs = hb // streams
    for h in range(hb):
        qh = sq_ref[:, h * HEAD_DIM_DSA:(h + 1) * HEAD_DIM_DSA]
        ql = jnp.dot(qh, wuk_ref[h], preferred_element_type=F32) * scale
        ql_ref[h // hs, (h % hs) * t:(h % hs + 1) * t, :] = ql.astype(BF16)
    for g in range(streams):
        _softmax_init(m_ref.at[g], l_ref.at[g], acc_ref.at[g])

    def chunk(kc, width, delta):
        off = pl.multiple_of(kc * tk, tk)
        c = c_ref[pl.ds(off, width), :]
        am = lax.bitcast_convert_type(key_ref[:, pl.ds(off, width)], F32)
        for g in range(streams):
            s = (_dot_nt(ql_ref[g], c, split).reshape(hs, t, width) + am[None]).reshape(hs * t, width)
            s_ref[g, :, 0:width] = s
            smax = None
            if delta is None:
                smax = _lane_chunk_max(s)
            else:
                _near_bias(s_ref.at[g], lambda j: bias_ref[0, g * hs + j], lambda j: bias_ref[1, g * hs + j],
                           delta, hs, t, width)
            _softmax_step(s_ref.at[g], width, c, m_ref.at[g], l_ref.at[g], acc_ref.at[g], smax, split)

    def far_body(kc, carry):
        chunk(kc, tk, None)
        return carry

    lax.fori_loop(0, jnp.maximum((qi - 1) // 2, 0), far_body, 0)
    half = qi // 2

    @pl.when(qi % 2 == 1)
    def _():
        chunk(half, tk, t // LANES)

    @pl.when(qi % 2 == 0)
    def _():
        @pl.when(half >= 1)
        def _():
            chunk(half - 1, tk, tk // LANES)
        chunk(half, t, 0)

    for h in range(hb):
        g, j = h // hs, h % hs
        inv_l = 1.0 / jnp.sum(l_ref[g, j * t:(j + 1) * t, :], axis=-1, keepdims=True)
        ol = (acc_ref[g, j * t:(j + 1) * t, :] * inv_l).astype(BF16)
        o = jnp.dot(ol, wuv_ref[h], preferred_element_type=F32)
        o_ref[:, h * HEAD_DIM_DSA:(h + 1) * HEAD_DIM_DSA] = o.astype(o_ref.dtype)


def _dsa_attention(proj, kv_g, w_ukt, w_uv, bias_blocks, batch, seq, topk):
    t = DSA_BLOCK
    nq = seq // t
    hb, hi = N_HEADS_DSA, N_HEADS_IDX
    width = hb * HEAD_DIM_DSA
    ns = DSA_STREAMS
    rs = hb // ns * t
    kernel = functools.partial(_dsa_kernel, topk=topk, scale=HEAD_DIM_DSA ** -0.5 * LOG2E)
    return pl.pallas_call(
        kernel,
        out_shape=jax.ShapeDtypeStruct((batch * seq, width), BF16),
        grid=(batch, nq),
        in_specs=[pl.BlockSpec((t, 1024), lambda b, qi: (b * nq + qi, COL_IQ // 1024)),
                  pl.BlockSpec((t, 1024), lambda b, qi: (b * nq + qi, COL_SQ // 1024)),
                  pl.BlockSpec((t, LANES), lambda b, qi: (b * nq + qi, COL_IW // LANES)),
                  pl.BlockSpec((seq, LANES), lambda b, qi: (b, COL_IK // LANES)),
                  pl.BlockSpec((seq, KV_LATENT), lambda b, qi: (b, COL_CKV // KV_LATENT)),
                  pl.BlockSpec((1, KV_LATENT), lambda b, qi: (0, 0)),
                  pl.BlockSpec((hb, HEAD_DIM_DSA, KV_LATENT), lambda b, qi: (0, 0, 0)),
                  pl.BlockSpec((hb, KV_LATENT, HEAD_DIM_DSA), lambda b, qi: (0, 0, 0)),
                  pl.BlockSpec((2, hb, LANES, LANES), lambda b, qi: (0, 0, 0, 0))],
        out_specs=pl.BlockSpec((t, width), lambda b, qi: (b * nq + qi, 0)),
        scratch_shapes=[pltpu.VMEM((seq, KV_LATENT), BF16),
                        pltpu.VMEM((t, seq), I32),
                        pltpu.VMEM((seq, t), I32),
                        pltpu.VMEM((hi * t, LANES), BF16),
                        pltpu.VMEM((hi, t, LANES), F32),
                        pltpu.VMEM((ns, rs, KV_LATENT), BF16),
                        pltpu.VMEM((ns, rs, KEY_CHUNK), F32),
                        pltpu.VMEM((ns, rs, LANES), F32),
                        pltpu.VMEM((ns, rs, LANES), F32),
                        pltpu.VMEM((ns, rs, KV_LATENT), F32)],
        compiler_params=_cparams(("parallel", "arbitrary")),
        name="dsa_attn",
    )(proj, proj, proj, proj, proj, kv_g, w_ukt, w_uv, bias_blocks)


def _pack_halves(x):
    c = x.shape[1] // 2
    lo = lax.bitcast_convert_type(x[:, :c].astype(BF16).astype(F32), I32)
    hi = lax.bitcast_convert_type(x[:, c:].astype(BF16).astype(F32), I32)
    return lax.shift_right_logical(lo, 16) | (hi & jnp.int32(-65536))


def _unpack_halves(w):
    lo = lax.bitcast_convert_type(lax.shift_left(w, 16), F32)
    hi = lax.bitcast_convert_type(w & jnp.int32(-65536), F32)
    return lo, hi


def _merge_kernel(x_ref, yd_ref, ys_ref, ga_ref, gb_ref, wd_ref, ws_ref, wo_ref, g_ref, wrh_ref, wrl_ref, br_ref,
                  h_ref, hn_ref, route_ref):
    bd = _dot_split(yd_ref[...], wd_ref[...])
    bs = _dot_split(ys_ref[...], ws_ref[...])
    merged = (jax.nn.sigmoid(ga_ref[...].astype(F32)) * bd + jax.nn.sigmoid(gb_ref[...].astype(F32)) * bs)
    h = x_ref[...] + _dot_split(merged.astype(BF16), wo_ref[...])
    h_ref[...] = h
    ms = jnp.mean(h * h, axis=-1, keepdims=True)
    hn = h * lax.rsqrt(ms + EPS) * g_ref[...]
    hn_ref[...] = _pack_halves(hn)

    hn_hi = hn.astype(BF16)
    hn_lo = (hn - hn_hi.astype(F32)).astype(BF16)
    logits = (_dot_split(hn_hi, wrh_ref[...]) + _dot_split(hn_lo, wrh_ref[...])
              + _dot_split(hn_hi, wrl_ref[...]))
    logits = logits + br_ref[...]
    lane = lax.broadcasted_iota(I32, logits.shape, 1)
    vals, ids = [], []
    for _ in range(TOP_K_EXPERTS):
        mx = jnp.max(logits, axis=-1, keepdims=True)
        ix = jnp.min(jnp.where(logits == mx, lane, LANES), axis=-1, keepdims=True)
        vals.append(mx)
        ids.append(ix)
        logits = jnp.where(lane == ix, -jnp.inf, logits)
    es = [jnp.exp(v - vals[0]) for v in vals]
    inv = 1.0 / (es[0] + es[1] + es[2] + es[3])
    route = jnp.zeros(logits.shape, F32)
    for k in range(TOP_K_EXPERTS):
        route = jnp.where(lane == k, es[k] * inv, route)
        route = jnp.where(lane == TOP_K_EXPERTS + k, ids[k].astype(F32), route)
    route_ref[...] = route


def _merge(x2, y_diff, y_dsa, proj, w_bd, w_bs, w_out, g_ffn, w_router_hi, w_router_lo, b_router, tm):
    n, d = x2.shape
    row = lambda i: (i, 0)
    const = lambda i: (0, 0)
    return pl.pallas_call(
        _merge_kernel,
        out_shape=(jax.ShapeDtypeStruct((n, d), F32),
                   jax.ShapeDtypeStruct((n, d // 2), I32),
                   jax.ShapeDtypeStruct((n, LANES), F32)),
        grid=(n // tm,),
        in_specs=[pl.BlockSpec((tm, d), row),
                  pl.BlockSpec((tm, d), row),
                  pl.BlockSpec((tm, d), row),
                  pl.BlockSpec((tm, d), lambda i: (i, COL_GA // 1024)),
                  pl.BlockSpec((tm, d), lambda i: (i, COL_GB // 1024)),
                  pl.BlockSpec((d, d), const),
                  pl.BlockSpec((d, d), const),
                  pl.BlockSpec((d, d), const),
                  pl.BlockSpec((1, d), const),
                  pl.BlockSpec((d, LANES), const),
                  pl.BlockSpec((d, LANES), const),
                  pl.BlockSpec((1, LANES), const)],
        out_specs=(pl.BlockSpec((tm, d), row),
                   pl.BlockSpec((tm, d // 2), row),
                   pl.BlockSpec((tm, LANES), row)),
        compiler_params=_cparams(("parallel",)),
        name="merge_router",
    )(x2, y_diff, y_dsa, proj, proj, w_bd, w_bs, w_out, g_ffn, w_router_hi, w_router_lo, b_router)


def _regroup_kernel(w_ref, p_ref, o_ref):
    pw = p_ref.shape[0]
    for j in range(w_ref.shape[2] // pw):
        w = w_ref[0, :, j * pw:(j + 1) * pw].astype(BF16)
        o_ref[0, :, j * pw:(j + 1) * pw] = jnp.dot(w, p_ref[...], preferred_element_type=F32).astype(BF16)


def _regroup_gate_up(w_gu, rows):
    e, d, f2 = w_gu.shape
    pw = 2 * LANES
    src = jnp.arange(pw, dtype=I32)
    dst = (src % 2) * LANES + src // 2
    perm = (dst[:, None] == jnp.arange(pw, dtype=I32)[None, :]).astype(BF16)
    return pl.pallas_call(
        _regroup_kernel,
        out_shape=jax.ShapeDtypeStruct((e, d, f2), BF16),
        grid=(e, d // rows),
        in_specs=[pl.BlockSpec((1, rows, f2), lambda i, j: (i, j, 0)),
                  pl.BlockSpec((pw, pw), lambda i, j: (0, 0))],
        out_specs=pl.BlockSpec((1, rows, f2), lambda i, j: (i, j, 0)),
        compiler_params=_cparams(("parallel", "parallel")),
        name="regroup_gate_up",
    )(w_gu, perm)


def _ffn_kernel(be_ref, nu_ref, x_ref, wgu_ref, wd_ref, bgu_ref, bd_ref, o_ref):
    @pl.when(pl.program_id(0) < nu_ref[0])
    def _():
        x_lo, x_hi = _unpack_halves(x_ref[...])
        half = x_lo.shape[1]
        gu = (jnp.dot(x_lo.astype(BF16), wgu_ref[0, 0:half, :], preferred_element_type=F32)
              + jnp.dot(x_hi.astype(BF16), wgu_ref[0, half:2 * half, :], preferred_element_type=F32)
              + bgu_ref[0])
        acts = []
        for j in range(gu.shape[1] // (2 * LANES)):
            gate = jnp.minimum(gu[:, 2 * j * LANES:(2 * j + 1) * LANES], SWIGLU_LIMIT)
            up = jnp.clip(gu[:, (2 * j + 1) * LANES:(2 * j + 2) * LANES], -SWIGLU_LIMIT, SWIGLU_LIMIT)
            glu = gate * jax.nn.sigmoid(gate * SWIGLU_ALPHA)
            acts.append(((up + 1.0) * glu).astype(BF16))
        a = jnp.concatenate(acts, axis=1)
        y = jnp.dot(a, wd_ref[0], preferred_element_type=F32) + bd_ref[0]
        o_ref[...] = _pack_halves(y)

    @pl.when(pl.program_id(0) >= nu_ref[0])
    def _():
        o_ref[...] = jnp.zeros(o_ref.shape, o_ref.dtype)


def _expert_ffn(blk_exp, n_used, xs, wgu, wd, bgu, bd):
    p, dw = xs.shape
    f, d = wd.shape[1], wd.shape[2]
    nblk = p // MOE_ROWS
    wmap = lambda i, be, nu: (be[i], 0, 0)
    grid_spec = pltpu.PrefetchScalarGridSpec(
        num_scalar_prefetch=2,
        grid=(nblk,),
        in_specs=[pl.BlockSpec((MOE_ROWS, dw), lambda i, be, nu: (i, 0)),
                  pl.BlockSpec((1, d, 2 * f), wmap),
                  pl.BlockSpec((1, f, d), wmap),
                  pl.BlockSpec((1, 1, 2 * f), wmap),
                  pl.BlockSpec((1, 1, d), wmap)],
        out_specs=pl.BlockSpec((MOE_ROWS, dw), lambda i, be, nu: (i, 0)),
    )
    return pl.pallas_call(
        _ffn_kernel,
        out_shape=jax.ShapeDtypeStruct((p, dw), I32),
        grid_spec=grid_spec,
        compiler_params=_cparams(("arbitrary",)),
        name="expert_ffn",
    )(blk_exp, n_used, xs, wgu, wd, bgu, bd)


def _route_kernel(route_ref, dest_ref, cnt_ref, u_ref, carry_ref, pstart_ref, *, block_rows):
    ph, i = pl.program_id(0), pl.program_id(1)
    tm = route_ref.shape[0]

    @pl.when((ph == 0) & (i == 0))
    def _():
        r = lax.broadcasted_iota(I32, (tm, tm), 0)
        c = lax.broadcasted_iota(I32, (tm, tm), 1)
        u_ref[...] = jnp.where(r < c, 1.0, 0.0).astype(BF16)
        carry_ref[...] = jnp.zeros(carry_ref.shape, F32)

    @pl.when((ph == 1) & (i == 0))
    def _():
        counts = carry_ref[...]
        cnt_ref[...] = counts
        padded = jnp.ceil(counts * (1.0 / block_rows)) * block_rows
        r = lax.broadcasted_iota(I32, (LANES, LANES), 0)
        c = lax.broadcasted_iota(I32, (LANES, LANES), 1)
        lower = jnp.where(c < r, 1.0, 0.0).astype(F32)
        pstart_ref[...] = jnp.dot(lower, padded, preferred_element_type=F32, precision=lax.Precision.HIGHEST)
        carry_ref[...] = jnp.zeros(carry_ref.shape, F32)

    rt = route_ref[...].T
    sub = lax.broadcasted_iota(I32, (LANES, tm), 0)
    hits = [sub == rt[TOP_K_EXPERTS + k:TOP_K_EXPERTS + k + 1, :].astype(I32) for k in range(TOP_K_EXPERTS)]
    m = jnp.zeros((LANES, tm), F32)
    for hit in hits:
        m = m + jnp.where(hit, 1.0, 0.0)
    tile_counts = jnp.broadcast_to(jnp.sum(m, axis=1, keepdims=True), (LANES, LANES))

    @pl.when(ph == 0)
    def _():
        dest_ref[...] = jnp.zeros(dest_ref.shape, I32)

    @pl.when(ph == 1)
    def _():
        prefix = jnp.dot(m.astype(BF16), u_ref[...], preferred_element_type=F32)
        rank = prefix + (pstart_ref[:, 0:1] + carry_ref[:, 0:1])
        rows = [jnp.sum(jnp.where(hit, rank, 0.0), axis=0, keepdims=True) for hit in hits]
        rows.append(jnp.zeros((dest_ref.shape[0] - TOP_K_EXPERTS, tm), F32))
        dest_ref[...] = jnp.concatenate(rows, axis=0).astype(I32)

    carry_ref[...] = carry_ref[...] + tile_counts


def _route_rows(route, tm):
    n = route.shape[0]
    nt = n // tm
    kernel = functools.partial(_route_kernel, block_rows=MOE_ROWS)
    return pl.pallas_call(
        kernel,
        out_shape=(jax.ShapeDtypeStruct((8, n + tm), I32), jax.ShapeDtypeStruct((LANES, LANES), F32)),
        grid=(2, nt),
        in_specs=[pl.BlockSpec((tm, LANES), lambda ph, i: (i, 0))],
        out_specs=(pl.BlockSpec((8, tm), lambda ph, i: (0, ph * i + (1 - ph) * nt)),
                   pl.BlockSpec((LANES, LANES), lambda ph, i: (0, 0))),
        scratch_shapes=[pltpu.VMEM((tm, tm), BF16),
                        pltpu.VMEM((LANES, LANES), F32),
                        pltpu.VMEM((LANES, LANES), F32)],
        compiler_params=_cparams(("arbitrary", "arbitrary")),
        name="route_rows",
    )(route)


SC_WINDOW = 128
SC_WORKERS = 32


def _sc_mesh():
    return plsc.VectorSubcoreMesh(core_axis_name="c", subcore_axis_name="s")


def _sc_scatter_rows(src, dest, p):
    n, d = src.shape
    per = n // (SC_WINDOW * SC_WORKERS)

    @pl.kernel(out_type=jax.ShapeDtypeStruct((p, d), src.dtype), mesh=_sc_mesh(),
               scratch_types=[pltpu.VMEM((dest.shape[0], SC_WINDOW), I32), pltpu.VMEM((SC_WINDOW, d), src.dtype)])
    def scatter(src_hbm, idx_hbm, out_hbm, idx_vmem, buf):
        wid = lax.axis_index("c") * (SC_WORKERS // 2) + lax.axis_index("s")

        @pl.loop(0, per)
        def _(j):
            off = (wid * per + j) * SC_WINDOW
            pltpu.sync_copy(idx_hbm.at[:, pl.ds(off, SC_WINDOW)], idx_vmem)
            pltpu.sync_copy(src_hbm.at[pl.ds(off, SC_WINDOW), :], buf)
            for k in range(TOP_K_EXPERTS):
                pltpu.sync_copy(buf, out_hbm.at[idx_vmem.at[k]])

    return scatter(src, dest)


def _sc_gather_rows(src, dest):
    n = dest.shape[1]
    d = src.shape[1]
    per = n // (SC_WINDOW * SC_WORKERS)

    @pl.kernel(out_type=jax.ShapeDtypeStruct((TOP_K_EXPERTS * n, d), src.dtype), mesh=_sc_mesh(),
               scratch_types=[pltpu.VMEM((dest.shape[0], SC_WINDOW), I32), pltpu.VMEM((SC_WINDOW, d), src.dtype)])
    def gather(src_hbm, idx_hbm, out_hbm, idx_vmem, buf):
        wid = lax.axis_index("c") * (SC_WORKERS // 2) + lax.axis_index("s")

        @pl.loop(0, per)
        def _(j):
            off = (wid * per + j) * SC_WINDOW
            pltpu.sync_copy(idx_hbm.at[:, pl.ds(off, SC_WINDOW)], idx_vmem)
            for k in range(TOP_K_EXPERTS):
                pltpu.sync_copy(src_hbm.at[idx_vmem.at[k]], buf)
                pltpu.sync_copy(buf, out_hbm.at[pl.ds(k * n + off, SC_WINDOW), :])

    return gather(src, dest)


def _combine_kernel(h_ref, y_ref, route_ref, g_ref, o_ref):
    half = h_ref.shape[1] // 2
    h_lo, h_hi = h_ref[:, 0:half], h_ref[:, half:2 * half]
    route = route_ref[...]
    for k in range(TOP_K_EXPERTS):
        y_lo, y_hi = _unpack_halves(y_ref[k])
        gate = route[:, k:k + 1]
        h_lo = h_lo + gate * y_lo
        h_hi = h_hi + gate * y_hi
    ms = (jnp.sum(h_lo * h_lo, axis=-1, keepdims=True)
          + jnp.sum(h_hi * h_hi, axis=-1, keepdims=True)) * (1.0 / (2 * half))
    inv = lax.rsqrt(ms + EPS)
    o_ref[:, 0:half] = h_lo * inv * g_ref[:, 0:half]
    o_ref[:, half:2 * half] = h_hi * inv * g_ref[:, half:2 * half]


def _combine(h1, yg, route, g_final, tm):
    n, d = h1.shape
    return pl.pallas_call(
        _combine_kernel,
        out_shape=jax.ShapeDtypeStruct((n, d), F32),
        grid=(n // tm,),
        in_specs=[pl.BlockSpec((tm, d), lambda i: (i, 0)),
                  pl.BlockSpec((TOP_K_EXPERTS, tm, d // 2), lambda i: (0, i, 0)),
                  pl.BlockSpec((tm, LANES), lambda i: (i, 0)),
                  pl.BlockSpec((1, d), lambda i: (0, 0))],
        out_specs=pl.BlockSpec((tm, d), lambda i: (i, 0)),
        compiler_params=_cparams(("parallel",)),
        name="combine_norm",
    )(h1, yg, route, g_final)


def _t5_bucket(dist):
    n = jnp.maximum(dist, 0)
    max_exact = N_BUCKETS // 2
    nf = jnp.maximum(n, 1).astype(F32)
    large = max_exact + (jnp.log(nf / max_exact) / math.log(MAX_DISTANCE / max_exact)
                         * (N_BUCKETS - max_exact)).astype(I32)
    large = jnp.minimum(large, N_BUCKETS - 1)
    return jnp.where(n < max_exact, n, large)


def _bias_blocks(bias_tab):
    t = LANES
    assert MAX_DISTANCE <= LANES
    r = jnp.arange(t, dtype=I32)[:, None]
    c = jnp.arange(t, dtype=I32)[None, :]
    rel = (bias_tab - bias_tab[N_BUCKETS - 1][None, :]).astype(F32)
    tiles = []
    buckets = jnp.arange(N_BUCKETS, dtype=I32)[:, None, None]
    for delta in (0, t):
        dist = r - c + delta
        hit = _t5_bucket(dist)[None] == buckets
        b = jnp.sum(jnp.where(hit[:, None], rel[:, :, None, None], 0.0), axis=0)
        tiles.append(jnp.where((dist >= 0)[None], b * LOG2E, MASK_VALUE))
    return jnp.stack(tiles)


def _regroup_w_in(w_in):
    sizes = (1024, 1024, 1024, 1024, KV_LATENT, 1024, HEAD_DIM_IDX, N_HEADS_IDX, D_MODEL, D_MODEL)
    parts, off = [], 0
    for sz in sizes:
        parts.append(w_in[:, off:off + sz])
        off += sz
    dq, dk, dv, sq, ckv, iq, ik, iw, ga, gb = parts
    dq = dq * (HEAD_DIM_DIFF ** -0.5 * LOG2E)
    iw = iw * ((N_HEADS_IDX ** -0.5) * (HEAD_DIM_IDX ** -0.5))
    pad = jnp.zeros((w_in.shape[0], PROJ_WIDTH - COL_IW - N_HEADS_IDX), w_in.dtype)
    w = jnp.concatenate([dq, dk, dv, sq, iq, ga, gb, ckv, ik, ik, iw, pad], axis=1)
    return w.astype(BF16)


def _block_tables(counts, n_assign):
    e, bm = N_EXPERTS, MOE_ROWS
    padded = (counts + bm - 1) // bm * bm
    pends = jnp.cumsum(padded)
    nblk = -(-(n_assign + e * (bm - 1)) // bm)
    first_row = jnp.arange(nblk, dtype=I32) * bm
    blk_exp = jnp.minimum(jnp.sum((pends[None, :] <= first_row[:, None]).astype(I32), axis=1), e - 1)
    n_used = (pends[-1] // bm).astype(I32).reshape(1)
    return blk_exp, n_used, nblk


def kernel(x, norm_attn_g, w_in, rel_bias, lam_q1, lam_k1, lam_q2, lam_k2, diff_subln_g, kv_norm_g, w_uk, w_uv,
           w_branch_diff, w_branch_dsa, w_out, norm_ffn_g, w_router, b_router, w_gate_up, b_gate_up, w_down,
           b_down, norm_final_g):
    batch, seq, d = x.shape
    n = batch * seq
    assert norm_attn_g.shape[0] == 1, "single-layer kernel"
    assert seq % DIFF_BLOCK == 0 and seq % KEY_CHUNK == 0 and d == D_MODEL
    row_tile = math.gcd(n, 1024)

    x2 = x.reshape(n, d)
    proj = _inproj(x2, norm_attn_g[0].reshape(1, d), _regroup_w_in(w_in[0]), row_tile, 1280)

    lam_init = 0.8 - 0.6 * math.exp(-0.3 * 0)
    lam = (jnp.exp(jnp.sum(lam_q1[0].astype(F32) * lam_k1[0].astype(F32)))
           - jnp.exp(jnp.sum(lam_q2[0].astype(F32) * lam_k2[0].astype(F32))) + lam_init)
    y_diff = _diff_attention(proj, lam.reshape(1, 1).astype(F32), _bias_blocks(rel_bias[:, :N_HEADS_DIFF]),
                             diff_subln_g[0].reshape(1, -1).astype(F32), batch, seq, 1.0 - lam_init)

    y_dsa = _dsa_attention(proj, kv_norm_g[0].reshape(1, -1).astype(F32),
                           w_uk[0].transpose(0, 2, 1).astype(BF16), w_uv[0].astype(BF16),
                           _bias_blocks(rel_bias[:, N_HEADS_DIFF:]), batch, seq, min(TOPK_MAX, seq // 4))

    w_r = jnp.zeros((d, LANES), F32).at[:, :N_EXPERTS].set(w_router[0].astype(F32))
    b_r = jnp.full((1, LANES), MASK_VALUE, F32).at[0, :N_EXPERTS].set(b_router[0].astype(F32))
    h1, hn, route = _merge(x2, y_diff, y_dsa, proj, w_branch_diff[0].astype(BF16), w_branch_dsa[0].astype(BF16),
                           w_out[0].astype(BF16), norm_ffn_g[0].reshape(1, d).astype(F32),
                           w_r.astype(BF16), (w_r - w_r.astype(BF16).astype(F32)).astype(BF16), b_r,
                           math.gcd(n, 512))

    dest, counts = _route_rows(route, math.gcd(n, 1024))
    dest = dest[:, :n]
    blk_exp, n_used, nblk = _block_tables(counts[:N_EXPERTS, 0].astype(I32), n * TOP_K_EXPERTS)
    xs = _sc_scatter_rows(hn, dest, nblk * MOE_ROWS)
    e, f = N_EXPERTS, D_EXPERT
    b_gu = b_gate_up[0].astype(F32).reshape(e, f // LANES, LANES, 2).transpose(0, 1, 3, 2).reshape(e, 1, 2 * f)
    ys = _expert_ffn(blk_exp, n_used, xs, _regroup_gate_up(w_gate_up[0], 512), w_down[0].astype(BF16),
                     b_gu, b_down[0][:, None, :].astype(F32))
    yg = _sc_gather_rows(ys, dest).reshape(TOP_K_EXPERTS, n, d // 2)
    out = _combine(h1, yg, route, norm_final_g.reshape(1, d).astype(F32), math.gcd(n, 512))
    return out.reshape(batch, seq, d)
```

```python
import functools
import math

import jax
import jax.numpy as jnp
from jax import lax
from jax.experimental import pallas as pl
from jax.experimental.pallas import tpu as pltpu
from jax.experimental.pallas import tpu_sc as plsc

F32 = jnp.float32
BF16 = jnp.bfloat16
I32 = jnp.int32

D_MODEL = 1024
N_HEADS_DIFF = 8
HEAD_DIM_DIFF = 64
N_HEADS_DSA = 8
HEAD_DIM_DSA = 128
KV_LATENT = 256
N_HEADS_IDX = 16
HEAD_DIM_IDX = 64
TOPK_MAX = 256
N_BUCKETS = 32
MAX_DISTANCE = 128
N_EXPERTS = 32
TOP_K_EXPERTS = 4
D_EXPERT = 1024
SWIGLU_LIMIT = 7.0
SWIGLU_ALPHA = 1.702
EPS = 1e-6

LANES = 128
DIFF_BLOCK = 512
DIFF_HEADS_PER_STEP = 4
DSA_BLOCK = 256
KEY_CHUNK = 512
DSA_STREAMS = 2
MOE_ROWS = 512
PROJ_WIDTH = 7680
VMEM_LIMIT = 56 * 1024 * 1024

COL_DQ, COL_DK, COL_DV, COL_SQ, COL_IQ, COL_GA, COL_GB = (i * 1024 for i in range(7))
COL_CKV = 7168
COL_IK = 7424
COL_IW = 7552

LOG2E = math.log2(math.e)
MASK_VALUE = -1e30
M_INIT = -1e29
INT_MIN = -2 ** 31


def _cparams(sem):
    return pltpu.CompilerParams(dimension_semantics=sem, vmem_limit_bytes=VMEM_LIMIT)


def _inproj_kernel(x_ref, g_ref, w_ref, o_ref, xn_ref):
    @pl.when(pl.program_id(1) == 0)
    def _():
        x = x_ref[...]
        ms = jnp.mean(x * x, axis=-1, keepdims=True)
        xn_ref[...] = (x * lax.rsqrt(ms + EPS) * g_ref[...]).astype(BF16)

    o_ref[...] = jnp.dot(xn_ref[...], w_ref[...], preferred_element_type=F32).astype(o_ref.dtype)


def _inproj(x2, g, w, tm, tn):
    n, d = x2.shape
    width = w.shape[1]
    return pl.pallas_call(
        _inproj_kernel,
        out_shape=jax.ShapeDtypeStruct((n, width), BF16),
        grid=(n // tm, width // tn),
        in_specs=[pl.BlockSpec((tm, d), lambda i, j: (i, 0)),
                  pl.BlockSpec((1, d), lambda i, j: (0, 0)),
                  pl.BlockSpec((d, tn), lambda i, j: (0, j))],
        out_specs=pl.BlockSpec((tm, tn), lambda i, j: (i, j)),
        scratch_shapes=[pltpu.VMEM((tm, d), BF16)],
        compiler_params=_cparams(("parallel", "arbitrary")),
        name="inproj",
    )(x2, g, w)


def _lane_chunk_max(s):
    smax = s[:, 0:LANES]
    for j in range(1, s.shape[1] // LANES):
        smax = jnp.maximum(smax, s[:, j * LANES:(j + 1) * LANES])
    return smax


def _softmax_step(s_ref, tk, v, m_ref, l_ref, acc_ref, smax=None, split=True):
    nl = tk // LANES
    if smax is None:
        smax = _lane_chunk_max(s_ref[:, 0:tk])
    m_prev = m_ref[...]
    m_new = jnp.maximum(m_prev, jnp.max(smax, axis=-1, keepdims=True))
    alpha = jnp.exp2(m_prev - m_new)
    psum = None
    ps = []
    for j in range(nl):
        pj = jnp.exp2(s_ref[:, j * LANES:(j + 1) * LANES] - m_new)
        psum = pj if psum is None else psum + pj
        ps.append(pj.astype(BF16))
    l_ref[...] = alpha * l_ref[...] + psum
    pv = _dot_split(jnp.concatenate(ps, axis=1), v, split)
    e = acc_ref.shape[1]
    a = alpha if e == LANES else jnp.concatenate([alpha] * (e // LANES), axis=1)
    acc_ref[...] = a * acc_ref[...] + pv
    m_ref[...] = m_new


def _softmax_init(m_ref, l_ref, acc_ref):
    m_ref[...] = jnp.full(m_ref.shape, M_INIT, F32)
    l_ref[...] = jnp.zeros(l_ref.shape, F32)
    acc_ref[...] = jnp.zeros(acc_ref.shape, F32)


def _softmax_result(l_ref, acc_ref):
    return acc_ref[...] * (1.0 / jnp.sum(l_ref[...], axis=-1, keepdims=True))


def _near_bias(s_ref, d0, d1, delta, groups, t, tk):
    for g in range(groups):
        for rb in range(t // LANES):
            for cb in range(tk // LANES):
                bd = delta + rb - cb
                rows = slice(g * t + rb * LANES, g * t + (rb + 1) * LANES)
                cols = slice(cb * LANES, (cb + 1) * LANES)
                if bd == 0:
                    s_ref[rows, cols] = s_ref[rows, cols] + d0(g)
                elif bd == 1:
                    s_ref[rows, cols] = s_ref[rows, cols] + d1(g)
                elif bd < 0:
                    s_ref[rows, cols] = jnp.full((LANES, LANES), MASK_VALUE, F32)


def _pipelined_far(n_far, issue, finish):
    odd = n_far % 2

    @pl.when(odd == 1)
    def _():
        issue(0, 1)
        issue(1, 0)
        finish(0, 1)

    @pl.when(odd == 0)
    def _():
        issue(0, 0)

    def body(j, carry):
        a = odd + 2 * j
        issue(a + 1, 1)
        finish(a, 0)
        issue(a + 2, 0)
        finish(a + 1, 1)
        return carry

    lax.fori_loop(0, (n_far - odd) // 2, body, 0)


def _dot_nt(a, b, split=True):
    dn = (((1,), (1,)), ((), ()))
    if not split:
        return lax.dot_general(a, b, dn, preferred_element_type=F32)
    h = a.shape[0] // 2
    return jnp.concatenate([lax.dot_general(a[:h], b, dn, preferred_element_type=F32),
                            lax.dot_general(a[h:], b, dn, preferred_element_type=F32)], axis=0)


def _dot_split(a, b, split=True):
    if not split:
        return jnp.dot(a, b, preferred_element_type=F32)
    h = a.shape[0] // 2
    return jnp.concatenate([jnp.dot(a[:h], b, preferred_element_type=F32),
                            jnp.dot(a[h:], b, preferred_element_type=F32)], axis=0)


def _diff_kernel(lam_ref, q_ref, k_ref, v_ref, bias_ref, g_ref, o_ref, q2_ref, s_ref, m_ref, l_ref, acc_ref,
                 *, out_scale, heads):
    t = q_ref.shape[0]
    e = 2 * HEAD_DIM_DIFF
    qi = pl.program_id(2)
    cols = [slice(hh * e, (hh + 1) * e) for hh in range(heads)]

    lane = lax.broadcasted_iota(I32, (t, e), 1)
    for hh in range(heads):
        q = q_ref[:, cols[hh]]
        zero = jnp.zeros_like(q)
        q2_ref[hh, 0:t, :] = jnp.where(lane < HEAD_DIM_DIFF, q, zero)
        q2_ref[hh, t:2 * t, :] = jnp.where(lane >= HEAD_DIM_DIFF, q, zero)
        _softmax_init(m_ref.at[hh], l_ref.at[hh], acc_ref.at[hh])

    def chunk(kc, delta):
        off = pl.multiple_of(kc * t, t)
        for hh in range(heads):
            s = _dot_nt(q2_ref[hh], k_ref[pl.ds(off, t), cols[hh]])
            s_ref[hh] = s
            smax = None
            if delta is None:
                smax = _lane_chunk_max(s)
            else:
                _near_bias(s_ref.at[hh], lambda g: bias_ref[0, hh], lambda g: bias_ref[1, hh], delta, 2, t, t)
            _softmax_step(s_ref.at[hh], t, v_ref[pl.ds(off, t), cols[hh]], m_ref.at[hh], l_ref.at[hh],
                          acc_ref.at[hh], smax)

    def far_body(kc, carry):
        chunk(kc, None)
        return carry

    lax.fori_loop(0, jnp.maximum(qi - 1, 0), far_body, 0)

    @pl.when(qi >= 1)
    def _():
        chunk(qi - 1, t // LANES)

    chunk(qi, 0)

    for hh in range(heads):
        o = _softmax_result(l_ref.at[hh], acc_ref.at[hh])
        o = o[0:t, :] - lam_ref[0, 0] * o[t:2 * t, :]
        ms = jnp.mean(o * o, axis=-1, keepdims=True)
        o_ref[:, cols[hh]] = (o * lax.rsqrt(ms + EPS) * g_ref[...] * out_scale).astype(o_ref.dtype)


def _diff_attention(proj, lam, bias_blocks, subln_g, batch, seq, out_scale):
    t = DIFF_BLOCK
    nq = seq // t
    h = N_HEADS_DIFF
    e = 2 * HEAD_DIM_DIFF
    hp = DIFF_HEADS_PER_STEP
    w = hp * e
    kernel = functools.partial(_diff_kernel, out_scale=out_scale, heads=hp)
    return pl.pallas_call(
        kernel,
        out_shape=jax.ShapeDtypeStruct((batch * seq, h * e), BF16),
        grid=(batch, h // hp, nq),
        in_specs=[pl.BlockSpec(memory_space=pltpu.SMEM),
                  pl.BlockSpec((t, w), lambda b, hh, qi: (b * nq + qi, COL_DQ // w + hh)),
                  pl.BlockSpec((seq, w), lambda b, hh, qi: (b, COL_DK // w + hh)),
                  pl.BlockSpec((seq, w), lambda b, hh, qi: (b, COL_DV // w + hh)),
                  pl.BlockSpec((2, hp, LANES, LANES), lambda b, hh, qi: (0, hh, 0, 0)),
                  pl.BlockSpec((1, e), lambda b, hh, qi: (0, 0))],
        out_specs=pl.BlockSpec((t, w), lambda b, hh, qi: (b * nq + qi, hh)),
        scratch_shapes=[pltpu.VMEM((hp, 2 * t, e), BF16),
                        pltpu.VMEM((hp, 2 * t, t), F32),
                        pltpu.VMEM((hp, 2 * t, LANES), F32),
                        pltpu.VMEM((hp, 2 * t, LANES), F32),
                        pltpu.VMEM((hp, 2 * t, e), F32)],
        compiler_params=_cparams(("parallel", "parallel", "arbitrary")),
        name="diff_attn",
    )(lam, proj, proj, proj, bias_blocks, subln_g)


def _sortable_key(x):
    bits = lax.bitcast_convert_type(x, I32)
    return bits ^ ((bits >> 31) & jnp.int32(0x7FFFFFFF))


def _dsa_kernel(iq_ref, sq_ref, iw_ref, ik_ref, ckv_ref, kvg_ref, wuk_ref, wuv_ref, bias_ref, o_ref,
                c_ref, key_ref, keyt_ref, qi_ref, wb_ref, ql_ref, s_ref, m_ref, l_ref, acc_ref, *, topk, scale):
    t = iq_ref.shape[0]
    tk = KEY_CHUNK
    qi = pl.program_id(1)
    n_chunks = qi + 1
    hi, hb = N_HEADS_IDX, N_HEADS_DSA

    @pl.when(qi == 0)
    def _():
        ckv = ckv_ref[...].astype(F32)
        ms = jnp.mean(ckv * ckv, axis=-1, keepdims=True)
        c_ref[...] = (ckv * lax.rsqrt(ms + EPS) * kvg_ref[...]).astype(BF16)

    rg = qi_ref.shape[2]
    lane = lax.broadcasted_iota(I32, (t, LANES), 1)
    for h in range(hi):
        blk = iq_ref[:, (h // 2) * LANES:(h // 2 + 1) * LANES]
        keep = (lane < HEAD_DIM_IDX) if h % 2 == 0 else (lane >= HEAD_DIM_IDX)
        qi_ref[:, h, :, :] = jnp.where(keep, blk, jnp.zeros_like(blk)).reshape(t // rg, rg, LANES)
        wb_ref[:, h, :, :] = jnp.broadcast_to(iw_ref[:, h:h + 1].astype(F32), (t, LANES)).reshape(t // rg, rg, LANES)

    def score_chunk(kc, diag):
        off = pl.multiple_of(kc * t, t)
        d = _dot_nt(qi_ref[...].reshape(hi * t, LANES), ik_ref[pl.ds(off, t), :]).reshape(t // rg, hi, rg, t)
        w = jnp.concatenate([wb_ref[...]] * (t // LANES), axis=-1)
        sc = jnp.sum(jnp.maximum(d, 0.0) * w, axis=1).reshape(t, t)
        key = _sortable_key(sc + 0.0)
        if diag:
            row = lax.broadcasted_iota(I32, (t, t), 0)
            col = lax.broadcasted_iota(I32, (t, t), 1)
            key = jnp.where(col <= row, key, jnp.int32(INT_MIN))
        key_ref[:, pl.ds(off, t)] = key
        keyt_ref[pl.ds(off, t), :] = key.T

    def score_body(kc, carry):
        score_chunk(kc, False)
        return carry

    lax.fori_loop(0, qi, score_body, 0)
    score_chunk(qi, True)

    @pl.when(qi % 2 == 0)
    def _():
        off = pl.multiple_of((qi + 1) * t, t)
        key_ref[:, pl.ds(off, t)] = jnp.full((t, t), INT_MIN, I32)
        keyt_ref[pl.ds(off, t), :] = jnp.full((t, t), INT_MIN, I32)

    n_steps = (qi + 2) // 2

    def bit_body(i, cur):
        bit = lax.shift_left(jnp.int32(1), 31 - i)
        cand = cur | bit
        cand_s = (cand ^ jnp.int32(INT_MIN))[None]

        def body(kc, cnt):
            off = pl.multiple_of(kc * tk, tk)
            k = keyt_ref[pl.ds(off, tk), :].reshape(tk // 8, 8, t)
            return cnt + jnp.sum(jnp.where(k >= cand_s, 1, 0), axis=0)

        cnt = lax.fori_loop(0, n_steps, body, jnp.zeros((8, t), I32))
        total = jnp.sum(cnt, axis=0, keepdims=True)
        return jnp.where(total >= topk, cand, cur)

    cur = lax.fori_loop(0, 32, bit_body, jnp.zeros((8, t), I32))
    thr = jnp.maximum(cur ^ jnp.int32(INT_MIN), jnp.int32(INT_MIN + 1))
    thr_b = jnp.broadcast_to(thr[0:1, :], (LANES, t)).T
    thr_w = jnp.concatenate([thr_b] * (tk // LANES), axis=1)

    def mask_body(kc, carry):
        off = pl.multiple_of(kc * tk, tk)
        am = jnp.where(key_ref[:, pl.ds(off, tk)] >= thr_w, 0.0, MASK_VALUE).astype(F32)
        key_ref[:, pl.ds(off, tk)] = lax.bitcast_convert_type(am, I32)
        return carry

    lax.fori_loop(0, n_steps, mask_body, 0)

    streams = ql_ref.shape[0]
    split = streams == 1
    hs = hb // streams
    for h in range(hb):
        qh = sq_ref[:, h * HEAD_DIM_DSA:(h + 1) * HEAD_DIM_DSA]
        ql = jnp.dot(qh, wuk_ref[h], preferred_element_type=F32) * scale
        ql_ref[h // hs, (h % hs) * t:(h % hs + 1) * t, :] = ql.astype(BF16)
    for g in range(streams):
        _softmax_init(m_ref.at[g], l_ref.at[g], acc_ref.at[g])

    def chunk(kc, width, delta):
        off = pl.multiple_of(kc * tk, tk)
        c = c_ref[pl.ds(off, width), :]
        am = lax.bitcast_convert_type(key_ref[:, pl.ds(off, width)], F32)
        for g in range(streams):
            s = (_dot_nt(ql_ref[g], c, split).reshape(hs, t, width) + am[None]).reshape(hs * t, width)
            s_ref[g, :, 0:width] = s
            smax = None
            if delta is None:
                smax = _lane_chunk_max(s)
            else:
                _near_bias(s_ref.at[g], lambda j: bias_ref[0, g * hs + j], lambda j: bias_ref[1, g * hs + j],
                           delta, hs, t, width)
            _softmax_step(s_ref.at[g], width, c, m_ref.at[g], l_ref.at[g], acc_ref.at[g], smax, split)

    def far_body(kc, carry):
        chunk(kc, tk, None)
        return carry

    lax.fori_loop(0, jnp.maximum((qi - 1) // 2, 0), far_body, 0)
    half = qi // 2

    @pl.when(qi % 2 == 1)
    def _():
        chunk(half, tk, t // LANES)

    @pl.when(qi % 2 == 0)
    def _():
        @pl.when(half >= 1)
        def _():
            chunk(half - 1, tk, tk // LANES)
        chunk(half, t, 0)

    for h in range(hb):
        g, j = h // hs, h % hs
        inv_l = 1.0 / jnp.sum(l_ref[g, j * t:(j + 1) * t, :], axis=-1, keepdims=True)
        ol = (acc_ref[g, j * t:(j + 1) * t, :] * inv_l).astype(BF16)
        o = jnp.dot(ol, wuv_ref[h], preferred_element_type=F32)
        o_ref[:, h * HEAD_DIM_DSA:(h + 1) * HEAD_DIM_DSA] = o.astype(o_ref.dtype)


def _dsa_attention(proj, kv_g, w_ukt, w_uv, bias_blocks, batch, seq, topk):
    t = DSA_BLOCK
    nq = seq // t
    hb, hi = N_HEADS_DSA, N_HEADS_IDX
    width = hb * HEAD_DIM_DSA
    ns = DSA_STREAMS
    rs = hb // ns * t
    kernel = functools.partial(_dsa_kernel, topk=topk, scale=HEAD_DIM_DSA ** -0.5 * LOG2E)
    return pl.pallas_call(
        kernel,
        out_shape=jax.ShapeDtypeStruct((batch * seq, width), BF16),
        grid=(batch, nq),
        in_specs=[pl.BlockSpec((t, 1024), lambda b, qi: (b * nq + qi, COL_IQ // 1024)),
                  pl.BlockSpec((t, 1024), lambda b, qi: (b * nq + qi, COL_SQ // 1024)),
                  pl.BlockSpec((t, LANES), lambda b, qi: (b * nq + qi, COL_IW // LANES)),
                  pl.BlockSpec((seq, LANES), lambda b, qi: (b, COL_IK // LANES)),
                  pl.BlockSpec((seq, KV_LATENT), lambda b, qi: (b, COL_CKV // KV_LATENT)),
                  pl.BlockSpec((1, KV_LATENT), lambda b, qi: (0, 0)),
                  pl.BlockSpec((hb, HEAD_DIM_DSA, KV_LATENT), lambda b, qi: (0, 0, 0)),
                  pl.BlockSpec((hb, KV_LATENT, HEAD_DIM_DSA), lambda b, qi: (0, 0, 0)),
                  pl.BlockSpec((2, hb, LANES, LANES), lambda b, qi: (0, 0, 0, 0))],
        out_specs=pl.BlockSpec((t, width), lambda b, qi: (b * nq + qi, 0)),
        scratch_shapes=[pltpu.VMEM((seq, KV_LATENT), BF16),
                        pltpu.VMEM((t, seq), I32),
                        pltpu.VMEM((seq, t), I32),
                        pltpu.VMEM((t // 16, hi, 16, LANES), BF16),
                        pltpu.VMEM((t // 16, hi, 16, LANES), F32),
                        pltpu.VMEM((ns, rs, KV_LATENT), BF16),
                        pltpu.VMEM((ns, rs, KEY_CHUNK), F32),
                        pltpu.VMEM((ns, rs, LANES), F32),
                        pltpu.VMEM((ns, rs, LANES), F32),
                        pltpu.VMEM((ns, rs, KV_LATENT), F32)],
        compiler_params=_cparams(("parallel", "arbitrary")),
        name="dsa_attn",
    )(proj, proj, proj, proj, proj, kv_g, w_ukt, w_uv, bias_blocks)


def _pack_halves(x):
    c = x.shape[1] // 2
    lo = lax.bitcast_convert_type(x[:, :c].astype(BF16).astype(F32), I32)
    hi = lax.bitcast_convert_type(x[:, c:].astype(BF16).astype(F32), I32)
    return lax.shift_right_logical(lo, 16) | (hi & jnp.int32(-65536))


def _unpack_halves(w):
    lo = lax.bitcast_convert_type(lax.shift_left(w, 16), F32)
    hi = lax.bitcast_convert_type(w & jnp.int32(-65536), F32)
    return lo, hi


def _merge_kernel(x_ref, yd_ref, ys_ref, ga_ref, gb_ref, wd_ref, ws_ref, wo_ref, g_ref, wrh_ref, wrl_ref, br_ref,
                  h_ref, hn_ref, route_ref):
    bd = _dot_split(yd_ref[...], wd_ref[...])
    bs = _dot_split(ys_ref[...], ws_ref[...])
    merged = (jax.nn.sigmoid(ga_ref[...].astype(F32)) * bd + jax.nn.sigmoid(gb_ref[...].astype(F32)) * bs)
    h = x_ref[...] + _dot_split(merged.astype(BF16), wo_ref[...])
    h_ref[...] = h
    ms = jnp.mean(h * h, axis=-1, keepdims=True)
    hn = h * lax.rsqrt(ms + EPS) * g_ref[...]
    hn_ref[...] = _pack_halves(hn)

    hn_hi = hn.astype(BF16)
    hn_lo = (hn - hn_hi.astype(F32)).astype(BF16)
    logits = (_dot_split(hn_hi, wrh_ref[...]) + _dot_split(hn_lo, wrh_ref[...])
              + _dot_split(hn_hi, wrl_ref[...]))
    logits = logits + br_ref[...]
    lane = lax.broadcasted_iota(I32, logits.shape, 1)
    vals, ids = [], []
    for _ in range(TOP_K_EXPERTS):
        mx = jnp.max(logits, axis=-1, keepdims=True)
        ix = jnp.min(jnp.where(logits == mx, lane, LANES), axis=-1, keepdims=True)
        vals.append(mx)
        ids.append(ix)
        logits = jnp.where(lane == ix, -jnp.inf, logits)
    es = [jnp.exp(v - vals[0]) for v in vals]
    inv = 1.0 / (es[0] + es[1] + es[2] + es[3])
    route = jnp.zeros(logits.shape, F32)
    for k in range(TOP_K_EXPERTS):
        route = jnp.where(lane == k, es[k] * inv, route)
        route = jnp.where(lane == TOP_K_EXPERTS + k, ids[k].astype(F32), route)
    route_ref[...] = route


def _merge(x2, y_diff, y_dsa, proj, w_bd, w_bs, w_out, g_ffn, w_router_hi, w_router_lo, b_router, tm):
    n, d = x2.shape
    row = lambda i: (i, 0)
    const = lambda i: (0, 0)
    return pl.pallas_call(
        _merge_kernel,
        out_shape=(jax.ShapeDtypeStruct((n, d), F32),
                   jax.ShapeDtypeStruct((n, d // 2), I32),
                   jax.ShapeDtypeStruct((n, LANES), F32)),
        grid=(n // tm,),
        in_specs=[pl.BlockSpec((tm, d), row),
                  pl.BlockSpec((tm, d), row),
                  pl.BlockSpec((tm, d), row),
                  pl.BlockSpec((tm, d), lambda i: (i, COL_GA // 1024)),
                  pl.BlockSpec((tm, d), lambda i: (i, COL_GB // 1024)),
                  pl.BlockSpec((d, d), const),
                  pl.BlockSpec((d, d), const),
                  pl.BlockSpec((d, d), const),
                  pl.BlockSpec((1, d), const),
                  pl.BlockSpec((d, LANES), const),
                  pl.BlockSpec((d, LANES), const),
                  pl.BlockSpec((1, LANES), const)],
        out_specs=(pl.BlockSpec((tm, d), row),
                   pl.BlockSpec((tm, d // 2), row),
                   pl.BlockSpec((tm, LANES), row)),
        compiler_params=_cparams(("parallel",)),
        name="merge_router",
    )(x2, y_diff, y_dsa, proj, proj, w_bd, w_bs, w_out, g_ffn, w_router_hi, w_router_lo, b_router)


def _regroup_kernel(w_ref, p_ref, o_ref):
    pw = p_ref.shape[0]
    for j in range(w_ref.shape[2] // pw):
        w = w_ref[0, :, j * pw:(j + 1) * pw].astype(BF16)
        o_ref[0, :, j * pw:(j + 1) * pw] = jnp.dot(w, p_ref[...], preferred_element_type=F32).astype(BF16)


def _regroup_gate_up(w_gu, rows):
    e, d, f2 = w_gu.shape
    pw = 2 * LANES
    src = jnp.arange(pw, dtype=I32)
    dst = (src % 2) * LANES + src // 2
    perm = (dst[:, None] == jnp.arange(pw, dtype=I32)[None, :]).astype(BF16)
    return pl.pallas_call(
        _regroup_kernel,
        out_shape=jax.ShapeDtypeStruct((e, d, f2), BF16),
        grid=(e, d // rows),
        in_specs=[pl.BlockSpec((1, rows, f2), lambda i, j: (i, j, 0)),
                  pl.BlockSpec((pw, pw), lambda i, j: (0, 0))],
        out_specs=pl.BlockSpec((1, rows, f2), lambda i, j: (i, j, 0)),
        compiler_params=_cparams(("parallel", "parallel")),
        name="regroup_gate_up",
    )(w_gu, perm)


def _ffn_kernel(be_ref, nu_ref, x_ref, wgu_ref, wd_ref, bgu_ref, bd_ref, o_ref):
    @pl.when(pl.program_id(0) < nu_ref[0])
    def _():
        x_lo, x_hi = _unpack_halves(x_ref[...])
        half = x_lo.shape[1]
        gu = (jnp.dot(x_lo.astype(BF16), wgu_ref[0, 0:half, :], preferred_element_type=F32)
              + jnp.dot(x_hi.astype(BF16), wgu_ref[0, half:2 * half, :], preferred_element_type=F32)
              + bgu_ref[0])
        acts = []
        for j in range(gu.shape[1] // (2 * LANES)):
            gate = jnp.minimum(gu[:, 2 * j * LANES:(2 * j + 1) * LANES], SWIGLU_LIMIT)
            up = jnp.clip(gu[:, (2 * j + 1) * LANES:(2 * j + 2) * LANES], -SWIGLU_LIMIT, SWIGLU_LIMIT)
            glu = gate * jax.nn.sigmoid(gate * SWIGLU_ALPHA)
            acts.append(((up + 1.0) * glu).astype(BF16))
        a = jnp.concatenate(acts, axis=1)
        y = jnp.dot(a, wd_ref[0], preferred_element_type=F32) + bd_ref[0]
        o_ref[...] = _pack_halves(y)

    @pl.when(pl.program_id(0) >= nu_ref[0])
    def _():
        o_ref[...] = jnp.zeros(o_ref.shape, o_ref.dtype)


def _expert_ffn(blk_exp, n_used, xs, wgu, wd, bgu, bd):
    p, dw = xs.shape
    f, d = wd.shape[1], wd.shape[2]
    nblk = p // MOE_ROWS
    wmap = lambda i, be, nu: (be[i], 0, 0)
    grid_spec = pltpu.PrefetchScalarGridSpec(
        num_scalar_prefetch=2,
        grid=(nblk,),
        in_specs=[pl.BlockSpec((MOE_ROWS, dw), lambda i, be, nu: (i, 0)),
                  pl.BlockSpec((1, d, 2 * f), wmap),
                  pl.BlockSpec((1, f, d), wmap),
                  pl.BlockSpec((1, 1, 2 * f), wmap),
                  pl.BlockSpec((1, 1, d), wmap)],
        out_specs=pl.BlockSpec((MOE_ROWS, dw), lambda i, be, nu: (i, 0)),
    )
    return pl.pallas_call(
        _ffn_kernel,
        out_shape=jax.ShapeDtypeStruct((p, dw), I32),
        grid_spec=grid_spec,
        compiler_params=_cparams(("arbitrary",)),
        name="expert_ffn",
    )(blk_exp, n_used, xs, wgu, wd, bgu, bd)


def _route_kernel(route_ref, dest_ref, cnt_ref, u_ref, carry_ref, pstart_ref, *, block_rows):
    ph, i = pl.program_id(0), pl.program_id(1)
    tm = route_ref.shape[0]

    @pl.when((ph == 0) & (i == 0))
    def _():
        r = lax.broadcasted_iota(I32, (tm, tm), 0)
        c = lax.broadcasted_iota(I32, (tm, tm), 1)
        u_ref[...] = jnp.where(r < c, 1.0, 0.0).astype(BF16)
        carry_ref[...] = jnp.zeros(carry_ref.shape, F32)

    @pl.when((ph == 1) & (i == 0))
    def _():
        counts = carry_ref[...]
        cnt_ref[...] = counts
        padded = jnp.ceil(counts * (1.0 / block_rows)) * block_rows
        r = lax.broadcasted_iota(I32, (LANES, LANES), 0)
        c = lax.broadcasted_iota(I32, (LANES, LANES), 1)
        lower = jnp.where(c < r, 1.0, 0.0).astype(F32)
        pstart_ref[...] = jnp.dot(lower, padded, preferred_element_type=F32, precision=lax.Precision.HIGHEST)
        carry_ref[...] = jnp.zeros(carry_ref.shape, F32)

    rt = route_ref[...].T
    sub = lax.broadcasted_iota(I32, (LANES, tm), 0)
    hits = [sub == rt[TOP_K_EXPERTS + k:TOP_K_EXPERTS + k + 1, :].astype(I32) for k in range(TOP_K_EXPERTS)]
    m = jnp.zeros((LANES, tm), F32)
    for hit in hits:
        m = m + jnp.where(hit, 1.0, 0.0)
    tile_counts = jnp.broadcast_to(jnp.sum(m, axis=1, keepdims=True), (LANES, LANES))

    @pl.when(ph == 0)
    def _():
        dest_ref[...] = jnp.zeros(dest_ref.shape, I32)

    @pl.when(ph == 1)
    def _():
        prefix = jnp.dot(m.astype(BF16), u_ref[...], preferred_element_type=F32)
        rank = prefix + (pstart_ref[:, 0:1] + carry_ref[:, 0:1])
        rows = [jnp.sum(jnp.where(hit, rank, 0.0), axis=0, keepdims=True) for hit in hits]
        rows.append(jnp.zeros((dest_ref.shape[0] - TOP_K_EXPERTS, tm), F32))
        dest_ref[...] = jnp.concatenate(rows, axis=0).astype(I32)

    carry_ref[...] = carry_ref[...] + tile_counts


def _route_rows(route, tm):
    n = route.shape[0]
    nt = n // tm
    kernel = functools.partial(_route_kernel, block_rows=MOE_ROWS)
    return pl.pallas_call(
        kernel,
        out_shape=(jax.ShapeDtypeStruct((8, n + tm), I32), jax.ShapeDtypeStruct((LANES, LANES), F32)),
        grid=(2, nt),
        in_specs=[pl.BlockSpec((tm, LANES), lambda ph, i: (i, 0))],
        out_specs=(pl.BlockSpec((8, tm), lambda ph, i: (0, ph * i + (1 - ph) * nt)),
                   pl.BlockSpec((LANES, LANES), lambda ph, i: (0, 0))),
        scratch_shapes=[pltpu.VMEM((tm, tm), BF16),
                        pltpu.VMEM((LANES, LANES), F32),
                        pltpu.VMEM((LANES, LANES), F32)],
        compiler_params=_cparams(("arbitrary", "arbitrary")),
        name="route_rows",
    )(route)


SC_WINDOW = 128
SC_WORKERS = 32


def _sc_mesh():
    return plsc.VectorSubcoreMesh(core_axis_name="c", subcore_axis_name="s")


def _sc_scatter_rows(src, dest, p):
    n, d = src.shape
    per = n // (SC_WINDOW * SC_WORKERS)

    @pl.kernel(out_type=jax.ShapeDtypeStruct((p, d), src.dtype), mesh=_sc_mesh(),
               scratch_types=[pltpu.VMEM((dest.shape[0], SC_WINDOW), I32), pltpu.VMEM((SC_WINDOW, d), src.dtype)])
    def scatter(src_hbm, idx_hbm, out_hbm, idx_vmem, buf):
        wid = lax.axis_index("c") * (SC_WORKERS // 2) + lax.axis_index("s")

        @pl.loop(0, per)
        def _(j):
            off = (wid * per + j) * SC_WINDOW
            pltpu.sync_copy(idx_hbm.at[:, pl.ds(off, SC_WINDOW)], idx_vmem)
            pltpu.sync_copy(src_hbm.at[pl.ds(off, SC_WINDOW), :], buf)
            for k in range(TOP_K_EXPERTS):
                pltpu.sync_copy(buf, out_hbm.at[idx_vmem.at[k]])

    return scatter(src, dest)


def _sc_gather_rows(src, dest):
    n = dest.shape[1]
    d = src.shape[1]
    per = n // (SC_WINDOW * SC_WORKERS)

    @pl.kernel(out_type=jax.ShapeDtypeStruct((TOP_K_EXPERTS * n, d), src.dtype), mesh=_sc_mesh(),
               scratch_types=[pltpu.VMEM((dest.shape[0], SC_WINDOW), I32), pltpu.VMEM((SC_WINDOW, d), src.dtype)])
    def gather(src_hbm, idx_hbm, out_hbm, idx_vmem, buf):
        wid = lax.axis_index("c") * (SC_WORKERS // 2) + lax.axis_index("s")

        @pl.loop(0, per)
        def _(j):
            off = (wid * per + j) * SC_WINDOW
            pltpu.sync_copy(idx_hbm.at[:, pl.ds(off, SC_WINDOW)], idx_vmem)
            for k in range(TOP_K_EXPERTS):
                pltpu.sync_copy(src_hbm.at[idx_vmem.at[k]], buf)
                pltpu.sync_copy(buf, out_hbm.at[pl.ds(k * n + off, SC_WINDOW), :])

    return gather(src, dest)


def _combine_kernel(h_ref, y_ref, route_ref, g_ref, o_ref):
    half = h_ref.shape[1] // 2
    h_lo, h_hi = h_ref[:, 0:half], h_ref[:, half:2 * half]
    route = route_ref[...]
    for k in range(TOP_K_EXPERTS):
        y_lo, y_hi = _unpack_halves(y_ref[k])
        gate = route[:, k:k + 1]
        h_lo = h_lo + gate * y_lo
        h_hi = h_hi + gate * y_hi
    ms = (jnp.sum(h_lo * h_lo, axis=-1, keepdims=True)
          + jnp.sum(h_hi * h_hi, axis=-1, keepdims=True)) * (1.0 / (2 * half))
    inv = lax.rsqrt(ms + EPS)
    o_ref[:, 0:half] = h_lo * inv * g_ref[:, 0:half]
    o_ref[:, half:2 * half] = h_hi * inv * g_ref[:, half:2 * half]


def _combine(h1, yg, route, g_final, tm):
    n, d = h1.shape
    return pl.pallas_call(
        _combine_kernel,
        out_shape=jax.ShapeDtypeStruct((n, d), F32),
        grid=(n // tm,),
        in_specs=[pl.BlockSpec((tm, d), lambda i: (i, 0)),
                  pl.BlockSpec((TOP_K_EXPERTS, tm, d // 2), lambda i: (0, i, 0)),
                  pl.BlockSpec((tm, LANES), lambda i: (i, 0)),
                  pl.BlockSpec((1, d), lambda i: (0, 0))],
        out_specs=pl.BlockSpec((tm, d), lambda i: (i, 0)),
        compiler_params=_cparams(("parallel",)),
        name="combine_norm",
    )(h1, yg, route, g_final)


def _t5_bucket(dist):
    n = jnp.maximum(dist, 0)
    max_exact = N_BUCKETS // 2
    nf = jnp.maximum(n, 1).astype(F32)
    large = max_exact + (jnp.log(nf / max_exact) / math.log(MAX_DISTANCE / max_exact)
                         * (N_BUCKETS - max_exact)).astype(I32)
    large = jnp.minimum(large, N_BUCKETS - 1)
    return jnp.where(n < max_exact, n, large)


def _bias_blocks(bias_tab):
    t = LANES
    assert MAX_DISTANCE <= LANES
    r = jnp.arange(t, dtype=I32)[:, None]
    c = jnp.arange(t, dtype=I32)[None, :]
    rel = (bias_tab - bias_tab[N_BUCKETS - 1][None, :]).astype(F32)
    tiles = []
    buckets = jnp.arange(N_BUCKETS, dtype=I32)[:, None, None]
    for delta in (0, t):
        dist = r - c + delta
        hit = _t5_bucket(dist)[None] == buckets
        b = jnp.sum(jnp.where(hit[:, None], rel[:, :, None, None], 0.0), axis=0)
        tiles.append(jnp.where((dist >= 0)[None], b * LOG2E, MASK_VALUE))
    return jnp.stack(tiles)


def _regroup_w_in(w_in):
    sizes = (1024, 1024, 1024, 1024, KV_LATENT, 1024, HEAD_DIM_IDX, N_HEADS_IDX, D_MODEL, D_MODEL)
    parts, off = [], 0
    for sz in sizes:
        parts.append(w_in[:, off:off + sz])
        off += sz
    dq, dk, dv, sq, ckv, iq, ik, iw, ga, gb = parts
    dq = dq * (HEAD_DIM_DIFF ** -0.5 * LOG2E)
    iw = iw * ((N_HEADS_IDX ** -0.5) * (HEAD_DIM_IDX ** -0.5))
    pad = jnp.zeros((w_in.shape[0], PROJ_WIDTH - COL_IW - N_HEADS_IDX), w_in.dtype)
    w = jnp.concatenate([dq, dk, dv, sq, iq, ga, gb, ckv, ik, ik, iw, pad], axis=1)
    return w.astype(BF16)


def _block_tables(counts, n_assign):
    e, bm = N_EXPERTS, MOE_ROWS
    padded = (counts + bm - 1) // bm * bm
    pends = jnp.cumsum(padded)
    nblk = -(-(n_assign + e * (bm - 1)) // bm)
    first_row = jnp.arange(nblk, dtype=I32) * bm
    blk_exp = jnp.minimum(jnp.sum((pends[None, :] <= first_row[:, None]).astype(I32), axis=1), e - 1)
    n_used = (pends[-1] // bm).astype(I32).reshape(1)
    return blk_exp, n_used, nblk


def kernel(x, norm_attn_g, w_in, rel_bias, lam_q1, lam_k1, lam_q2, lam_k2, diff_subln_g, kv_norm_g, w_uk, w_uv,
           w_branch_diff, w_branch_dsa, w_out, norm_ffn_g, w_router, b_router, w_gate_up, b_gate_up, w_down,
           b_down, norm_final_g):
    batch, seq, d = x.shape
    n = batch * seq
    assert norm_attn_g.shape[0] == 1, "single-layer kernel"
    assert seq % DIFF_BLOCK == 0 and seq % KEY_CHUNK == 0 and d == D_MODEL
    row_tile = math.gcd(n, 1024)

    x2 = x.reshape(n, d)
    proj = _inproj(x2, norm_attn_g[0].reshape(1, d), _regroup_w_in(w_in[0]), row_tile, 1280)

    lam_init = 0.8 - 0.6 * math.exp(-0.3 * 0)
    lam = (jnp.exp(jnp.sum(lam_q1[0].astype(F32) * lam_k1[0].astype(F32)))
           - jnp.exp(jnp.sum(lam_q2[0].astype(F32) * lam_k2[0].astype(F32))) + lam_init)
    y_diff = _diff_attention(proj, lam.reshape(1, 1).astype(F32), _bias_blocks(rel_bias[:, :N_HEADS_DIFF]),
                             diff_subln_g[0].reshape(1, -1).astype(F32), batch, seq, 1.0 - lam_init)

    y_dsa = _dsa_attention(proj, kv_norm_g[0].reshape(1, -1).astype(F32),
                           w_uk[0].transpose(0, 2, 1).astype(BF16), w_uv[0].astype(BF16),
                           _bias_blocks(rel_bias[:, N_HEADS_DIFF:]), batch, seq, min(TOPK_MAX, seq // 4))

    w_r = jnp.zeros((d, LANES), F32).at[:, :N_EXPERTS].set(w_router[0].astype(F32))
    b_r = jnp.full((1, LANES), MASK_VALUE, F32).at[0, :N_EXPERTS].set(b_router[0].astype(F32))
    h1, hn, route = _merge(x2, y_diff, y_dsa, proj, w_branch_diff[0].astype(BF16), w_branch_dsa[0].astype(BF16),
                           w_out[0].astype(BF16), norm_ffn_g[0].reshape(1, d).astype(F32),
                           w_r.astype(BF16), (w_r - w_r.astype(BF16).astype(F32)).astype(BF16), b_r,
                           math.gcd(n, 512))

    dest, counts = _route_rows(route, math.gcd(n, 1024))
    dest = dest[:, :n]
    blk_exp, n_used, nblk = _block_tables(counts[:N_EXPERTS, 0].astype(I32), n * TOP_K_EXPERTS)
    xs = _sc_scatter_rows(hn, dest, nblk * MOE_ROWS)
    e, f = N_EXPERTS, D_EXPERT
    b_gu = b_gate_up[0].astype(F32).reshape(e, f // LANES, LANES, 2).transpose(0, 1, 3, 2).reshape(e, 1, 2 * f)
    ys = _expert_ffn(blk_exp, n_used, xs, _regroup_gate_up(w_gate_up[0], 512), w_down[0].astype(BF16),
                     b_gu, b_down[0][:, None, :].astype(F32))
    yg = _sc_gather_rows(ys, dest).reshape(TOP_K_EXPERTS, n, d // 2)
    out = _combine(h1, yg, route, norm_final_g.reshape(1, d).astype(F32), math.gcd(n, 512))
    return out.reshape(batch, seq, d)
```

```python
import functools
import math

import jax
import jax.numpy as jnp
from jax import lax
from jax.experimental import pallas as pl
from jax.experimental.pallas import tpu as pltpu
from jax.experimental.pallas import tpu_sc as plsc

F32 = jnp.float32
BF16 = jnp.bfloat16
I32 = jnp.int32

D_MODEL = 1024
N_HEADS_DIFF = 8
HEAD_DIM_DIFF = 64
N_HEADS_DSA = 8
HEAD_DIM_DSA = 128
KV_LATENT = 256
N_HEADS_IDX = 16
HEAD_DIM_IDX = 64
TOPK_MAX = 256
N_BUCKETS = 32
MAX_DISTANCE = 128
N_EXPERTS = 32
TOP_K_EXPERTS = 4
D_EXPERT = 1024
SWIGLU_LIMIT = 7.0
SWIGLU_ALPHA = 1.702
EPS = 1e-6

LANES = 128
DIFF_BLOCK = 512
DIFF_HEADS_PER_STEP = 4
DSA_BLOCK = 256
KEY_CHUNK = 512
DSA_STREAMS = 2
MOE_ROWS = 512
PROJ_WIDTH = 7680
VMEM_LIMIT = 56 * 1024 * 1024

COL_DQ, COL_DK, COL_DV, COL_SQ, COL_IQ, COL_GA, COL_GB = (i * 1024 for i in range(7))
COL_CKV = 7168
COL_IK = 7424
COL_IW = 7552

LOG2E = math.log2(math.e)
MASK_VALUE = -1e30
M_INIT = -1e29
INT_MIN = -2 ** 31


def _cparams(sem):
    return pltpu.CompilerParams(dimension_semantics=sem, vmem_limit_bytes=VMEM_LIMIT)


def _inproj_kernel(x_ref, g_ref, w_ref, o_ref, xn_ref):
    @pl.when(pl.program_id(1) == 0)
    def _():
        x = x_ref[...]
        ms = jnp.mean(x * x, axis=-1, keepdims=True)
        xn_ref[...] = (x * lax.rsqrt(ms + EPS) * g_ref[...]).astype(BF16)

    o_ref[...] = jnp.dot(xn_ref[...], w_ref[...], preferred_element_type=F32).astype(o_ref.dtype)


def _inproj(x2, g, w, tm, tn):
    n, d = x2.shape
    width = w.shape[1]
    return pl.pallas_call(
        _inproj_kernel,
        out_shape=jax.ShapeDtypeStruct((n, width), BF16),
        grid=(n // tm, width // tn),
        in_specs=[pl.BlockSpec((tm, d), lambda i, j: (i, 0)),
                  pl.BlockSpec((1, d), lambda i, j: (0, 0)),
                  pl.BlockSpec((d, tn), lambda i, j: (0, j))],
        out_specs=pl.BlockSpec((tm, tn), lambda i, j: (i, j)),
        scratch_shapes=[pltpu.VMEM((tm, d), BF16)],
        compiler_params=_cparams(("parallel", "arbitrary")),
        name="inproj",
    )(x2, g, w)


def _lane_chunk_max(s):
    smax = s[:, 0:LANES]
    for j in range(1, s.shape[1] // LANES):
        smax = jnp.maximum(smax, s[:, j * LANES:(j + 1) * LANES])
    return smax


def _softmax_step(s_ref, tk, v, m_ref, l_ref, acc_ref, smax=None, split=True):
    nl = tk // LANES
    if smax is None:
        smax = _lane_chunk_max(s_ref[:, 0:tk])
    m_prev = m_ref[...]
    m_new = jnp.maximum(m_prev, jnp.max(smax, axis=-1, keepdims=True))
    alpha = jnp.exp2(m_prev - m_new)
    psum = None
    ps = []
    for j in range(nl):
        pj = jnp.exp2(s_ref[:, j * LANES:(j + 1) * LANES] - m_new)
        psum = pj if psum is None else psum + pj
        ps.append(pj.astype(BF16))
    l_ref[...] = alpha * l_ref[...] + psum
    pv = _dot_split(jnp.concatenate(ps, axis=1), v, split)
    e = acc_ref.shape[1]
    a = alpha if e == LANES else jnp.concatenate([alpha] * (e // LANES), axis=1)
    acc_ref[...] = a * acc_ref[...] + pv
    m_ref[...] = m_new


def _softmax_init(m_ref, l_ref, acc_ref):
    m_ref[...] = jnp.full(m_ref.shape, M_INIT, F32)
    l_ref[...] = jnp.zeros(l_ref.shape, F32)
    acc_ref[...] = jnp.zeros(acc_ref.shape, F32)


def _softmax_result(l_ref, acc_ref):
    return acc_ref[...] * (1.0 / jnp.sum(l_ref[...], axis=-1, keepdims=True))


def _near_bias(s_ref, d0, d1, delta, groups, t, tk):
    for g in range(groups):
        for rb in range(t // LANES):
            for cb in range(tk // LANES):
                bd = delta + rb - cb
                rows = slice(g * t + rb * LANES, g * t + (rb + 1) * LANES)
                cols = slice(cb * LANES, (cb + 1) * LANES)
                if bd == 0:
                    s_ref[rows, cols] = s_ref[rows, cols] + d0(g)
                elif bd == 1:
                    s_ref[rows, cols] = s_ref[rows, cols] + d1(g)
                elif bd < 0:
                    s_ref[rows, cols] = jnp.full((LANES, LANES), MASK_VALUE, F32)


def _pipelined_far(n_far, issue, finish):
    odd = n_far % 2

    @pl.when(odd == 1)
    def _():
        issue(0, 1)
        issue(1, 0)
        finish(0, 1)

    @pl.when(odd == 0)
    def _():
        issue(0, 0)

    def body(j, carry):
        a = odd + 2 * j
        issue(a + 1, 1)
        finish(a, 0)
        issue(a + 2, 0)
        finish(a + 1, 1)
        return carry

    lax.fori_loop(0, (n_far - odd) // 2, body, 0)


def _dot_nt(a, b, split=True):
    dn = (((1,), (1,)), ((), ()))
    if not split:
        return lax.dot_general(a, b, dn, preferred_element_type=F32)
    h = a.shape[0] // 2
    return jnp.concatenate([lax.dot_general(a[:h], b, dn, preferred_element_type=F32),
                            lax.dot_general(a[h:], b, dn, preferred_element_type=F32)], axis=0)


def _dot_split(a, b, split=True):
    if not split:
        return jnp.dot(a, b, preferred_element_type=F32)
    h = a.shape[0] // 2
    return jnp.concatenate([jnp.dot(a[:h], b, preferred_element_type=F32),
                            jnp.dot(a[h:], b, preferred_element_type=F32)], axis=0)


def _diff_kernel(lam_ref, q_ref, k_ref, v_ref, bias_ref, g_ref, o_ref, q2_ref, s_ref, m_ref, l_ref, acc_ref,
                 *, out_scale, heads):
    t = q_ref.shape[0]
    e = 2 * HEAD_DIM_DIFF
    qi = pl.program_id(2)
    cols = [slice(hh * e, (hh + 1) * e) for hh in range(heads)]

    lane = lax.broadcasted_iota(I32, (t, e), 1)
    for hh in range(heads):
        q = q_ref[:, cols[hh]]
        zero = jnp.zeros_like(q)
        q2_ref[hh, 0:t, :] = jnp.where(lane < HEAD_DIM_DIFF, q, zero)
        q2_ref[hh, t:2 * t, :] = jnp.where(lane >= HEAD_DIM_DIFF, q, zero)
        _softmax_init(m_ref.at[hh], l_ref.at[hh], acc_ref.at[hh])

    def chunk(kc, delta):
        off = pl.multiple_of(kc * t, t)
        for hh in range(heads):
            s = _dot_nt(q2_ref[hh], k_ref[pl.ds(off, t), cols[hh]])
            s_ref[hh] = s
            smax = None
            if delta is None:
                smax = _lane_chunk_max(s)
            else:
                _near_bias(s_ref.at[hh], lambda g: bias_ref[0, hh], lambda g: bias_ref[1, hh], delta, 2, t, t)
            _softmax_step(s_ref.at[hh], t, v_ref[pl.ds(off, t), cols[hh]], m_ref.at[hh], l_ref.at[hh],
                          acc_ref.at[hh], smax)

    def far_body(kc, carry):
        chunk(kc, None)
        return carry

    lax.fori_loop(0, jnp.maximum(qi - 1, 0), far_body, 0)

    @pl.when(qi >= 1)
    def _():
        chunk(qi - 1, t // LANES)

    chunk(qi, 0)

    for hh in range(heads):
        o = _softmax_result(l_ref.at[hh], acc_ref.at[hh])
        o = o[0:t, :] - lam_ref[0, 0] * o[t:2 * t, :]
        ms = jnp.mean(o * o, axis=-1, keepdims=True)
        o_ref[:, cols[hh]] = (o * lax.rsqrt(ms + EPS) * g_ref[...] * out_scale).astype(o_ref.dtype)


def _diff_attention(proj, lam, bias_blocks, subln_g, batch, seq, out_scale):
    t = DIFF_BLOCK
    nq = seq // t
    h = N_HEADS_DIFF
    e = 2 * HEAD_DIM_DIFF
    hp = DIFF_HEADS_PER_STEP
    w = hp * e
    kernel = functools.partial(_diff_kernel, out_scale=out_scale, heads=hp)
    return pl.pallas_call(
        kernel,
        out_shape=jax.ShapeDtypeStruct((batch * seq, h * e), BF16),
        grid=(batch, h // hp, nq),
        in_specs=[pl.BlockSpec(memory_space=pltpu.SMEM),
                  pl.BlockSpec((t, w), lambda b, hh, qi: (b * nq + qi, COL_DQ // w + hh)),
                  pl.BlockSpec((seq, w), lambda b, hh, qi: (b, COL_DK // w + hh)),
                  pl.BlockSpec((seq, w), lambda b, hh, qi: (b, COL_DV // w + hh)),
                  pl.BlockSpec((2, hp, LANES, LANES), lambda b, hh, qi: (0, hh, 0, 0)),
                  pl.BlockSpec((1, e), lambda b, hh, qi: (0, 0))],
        out_specs=pl.BlockSpec((t, w), lambda b, hh, qi: (b * nq + qi, hh)),
        scratch_shapes=[pltpu.VMEM((hp, 2 * t, e), BF16),
                        pltpu.VMEM((hp, 2 * t, t), F32),
                        pltpu.VMEM((hp, 2 * t, LANES), F32),
                        pltpu.VMEM((hp, 2 * t, LANES), F32),
                        pltpu.VMEM((hp, 2 * t, e), F32)],
        compiler_params=_cparams(("parallel", "parallel", "arbitrary")),
        name="diff_attn",
    )(lam, proj, proj, proj, bias_blocks, subln_g)


def _sortable_key(x):
    bits = lax.bitcast_convert_type(x, I32)
    return bits ^ ((bits >> 31) & jnp.int32(0x7FFFFFFF))


def _dsa_kernel(iq_ref, sq_ref, iw_ref, ik_ref, ckv_ref, kvg_ref, wuk_ref, wuv_ref, bias_ref, o_ref,
                c_ref, key_ref, keyt_ref, keyh_ref, qi_ref, wb_ref, ql_ref, s_ref, m_ref, l_ref, acc_ref, *, topk, scale):
    t = iq_ref.shape[0]
    tk = KEY_CHUNK
    qi = pl.program_id(1)
    n_chunks = qi + 1
    hi, hb = N_HEADS_IDX, N_HEADS_DSA

    @pl.when(qi == 0)
    def _():
        ckv = ckv_ref[...].astype(F32)
        ms = jnp.mean(ckv * ckv, axis=-1, keepdims=True)
        c_ref[...] = (ckv * lax.rsqrt(ms + EPS) * kvg_ref[...]).astype(BF16)

    rg = qi_ref.shape[2]
    lane = lax.broadcasted_iota(I32, (t, LANES), 1)
    for h in range(hi):
        blk = iq_ref[:, (h // 2) * LANES:(h // 2 + 1) * LANES]
        keep = (lane < HEAD_DIM_IDX) if h % 2 == 0 else (lane >= HEAD_DIM_IDX)
        qi_ref[:, h, :, :] = jnp.where(keep, blk, jnp.zeros_like(blk)).reshape(t // rg, rg, LANES)
        wb_ref[:, h, :, :] = jnp.broadcast_to(iw_ref[:, h:h + 1].astype(F32), (t, LANES)).reshape(t // rg, rg, LANES)

    def score_chunk(kc, diag):
        off = pl.multiple_of(kc * t, t)
        d = _dot_nt(qi_ref[...].reshape(hi * t, LANES), ik_ref[pl.ds(off, t), :]).reshape(t // rg, hi, rg, t)
        w = jnp.concatenate([wb_ref[...]] * (t // LANES), axis=-1)
        sc = jnp.sum(jnp.maximum(d, 0.0) * w, axis=1).reshape(t, t)
        key = _sortable_key(sc + 0.0)
        if diag:
            row = lax.broadcasted_iota(I32, (t, t), 0)
            col = lax.broadcasted_iota(I32, (t, t), 1)
            key = jnp.where(col <= row, key, jnp.int32(INT_MIN))
        key_ref[:, pl.ds(off, t)] = key
        key_t = key.T
        keyt_ref[pl.ds(off, t), :] = key_t
        keyh_ref[pl.ds(off, t), :] = (key_t >> 16).astype(jnp.int16)

    def score_body(kc, carry):
        score_chunk(kc, False)
        return carry

    lax.fori_loop(0, qi, score_body, 0)
    score_chunk(qi, True)

    @pl.when(qi % 2 == 0)
    def _():
        off = pl.multiple_of((qi + 1) * t, t)
        key_ref[:, pl.ds(off, t)] = jnp.full((t, t), INT_MIN, I32)
        keyt_ref[pl.ds(off, t), :] = jnp.full((t, t), INT_MIN, I32)
        keyh_ref[pl.ds(off, t), :] = jnp.full((t, t), INT_MIN >> 16, jnp.int16)

    n_steps = (qi + 2) // 2

    def count_ge(cand_s):
        def body(kc, cnt):
            off = pl.multiple_of(kc * tk, tk)
            k = keyt_ref[pl.ds(off, tk), :].reshape(tk // 8, 8, t)
            return cnt + jnp.sum(jnp.where(k >= cand_s[None], 1, 0), axis=0)

        cnt = lax.fori_loop(0, n_steps, body, jnp.zeros((8, t), I32))
        return jnp.sum(cnt, axis=0, keepdims=True)

    def count_packed(c_row):
        c16 = jnp.broadcast_to(c_row.astype(jnp.int16), (16, t))

        def body(kc, cnt):
            off = pl.multiple_of(kc * tk, tk)
            k = keyh_ref[pl.ds(off, tk), :].reshape(tk // 16, 16, t)
            hit = jnp.where(k >= c16[None], jnp.int16(1), jnp.int16(0))
            for j in range(tk // 16):
                cnt = cnt + hit[j]
            return cnt

        cnt = lax.fori_loop(0, n_steps, body, jnp.zeros((16, t), jnp.int16))
        return jnp.sum(cnt.astype(I32), axis=0, keepdims=True)

    def make_bit_body(count):
        def bit_body(i, carry):
            cur, n_ge = carry
            bit = lax.shift_left(jnp.int32(1), 31 - i)
            cand = cur | bit
            total = count(cand ^ jnp.int32(INT_MIN))
            accept = total >= topk
            return jnp.where(accept, cand, cur), jnp.where(accept, total, n_ge)
        return bit_body

    zeros8 = jnp.zeros((8, t), I32)
    low_bias = jnp.int32(1 << 15)
    cur, n_ge = lax.fori_loop(
        0, 16, make_bit_body(lambda cand_s: count_packed(cand_s[0:1, :] >> 16)), (zeros8, zeros8))
    high = (cur ^ jnp.int32(INT_MIN)) >> 16
    above_high = jnp.where(high[0:1, :] == 2 ** 15 - 1, 0, count_packed(jnp.minimum(high[0:1, :], 2 ** 15 - 2) + 1))

    def repack_body(kc, carry):
        off = pl.multiple_of(kc * tk, tk)
        k = keyt_ref[pl.ds(off, tk), :].reshape(tk // 8, 8, t)
        low = jnp.where((k >> 16) == high[None], (k & jnp.int32(0xFFFF)) - low_bias, -low_bias)
        keyh_ref[pl.ds(off, tk), :] = low.reshape(tk, t).astype(jnp.int16)
        return carry

    lax.fori_loop(0, n_steps, repack_body, 0)
    cur, n_ge = lax.fori_loop(
        16, 32, make_bit_body(lambda cand_s: above_high + count_packed((cand_s[0:1, :] & jnp.int32(0xFFFF)) - low_bias)),
        (cur, n_ge))
    thr = jnp.maximum(cur ^ jnp.int32(INT_MIN), jnp.int32(INT_MIN + 1))

    def lanes_to_rows(v):
        b = jnp.broadcast_to(v[0:1, :], (LANES, t)).T
        return jnp.concatenate([b] * (tk // LANES), axis=1)

    thr_w = lanes_to_rows(thr)
    ties = jnp.max(n_ge) > topk

    @pl.when(jnp.logical_not(ties))
    def _():
        def mask_body(kc, carry):
            off = pl.multiple_of(kc * tk, tk)
            am = jnp.where(key_ref[:, pl.ds(off, tk)] >= thr_w, 0.0, MASK_VALUE).astype(F32)
            key_ref[:, pl.ds(off, tk)] = lax.bitcast_convert_type(am, I32)
            return carry

        lax.fori_loop(0, n_steps, mask_body, 0)

    @pl.when(ties)
    def _():
        above = jnp.where(thr == jnp.int32(2 ** 31 - 1), 0, count_ge(jnp.minimum(thr, jnp.int32(2 ** 31 - 2)) + 1))
        need_w = lanes_to_rows(topk - above).astype(F32)
        r = lax.broadcasted_iota(I32, (tk, tk), 0)
        c = lax.broadcasted_iota(I32, (tk, tk), 1)
        before = jnp.where(r < c, 1.0, 0.0).astype(BF16)
        ones_w = jnp.ones((tk, tk), BF16)

        def tie_body(kc, seen):
            off = pl.multiple_of(kc * tk, tk)
            k = key_ref[:, pl.ds(off, tk)]
            eq = k == thr_w
            eq_b = jnp.where(eq, 1.0, 0.0).astype(BF16)
            rank = seen + jnp.dot(eq_b, before, preferred_element_type=F32)
            keep = (k > thr_w) | (eq & (rank < need_w))
            key_ref[:, pl.ds(off, tk)] = lax.bitcast_convert_type(jnp.where(keep, 0.0, MASK_VALUE).astype(F32), I32)
            return seen + jnp.dot(eq_b, ones_w, preferred_element_type=F32)

        lax.fori_loop(0, n_steps, tie_body, jnp.zeros((t, tk), F32))

    streams = ql_ref.shape[0]
    split = streams == 1
    hs = hb // streams
    for h in range(hb):
        qh = sq_ref[:, h * HEAD_DIM_DSA:(h + 1) * HEAD_DIM_DSA]
        ql = jnp.dot(qh, wuk_ref[h], preferred_element_type=F32) * scale
        ql_ref[h // hs, (h % hs) * t:(h % hs + 1) * t, :] = ql.astype(BF16)
    for g in range(streams):
        _softmax_init(m_ref.at[g], l_ref.at[g], acc_ref.at[g])

    def chunk(kc, width, delta):
        off = pl.multiple_of(kc * tk, tk)
        c = c_ref[pl.ds(off, width), :]
        am = lax.bitcast_convert_type(key_ref[:, pl.ds(off, width)], F32)
        for g in range(streams):
            s = (_dot_nt(ql_ref[g], c, split).reshape(hs, t, width) + am[None]).reshape(hs * t, width)
            s_ref[g, :, 0:width] = s
            smax = None
            if delta is None:
                smax = _lane_chunk_max(s)
            else:
                _near_bias(s_ref.at[g], lambda j: bias_ref[0, g * hs + j], lambda j: bias_ref[1, g * hs + j],
                           delta, hs, t, width)
            _softmax_step(s_ref.at[g], width, c, m_ref.at[g], l_ref.at[g], acc_ref.at[g], smax, split)

    def far_body(kc, carry):
        chunk(kc, tk, None)
        return carry

    lax.fori_loop(0, jnp.maximum((qi - 1) // 2, 0), far_body, 0)
    half = qi // 2

    @pl.when(qi % 2 == 1)
    def _():
        chunk(half, tk, t // LANES)

    @pl.when(qi % 2 == 0)
    def _():
        @pl.when(half >= 1)
        def _():
            chunk(half - 1, tk, tk // LANES)
        chunk(half, t, 0)

    for h in range(hb):
        g, j = h // hs, h % hs
        inv_l = 1.0 / jnp.sum(l_ref[g, j * t:(j + 1) * t, :], axis=-1, keepdims=True)
        ol = (acc_ref[g, j * t:(j + 1) * t, :] * inv_l).astype(BF16)
        o = jnp.dot(ol, wuv_ref[h], preferred_element_type=F32)
        o_ref[:, h * HEAD_DIM_DSA:(h + 1) * HEAD_DIM_DSA] = o.astype(o_ref.dtype)


def _dsa_attention(proj, kv_g, w_ukt, w_uv, bias_blocks, batch, seq, topk):
    t = DSA_BLOCK
    nq = seq // t
    hb, hi = N_HEADS_DSA, N_HEADS_IDX
    width = hb * HEAD_DIM_DSA
    ns = DSA_STREAMS
    rs = hb // ns * t
    kernel = functools.partial(_dsa_kernel, topk=topk, scale=HEAD_DIM_DSA ** -0.5 * LOG2E)
    return pl.pallas_call(
        kernel,
        out_shape=jax.ShapeDtypeStruct((batch * seq, width), BF16),
        grid=(batch, nq),
        in_specs=[pl.BlockSpec((t, 1024), lambda b, qi: (b * nq + qi, COL_IQ // 1024)),
                  pl.BlockSpec((t, 1024), lambda b, qi: (b * nq + qi, COL_SQ // 1024)),
                  pl.BlockSpec((t, LANES), lambda b, qi: (b * nq + qi, COL_IW // LANES)),
                  pl.BlockSpec((seq, LANES), lambda b, qi: (b, COL_IK // LANES)),
                  pl.BlockSpec((seq, KV_LATENT), lambda b, qi: (b, COL_CKV // KV_LATENT)),
                  pl.BlockSpec((1, KV_LATENT), lambda b, qi: (0, 0)),
                  pl.BlockSpec((hb, HEAD_DIM_DSA, KV_LATENT), lambda b, qi: (0, 0, 0)),
                  pl.BlockSpec((hb, KV_LATENT, HEAD_DIM_DSA), lambda b, qi: (0, 0, 0)),
                  pl.BlockSpec((2, hb, LANES, LANES), lambda b, qi: (0, 0, 0, 0))],
        out_specs=pl.BlockSpec((t, width), lambda b, qi: (b * nq + qi, 0)),
        scratch_shapes=[pltpu.VMEM((seq, KV_LATENT), BF16),
                        pltpu.VMEM((t, seq), I32),
                        pltpu.VMEM((seq, t), I32),
                        pltpu.VMEM((seq, t), jnp.int16),
                        pltpu.VMEM((t // 16, hi, 16, LANES), BF16),
                        pltpu.VMEM((t // 16, hi, 16, LANES), F32),
                        pltpu.VMEM((ns, rs, KV_LATENT), BF16),
                        pltpu.VMEM((ns, rs, KEY_CHUNK), F32),
                        pltpu.VMEM((ns, rs, LANES), F32),
                        pltpu.VMEM((ns, rs, LANES), F32),
                        pltpu.VMEM((ns, rs, KV_LATENT), F32)],
        compiler_params=_cparams(("parallel", "arbitrary")),
        name="dsa_attn",
    )(proj, proj, proj, proj, proj, kv_g, w_ukt, w_uv, bias_blocks)


def _pack_halves(x):
    c = x.shape[1] // 2
    lo = lax.bitcast_convert_type(x[:, :c].astype(BF16).astype(F32), I32)
    hi = lax.bitcast_convert_type(x[:, c:].astype(BF16).astype(F32), I32)
    return lax.shift_right_logical(lo, 16) | (hi & jnp.int32(-65536))


def _unpack_halves(w):
    lo = lax.bitcast_convert_type(lax.shift_left(w, 16), F32)
    hi = lax.bitcast_convert_type(w & jnp.int32(-65536), F32)
    return lo, hi


def _merge_kernel(x_ref, yd_ref, ys_ref, ga_ref, gb_ref, wd_ref, ws_ref, wo_ref, g_ref, wrh_ref, wrl_ref, br_ref,
                  h_ref, hn_ref, route_ref):
    bd = _dot_split(yd_ref[...], wd_ref[...])
    bs = _dot_split(ys_ref[...], ws_ref[...])
    merged = (jax.nn.sigmoid(ga_ref[...].astype(F32)) * bd + jax.nn.sigmoid(gb_ref[...].astype(F32)) * bs)
    h = x_ref[...] + _dot_split(merged.astype(BF16), wo_ref[...])
    h_ref[...] = h
    ms = jnp.mean(h * h, axis=-1, keepdims=True)
    hn = h * lax.rsqrt(ms + EPS) * g_ref[...]
    hn_ref[...] = _pack_halves(hn)

    hn_hi = hn.astype(BF16)
    hn_lo = (hn - hn_hi.astype(F32)).astype(BF16)
    logits = (_dot_split(hn_hi, wrh_ref[...]) + _dot_split(hn_lo, wrh_ref[...])
              + _dot_split(hn_hi, wrl_ref[...]))
    logits = logits + br_ref[...]
    lane = lax.broadcasted_iota(I32, logits.shape, 1)
    vals, ids = [], []
    for _ in range(TOP_K_EXPERTS):
        mx = jnp.max(logits, axis=-1, keepdims=True)
        ix = jnp.min(jnp.where(logits == mx, lane, LANES), axis=-1, keepdims=True)
        vals.append(mx)
        ids.append(ix)
        logits = jnp.where(lane == ix, -jnp.inf, logits)
    es = [jnp.exp(v - vals[0]) for v in vals]
    inv = 1.0 / (es[0] + es[1] + es[2] + es[3])
    route = jnp.zeros(logits.shape, F32)
    for k in range(TOP_K_EXPERTS):
        route = jnp.where(lane == k, es[k] * inv, route)
        route = jnp.where(lane == TOP_K_EXPERTS + k, ids[k].astype(F32), route)
    route_ref[...] = route


def _merge(x2, y_diff, y_dsa, proj, w_bd, w_bs, w_out, g_ffn, w_router_hi, w_router_lo, b_router, tm):
    n, d = x2.shape
    row = lambda i: (i, 0)
    const = lambda i: (0, 0)
    return pl.pallas_call(
        _merge_kernel,
        out_shape=(jax.ShapeDtypeStruct((n, d), F32),
                   jax.ShapeDtypeStruct((n, d // 2), I32),
                   jax.ShapeDtypeStruct((n, LANES), F32)),
        grid=(n // tm,),
        in_specs=[pl.BlockSpec((tm, d), row),
                  pl.BlockSpec((tm, d), row),
                  pl.BlockSpec((tm, d), row),
                  pl.BlockSpec((tm, d), lambda i: (i, COL_GA // 1024)),
                  pl.BlockSpec((tm, d), lambda i: (i, COL_GB // 1024)),
                  pl.BlockSpec((d, d), const),
                  pl.BlockSpec((d, d), const),
                  pl.BlockSpec((d, d), const),
                  pl.BlockSpec((1, d), const),
                  pl.BlockSpec((d, LANES), const),
                  pl.BlockSpec((d, LANES), const),
                  pl.BlockSpec((1, LANES), const)],
        out_specs=(pl.BlockSpec((tm, d), row),
                   pl.BlockSpec((tm, d // 2), row),
                   pl.BlockSpec((tm, LANES), row)),
        compiler_params=_cparams(("parallel",)),
        name="merge_router",
    )(x2, y_diff, y_dsa, proj, proj, w_bd, w_bs, w_out, g_ffn, w_router_hi, w_router_lo, b_router)


def _regroup_kernel(w_ref, p_ref, o_ref):
    pw = p_ref.shape[0]
    for j in range(w_ref.shape[2] // pw):
        w = w_ref[0, :, j * pw:(j + 1) * pw].astype(BF16)
        o_ref[0, :, j * pw:(j + 1) * pw] = jnp.dot(w, p_ref[...], preferred_element_type=F32).astype(BF16)


def _regroup_gate_up(w_gu, rows):
    e, d, f2 = w_gu.shape
    pw = 2 * LANES
    src = jnp.arange(pw, dtype=I32)
    dst = (src % 2) * LANES + src // 2
    perm = (dst[:, None] == jnp.arange(pw, dtype=I32)[None, :]).astype(BF16)
    return pl.pallas_call(
        _regroup_kernel,
        out_shape=jax.ShapeDtypeStruct((e, d, f2), BF16),
        grid=(e, d // rows),
        in_specs=[pl.BlockSpec((1, rows, f2), lambda i, j: (i, j, 0)),
                  pl.BlockSpec((pw, pw), lambda i, j: (0, 0))],
        out_specs=pl.BlockSpec((1, rows, f2), lambda i, j: (i, j, 0)),
        compiler_params=_cparams(("parallel", "parallel")),
        name="regroup_gate_up",
    )(w_gu, perm)


def _ffn_kernel(be_ref, nu_ref, x_ref, wgu_ref, wd_ref, bgu_ref, bd_ref, o_ref):
    @pl.when(pl.program_id(0) < nu_ref[0])
    def _():
        x_lo, x_hi = _unpack_halves(x_ref[...])
        half = x_lo.shape[1]
        gu = (jnp.dot(x_lo.astype(BF16), wgu_ref[0, 0:half, :], preferred_element_type=F32)
              + jnp.dot(x_hi.astype(BF16), wgu_ref[0, half:2 * half, :], preferred_element_type=F32)
              + bgu_ref[0])
        acts = []
        for j in range(gu.shape[1] // (2 * LANES)):
            gate = jnp.minimum(gu[:, 2 * j * LANES:(2 * j + 1) * LANES], SWIGLU_LIMIT)
            up = jnp.clip(gu[:, (2 * j + 1) * LANES:(2 * j + 2) * LANES], -SWIGLU_LIMIT, SWIGLU_LIMIT)
            glu = gate * jax.nn.sigmoid(gate * SWIGLU_ALPHA)
            acts.append(((up + 1.0) * glu).astype(BF16))
        a = jnp.concatenate(acts, axis=1)
        y = jnp.dot(a, wd_ref[0], preferred_element_type=F32) + bd_ref[0]
        o_ref[...] = _pack_halves(y)

    @pl.when(pl.program_id(0) >= nu_ref[0])
    def _():
        o_ref[...] = jnp.zeros(o_ref.shape, o_ref.dtype)


def _expert_ffn(blk_exp, n_used, xs, wgu, wd, bgu, bd):
    p, dw = xs.shape
    f, d = wd.shape[1], wd.shape[2]
    nblk = p // MOE_ROWS
    wmap = lambda i, be, nu: (be[i], 0, 0)
    grid_spec = pltpu.PrefetchScalarGridSpec(
        num_scalar_prefetch=2,
        grid=(nblk,),
        in_specs=[pl.BlockSpec((MOE_ROWS, dw), lambda i, be, nu: (i, 0)),
                  pl.BlockSpec((1, d, 2 * f), wmap),
                  pl.BlockSpec((1, f, d), wmap),
                  pl.BlockSpec((1, 1, 2 * f), wmap),
                  pl.BlockSpec((1, 1, d), wmap)],
        out_specs=pl.BlockSpec((MOE_ROWS, dw), lambda i, be, nu: (i, 0)),
    )
    return pl.pallas_call(
        _ffn_kernel,
        out_shape=jax.ShapeDtypeStruct((p, dw), I32),
        grid_spec=grid_spec,
        compiler_params=_cparams(("arbitrary",)),
        name="expert_ffn",
    )(blk_exp, n_used, xs, wgu, wd, bgu, bd)


def _route_kernel(route_ref, dest_ref, cnt_ref, u_ref, carry_ref, pstart_ref, *, block_rows):
    ph, i = pl.program_id(0), pl.program_id(1)
    tm = route_ref.shape[0]

    @pl.when((ph == 0) & (i == 0))
    def _():
        r = lax.broadcasted_iota(I32, (tm, tm), 0)
        c = lax.broadcasted_iota(I32, (tm, tm), 1)
        u_ref[...] = jnp.where(r < c, 1.0, 0.0).astype(BF16)
        carry_ref[...] = jnp.zeros(carry_ref.shape, F32)

    @pl.when((ph == 1) & (i == 0))
    def _():
        counts = carry_ref[...]
        cnt_ref[...] = counts
        padded = jnp.ceil(counts * (1.0 / block_rows)) * block_rows
        r = lax.broadcasted_iota(I32, (LANES, LANES), 0)
        c = lax.broadcasted_iota(I32, (LANES, LANES), 1)
        lower = jnp.where(c < r, 1.0, 0.0).astype(F32)
        pstart_ref[...] = jnp.dot(lower, padded, preferred_element_type=F32, precision=lax.Precision.HIGHEST)
        carry_ref[...] = jnp.zeros(carry_ref.shape, F32)

    rt = route_ref[...].T
    sub = lax.broadcasted_iota(I32, (LANES, tm), 0)
    hits = [sub == rt[TOP_K_EXPERTS + k:TOP_K_EXPERTS + k + 1, :].astype(I32) for k in range(TOP_K_EXPERTS)]
    m = jnp.zeros((LANES, tm), F32)
    for hit in hits:
        m = m + jnp.where(hit, 1.0, 0.0)
    tile_counts = jnp.broadcast_to(jnp.sum(m, axis=1, keepdims=True), (LANES, LANES))

    @pl.when(ph == 0)
    def _():
        dest_ref[...] = jnp.zeros(dest_ref.shape, I32)

    @pl.when(ph == 1)
    def _():
        prefix = jnp.dot(m.astype(BF16), u_ref[...], preferred_element_type=F32)
        rank = prefix + (pstart_ref[:, 0:1] + carry_ref[:, 0:1])
        rows = [jnp.sum(jnp.where(hit, rank, 0.0), axis=0, keepdims=True) for hit in hits]
        rows.append(jnp.zeros((dest_ref.shape[0] - TOP_K_EXPERTS, tm), F32))
        dest_ref[...] = jnp.concatenate(rows, axis=0).astype(I32)

    carry_ref[...] = carry_ref[...] + tile_counts


def _route_rows(route, tm):
    n = route.shape[0]
    nt = n // tm
    kernel = functools.partial(_route_kernel, block_rows=MOE_ROWS)
    return pl.pallas_call(
        kernel,
        out_shape=(jax.ShapeDtypeStruct((8, n + tm), I32), jax.ShapeDtypeStruct((LANES, LANES), F32)),
        grid=(2, nt),
        in_specs=[pl.BlockSpec((tm, LANES), lambda ph, i: (i, 0))],
        out_specs=(pl.BlockSpec((8, tm), lambda ph, i: (0, ph * i + (1 - ph) * nt)),
                   pl.BlockSpec((LANES, LANES), lambda ph, i: (0, 0))),
        scratch_shapes=[pltpu.VMEM((tm, tm), BF16),
                        pltpu.VMEM((LANES, LANES), F32),
                        pltpu.VMEM((LANES, LANES), F32)],
        compiler_params=_cparams(("arbitrary", "arbitrary")),
        name="route_rows",
    )(route)


SC_WINDOW = 128
SC_WORKERS = 32


def _sc_mesh():
    return plsc.VectorSubcoreMesh(core_axis_name="c", subcore_axis_name="s")


def _sc_scatter_rows(src, dest, p):
    n, d = src.shape
    per = n // (SC_WINDOW * SC_WORKERS)

    @pl.kernel(out_type=jax.ShapeDtypeStruct((p, d), src.dtype), mesh=_sc_mesh(),
               scratch_types=[pltpu.VMEM((dest.shape[0], SC_WINDOW), I32), pltpu.VMEM((SC_WINDOW, d), src.dtype)])
    def scatter(src_hbm, idx_hbm, out_hbm, idx_vmem, buf):
        wid = lax.axis_index("c") * (SC_WORKERS // 2) + lax.axis_index("s")

        @pl.loop(0, per)
        def _(j):
            off = (wid * per + j) * SC_WINDOW
            pltpu.sync_copy(idx_hbm.at[:, pl.ds(off, SC_WINDOW)], idx_vmem)
            pltpu.sync_copy(src_hbm.at[pl.ds(off, SC_WINDOW), :], buf)
            for k in range(TOP_K_EXPERTS):
                pltpu.sync_copy(buf, out_hbm.at[idx_vmem.at[k]])

    return scatter(src, dest)


def _sc_gather_rows(src, dest):
    n = dest.shape[1]
    d = src.shape[1]
    per = n // (SC_WINDOW * SC_WORKERS)

    @pl.kernel(out_type=jax.ShapeDtypeStruct((TOP_K_EXPERTS * n, d), src.dtype), mesh=_sc_mesh(),
               scratch_types=[pltpu.VMEM((dest.shape[0], SC_WINDOW), I32), pltpu.VMEM((SC_WINDOW, d), src.dtype)])
    def gather(src_hbm, idx_hbm, out_hbm, idx_vmem, buf):
        wid = lax.axis_index("c") * (SC_WORKERS // 2) + lax.axis_index("s")

        @pl.loop(0, per)
        def _(j):
            off = (wid * per + j) * SC_WINDOW
            pltpu.sync_copy(idx_hbm.at[:, pl.ds(off, SC_WINDOW)], idx_vmem)
            for k in range(TOP_K_EXPERTS):
                pltpu.sync_copy(src_hbm.at[idx_vmem.at[k]], buf)
                pltpu.sync_copy(buf, out_hbm.at[pl.ds(k * n + off, SC_WINDOW), :])

    return gather(src, dest)


def _combine_kernel(h_ref, y_ref, route_ref, g_ref, o_ref):
    half = h_ref.shape[1] // 2
    h_lo, h_hi = h_ref[:, 0:half], h_ref[:, half:2 * half]
    route = route_ref[...]
    for k in range(TOP_K_EXPERTS):
        y_lo, y_hi = _unpack_halves(y_ref[k])
        gate = route[:, k:k + 1]
        h_lo = h_lo + gate * y_lo
        h_hi = h_hi + gate * y_hi
    ms = (jnp.sum(h_lo * h_lo, axis=-1, keepdims=True)
          + jnp.sum(h_hi * h_hi, axis=-1, keepdims=True)) * (1.0 / (2 * half))
    inv = lax.rsqrt(ms + EPS)
    o_ref[:, 0:half] = h_lo * inv * g_ref[:, 0:half]
    o_ref[:, half:2 * half] = h_hi * inv * g_ref[:, half:2 * half]


def _combine(h1, yg, route, g_final, tm):
    n, d = h1.shape
    return pl.pallas_call(
        _combine_kernel,
        out_shape=jax.ShapeDtypeStruct((n, d), F32),
        grid=(n // tm,),
        in_specs=[pl.BlockSpec((tm, d), lambda i: (i, 0)),
                  pl.BlockSpec((TOP_K_EXPERTS, tm, d // 2), lambda i: (0, i, 0)),
                  pl.BlockSpec((tm, LANES), lambda i: (i, 0)),
                  pl.BlockSpec((1, d), lambda i: (0, 0))],
        out_specs=pl.BlockSpec((tm, d), lambda i: (i, 0)),
        compiler_params=_cparams(("parallel",)),
        name="combine_norm",
    )(h1, yg, route, g_final)


def _t5_bucket(dist):
    n = jnp.maximum(dist, 0)
    max_exact = N_BUCKETS // 2
    nf = jnp.maximum(n, 1).astype(F32)
    large = max_exact + (jnp.log(nf / max_exact) / math.log(MAX_DISTANCE / max_exact)
                         * (N_BUCKETS - max_exact)).astype(I32)
    large = jnp.minimum(large, N_BUCKETS - 1)
    return jnp.where(n < max_exact, n, large)


def _bias_blocks(bias_tab):
    t = LANES
    assert MAX_DISTANCE <= LANES
    r = jnp.arange(t, dtype=I32)[:, None]
    c = jnp.arange(t, dtype=I32)[None, :]
    rel = (bias_tab - bias_tab[N_BUCKETS - 1][None, :]).astype(F32)
    tiles = []
    buckets = jnp.arange(N_BUCKETS, dtype=I32)[:, None, None]
    for delta in (0, t):
        dist = r - c + delta
        hit = _t5_bucket(dist)[None] == buckets
        b = jnp.sum(jnp.where(hit[:, None], rel[:, :, None, None], 0.0), axis=0)
        tiles.append(jnp.where((dist >= 0)[None], b * LOG2E, MASK_VALUE))
    return jnp.stack(tiles)


def _regroup_w_in(w_in):
    sizes = (1024, 1024, 1024, 1024, KV_LATENT, 1024, HEAD_DIM_IDX, N_HEADS_IDX, D_MODEL, D_MODEL)
    parts, off = [], 0
    for sz in sizes:
        parts.append(w_in[:, off:off + sz])
        off += sz
    dq, dk, dv, sq, ckv, iq, ik, iw, ga, gb = parts
    dq = dq * (HEAD_DIM_DIFF ** -0.5 * LOG2E)
    iw = iw * ((N_HEADS_IDX ** -0.5) * (HEAD_DIM_IDX ** -0.5))
    pad = jnp.zeros((w_in.shape[0], PROJ_WIDTH - COL_IW - N_HEADS_IDX), w_in.dtype)
    w = jnp.concatenate([dq, dk, dv, sq, iq, ga, gb, ckv, ik, ik, iw, pad], axis=1)
    return w.astype(BF16)


def _block_tables(counts, n_assign):
    e, bm = N_EXPERTS, MOE_ROWS
    padded = (counts + bm - 1) // bm * bm
    pends = jnp.cumsum(padded)
    nblk = -(-(n_assign + e * (bm - 1)) // bm)
    first_row = jnp.arange(nblk, dtype=I32) * bm
    blk_exp = jnp.minimum(jnp.sum((pends[None, :] <= first_row[:, None]).astype(I32), axis=1), e - 1)
    n_used = (pends[-1] // bm).astype(I32).reshape(1)
    return blk_exp, n_used, nblk


def kernel(x, norm_attn_g, w_in, rel_bias, lam_q1, lam_k1, lam_q2, lam_k2, diff_subln_g, kv_norm_g, w_uk, w_uv,
           w_branch_diff, w_branch_dsa, w_out, norm_ffn_g, w_router, b_router, w_gate_up, b_gate_up, w_down,
           b_down, norm_final_g):
    batch, seq, d = x.shape
    n = batch * seq
    assert norm_attn_g.shape[0] == 1, "single-layer kernel"
    assert seq % DIFF_BLOCK == 0 and seq % KEY_CHUNK == 0 and d == D_MODEL
    row_tile = math.gcd(n, 1024)

    x2 = x.reshape(n, d)
    proj = _inproj(x2, norm_attn_g[0].reshape(1, d), _regroup_w_in(w_in[0]), row_tile, 1280)

    lam_init = 0.8 - 0.6 * math.exp(-0.3 * 0)
    lam = (jnp.exp(jnp.sum(lam_q1[0].astype(F32) * lam_k1[0].astype(F32)))
           - jnp.exp(jnp.sum(lam_q2[0].astype(F32) * lam_k2[0].astype(F32))) + lam_init)
    y_diff = _diff_attention(proj, lam.reshape(1, 1).astype(F32), _bias_blocks(rel_bias[:, :N_HEADS_DIFF]),
                             diff_subln_g[0].reshape(1, -1).astype(F32), batch, seq, 1.0 - lam_init)

    y_dsa = _dsa_attention(proj, kv_norm_g[0].reshape(1, -1).astype(F32),
                           w_uk[0].transpose(0, 2, 1).astype(BF16), w_uv[0].astype(BF16),
                           _bias_blocks(rel_bias[:, N_HEADS_DIFF:]), batch, seq, min(TOPK_MAX, seq // 4))

    w_r = jnp.zeros((d, LANES), F32).at[:, :N_EXPERTS].set(w_router[0].astype(F32))
    b_r = jnp.full((1, LANES), MASK_VALUE, F32).at[0, :N_EXPERTS].set(b_router[0].astype(F32))
    h1, hn, route = _merge(x2, y_diff, y_dsa, proj, w_branch_diff[0].astype(BF16), w_branch_dsa[0].astype(BF16),
                           w_out[0].astype(BF16), norm_ffn_g[0].reshape(1, d).astype(F32),
                           w_r.astype(BF16), (w_r - w_r.astype(BF16).astype(F32)).astype(BF16), b_r,
                           math.gcd(n, 512))

    dest, counts = _route_rows(route, math.gcd(n, 1024))
    dest = dest[:, :n]
    blk_exp, n_used, nblk = _block_tables(counts[:N_EXPERTS, 0].astype(I32), n * TOP_K_EXPERTS)
    xs = _sc_scatter_rows(hn, dest, nblk * MOE_ROWS)
    e, f = N_EXPERTS, D_EXPERT
    b_gu = b_gate_up[0].astype(F32).reshape(e, f // LANES, LANES, 2).transpose(0, 1, 3, 2).reshape(e, 1, 2 * f)
    ys = _expert_ffn(blk_exp, n_used, xs, _regroup_gate_up(w_gate_up[0], 512), w_down[0].astype(BF16),
                     b_gu, b_down[0][:, None, :].astype(F32))
    yg = _sc_gather_rows(ys, dest).reshape(TOP_K_EXPERTS, n, d // 2)
    out = _combine(h1, yg, route, norm_final_g.reshape(1, d).astype(F32), math.gcd(n, 512))
    return out.reshape(batch, seq, d)
```

```python
import functools
import math

import jax
import jax.numpy as jnp
from jax import lax
from jax.experimental import pallas as pl
from jax.experimental.pallas import tpu as pltpu
from jax.experimental.pallas import tpu_sc as plsc

F32 = jnp.float32
BF16 = jnp.bfloat16
I32 = jnp.int32

D_MODEL = 1024
N_HEADS_DIFF = 8
HEAD_DIM_DIFF = 64
N_HEADS_DSA = 8
HEAD_DIM_DSA = 128
KV_LATENT = 256
N_HEADS_IDX = 16
HEAD_DIM_IDX = 64
TOPK_MAX = 256
N_BUCKETS = 32
MAX_DISTANCE = 128
N_EXPERTS = 32
TOP_K_EXPERTS = 4
D_EXPERT = 1024
SWIGLU_LIMIT = 7.0
SWIGLU_ALPHA = 1.702
EPS = 1e-6

LANES = 128
BF16_ROWS = 16
ROW_TILE = 1024
PROJ_COLS = 1280
TOKEN_TILE = 512
DIFF_BLOCK = 512
DIFF_HEADS_PER_STEP = 4
DSA_BLOCK = 256
KEY_CHUNK = 512
DSA_STREAMS = 2
MOE_ROWS = 512
PROJ_WIDTH = 7680
VMEM_LIMIT = 56 * 1024 * 1024

COL_DQ, COL_DK, COL_DV, COL_SQ, COL_IQ, COL_GA, COL_GB = (i * 1024 for i in range(7))
COL_CKV = 7168
COL_IK = 7424
COL_IW = 7552

LOG2E = math.log2(math.e)
MASK_VALUE = -1e30
M_INIT = -1e29
INT_MIN = -2 ** 31


def _cparams(sem):
    return pltpu.CompilerParams(dimension_semantics=sem, vmem_limit_bytes=VMEM_LIMIT)


def _inproj_kernel(x_ref, g_ref, w_ref, o_ref, xn_ref):
    @pl.when(pl.program_id(1) == 0)
    def _():
        x = x_ref[...]
        ms = jnp.mean(x * x, axis=-1, keepdims=True)
        xn_ref[...] = (x * lax.rsqrt(ms + EPS) * g_ref[...]).astype(BF16)

    o_ref[...] = jnp.dot(xn_ref[...], w_ref[...], preferred_element_type=F32).astype(o_ref.dtype)


def _inproj(x2, g, w, tm, tn):
    n, d = x2.shape
    width = w.shape[1]
    return pl.pallas_call(
        _inproj_kernel,
        out_shape=jax.ShapeDtypeStruct((n, width), BF16),
        grid=(n // tm, width // tn),
        in_specs=[pl.BlockSpec((tm, d), lambda i, j: (i, 0)),
                  pl.BlockSpec((1, d), lambda i, j: (0, 0)),
                  pl.BlockSpec((d, tn), lambda i, j: (0, j))],
        out_specs=pl.BlockSpec((tm, tn), lambda i, j: (i, j)),
        scratch_shapes=[pltpu.VMEM((tm, d), BF16)],
        compiler_params=_cparams(("parallel", "arbitrary")),
        name="inproj",
    )(x2, g, w)


def _lane_chunk_max(s):
    smax = s[:, 0:LANES]
    for j in range(1, s.shape[1] // LANES):
        smax = jnp.maximum(smax, s[:, j * LANES:(j + 1) * LANES])
    return smax


def _softmax_step(s_ref, tk, v, m_ref, l_ref, acc_ref, smax=None, split=True):
    nl = tk // LANES
    if smax is None:
        smax = _lane_chunk_max(s_ref[:, 0:tk])
    m_prev = m_ref[...]
    m_new = jnp.maximum(m_prev, jnp.max(smax, axis=-1, keepdims=True))
    alpha = jnp.exp2(m_prev - m_new)
    psum = None
    ps = []
    for j in range(nl):
        pj = jnp.exp2(s_ref[:, j * LANES:(j + 1) * LANES] - m_new)
        psum = pj if psum is None else psum + pj
        ps.append(pj.astype(BF16))
    l_ref[...] = alpha * l_ref[...] + psum
    pv = _dot_split(jnp.concatenate(ps, axis=1), v, split)
    e = acc_ref.shape[1]
    a = alpha if e == LANES else jnp.concatenate([alpha] * (e // LANES), axis=1)
    acc_ref[...] = a * acc_ref[...] + pv
    m_ref[...] = m_new


def _softmax_init(m_ref, l_ref, acc_ref):
    m_ref[...] = jnp.full(m_ref.shape, M_INIT, F32)
    l_ref[...] = jnp.zeros(l_ref.shape, F32)
    acc_ref[...] = jnp.zeros(acc_ref.shape, F32)


def _softmax_result(l_ref, acc_ref):
    return acc_ref[...] * (1.0 / jnp.sum(l_ref[...], axis=-1, keepdims=True))


def _near_bias(s_ref, d0, d1, delta, groups, t, tk):
    for g in range(groups):
        for rb in range(t // LANES):
            for cb in range(tk // LANES):
                bd = delta + rb - cb
                rows = slice(g * t + rb * LANES, g * t + (rb + 1) * LANES)
                cols = slice(cb * LANES, (cb + 1) * LANES)
                if bd == 0:
                    s_ref[rows, cols] = s_ref[rows, cols] + d0(g)
                elif bd == 1:
                    s_ref[rows, cols] = s_ref[rows, cols] + d1(g)
                elif bd < 0:
                    s_ref[rows, cols] = jnp.full((LANES, LANES), MASK_VALUE, F32)


def _dot_nt(a, b, split=True):
    dn = (((1,), (1,)), ((), ()))
    if not split:
        return lax.dot_general(a, b, dn, preferred_element_type=F32)
    h = a.shape[0] // 2
    return jnp.concatenate([lax.dot_general(a[:h], b, dn, preferred_element_type=F32),
                            lax.dot_general(a[h:], b, dn, preferred_element_type=F32)], axis=0)


def _dot_split(a, b, split=True):
    if not split:
        return jnp.dot(a, b, preferred_element_type=F32)
    h = a.shape[0] // 2
    return jnp.concatenate([jnp.dot(a[:h], b, preferred_element_type=F32),
                            jnp.dot(a[h:], b, preferred_element_type=F32)], axis=0)


def _diff_kernel(lam_ref, q_ref, k_ref, v_ref, bias_ref, g_ref, o_ref, q2_ref, s_ref, m_ref, l_ref, acc_ref,
                 *, out_scale, heads):
    t = q_ref.shape[0]
    e = 2 * HEAD_DIM_DIFF
    qi = pl.program_id(2)
    cols = [slice(hh * e, (hh + 1) * e) for hh in range(heads)]

    lane = lax.broadcasted_iota(I32, (t, e), 1)
    for hh in range(heads):
        q = q_ref[:, cols[hh]]
        zero = jnp.zeros_like(q)
        q2_ref[hh, 0:t, :] = jnp.where(lane < HEAD_DIM_DIFF, q, zero)
        q2_ref[hh, t:2 * t, :] = jnp.where(lane >= HEAD_DIM_DIFF, q, zero)
        _softmax_init(m_ref.at[hh], l_ref.at[hh], acc_ref.at[hh])

    def chunk(kc, delta):
        off = pl.multiple_of(kc * t, t)
        for hh in range(heads):
            s = _dot_nt(q2_ref[hh], k_ref[pl.ds(off, t), cols[hh]])
            s_ref[hh] = s
            smax = None
            if delta is None:
                smax = _lane_chunk_max(s)
            else:
                _near_bias(s_ref.at[hh], lambda g: bias_ref[0, hh], lambda g: bias_ref[1, hh], delta, 2, t, t)
            _softmax_step(s_ref.at[hh], t, v_ref[pl.ds(off, t), cols[hh]], m_ref.at[hh], l_ref.at[hh],
                          acc_ref.at[hh], smax)

    def far_body(kc, carry):
        chunk(kc, None)
        return carry

    lax.fori_loop(0, jnp.maximum(qi - 1, 0), far_body, 0)

    @pl.when(qi >= 1)
    def _():
        chunk(qi - 1, t // LANES)

    chunk(qi, 0)

    for hh in range(heads):
        o = _softmax_result(l_ref.at[hh], acc_ref.at[hh])
        o = o[0:t, :] - lam_ref[0, 0] * o[t:2 * t, :]
        ms = jnp.mean(o * o, axis=-1, keepdims=True)
        o_ref[:, cols[hh]] = (o * lax.rsqrt(ms + EPS) * g_ref[...] * out_scale).astype(o_ref.dtype)


def _diff_attention(proj, lam, bias_blocks, subln_g, batch, seq, out_scale):
    t = DIFF_BLOCK
    nq = seq // t
    h = N_HEADS_DIFF
    e = 2 * HEAD_DIM_DIFF
    hp = DIFF_HEADS_PER_STEP
    w = hp * e
    kernel = functools.partial(_diff_kernel, out_scale=out_scale, heads=hp)
    return pl.pallas_call(
        kernel,
        out_shape=jax.ShapeDtypeStruct((batch * seq, h * e), BF16),
        grid=(batch, h // hp, nq),
        in_specs=[pl.BlockSpec(memory_space=pltpu.SMEM),
                  pl.BlockSpec((t, w), lambda b, hh, qi: (b * nq + qi, COL_DQ // w + hh)),
                  pl.BlockSpec((seq, w), lambda b, hh, qi: (b, COL_DK // w + hh)),
                  pl.BlockSpec((seq, w), lambda b, hh, qi: (b, COL_DV // w + hh)),
                  pl.BlockSpec((2, hp, LANES, LANES), lambda b, hh, qi: (0, hh, 0, 0)),
                  pl.BlockSpec((1, e), lambda b, hh, qi: (0, 0))],
        out_specs=pl.BlockSpec((t, w), lambda b, hh, qi: (b * nq + qi, hh)),
        scratch_shapes=[pltpu.VMEM((hp, 2 * t, e), BF16),
                        pltpu.VMEM((hp, 2 * t, t), F32),
                        pltpu.VMEM((hp, 2 * t, LANES), F32),
                        pltpu.VMEM((hp, 2 * t, LANES), F32),
                        pltpu.VMEM((hp, 2 * t, e), F32)],
        compiler_params=_cparams(("parallel", "parallel", "arbitrary")),
        name="diff_attn",
    )(lam, proj, proj, proj, bias_blocks, subln_g)


def _sortable_key(x):
    bits = lax.bitcast_convert_type(x, I32)
    return bits ^ ((bits >> 31) & jnp.int32(0x7FFFFFFF))


def _dsa_kernel(iq_ref, sq_ref, iw_ref, ik_ref, ckv_ref, kvg_ref, wuk_ref, wuv_ref, bias_ref, o_ref,
                c_ref, key_ref, keyt_ref, keyh_ref, qi_ref, wb_ref, ql_ref, s_ref, m_ref, l_ref, acc_ref, *, topk, scale):
    t = iq_ref.shape[0]
    tk = KEY_CHUNK
    qi = pl.program_id(1)
    n_chunks = qi + 1
    hi, hb = N_HEADS_IDX, N_HEADS_DSA

    @pl.when(qi == 0)
    def _():
        ckv = ckv_ref[...].astype(F32)
        ms = jnp.mean(ckv * ckv, axis=-1, keepdims=True)
        c_ref[...] = (ckv * lax.rsqrt(ms + EPS) * kvg_ref[...]).astype(BF16)

    rg = qi_ref.shape[2]
    lane = lax.broadcasted_iota(I32, (t, LANES), 1)
    for h in range(hi):
        blk = iq_ref[:, (h // 2) * LANES:(h // 2 + 1) * LANES]
        keep = (lane < HEAD_DIM_IDX) if h % 2 == 0 else (lane >= HEAD_DIM_IDX)
        qi_ref[:, h, :, :] = jnp.where(keep, blk, jnp.zeros_like(blk)).reshape(t // rg, rg, LANES)
        wb_ref[:, h, :, :] = jnp.broadcast_to(iw_ref[:, h:h + 1].astype(F32), (t, LANES)).reshape(t // rg, rg, LANES)

    def score_chunk(kc, diag):
        off = pl.multiple_of(kc * t, t)
        d = _dot_nt(qi_ref[...].reshape(hi * t, LANES), ik_ref[pl.ds(off, t), :]).reshape(t // rg, hi, rg, t)
        w = jnp.concatenate([wb_ref[...]] * (t // LANES), axis=-1)
        sc = jnp.sum(jnp.maximum(d, 0.0) * w, axis=1).reshape(t, t)
        key = _sortable_key(sc + 0.0)
        if diag:
            row = lax.broadcasted_iota(I32, (t, t), 0)
            col = lax.broadcasted_iota(I32, (t, t), 1)
            key = jnp.where(col <= row, key, jnp.int32(INT_MIN))
        key_ref[:, pl.ds(off, t)] = key
        key_t = key.T
        keyt_ref[pl.ds(off, t), :] = key_t
        keyh_ref[pl.ds(off, t), :] = (key_t >> 16).astype(jnp.int16)

    def score_body(kc, carry):
        score_chunk(kc, False)
        return carry

    lax.fori_loop(0, qi, score_body, 0)
    score_chunk(qi, True)

    @pl.when(qi % 2 == 0)
    def _():
        off = pl.multiple_of((qi + 1) * t, t)
        key_ref[:, pl.ds(off, t)] = jnp.full((t, t), INT_MIN, I32)
        keyt_ref[pl.ds(off, t), :] = jnp.full((t, t), INT_MIN, I32)
        keyh_ref[pl.ds(off, t), :] = jnp.full((t, t), INT_MIN >> 16, jnp.int16)

    n_steps = (qi + 2) // 2

    def count_ge(cand_s):
        def body(kc, cnt):
            off = pl.multiple_of(kc * tk, tk)
            k = keyt_ref[pl.ds(off, tk), :].reshape(tk // 8, 8, t)
            return cnt + jnp.sum(jnp.where(k >= cand_s[None], 1, 0), axis=0)

        cnt = lax.fori_loop(0, n_steps, body, jnp.zeros((8, t), I32))
        return jnp.sum(cnt, axis=0, keepdims=True)

    def count_packed(c_row):
        c16 = jnp.broadcast_to(c_row.astype(jnp.int16), (16, t))

        def body(kc, cnt):
            off = pl.multiple_of(kc * tk, tk)
            k = keyh_ref[pl.ds(off, tk), :].reshape(tk // 16, 16, t)
            hit = jnp.where(k >= c16[None], jnp.int16(1), jnp.int16(0))
            for j in range(tk // 16):
                cnt = cnt + hit[j]
            return cnt

        cnt = lax.fori_loop(0, n_steps, body, jnp.zeros((16, t), jnp.int16))
        return jnp.sum(cnt.astype(I32), axis=0, keepdims=True)

    def make_bit_body(count):
        def bit_body(i, carry):
            cur, n_ge = carry
            bit = lax.shift_left(jnp.int32(1), 31 - i)
            cand = cur | bit
            total = count(cand ^ jnp.int32(INT_MIN))
            accept = total >= topk
            return jnp.where(accept, cand, cur), jnp.where(accept, total, n_ge)
        return bit_body

    zeros8 = jnp.zeros((8, t), I32)
    low_bias = jnp.int32(1 << 15)
    cur, n_ge = lax.fori_loop(
        0, 16, make_bit_body(lambda cand_s: count_packed(cand_s[0:1, :] >> 16)), (zeros8, zeros8))
    high = (cur ^ jnp.int32(INT_MIN)) >> 16
    above_high = jnp.where(high[0:1, :] == 2 ** 15 - 1, 0, count_packed(jnp.minimum(high[0:1, :], 2 ** 15 - 2) + 1))

    def repack_body(kc, carry):
        off = pl.multiple_of(kc * tk, tk)
        k = keyt_ref[pl.ds(off, tk), :].reshape(tk // 8, 8, t)
        low = jnp.where((k >> 16) == high[None], (k & jnp.int32(0xFFFF)) - low_bias, -low_bias)
        keyh_ref[pl.ds(off, tk), :] = low.reshape(tk, t).astype(jnp.int16)
        return carry

    lax.fori_loop(0, n_steps, repack_body, 0)
    cur, n_ge = lax.fori_loop(
        16, 32, make_bit_body(lambda cand_s: above_high + count_packed((cand_s[0:1, :] & jnp.int32(0xFFFF)) - low_bias)),
        (cur, n_ge))
    thr = jnp.maximum(cur ^ jnp.int32(INT_MIN), jnp.int32(INT_MIN + 1))

    def lanes_to_rows(v):
        b = jnp.broadcast_to(v[0:1, :], (LANES, t)).T
        return jnp.concatenate([b] * (tk // LANES), axis=1)

    thr_w = lanes_to_rows(thr)
    ties = jnp.max(n_ge) > topk

    @pl.when(jnp.logical_not(ties))
    def _():
        def mask_body(kc, carry):
            off = pl.multiple_of(kc * tk, tk)
            am = jnp.where(key_ref[:, pl.ds(off, tk)] >= thr_w, 0.0, MASK_VALUE).astype(F32)
            key_ref[:, pl.ds(off, tk)] = lax.bitcast_convert_type(am, I32)
            return carry

        lax.fori_loop(0, n_steps, mask_body, 0)

    @pl.when(ties)
    def _():
        above = jnp.where(thr == jnp.int32(2 ** 31 - 1), 0, count_ge(jnp.minimum(thr, jnp.int32(2 ** 31 - 2)) + 1))
        need_w = lanes_to_rows(topk - above).astype(F32)
        r = lax.broadcasted_iota(I32, (tk, tk), 0)
        c = lax.broadcasted_iota(I32, (tk, tk), 1)
        before = jnp.where(r < c, 1.0, 0.0).astype(BF16)
        ones_w = jnp.ones((tk, tk), BF16)

        def tie_body(kc, seen):
            off = pl.multiple_of(kc * tk, tk)
            k = key_ref[:, pl.ds(off, tk)]
            eq = k == thr_w
            eq_b = jnp.where(eq, 1.0, 0.0).astype(BF16)
            rank = seen + jnp.dot(eq_b, before, preferred_element_type=F32)
            keep = (k > thr_w) | (eq & (rank < need_w))
            key_ref[:, pl.ds(off, tk)] = lax.bitcast_convert_type(jnp.where(keep, 0.0, MASK_VALUE).astype(F32), I32)
            return seen + jnp.dot(eq_b, ones_w, preferred_element_type=F32)

        lax.fori_loop(0, n_steps, tie_body, jnp.zeros((t, tk), F32))

    streams = ql_ref.shape[0]
    split = streams == 1
    hs = hb // streams
    for h in range(hb):
        qh = sq_ref[:, h * HEAD_DIM_DSA:(h + 1) * HEAD_DIM_DSA]
        ql = jnp.dot(qh, wuk_ref[h], preferred_element_type=F32) * scale
        ql_ref[h // hs, (h % hs) * t:(h % hs + 1) * t, :] = ql.astype(BF16)
    for g in range(streams):
        _softmax_init(m_ref.at[g], l_ref.at[g], acc_ref.at[g])

    def chunk(kc, width, delta):
        off = pl.multiple_of(kc * tk, tk)
        c = c_ref[pl.ds(off, width), :]
        am = lax.bitcast_convert_type(key_ref[:, pl.ds(off, width)], F32)
        for g in range(streams):
            s = (_dot_nt(ql_ref[g], c, split).reshape(hs, t, width) + am[None]).reshape(hs * t, width)
            s_ref[g, :, 0:width] = s
            smax = None
            if delta is None:
                smax = _lane_chunk_max(s)
            else:
                _near_bias(s_ref.at[g], lambda j: bias_ref[0, g * hs + j], lambda j: bias_ref[1, g * hs + j],
                           delta, hs, t, width)
            _softmax_step(s_ref.at[g], width, c, m_ref.at[g], l_ref.at[g], acc_ref.at[g], smax, split)

    def far_body(kc, carry):
        chunk(kc, tk, None)
        return carry

    lax.fori_loop(0, jnp.maximum((qi - 1) // 2, 0), far_body, 0)
    half = qi // 2

    @pl.when(qi % 2 == 1)
    def _():
        chunk(half, tk, t // LANES)

    @pl.when(qi % 2 == 0)
    def _():
        @pl.when(half >= 1)
        def _():
            chunk(half - 1, tk, tk // LANES)
        chunk(half, t, 0)

    for h in range(hb):
        g, j = h // hs, h % hs
        inv_l = 1.0 / jnp.sum(l_ref[g, j * t:(j + 1) * t, :], axis=-1, keepdims=True)
        ol = (acc_ref[g, j * t:(j + 1) * t, :] * inv_l).astype(BF16)
        o = jnp.dot(ol, wuv_ref[h], preferred_element_type=F32)
        o_ref[:, h * HEAD_DIM_DSA:(h + 1) * HEAD_DIM_DSA] = o.astype(o_ref.dtype)


def _dsa_attention(proj, kv_g, w_ukt, w_uv, bias_blocks, batch, seq, topk):
    t = DSA_BLOCK
    nq = seq // t
    hb, hi = N_HEADS_DSA, N_HEADS_IDX
    width = hb * HEAD_DIM_DSA
    ns = DSA_STREAMS
    rs = hb // ns * t
    kernel = functools.partial(_dsa_kernel, topk=topk, scale=HEAD_DIM_DSA ** -0.5 * LOG2E)
    return pl.pallas_call(
        kernel,
        out_shape=jax.ShapeDtypeStruct((batch * seq, width), BF16),
        grid=(batch, nq),
        in_specs=[pl.BlockSpec((t, 1024), lambda b, qi: (b * nq + qi, COL_IQ // 1024)),
                  pl.BlockSpec((t, 1024), lambda b, qi: (b * nq + qi, COL_SQ // 1024)),
                  pl.BlockSpec((t, LANES), lambda b, qi: (b * nq + qi, COL_IW // LANES)),
                  pl.BlockSpec((seq, LANES), lambda b, qi: (b, COL_IK // LANES)),
                  pl.BlockSpec((seq, KV_LATENT), lambda b, qi: (b, COL_CKV // KV_LATENT)),
                  pl.BlockSpec((1, KV_LATENT), lambda b, qi: (0, 0)),
                  pl.BlockSpec((hb, HEAD_DIM_DSA, KV_LATENT), lambda b, qi: (0, 0, 0)),
                  pl.BlockSpec((hb, KV_LATENT, HEAD_DIM_DSA), lambda b, qi: (0, 0, 0)),
                  pl.BlockSpec((2, hb, LANES, LANES), lambda b, qi: (0, 0, 0, 0))],
        out_specs=pl.BlockSpec((t, width), lambda b, qi: (b * nq + qi, 0)),
        scratch_shapes=[pltpu.VMEM((seq, KV_LATENT), BF16),
                        pltpu.VMEM((t, seq), I32),
                        pltpu.VMEM((seq, t), I32),
                        pltpu.VMEM((seq, t), jnp.int16),
                        pltpu.VMEM((t // BF16_ROWS, hi, BF16_ROWS, LANES), BF16),
                        pltpu.VMEM((t // BF16_ROWS, hi, BF16_ROWS, LANES), F32),
                        pltpu.VMEM((ns, rs, KV_LATENT), BF16),
                        pltpu.VMEM((ns, rs, KEY_CHUNK), F32),
                        pltpu.VMEM((ns, rs, LANES), F32),
                        pltpu.VMEM((ns, rs, LANES), F32),
                        pltpu.VMEM((ns, rs, KV_LATENT), F32)],
        compiler_params=_cparams(("parallel", "arbitrary")),
        name="dsa_attn",
    )(proj, proj, proj, proj, proj, kv_g, w_ukt, w_uv, bias_blocks)


def _pack_halves(x):
    c = x.shape[1] // 2
    lo = lax.bitcast_convert_type(x[:, :c].astype(BF16).astype(F32), I32)
    hi = lax.bitcast_convert_type(x[:, c:].astype(BF16).astype(F32), I32)
    return lax.shift_right_logical(lo, 16) | (hi & jnp.int32(-65536))


def _unpack_halves(w):
    lo = lax.bitcast_convert_type(lax.shift_left(w, 16), F32)
    hi = lax.bitcast_convert_type(w & jnp.int32(-65536), F32)
    return lo, hi


def _merge_kernel(x_ref, yd_ref, ys_ref, ga_ref, gb_ref, wd_ref, ws_ref, wo_ref, g_ref, wrh_ref, wrl_ref, br_ref,
                  h_ref, hn_ref, route_ref):
    bd = _dot_split(yd_ref[...], wd_ref[...])
    bs = _dot_split(ys_ref[...], ws_ref[...])
    merged = (jax.nn.sigmoid(ga_ref[...].astype(F32)) * bd + jax.nn.sigmoid(gb_ref[...].astype(F32)) * bs)
    h = x_ref[...] + _dot_split(merged.astype(BF16), wo_ref[...])
    h_ref[...] = h
    ms = jnp.mean(h * h, axis=-1, keepdims=True)
    hn = h * lax.rsqrt(ms + EPS) * g_ref[...]
    hn_ref[...] = _pack_halves(hn)

    hn_hi = hn.astype(BF16)
    hn_lo = (hn - hn_hi.astype(F32)).astype(BF16)
    logits = (_dot_split(hn_hi, wrh_ref[...]) + _dot_split(hn_lo, wrh_ref[...])
              + _dot_split(hn_hi, wrl_ref[...]))
    logits = logits + br_ref[...]
    lane = lax.broadcasted_iota(I32, logits.shape, 1)
    vals, ids = [], []
    for _ in range(TOP_K_EXPERTS):
        mx = jnp.max(logits, axis=-1, keepdims=True)
        ix = jnp.min(jnp.where(logits == mx, lane, LANES), axis=-1, keepdims=True)
        vals.append(mx)
        ids.append(ix)
        logits = jnp.where(lane == ix, -jnp.inf, logits)
    es = [jnp.exp(v - vals[0]) for v in vals]
    inv = 1.0 / (es[0] + es[1] + es[2] + es[3])
    route = jnp.zeros(logits.shape, F32)
    for k in range(TOP_K_EXPERTS):
        route = jnp.where(lane == k, es[k] * inv, route)
        route = jnp.where(lane == TOP_K_EXPERTS + k, ids[k].astype(F32), route)
    route_ref[...] = route


def _merge(x2, y_diff, y_dsa, proj, w_bd, w_bs, w_out, g_ffn, w_router_hi, w_router_lo, b_router, tm):
    n, d = x2.shape
    row = lambda i: (i, 0)
    const = lambda i: (0, 0)
    return pl.pallas_call(
        _merge_kernel,
        out_shape=(jax.ShapeDtypeStruct((n, d), F32),
                   jax.ShapeDtypeStruct((n, d // 2), I32),
                   jax.ShapeDtypeStruct((n, LANES), F32)),
        grid=(n // tm,),
        in_specs=[pl.BlockSpec((tm, d), row),
                  pl.BlockSpec((tm, d), row),
                  pl.BlockSpec((tm, d), row),
                  pl.BlockSpec((tm, d), lambda i: (i, COL_GA // 1024)),
                  pl.BlockSpec((tm, d), lambda i: (i, COL_GB // 1024)),
                  pl.BlockSpec((d, d), const),
                  pl.BlockSpec((d, d), const),
                  pl.BlockSpec((d, d), const),
                  pl.BlockSpec((1, d), const),
                  pl.BlockSpec((d, LANES), const),
                  pl.BlockSpec((d, LANES), const),
                  pl.BlockSpec((1, LANES), const)],
        out_specs=(pl.BlockSpec((tm, d), row),
                   pl.BlockSpec((tm, d // 2), row),
                   pl.BlockSpec((tm, LANES), row)),
        compiler_params=_cparams(("parallel",)),
        name="merge_router",
    )(x2, y_diff, y_dsa, proj, proj, w_bd, w_bs, w_out, g_ffn, w_router_hi, w_router_lo, b_router)


def _regroup_kernel(w_ref, p_ref, o_ref):
    pw = p_ref.shape[0]
    for j in range(w_ref.shape[2] // pw):
        w = w_ref[0, :, j * pw:(j + 1) * pw].astype(BF16)
        o_ref[0, :, j * pw:(j + 1) * pw] = jnp.dot(w, p_ref[...], preferred_element_type=F32).astype(BF16)


def _regroup_gate_up(w_gu, rows):
    e, d, f2 = w_gu.shape
    pw = 2 * LANES
    src = jnp.arange(pw, dtype=I32)
    dst = (src % 2) * LANES + src // 2
    perm = (dst[:, None] == jnp.arange(pw, dtype=I32)[None, :]).astype(BF16)
    return pl.pallas_call(
        _regroup_kernel,
        out_shape=jax.ShapeDtypeStruct((e, d, f2), BF16),
        grid=(e, d // rows),
        in_specs=[pl.BlockSpec((1, rows, f2), lambda i, j: (i, j, 0)),
                  pl.BlockSpec((pw, pw), lambda i, j: (0, 0))],
        out_specs=pl.BlockSpec((1, rows, f2), lambda i, j: (i, j, 0)),
        compiler_params=_cparams(("parallel", "parallel")),
        name="regroup_gate_up",
    )(w_gu, perm)


def _ffn_kernel(be_ref, nu_ref, x_ref, wgu_ref, wd_ref, bgu_ref, bd_ref, o_ref):
    @pl.when(pl.program_id(0) < nu_ref[0])
    def _():
        x_lo, x_hi = _unpack_halves(x_ref[...])
        half = x_lo.shape[1]
        gu = (jnp.dot(x_lo.astype(BF16), wgu_ref[0, 0:half, :], preferred_element_type=F32)
              + jnp.dot(x_hi.astype(BF16), wgu_ref[0, half:2 * half, :], preferred_element_type=F32)
              + bgu_ref[0])
        acts = []
        for j in range(gu.shape[1] // (2 * LANES)):
            gate = jnp.minimum(gu[:, 2 * j * LANES:(2 * j + 1) * LANES], SWIGLU_LIMIT)
            up = jnp.clip(gu[:, (2 * j + 1) * LANES:(2 * j + 2) * LANES], -SWIGLU_LIMIT, SWIGLU_LIMIT)
            glu = gate * jax.nn.sigmoid(gate * SWIGLU_ALPHA)
            acts.append(((up + 1.0) * glu).astype(BF16))
        a = jnp.concatenate(acts, axis=1)
        y = jnp.dot(a, wd_ref[0], preferred_element_type=F32) + bd_ref[0]
        o_ref[...] = _pack_halves(y)

    @pl.when(pl.program_id(0) >= nu_ref[0])
    def _():
        o_ref[...] = jnp.zeros(o_ref.shape, o_ref.dtype)


def _expert_ffn(blk_exp, n_used, xs, wgu, wd, bgu, bd):
    p, dw = xs.shape
    f, d = wd.shape[1], wd.shape[2]
    nblk = p // MOE_ROWS
    wmap = lambda i, be, nu: (be[i], 0, 0)
    grid_spec = pltpu.PrefetchScalarGridSpec(
        num_scalar_prefetch=2,
        grid=(nblk,),
        in_specs=[pl.BlockSpec((MOE_ROWS, dw), lambda i, be, nu: (i, 0)),
                  pl.BlockSpec((1, d, 2 * f), wmap),
                  pl.BlockSpec((1, f, d), wmap),
                  pl.BlockSpec((1, 1, 2 * f), wmap),
                  pl.BlockSpec((1, 1, d), wmap)],
        out_specs=pl.BlockSpec((MOE_ROWS, dw), lambda i, be, nu: (i, 0)),
    )
    return pl.pallas_call(
        _ffn_kernel,
        out_shape=jax.ShapeDtypeStruct((p, dw), I32),
        grid_spec=grid_spec,
        compiler_params=_cparams(("arbitrary",)),
        name="expert_ffn",
    )(blk_exp, n_used, xs, wgu, wd, bgu, bd)


def _route_kernel(route_ref, dest_ref, cnt_ref, u_ref, carry_ref, pstart_ref, *, block_rows):
    ph, i = pl.program_id(0), pl.program_id(1)
    tm = route_ref.shape[0]

    @pl.when((ph == 0) & (i == 0))
    def _():
        r = lax.broadcasted_iota(I32, (tm, tm), 0)
        c = lax.broadcasted_iota(I32, (tm, tm), 1)
        u_ref[...] = jnp.where(r < c, 1.0, 0.0).astype(BF16)
        carry_ref[...] = jnp.zeros(carry_ref.shape, F32)

    @pl.when((ph == 1) & (i == 0))
    def _():
        counts = carry_ref[...]
        cnt_ref[...] = counts
        padded = jnp.ceil(counts * (1.0 / block_rows)) * block_rows
        r = lax.broadcasted_iota(I32, (LANES, LANES), 0)
        c = lax.broadcasted_iota(I32, (LANES, LANES), 1)
        lower = jnp.where(c < r, 1.0, 0.0).astype(F32)
        pstart_ref[...] = jnp.dot(lower, padded, preferred_element_type=F32, precision=lax.Precision.HIGHEST)
        carry_ref[...] = jnp.zeros(carry_ref.shape, F32)

    rt = route_ref[...].T
    sub = lax.broadcasted_iota(I32, (LANES, tm), 0)
    hits = [sub == rt[TOP_K_EXPERTS + k:TOP_K_EXPERTS + k + 1, :].astype(I32) for k in range(TOP_K_EXPERTS)]
    m = jnp.zeros((LANES, tm), F32)
    for hit in hits:
        m = m + jnp.where(hit, 1.0, 0.0)
    tile_counts = jnp.broadcast_to(jnp.sum(m, axis=1, keepdims=True), (LANES, LANES))

    @pl.when(ph == 0)
    def _():
        dest_ref[...] = jnp.zeros(dest_ref.shape, I32)

    @pl.when(ph == 1)
    def _():
        prefix = jnp.dot(m.astype(BF16), u_ref[...], preferred_element_type=F32)
        rank = prefix + (pstart_ref[:, 0:1] + carry_ref[:, 0:1])
        rows = [jnp.sum(jnp.where(hit, rank, 0.0), axis=0, keepdims=True) for hit in hits]
        rows.append(jnp.zeros((dest_ref.shape[0] - TOP_K_EXPERTS, tm), F32))
        dest_ref[...] = jnp.concatenate(rows, axis=0).astype(I32)

    carry_ref[...] = carry_ref[...] + tile_counts


def _route_rows(route, tm):
    n = route.shape[0]
    nt = n // tm
    kernel = functools.partial(_route_kernel, block_rows=MOE_ROWS)
    return pl.pallas_call(
        kernel,
        out_shape=(jax.ShapeDtypeStruct((8, n + tm), I32), jax.ShapeDtypeStruct((LANES, LANES), F32)),
        grid=(2, nt),
        in_specs=[pl.BlockSpec((tm, LANES), lambda ph, i: (i, 0))],
        out_specs=(pl.BlockSpec((8, tm), lambda ph, i: (0, ph * i + (1 - ph) * nt)),
                   pl.BlockSpec((LANES, LANES), lambda ph, i: (0, 0))),
        scratch_shapes=[pltpu.VMEM((tm, tm), BF16),
                        pltpu.VMEM((LANES, LANES), F32),
                        pltpu.VMEM((LANES, LANES), F32)],
        compiler_params=_cparams(("arbitrary", "arbitrary")),
        name="route_rows",
    )(route)


SC_WINDOW = 128
SC_WORKERS = 32


def _sc_mesh():
    return plsc.VectorSubcoreMesh(core_axis_name="c", subcore_axis_name="s")


def _sc_scatter_rows(src, dest, p):
    n, d = src.shape
    assert n % (SC_WINDOW * SC_WORKERS) == 0, "token count must split evenly over the vector subcores"
    per = n // (SC_WINDOW * SC_WORKERS)

    @pl.kernel(out_type=jax.ShapeDtypeStruct((p, d), src.dtype), mesh=_sc_mesh(),
               scratch_types=[pltpu.VMEM((dest.shape[0], SC_WINDOW), I32), pltpu.VMEM((SC_WINDOW, d), src.dtype)])
    def scatter(src_hbm, idx_hbm, out_hbm, idx_vmem, buf):
        wid = lax.axis_index("c") * (SC_WORKERS // 2) + lax.axis_index("s")

        @pl.loop(0, per)
        def _(j):
            off = (wid * per + j) * SC_WINDOW
            pltpu.sync_copy(idx_hbm.at[:, pl.ds(off, SC_WINDOW)], idx_vmem)
            pltpu.sync_copy(src_hbm.at[pl.ds(off, SC_WINDOW), :], buf)
            for k in range(TOP_K_EXPERTS):
                pltpu.sync_copy(buf, out_hbm.at[idx_vmem.at[k]])

    return scatter(src, dest)


def _sc_gather_rows(src, dest):
    n = dest.shape[1]
    d = src.shape[1]
    assert n % (SC_WINDOW * SC_WORKERS) == 0, "token count must split evenly over the vector subcores"
    per = n // (SC_WINDOW * SC_WORKERS)

    @pl.kernel(out_type=jax.ShapeDtypeStruct((TOP_K_EXPERTS * n, d), src.dtype), mesh=_sc_mesh(),
               scratch_types=[pltpu.VMEM((dest.shape[0], SC_WINDOW), I32), pltpu.VMEM((SC_WINDOW, d), src.dtype)])
    def gather(src_hbm, idx_hbm, out_hbm, idx_vmem, buf):
        wid = lax.axis_index("c") * (SC_WORKERS // 2) + lax.axis_index("s")

        @pl.loop(0, per)
        def _(j):
            off = (wid * per + j) * SC_WINDOW
            pltpu.sync_copy(idx_hbm.at[:, pl.ds(off, SC_WINDOW)], idx_vmem)
            for k in range(TOP_K_EXPERTS):
                pltpu.sync_copy(src_hbm.at[idx_vmem.at[k]], buf)
                pltpu.sync_copy(buf, out_hbm.at[pl.ds(k * n + off, SC_WINDOW), :])

    return gather(src, dest)


def _combine_kernel(h_ref, y_ref, route_ref, g_ref, o_ref):
    half = h_ref.shape[1] // 2
    h_lo, h_hi = h_ref[:, 0:half], h_ref[:, half:2 * half]
    route = route_ref[...]
    for k in range(TOP_K_EXPERTS):
        y_lo, y_hi = _unpack_halves(y_ref[k])
        gate = route[:, k:k + 1]
        h_lo = h_lo + gate * y_lo
        h_hi = h_hi + gate * y_hi
    ms = (jnp.sum(h_lo * h_lo, axis=-1, keepdims=True)
          + jnp.sum(h_hi * h_hi, axis=-1, keepdims=True)) * (1.0 / (2 * half))
    inv = lax.rsqrt(ms + EPS)
    o_ref[:, 0:half] = h_lo * inv * g_ref[:, 0:half]
    o_ref[:, half:2 * half] = h_hi * inv * g_ref[:, half:2 * half]


def _combine(h1, yg, route, g_final, tm):
    n, d = h1.shape
    return pl.pallas_call(
        _combine_kernel,
        out_shape=jax.ShapeDtypeStruct((n, d), F32),
        grid=(n // tm,),
        in_specs=[pl.BlockSpec((tm, d), lambda i: (i, 0)),
                  pl.BlockSpec((TOP_K_EXPERTS, tm, d // 2), lambda i: (0, i, 0)),
                  pl.BlockSpec((tm, LANES), lambda i: (i, 0)),
                  pl.BlockSpec((1, d), lambda i: (0, 0))],
        out_specs=pl.BlockSpec((tm, d), lambda i: (i, 0)),
        compiler_params=_cparams(("parallel",)),
        name="combine_norm",
    )(h1, yg, route, g_final)


def _t5_bucket(dist):
    n = jnp.maximum(dist, 0)
    max_exact = N_BUCKETS // 2
    nf = jnp.maximum(n, 1).astype(F32)
    large = max_exact + (jnp.log(nf / max_exact) / math.log(MAX_DISTANCE / max_exact)
                         * (N_BUCKETS - max_exact)).astype(I32)
    large = jnp.minimum(large, N_BUCKETS - 1)
    return jnp.where(n < max_exact, n, large)


def _bias_blocks(bias_tab):
    t = LANES
    assert MAX_DISTANCE <= LANES
    r = jnp.arange(t, dtype=I32)[:, None]
    c = jnp.arange(t, dtype=I32)[None, :]
    rel = (bias_tab - bias_tab[N_BUCKETS - 1][None, :]).astype(F32)
    tiles = []
    buckets = jnp.arange(N_BUCKETS, dtype=I32)[:, None, None]
    for delta in (0, t):
        dist = r - c + delta
        hit = _t5_bucket(dist)[None] == buckets
        b = jnp.sum(jnp.where(hit[:, None], rel[:, :, None, None], 0.0), axis=0)
        tiles.append(jnp.where((dist >= 0)[None], b * LOG2E, MASK_VALUE))
    return jnp.stack(tiles)


def _regroup_w_in(w_in):
    sizes = (1024, 1024, 1024, 1024, KV_LATENT, 1024, HEAD_DIM_IDX, N_HEADS_IDX, D_MODEL, D_MODEL)
    parts, off = [], 0
    for sz in sizes:
        parts.append(w_in[:, off:off + sz])
        off += sz
    dq, dk, dv, sq, ckv, iq, ik, iw, ga, gb = parts
    dq = dq * (HEAD_DIM_DIFF ** -0.5 * LOG2E)
    iw = iw * ((N_HEADS_IDX ** -0.5) * (HEAD_DIM_IDX ** -0.5))
    pad = jnp.zeros((w_in.shape[0], PROJ_WIDTH - COL_IW - N_HEADS_IDX), w_in.dtype)
    w = jnp.concatenate([dq, dk, dv, sq, iq, ga, gb, ckv, ik, ik, iw, pad], axis=1)
    return w.astype(BF16)


def _block_tables(counts, n_assign):
    e, bm = N_EXPERTS, MOE_ROWS
    padded = (counts + bm - 1) // bm * bm
    pends = jnp.cumsum(padded)
    nblk = -(-(n_assign + e * (bm - 1)) // bm)
    first_row = jnp.arange(nblk, dtype=I32) * bm
    blk_exp = jnp.minimum(jnp.sum((pends[None, :] <= first_row[:, None]).astype(I32), axis=1), e - 1)
    n_used = (pends[-1] // bm).astype(I32).reshape(1)
    return blk_exp, n_used, nblk


def kernel(x, norm_attn_g, w_in, rel_bias, lam_q1, lam_k1, lam_q2, lam_k2, diff_subln_g, kv_norm_g, w_uk, w_uv,
           w_branch_diff, w_branch_dsa, w_out, norm_ffn_g, w_router, b_router, w_gate_up, b_gate_up, w_down,
           b_down, norm_final_g):
    batch, seq, d = x.shape
    n = batch * seq
    assert norm_attn_g.shape[0] == 1, "single-layer kernel"
    assert seq % DIFF_BLOCK == 0 and seq % KEY_CHUNK == 0 and d == D_MODEL
    assert seq <= (2 ** 15 - 1) * BF16_ROWS, "int16 per-element key counts"
    row_tile = math.gcd(n, ROW_TILE)
    token_tile = math.gcd(n, TOKEN_TILE)

    x2 = x.reshape(n, d)
    proj = _inproj(x2, norm_attn_g[0].reshape(1, d), _regroup_w_in(w_in[0]), row_tile, PROJ_COLS)

    lam_init = 0.8 - 0.6 * math.exp(-0.3 * 0)
    lam = (jnp.exp(jnp.sum(lam_q1[0].astype(F32) * lam_k1[0].astype(F32)))
           - jnp.exp(jnp.sum(lam_q2[0].astype(F32) * lam_k2[0].astype(F32))) + lam_init)
    y_diff = _diff_attention(proj, lam.reshape(1, 1).astype(F32), _bias_blocks(rel_bias[:, :N_HEADS_DIFF]),
                             diff_subln_g[0].reshape(1, -1).astype(F32), batch, seq, 1.0 - lam_init)

    y_dsa = _dsa_attention(proj, kv_norm_g[0].reshape(1, -1).astype(F32),
                           w_uk[0].transpose(0, 2, 1).astype(BF16), w_uv[0].astype(BF16),
                           _bias_blocks(rel_bias[:, N_HEADS_DIFF:]), batch, seq, min(TOPK_MAX, seq // 4))

    w_r = jnp.zeros((d, LANES), F32).at[:, :N_EXPERTS].set(w_router[0].astype(F32))
    b_r = jnp.full((1, LANES), MASK_VALUE, F32).at[0, :N_EXPERTS].set(b_router[0].astype(F32))
    h1, hn, route = _merge(x2, y_diff, y_dsa, proj, w_branch_diff[0].astype(BF16), w_branch_dsa[0].astype(BF16),
                           w_out[0].astype(BF16), norm_ffn_g[0].reshape(1, d).astype(F32),
                           w_r.astype(BF16), (w_r - w_r.astype(BF16).astype(F32)).astype(BF16), b_r,
                           token_tile)

    dest, counts = _route_rows(route, row_tile)
    dest = dest[:, :n]
    blk_exp, n_used, nblk = _block_tables(counts[:N_EXPERTS, 0].astype(I32), n * TOP_K_EXPERTS)
    xs = _sc_scatter_rows(hn, dest, nblk * MOE_ROWS)
    e, f = N_EXPERTS, D_EXPERT
    b_gu = b_gate_up[0].astype(F32).reshape(e, f // LANES, LANES, 2).transpose(0, 1, 3, 2).reshape(e, 1, 2 * f)
    ys = _expert_ffn(blk_exp, n_used, xs, _regroup_gate_up(w_gate_up[0], TOKEN_TILE), w_down[0].astype(BF16),
                     b_gu, b_down[0][:, None, :].astype(F32))
    yg = _sc_gather_rows(ys, dest).reshape(TOP_K_EXPERTS, n, d // 2)
    out = _combine(h1, yg, route, norm_final_g.reshape(1, d).astype(F32), token_tile)
    return out.reshape(batch, seq, d)
```

```python
import functools
import math

import jax
import jax.numpy as jnp
from jax import lax
from jax.experimental import pallas as pl
from jax.experimental.pallas import tpu as pltpu
from jax.experimental.pallas import tpu_sc as plsc

F32 = jnp.float32
BF16 = jnp.bfloat16
I32 = jnp.int32

D_MODEL = 1024
N_HEADS_DIFF = 8
HEAD_DIM_DIFF = 64
N_HEADS_DSA = 8
HEAD_DIM_DSA = 128
KV_LATENT = 256
N_HEADS_IDX = 16
HEAD_DIM_IDX = 64
TOPK_MAX = 256
N_BUCKETS = 32
MAX_DISTANCE = 128
N_EXPERTS = 32
TOP_K_EXPERTS = 4
D_EXPERT = 1024
SWIGLU_LIMIT = 7.0
SWIGLU_ALPHA = 1.702
EPS = 1e-6

LANES = 128
BF16_ROWS = 16
ROW_TILE = 1024
PROJ_COLS = 1280
TOKEN_TILE = 512
DIFF_BLOCK = 512
DIFF_HEADS_PER_STEP = 4
DSA_BLOCK = 256
KEY_CHUNK = 512
DSA_STREAMS = 2
MOE_ROWS = 512
PROJ_WIDTH = 7680
VMEM_LIMIT = 56 * 1024 * 1024

COL_DQ, COL_DK, COL_DV, COL_SQ, COL_IQ, COL_GA, COL_GB = (i * 1024 for i in range(7))
COL_CKV = 7168
COL_IK = 7424
COL_IW = 7552

LOG2E = math.log2(math.e)
MASK_VALUE = -1e30
M_INIT = -1e29
INT_MIN = -2 ** 31


def _cparams(sem):
    return pltpu.CompilerParams(dimension_semantics=sem, vmem_limit_bytes=VMEM_LIMIT)


def _inproj_kernel(x_ref, g_ref, w_ref, o_ref, xn_ref):
    @pl.when(pl.program_id(1) == 0)
    def _():
        x = x_ref[...]
        ms = jnp.mean(x * x, axis=-1, keepdims=True)
        xn_ref[...] = (x * lax.rsqrt(ms + EPS) * g_ref[...]).astype(BF16)

    o_ref[...] = jnp.dot(xn_ref[...], w_ref[...], preferred_element_type=F32).astype(o_ref.dtype)


def _inproj(x2, g, w, tm, tn):
    n, d = x2.shape
    width = w.shape[1]
    return pl.pallas_call(
        _inproj_kernel,
        out_shape=jax.ShapeDtypeStruct((n, width), BF16),
        grid=(n // tm, width // tn),
        in_specs=[pl.BlockSpec((tm, d), lambda i, j: (i, 0)),
                  pl.BlockSpec((1, d), lambda i, j: (0, 0)),
                  pl.BlockSpec((d, tn), lambda i, j: (0, j))],
        out_specs=pl.BlockSpec((tm, tn), lambda i, j: (i, j)),
        scratch_shapes=[pltpu.VMEM((tm, d), BF16)],
        compiler_params=_cparams(("parallel", "arbitrary")),
        name="inproj",
    )(x2, g, w)


def _lane_chunk_max(s):
    smax = s[:, 0:LANES]
    for j in range(1, s.shape[1] // LANES):
        smax = jnp.maximum(smax, s[:, j * LANES:(j + 1) * LANES])
    return smax


def _softmax_step(s_ref, tk, v, m_ref, l_ref, acc_ref, smax=None, split=True):
    nl = tk // LANES
    if smax is None:
        smax = _lane_chunk_max(s_ref[:, 0:tk])
    m_prev = m_ref[...]
    m_new = jnp.maximum(m_prev, jnp.max(smax, axis=-1, keepdims=True))
    alpha = jnp.exp2(m_prev - m_new)
    psum = None
    ps = []
    for j in range(nl):
        pj = jnp.exp2(s_ref[:, j * LANES:(j + 1) * LANES] - m_new)
        psum = pj if psum is None else psum + pj
        ps.append(pj.astype(BF16))
    l_ref[...] = alpha * l_ref[...] + psum
    pv = _dot_split(jnp.concatenate(ps, axis=1), v, split)
    e = acc_ref.shape[1]
    a = alpha if e == LANES else jnp.concatenate([alpha] * (e // LANES), axis=1)
    acc_ref[...] = a * acc_ref[...] + pv
    m_ref[...] = m_new


def _softmax_init(m_ref, l_ref, acc_ref):
    m_ref[...] = jnp.full(m_ref.shape, M_INIT, F32)
    l_ref[...] = jnp.zeros(l_ref.shape, F32)
    acc_ref[...] = jnp.zeros(acc_ref.shape, F32)


def _softmax_result(l_ref, acc_ref):
    return acc_ref[...] * (1.0 / jnp.sum(l_ref[...], axis=-1, keepdims=True))


def _near_bias(s_ref, d0, d1, delta, groups, t, tk):
    for g in range(groups):
        for rb in range(t // LANES):
            for cb in range(tk // LANES):
                bd = delta + rb - cb
                rows = slice(g * t + rb * LANES, g * t + (rb + 1) * LANES)
                cols = slice(cb * LANES, (cb + 1) * LANES)
                if bd == 0:
                    s_ref[rows, cols] = s_ref[rows, cols] + d0(g)
                elif bd == 1:
                    s_ref[rows, cols] = s_ref[rows, cols] + d1(g)
                elif bd < 0:
                    s_ref[rows, cols] = jnp.full((LANES, LANES), MASK_VALUE, F32)


def _dot_nt(a, b, split=True):
    dn = (((1,), (1,)), ((), ()))
    if not split:
        return lax.dot_general(a, b, dn, preferred_element_type=F32)
    h = a.shape[0] // 2
    return jnp.concatenate([lax.dot_general(a[:h], b, dn, preferred_element_type=F32),
                            lax.dot_general(a[h:], b, dn, preferred_element_type=F32)], axis=0)


def _dot_split(a, b, split=True):
    if not split:
        return jnp.dot(a, b, preferred_element_type=F32)
    h = a.shape[0] // 2
    return jnp.concatenate([jnp.dot(a[:h], b, preferred_element_type=F32),
                            jnp.dot(a[h:], b, preferred_element_type=F32)], axis=0)


def _diff_kernel(lam_ref, q_ref, k_ref, v_ref, bias_ref, g_ref, o_ref, q2_ref, s_ref, m_ref, l_ref, acc_ref,
                 *, out_scale, heads):
    t = q_ref.shape[0]
    e = 2 * HEAD_DIM_DIFF
    qi = pl.program_id(2)
    cols = [slice(hh * e, (hh + 1) * e) for hh in range(heads)]

    lane = lax.broadcasted_iota(I32, (t, e), 1)
    for hh in range(heads):
        q = q_ref[:, cols[hh]]
        zero = jnp.zeros_like(q)
        q2_ref[hh, 0:t, :] = jnp.where(lane < HEAD_DIM_DIFF, q, zero)
        q2_ref[hh, t:2 * t, :] = jnp.where(lane >= HEAD_DIM_DIFF, q, zero)
        _softmax_init(m_ref.at[hh], l_ref.at[hh], acc_ref.at[hh])

    def chunk(kc, delta):
        off = pl.multiple_of(kc * t, t)
        for hh in range(heads):
            s = _dot_nt(q2_ref[hh], k_ref[pl.ds(off, t), cols[hh]])
            s_ref[hh] = s
            smax = None
            if delta is None:
                smax = _lane_chunk_max(s)
            else:
                _near_bias(s_ref.at[hh], lambda g: bias_ref[0, hh], lambda g: bias_ref[1, hh], delta, 2, t, t)
            _softmax_step(s_ref.at[hh], t, v_ref[pl.ds(off, t), cols[hh]], m_ref.at[hh], l_ref.at[hh],
                          acc_ref.at[hh], smax)

    def far_body(kc, carry):
        chunk(kc, None)
        return carry

    lax.fori_loop(0, jnp.maximum(qi - 1, 0), far_body, 0)

    @pl.when(qi >= 1)
    def _():
        chunk(qi - 1, t // LANES)

    chunk(qi, 0)

    for hh in range(heads):
        o = _softmax_result(l_ref.at[hh], acc_ref.at[hh])
        o = o[0:t, :] - lam_ref[0, 0] * o[t:2 * t, :]
        ms = jnp.mean(o * o, axis=-1, keepdims=True)
        o_ref[:, cols[hh]] = (o * lax.rsqrt(ms + EPS) * g_ref[...] * out_scale).astype(o_ref.dtype)


def _diff_attention(proj, lam, bias_blocks, subln_g, batch, seq, out_scale):
    t = DIFF_BLOCK
    nq = seq // t
    h = N_HEADS_DIFF
    e = 2 * HEAD_DIM_DIFF
    hp = DIFF_HEADS_PER_STEP
    w = hp * e
    kernel = functools.partial(_diff_kernel, out_scale=out_scale, heads=hp)
    return pl.pallas_call(
        kernel,
        out_shape=jax.ShapeDtypeStruct((batch * seq, h * e), BF16),
        grid=(batch, h // hp, nq),
        in_specs=[pl.BlockSpec(memory_space=pltpu.SMEM),
                  pl.BlockSpec((t, w), lambda b, hh, qi: (b * nq + qi, COL_DQ // w + hh)),
                  pl.BlockSpec((seq, w), lambda b, hh, qi: (b, COL_DK // w + hh)),
                  pl.BlockSpec((seq, w), lambda b, hh, qi: (b, COL_DV // w + hh)),
                  pl.BlockSpec((2, hp, LANES, LANES), lambda b, hh, qi: (0, hh, 0, 0)),
                  pl.BlockSpec((1, e), lambda b, hh, qi: (0, 0))],
        out_specs=pl.BlockSpec((t, w), lambda b, hh, qi: (b * nq + qi, hh)),
        scratch_shapes=[pltpu.VMEM((hp, 2 * t, e), BF16),
                        pltpu.VMEM((hp, 2 * t, t), F32),
                        pltpu.VMEM((hp, 2 * t, LANES), F32),
                        pltpu.VMEM((hp, 2 * t, LANES), F32),
                        pltpu.VMEM((hp, 2 * t, e), F32)],
        compiler_params=_cparams(("parallel", "parallel", "arbitrary")),
        name="diff_attn",
    )(lam, proj, proj, proj, bias_blocks, subln_g)


def _sortable_key(x):
    bits = lax.bitcast_convert_type(x, I32)
    return bits ^ ((bits >> 31) & jnp.int32(0x7FFFFFFF))


def _dsa_kernel(iq_ref, sq_ref, iw_ref, ik_ref, ckv_ref, kvg_ref, wuk_ref, wuv_ref, bias_ref, o_ref,
                c_ref, key_ref, keyt_ref, keyh_ref, qi_ref, wb_ref, ql_ref, s_ref, m_ref, l_ref, acc_ref, *, topk, scale):
    t = iq_ref.shape[0]
    tk = KEY_CHUNK
    qi = pl.program_id(1)
    n_chunks = qi + 1
    hi, hb = N_HEADS_IDX, N_HEADS_DSA

    @pl.when(qi == 0)
    def _():
        ckv = ckv_ref[...].astype(F32)
        ms = jnp.mean(ckv * ckv, axis=-1, keepdims=True)
        c_ref[...] = (ckv * lax.rsqrt(ms + EPS) * kvg_ref[...]).astype(BF16)

    rg = qi_ref.shape[2]
    lane = lax.broadcasted_iota(I32, (t, LANES), 1)
    for h in range(hi):
        blk = iq_ref[:, (h // 2) * LANES:(h // 2 + 1) * LANES]
        keep = (lane < HEAD_DIM_IDX) if h % 2 == 0 else (lane >= HEAD_DIM_IDX)
        qi_ref[:, h, :, :] = jnp.where(keep, blk, jnp.zeros_like(blk)).reshape(t // rg, rg, LANES)
        wb_ref[:, h, :, :] = jnp.broadcast_to(iw_ref[:, h:h + 1].astype(F32), (t, LANES)).reshape(t // rg, rg, LANES)

    def score_chunk(kc, diag):
        off = pl.multiple_of(kc * t, t)
        d = _dot_nt(qi_ref[...].reshape(hi * t, LANES), ik_ref[pl.ds(off, t), :]).reshape(t // rg, hi, rg, t)
        w = jnp.concatenate([wb_ref[...]] * (t // LANES), axis=-1)
        sc = jnp.sum(jnp.maximum(d, 0.0) * w, axis=1).reshape(t, t)
        key = _sortable_key(sc + 0.0)
        if diag:
            row = lax.broadcasted_iota(I32, (t, t), 0)
            col = lax.broadcasted_iota(I32, (t, t), 1)
            key = jnp.where(col <= row, key, jnp.int32(INT_MIN))
        key_ref[:, pl.ds(off, t)] = key
        key_t = key.T
        keyt_ref[pl.ds(off, t), :] = key_t
        keyh_ref[pl.ds(off, t), :] = (key_t >> 16).astype(jnp.int16)

    def score_body(kc, carry):
        score_chunk(kc, False)
        return carry

    lax.fori_loop(0, qi, score_body, 0)
    score_chunk(qi, True)

    @pl.when(qi % 2 == 0)
    def _():
        off = pl.multiple_of((qi + 1) * t, t)
        key_ref[:, pl.ds(off, t)] = jnp.full((t, t), INT_MIN, I32)
        keyt_ref[pl.ds(off, t), :] = jnp.full((t, t), INT_MIN, I32)
        keyh_ref[pl.ds(off, t), :] = jnp.full((t, t), INT_MIN >> 16, jnp.int16)

    n_steps = (qi + 2) // 2

    def count_ge(cand_s):
        def body(kc, cnt):
            off = pl.multiple_of(kc * tk, tk)
            k = keyt_ref[pl.ds(off, tk), :].reshape(tk // 8, 8, t)
            return cnt + jnp.sum(jnp.where(k >= cand_s[None], 1, 0), axis=0)

        cnt = lax.fori_loop(0, n_steps, body, jnp.zeros((8, t), I32))
        return jnp.sum(cnt, axis=0, keepdims=True)

    def count_packed(c_row):
        c16 = jnp.broadcast_to(c_row.astype(jnp.int16), (16, t))

        def body(kc, cnt):
            off = pl.multiple_of(kc * tk, tk)
            k = keyh_ref[pl.ds(off, tk), :].reshape(tk // 16, 16, t)
            hit = jnp.where(k >= c16[None], jnp.int16(1), jnp.int16(0))
            for j in range(tk // 16):
                cnt = cnt + hit[j]
            return cnt

        cnt = lax.fori_loop(0, n_steps, body, jnp.zeros((16, t), jnp.int16))
        return jnp.sum(cnt.astype(I32), axis=0, keepdims=True)

    def make_bit_body(count):
        def bit_body(i, carry):
            cur, n_ge = carry
            bit = lax.shift_left(jnp.int32(1), 31 - i)
            cand = cur | bit
            total = count(cand ^ jnp.int32(INT_MIN))
            accept = total >= topk
            return jnp.where(accept, cand, cur), jnp.where(accept, total, n_ge)
        return bit_body

    zeros8 = jnp.zeros((8, t), I32)
    low_bias = jnp.int32(1 << 15)
    cur, n_ge = lax.fori_loop(
        0, 16, make_bit_body(lambda cand_s: count_packed(cand_s[0:1, :] >> 16)), (zeros8, zeros8))
    high = (cur ^ jnp.int32(INT_MIN)) >> 16
    above_high = jnp.where(high[0:1, :] == 2 ** 15 - 1, 0, count_packed(jnp.minimum(high[0:1, :], 2 ** 15 - 2) + 1))

    def repack_body(kc, carry):
        off = pl.multiple_of(kc * tk, tk)
        k = keyt_ref[pl.ds(off, tk), :].reshape(tk // 8, 8, t)
        low = jnp.where((k >> 16) == high[None], (k & jnp.int32(0xFFFF)) - low_bias, -low_bias)
        keyh_ref[pl.ds(off, tk), :] = low.reshape(tk, t).astype(jnp.int16)
        return carry

    lax.fori_loop(0, n_steps, repack_body, 0)
    cur, n_ge = lax.fori_loop(
        16, 32, make_bit_body(lambda cand_s: above_high + count_packed((cand_s[0:1, :] & jnp.int32(0xFFFF)) - low_bias)),
        (cur, n_ge))
    thr = jnp.maximum(cur ^ jnp.int32(INT_MIN), jnp.int32(INT_MIN + 1))

    def lanes_to_rows(v):
        b = jnp.broadcast_to(v[0:1, :], (LANES, t)).T
        return jnp.concatenate([b] * (tk // LANES), axis=1)

    thr_w = lanes_to_rows(thr)
    ties = jnp.max(n_ge) > topk

    @pl.when(jnp.logical_not(ties))
    def _():
        def mask_body(kc, carry):
            off = pl.multiple_of(kc * tk, tk)
            am = jnp.where(key_ref[:, pl.ds(off, tk)] >= thr_w, 0.0, MASK_VALUE).astype(F32)
            key_ref[:, pl.ds(off, tk)] = lax.bitcast_convert_type(am, I32)
            return carry

        lax.fori_loop(0, n_steps, mask_body, 0)

    @pl.when(ties)
    def _():
        above = jnp.where(thr == jnp.int32(2 ** 31 - 1), 0, count_ge(jnp.minimum(thr, jnp.int32(2 ** 31 - 2)) + 1))
        need_w = lanes_to_rows(topk - above).astype(F32)
        r = lax.broadcasted_iota(I32, (tk, tk), 0)
        c = lax.broadcasted_iota(I32, (tk, tk), 1)
        before = jnp.where(r < c, 1.0, 0.0).astype(BF16)
        ones_w = jnp.ones((tk, tk), BF16)

        def tie_body(kc, seen):
            off = pl.multiple_of(kc * tk, tk)
            k = key_ref[:, pl.ds(off, tk)]
            eq = k == thr_w
            eq_b = jnp.where(eq, 1.0, 0.0).astype(BF16)
            rank = seen + jnp.dot(eq_b, before, preferred_element_type=F32)
            keep = (k > thr_w) | (eq & (rank < need_w))
            key_ref[:, pl.ds(off, tk)] = lax.bitcast_convert_type(jnp.where(keep, 0.0, MASK_VALUE).astype(F32), I32)
            return seen + jnp.dot(eq_b, ones_w, preferred_element_type=F32)

        lax.fori_loop(0, n_steps, tie_body, jnp.zeros((t, tk), F32))

    streams = ql_ref.shape[0]
    split = streams == 1
    hs = hb // streams
    for h in range(hb):
        qh = sq_ref[:, h * HEAD_DIM_DSA:(h + 1) * HEAD_DIM_DSA]
        ql = jnp.dot(qh, wuk_ref[h], preferred_element_type=F32) * scale
        ql_ref[h // hs, (h % hs) * t:(h % hs + 1) * t, :] = ql.astype(BF16)
    for g in range(streams):
        _softmax_init(m_ref.at[g], l_ref.at[g], acc_ref.at[g])

    def chunk(kc, width, delta):
        off = pl.multiple_of(kc * tk, tk)
        c = c_ref[pl.ds(off, width), :]
        am = lax.bitcast_convert_type(key_ref[:, pl.ds(off, width)], F32)
        for g in range(streams):
            s = (_dot_nt(ql_ref[g], c, split).reshape(hs, t, width) + am[None]).reshape(hs * t, width)
            s_ref[g, :, 0:width] = s
            smax = None
            if delta is None:
                smax = _lane_chunk_max(s)
            else:
                _near_bias(s_ref.at[g], lambda j: bias_ref[0, g * hs + j], lambda j: bias_ref[1, g * hs + j],
                           delta, hs, t, width)
            _softmax_step(s_ref.at[g], width, c, m_ref.at[g], l_ref.at[g], acc_ref.at[g], smax, split)

    def far_body(kc, carry):
        chunk(kc, tk, None)
        return carry

    lax.fori_loop(0, jnp.maximum((qi - 1) // 2, 0), far_body, 0)
    half = qi // 2

    @pl.when(qi % 2 == 1)
    def _():
        chunk(half, tk, t // LANES)

    @pl.when(qi % 2 == 0)
    def _():
        @pl.when(half >= 1)
        def _():
            chunk(half - 1, tk, tk // LANES)
        chunk(half, t, 0)

    for h in range(hb):
        g, j = h // hs, h % hs
        inv_l = 1.0 / jnp.sum(l_ref[g, j * t:(j + 1) * t, :], axis=-1, keepdims=True)
        ol = (acc_ref[g, j * t:(j + 1) * t, :] * inv_l).astype(BF16)
        o = jnp.dot(ol, wuv_ref[h], preferred_element_type=F32)
        o_ref[:, h * HEAD_DIM_DSA:(h + 1) * HEAD_DIM_DSA] = o.astype(o_ref.dtype)


def _dsa_attention(proj, kv_g, w_ukt, w_uv, bias_blocks, batch, seq, topk):
    t = DSA_BLOCK
    nq = seq // t
    hb, hi = N_HEADS_DSA, N_HEADS_IDX
    width = hb * HEAD_DIM_DSA
    ns = DSA_STREAMS
    rs = hb // ns * t
    kernel = functools.partial(_dsa_kernel, topk=topk, scale=HEAD_DIM_DSA ** -0.5 * LOG2E)
    return pl.pallas_call(
        kernel,
        out_shape=jax.ShapeDtypeStruct((batch * seq, width), BF16),
        grid=(batch, nq),
        in_specs=[pl.BlockSpec((t, 1024), lambda b, qi: (b * nq + qi, COL_IQ // 1024)),
                  pl.BlockSpec((t, 1024), lambda b, qi: (b * nq + qi, COL_SQ // 1024)),
                  pl.BlockSpec((t, LANES), lambda b, qi: (b * nq + qi, COL_IW // LANES)),
                  pl.BlockSpec((seq, LANES), lambda b, qi: (b, COL_IK // LANES)),
                  pl.BlockSpec((seq, KV_LATENT), lambda b, qi: (b, COL_CKV // KV_LATENT)),
                  pl.BlockSpec((1, KV_LATENT), lambda b, qi: (0, 0)),
                  pl.BlockSpec((hb, HEAD_DIM_DSA, KV_LATENT), lambda b, qi: (0, 0, 0)),
                  pl.BlockSpec((hb, KV_LATENT, HEAD_DIM_DSA), lambda b, qi: (0, 0, 0)),
                  pl.BlockSpec((2, hb, LANES, LANES), lambda b, qi: (0, 0, 0, 0))],
        out_specs=pl.BlockSpec((t, width), lambda b, qi: (b * nq + qi, 0)),
        scratch_shapes=[pltpu.VMEM((seq, KV_LATENT), BF16),
                        pltpu.VMEM((t, seq), I32),
                        pltpu.VMEM((seq, t), I32),
                        pltpu.VMEM((seq, t), jnp.int16),
                        pltpu.VMEM((t // BF16_ROWS, hi, BF16_ROWS, LANES), BF16),
                        pltpu.VMEM((t // BF16_ROWS, hi, BF16_ROWS, LANES), F32),
                        pltpu.VMEM((ns, rs, KV_LATENT), BF16),
                        pltpu.VMEM((ns, rs, KEY_CHUNK), F32),
                        pltpu.VMEM((ns, rs, LANES), F32),
                        pltpu.VMEM((ns, rs, LANES), F32),
                        pltpu.VMEM((ns, rs, KV_LATENT), F32)],
        compiler_params=_cparams(("parallel", "arbitrary")),
        name="dsa_attn",
    )(proj, proj, proj, proj, proj, kv_g, w_ukt, w_uv, bias_blocks)


def _pack_halves(x):
    c = x.shape[1] // 2
    lo = lax.bitcast_convert_type(x[:, :c].astype(BF16).astype(F32), I32)
    hi = lax.bitcast_convert_type(x[:, c:].astype(BF16).astype(F32), I32)
    return lax.shift_right_logical(lo, 16) | (hi & jnp.int32(-65536))


def _unpack_halves(w):
    lo = lax.bitcast_convert_type(lax.shift_left(w, 16), F32)
    hi = lax.bitcast_convert_type(w & jnp.int32(-65536), F32)
    return lo, hi


def _merge_kernel(x_ref, yd_ref, ys_ref, ga_ref, gb_ref, wd_ref, ws_ref, wo_ref, g_ref, wrh_ref, wrl_ref, br_ref,
                  h_ref, hn_ref, route_ref):
    bd = _dot_split(yd_ref[...], wd_ref[...])
    bs = _dot_split(ys_ref[...], ws_ref[...])
    merged = (jax.nn.sigmoid(ga_ref[...].astype(F32)) * bd + jax.nn.sigmoid(gb_ref[...].astype(F32)) * bs)
    h = x_ref[...] + _dot_split(merged.astype(BF16), wo_ref[...])
    h_ref[...] = h
    ms = jnp.mean(h * h, axis=-1, keepdims=True)
    hn = h * lax.rsqrt(ms + EPS) * g_ref[...]
    hn_ref[...] = _pack_halves(hn)

    hn_hi = hn.astype(BF16)
    hn_lo = (hn - hn_hi.astype(F32)).astype(BF16)
    logits = (_dot_split(hn_hi, wrh_ref[...]) + _dot_split(hn_lo, wrh_ref[...])
              + _dot_split(hn_hi, wrl_ref[...]))
    logits = logits + br_ref[...]
    lane = lax.broadcasted_iota(I32, logits.shape, 1)
    vals, ids = [], []
    for _ in range(TOP_K_EXPERTS):
        mx = jnp.max(logits, axis=-1, keepdims=True)
        ix = jnp.min(jnp.where(logits == mx, lane, LANES), axis=-1, keepdims=True)
        vals.append(mx)
        ids.append(ix)
        logits = jnp.where(lane == ix, -jnp.inf, logits)
    es = [jnp.exp(v - vals[0]) for v in vals]
    inv = 1.0 / (es[0] + es[1] + es[2] + es[3])
    route = jnp.zeros(logits.shape, F32)
    for k in range(TOP_K_EXPERTS):
        route = jnp.where(lane == k, es[k] * inv, route)
        route = jnp.where(lane == TOP_K_EXPERTS + k, ids[k].astype(F32), route)
    route_ref[...] = route


def _merge(x2, y_diff, y_dsa, proj, w_bd, w_bs, w_out, g_ffn, w_router_hi, w_router_lo, b_router, tm):
    n, d = x2.shape
    row = lambda i: (i, 0)
    const = lambda i: (0, 0)
    return pl.pallas_call(
        _merge_kernel,
        out_shape=(jax.ShapeDtypeStruct((n, d), F32),
                   jax.ShapeDtypeStruct((n, d // 2), I32),
                   jax.ShapeDtypeStruct((n, LANES), F32)),
        grid=(n // tm,),
        in_specs=[pl.BlockSpec((tm, d), row),
                  pl.BlockSpec((tm, d), row),
                  pl.BlockSpec((tm, d), row),
                  pl.BlockSpec((tm, d), lambda i: (i, COL_GA // 1024)),
                  pl.BlockSpec((tm, d), lambda i: (i, COL_GB // 1024)),
                  pl.BlockSpec((d, d), const),
                  pl.BlockSpec((d, d), const),
                  pl.BlockSpec((d, d), const),
                  pl.BlockSpec((1, d), const),
                  pl.BlockSpec((d, LANES), const),
                  pl.BlockSpec((d, LANES), const),
                  pl.BlockSpec((1, LANES), const)],
        out_specs=(pl.BlockSpec((tm, d), row),
                   pl.BlockSpec((tm, d // 2), row),
                   pl.BlockSpec((tm, LANES), row)),
        compiler_params=_cparams(("parallel",)),
        name="merge_router",
    )(x2, y_diff, y_dsa, proj, proj, w_bd, w_bs, w_out, g_ffn, w_router_hi, w_router_lo, b_router)


def _regroup_kernel(w_ref, p_ref, o_ref):
    pw = p_ref.shape[0]
    for j in range(w_ref.shape[2] // pw):
        w = w_ref[0, :, j * pw:(j + 1) * pw].astype(BF16)
        o_ref[0, :, j * pw:(j + 1) * pw] = jnp.dot(w, p_ref[...], preferred_element_type=F32).astype(BF16)


def _regroup_gate_up(w_gu, rows):
    e, d, f2 = w_gu.shape
    pw = 2 * LANES
    src = jnp.arange(pw, dtype=I32)
    dst = (src % 2) * LANES + src // 2
    perm = (dst[:, None] == jnp.arange(pw, dtype=I32)[None, :]).astype(BF16)
    return pl.pallas_call(
        _regroup_kernel,
        out_shape=jax.ShapeDtypeStruct((e, d, f2), BF16),
        grid=(e, d // rows),
        in_specs=[pl.BlockSpec((1, rows, f2), lambda i, j: (i, j, 0)),
                  pl.BlockSpec((pw, pw), lambda i, j: (0, 0))],
        out_specs=pl.BlockSpec((1, rows, f2), lambda i, j: (i, j, 0)),
        compiler_params=_cparams(("parallel", "parallel")),
        name="regroup_gate_up",
    )(w_gu, perm)


def _ffn_kernel(be_ref, nu_ref, x_ref, wgu_ref, wd_ref, bgu_ref, bd_ref, o_ref):
    @pl.when(pl.program_id(0) < nu_ref[0])
    def _():
        x_lo, x_hi = _unpack_halves(x_ref[...])
        half = x_lo.shape[1]
        gu = (jnp.dot(x_lo.astype(BF16), wgu_ref[0, 0:half, :], preferred_element_type=F32)
              + jnp.dot(x_hi.astype(BF16), wgu_ref[0, half:2 * half, :], preferred_element_type=F32)
              + bgu_ref[0])
        acts = []
        for j in range(gu.shape[1] // (2 * LANES)):
            gate = jnp.minimum(gu[:, 2 * j * LANES:(2 * j + 1) * LANES], SWIGLU_LIMIT)
            up = jnp.clip(gu[:, (2 * j + 1) * LANES:(2 * j + 2) * LANES], -SWIGLU_LIMIT, SWIGLU_LIMIT)
            glu = gate * jax.nn.sigmoid(gate * SWIGLU_ALPHA)
            acts.append(((up + 1.0) * glu).astype(BF16))
        a = jnp.concatenate(acts, axis=1)
        y = jnp.dot(a, wd_ref[0].astype(BF16), preferred_element_type=F32) + bd_ref[0]
        o_ref[...] = _pack_halves(y)

    @pl.when(pl.program_id(0) >= nu_ref[0])
    def _():
        o_ref[...] = jnp.zeros(o_ref.shape, o_ref.dtype)


def _expert_ffn(blk_exp, n_used, xs, wgu, wd, bgu, bd):
    p, dw = xs.shape
    f, d = wd.shape[1], wd.shape[2]
    nblk = p // MOE_ROWS
    wmap = lambda i, be, nu: (be[i], 0, 0)
    grid_spec = pltpu.PrefetchScalarGridSpec(
        num_scalar_prefetch=2,
        grid=(nblk,),
        in_specs=[pl.BlockSpec((MOE_ROWS, dw), lambda i, be, nu: (i, 0)),
                  pl.BlockSpec((1, d, 2 * f), wmap),
                  pl.BlockSpec((1, f, d), wmap),
                  pl.BlockSpec((1, 1, 2 * f), wmap),
                  pl.BlockSpec((1, 1, d), wmap)],
        out_specs=pl.BlockSpec((MOE_ROWS, dw), lambda i, be, nu: (i, 0)),
    )
    return pl.pallas_call(
        _ffn_kernel,
        out_shape=jax.ShapeDtypeStruct((p, dw), I32),
        grid_spec=grid_spec,
        compiler_params=_cparams(("arbitrary",)),
        name="expert_ffn",
    )(blk_exp, n_used, xs, wgu, wd, bgu, bd)


def _route_kernel(route_ref, dest_ref, cnt_ref, u_ref, carry_ref, pstart_ref, *, block_rows):
    ph, i = pl.program_id(0), pl.program_id(1)
    tm = route_ref.shape[0]

    @pl.when((ph == 0) & (i == 0))
    def _():
        r = lax.broadcasted_iota(I32, (tm, tm), 0)
        c = lax.broadcasted_iota(I32, (tm, tm), 1)
        u_ref[...] = jnp.where(r < c, 1.0, 0.0).astype(BF16)
        carry_ref[...] = jnp.zeros(carry_ref.shape, F32)

    @pl.when((ph == 1) & (i == 0))
    def _():
        counts = carry_ref[...]
        cnt_ref[...] = counts
        padded = jnp.ceil(counts * (1.0 / block_rows)) * block_rows
        r = lax.broadcasted_iota(I32, (LANES, LANES), 0)
        c = lax.broadcasted_iota(I32, (LANES, LANES), 1)
        lower = jnp.where(c < r, 1.0, 0.0).astype(F32)
        pstart_ref[...] = jnp.dot(lower, padded, preferred_element_type=F32, precision=lax.Precision.HIGHEST)
        carry_ref[...] = jnp.zeros(carry_ref.shape, F32)

    rt = route_ref[...].T
    sub = lax.broadcasted_iota(I32, (LANES, tm), 0)
    hits = [sub == rt[TOP_K_EXPERTS + k:TOP_K_EXPERTS + k + 1, :].astype(I32) for k in range(TOP_K_EXPERTS)]
    m = jnp.zeros((LANES, tm), F32)
    for hit in hits:
        m = m + jnp.where(hit, 1.0, 0.0)
    tile_counts = jnp.broadcast_to(jnp.sum(m, axis=1, keepdims=True), (LANES, LANES))

    @pl.when(ph == 0)
    def _():
        dest_ref[...] = jnp.zeros(dest_ref.shape, I32)

    @pl.when(ph == 1)
    def _():
        prefix = jnp.dot(m.astype(BF16), u_ref[...], preferred_element_type=F32)
        rank = prefix + (pstart_ref[:, 0:1] + carry_ref[:, 0:1])
        rows = [jnp.sum(jnp.where(hit, rank, 0.0), axis=0, keepdims=True) for hit in hits]
        rows.append(jnp.zeros((dest_ref.shape[0] - TOP_K_EXPERTS, tm), F32))
        dest_ref[...] = jnp.concatenate(rows, axis=0).astype(I32)

    carry_ref[...] = carry_ref[...] + tile_counts


def _route_rows(route, tm):
    n = route.shape[0]
    nt = n // tm
    kernel = functools.partial(_route_kernel, block_rows=MOE_ROWS)
    return pl.pallas_call(
        kernel,
        out_shape=(jax.ShapeDtypeStruct((8, n + tm), I32), jax.ShapeDtypeStruct((LANES, LANES), F32)),
        grid=(2, nt),
        in_specs=[pl.BlockSpec((tm, LANES), lambda ph, i: (i, 0))],
        out_specs=(pl.BlockSpec((8, tm), lambda ph, i: (0, ph * i + (1 - ph) * nt)),
                   pl.BlockSpec((LANES, LANES), lambda ph, i: (0, 0))),
        scratch_shapes=[pltpu.VMEM((tm, tm), BF16),
                        pltpu.VMEM((LANES, LANES), F32),
                        pltpu.VMEM((LANES, LANES), F32)],
        compiler_params=_cparams(("arbitrary", "arbitrary")),
        name="route_rows",
    )(route)


SC_WINDOW = 128
SC_WORKERS = 32


def _sc_mesh():
    return plsc.VectorSubcoreMesh(core_axis_name="c", subcore_axis_name="s")


def _sc_scatter_rows(src, dest, p):
    n, d = src.shape
    assert n % (SC_WINDOW * SC_WORKERS) == 0, "token count must split evenly over the vector subcores"
    per = n // (SC_WINDOW * SC_WORKERS)

    @pl.kernel(out_type=jax.ShapeDtypeStruct((p, d), src.dtype), mesh=_sc_mesh(),
               scratch_types=[pltpu.VMEM((dest.shape[0], SC_WINDOW), I32), pltpu.VMEM((SC_WINDOW, d), src.dtype)])
    def scatter(src_hbm, idx_hbm, out_hbm, idx_vmem, buf):
        wid = lax.axis_index("c") * (SC_WORKERS // 2) + lax.axis_index("s")

        @pl.loop(0, per)
        def _(j):
            off = (wid * per + j) * SC_WINDOW
            pltpu.sync_copy(idx_hbm.at[:, pl.ds(off, SC_WINDOW)], idx_vmem)
            pltpu.sync_copy(src_hbm.at[pl.ds(off, SC_WINDOW), :], buf)
            for k in range(TOP_K_EXPERTS):
                pltpu.sync_copy(buf, out_hbm.at[idx_vmem.at[k]])

    return scatter(src, dest)


def _sc_gather_rows(src, dest):
    n = dest.shape[1]
    d = src.shape[1]
    assert n % (SC_WINDOW * SC_WORKERS) == 0, "token count must split evenly over the vector subcores"
    per = n // (SC_WINDOW * SC_WORKERS)

    @pl.kernel(out_type=jax.ShapeDtypeStruct((TOP_K_EXPERTS * n, d), src.dtype), mesh=_sc_mesh(),
               scratch_types=[pltpu.VMEM((dest.shape[0], SC_WINDOW), I32), pltpu.VMEM((SC_WINDOW, d), src.dtype)])
    def gather(src_hbm, idx_hbm, out_hbm, idx_vmem, buf):
        wid = lax.axis_index("c") * (SC_WORKERS // 2) + lax.axis_index("s")

        @pl.loop(0, per)
        def _(j):
            off = (wid * per + j) * SC_WINDOW
            pltpu.sync_copy(idx_hbm.at[:, pl.ds(off, SC_WINDOW)], idx_vmem)
            for k in range(TOP_K_EXPERTS):
                pltpu.sync_copy(src_hbm.at[idx_vmem.at[k]], buf)
                pltpu.sync_copy(buf, out_hbm.at[pl.ds(k * n + off, SC_WINDOW), :])

    return gather(src, dest)


def _combine_kernel(h_ref, y_ref, route_ref, g_ref, o_ref):
    half = h_ref.shape[1] // 2
    h_lo, h_hi = h_ref[:, 0:half], h_ref[:, half:2 * half]
    route = route_ref[...]
    for k in range(TOP_K_EXPERTS):
        y_lo, y_hi = _unpack_halves(y_ref[k])
        gate = route[:, k:k + 1]
        h_lo = h_lo + gate * y_lo
        h_hi = h_hi + gate * y_hi
    ms = (jnp.sum(h_lo * h_lo, axis=-1, keepdims=True)
          + jnp.sum(h_hi * h_hi, axis=-1, keepdims=True)) * (1.0 / (2 * half))
    inv = lax.rsqrt(ms + EPS)
    o_ref[:, 0:half] = h_lo * inv * g_ref[:, 0:half]
    o_ref[:, half:2 * half] = h_hi * inv * g_ref[:, half:2 * half]


def _combine(h1, yg, route, g_final, tm):
    n, d = h1.shape
    return pl.pallas_call(
        _combine_kernel,
        out_shape=jax.ShapeDtypeStruct((n, d), F32),
        grid=(n // tm,),
        in_specs=[pl.BlockSpec((tm, d), lambda i: (i, 0)),
                  pl.BlockSpec((TOP_K_EXPERTS, tm, d // 2), lambda i: (0, i, 0)),
                  pl.BlockSpec((tm, LANES), lambda i: (i, 0)),
                  pl.BlockSpec((1, d), lambda i: (0, 0))],
        out_specs=pl.BlockSpec((tm, d), lambda i: (i, 0)),
        compiler_params=_cparams(("parallel",)),
        name="combine_norm",
    )(h1, yg, route, g_final)


def _t5_bucket(dist):
    n = jnp.maximum(dist, 0)
    max_exact = N_BUCKETS // 2
    nf = jnp.maximum(n, 1).astype(F32)
    large = max_exact + (jnp.log(nf / max_exact) / math.log(MAX_DISTANCE / max_exact)
                         * (N_BUCKETS - max_exact)).astype(I32)
    large = jnp.minimum(large, N_BUCKETS - 1)
    return jnp.where(n < max_exact, n, large)


def _bias_blocks(bias_tab):
    t = LANES
    assert MAX_DISTANCE <= LANES
    r = jnp.arange(t, dtype=I32)[:, None]
    c = jnp.arange(t, dtype=I32)[None, :]
    rel = (bias_tab - bias_tab[N_BUCKETS - 1][None, :]).astype(F32)
    tiles = []
    buckets = jnp.arange(N_BUCKETS, dtype=I32)[:, None, None]
    for delta in (0, t):
        dist = r - c + delta
        hit = _t5_bucket(dist)[None] == buckets
        b = jnp.sum(jnp.where(hit[:, None], rel[:, :, None, None], 0.0), axis=0)
        tiles.append(jnp.where((dist >= 0)[None], b * LOG2E, MASK_VALUE))
    return jnp.stack(tiles)


def _regroup_w_in(w_in):
    sizes = (1024, 1024, 1024, 1024, KV_LATENT, 1024, HEAD_DIM_IDX, N_HEADS_IDX, D_MODEL, D_MODEL)
    parts, off = [], 0
    for sz in sizes:
        parts.append(w_in[:, off:off + sz])
        off += sz
    dq, dk, dv, sq, ckv, iq, ik, iw, ga, gb = parts
    dq = dq * (HEAD_DIM_DIFF ** -0.5 * LOG2E)
    iw = iw * ((N_HEADS_IDX ** -0.5) * (HEAD_DIM_IDX ** -0.5))
    pad = jnp.zeros((w_in.shape[0], PROJ_WIDTH - COL_IW - N_HEADS_IDX), w_in.dtype)
    w = jnp.concatenate([dq, dk, dv, sq, iq, ga, gb, ckv, ik, ik, iw, pad], axis=1)
    return w.astype(BF16)


def _block_tables(counts, n_assign):
    e, bm = N_EXPERTS, MOE_ROWS
    padded = (counts + bm - 1) // bm * bm
    pends = jnp.cumsum(padded)
    nblk = -(-(n_assign + e * (bm - 1)) // bm)
    first_row = jnp.arange(nblk, dtype=I32) * bm
    blk_exp = jnp.minimum(jnp.sum((pends[None, :] <= first_row[:, None]).astype(I32), axis=1), e - 1)
    n_used = (pends[-1] // bm).astype(I32).reshape(1)
    return blk_exp, n_used, nblk


def kernel(x, norm_attn_g, w_in, rel_bias, lam_q1, lam_k1, lam_q2, lam_k2, diff_subln_g, kv_norm_g, w_uk, w_uv,
           w_branch_diff, w_branch_dsa, w_out, norm_ffn_g, w_router, b_router, w_gate_up, b_gate_up, w_down,
           b_down, norm_final_g):
    batch, seq, d = x.shape
    n = batch * seq
    assert norm_attn_g.shape[0] == 1, "single-layer kernel"
    assert seq % DIFF_BLOCK == 0 and seq % KEY_CHUNK == 0 and d == D_MODEL
    assert seq <= (2 ** 15 - 1) * BF16_ROWS, "int16 per-element key counts"
    row_tile = math.gcd(n, ROW_TILE)
    token_tile = math.gcd(n, TOKEN_TILE)

    x2 = x.reshape(n, d)
    proj = _inproj(x2, norm_attn_g[0].reshape(1, d), _regroup_w_in(w_in[0]), row_tile, PROJ_COLS)

    lam_init = 0.8 - 0.6 * math.exp(-0.3 * 0)
    lam = (jnp.exp(jnp.sum(lam_q1[0].astype(F32) * lam_k1[0].astype(F32)))
           - jnp.exp(jnp.sum(lam_q2[0].astype(F32) * lam_k2[0].astype(F32))) + lam_init)
    y_diff = _diff_attention(proj, lam.reshape(1, 1).astype(F32), _bias_blocks(rel_bias[:, :N_HEADS_DIFF]),
                             diff_subln_g[0].reshape(1, -1).astype(F32), batch, seq, 1.0 - lam_init)

    y_dsa = _dsa_attention(proj, kv_norm_g[0].reshape(1, -1).astype(F32),
                           w_uk[0].transpose(0, 2, 1).astype(BF16), w_uv[0].astype(BF16),
                           _bias_blocks(rel_bias[:, N_HEADS_DIFF:]), batch, seq, min(TOPK_MAX, seq // 4))

    w_r = jnp.zeros((d, LANES), F32).at[:, :N_EXPERTS].set(w_router[0].astype(F32))
    b_r = jnp.full((1, LANES), MASK_VALUE, F32).at[0, :N_EXPERTS].set(b_router[0].astype(F32))
    h1, hn, route = _merge(x2, y_diff, y_dsa, proj, w_branch_diff[0].astype(BF16), w_branch_dsa[0].astype(BF16),
                           w_out[0].astype(BF16), norm_ffn_g[0].reshape(1, d).astype(F32),
                           w_r.astype(BF16), (w_r - w_r.astype(BF16).astype(F32)).astype(BF16), b_r,
                           token_tile)

    dest, counts = _route_rows(route, row_tile)
    dest = dest[:, :n]
    blk_exp, n_used, nblk = _block_tables(counts[:N_EXPERTS, 0].astype(I32), n * TOP_K_EXPERTS)
    xs = _sc_scatter_rows(hn, dest, nblk * MOE_ROWS)
    e, f = N_EXPERTS, D_EXPERT
    b_gu = b_gate_up[0].astype(F32).reshape(e, f // LANES, LANES, 2).transpose(0, 1, 3, 2).reshape(e, 1, 2 * f)
    ys = _expert_ffn(blk_exp, n_used, xs, _regroup_gate_up(w_gate_up[0], TOKEN_TILE), w_down[0],
                     b_gu, b_down[0][:, None, :].astype(F32))
    yg = _sc_gather_rows(ys, dest).reshape(TOP_K_EXPERTS, n, d // 2)
    out = _combine(h1, yg, route, norm_final_g.reshape(1, d).astype(F32), token_tile)
    return out.reshape(batch, seq, d)
```

```python
import functools
import math

import jax
import jax.numpy as jnp
from jax import lax
from jax.experimental import pallas as pl
from jax.experimental.pallas import tpu as pltpu
from jax.experimental.pallas import tpu_sc as plsc

F32 = jnp.float32
BF16 = jnp.bfloat16
I32 = jnp.int32

D_MODEL = 1024
N_HEADS_DIFF = 8
HEAD_DIM_DIFF = 64
N_HEADS_DSA = 8
HEAD_DIM_DSA = 128
KV_LATENT = 256
N_HEADS_IDX = 16
HEAD_DIM_IDX = 64
TOPK_MAX = 256
N_BUCKETS = 32
MAX_DISTANCE = 128
N_EXPERTS = 32
TOP_K_EXPERTS = 4
D_EXPERT = 1024
SWIGLU_LIMIT = 7.0
SWIGLU_ALPHA = 1.702
EPS = 1e-6

LANES = 128
BF16_ROWS = 16
ROW_TILE = 1024
PROJ_COLS = 1280
TOKEN_TILE = 512
DIFF_BLOCK = 512
DIFF_HEADS_PER_STEP = 4
DSA_BLOCK = 256
KEY_CHUNK = 512
DSA_STREAMS = 2
MOE_ROWS = 512
PROJ_WIDTH = 7680
VMEM_LIMIT = 56 * 1024 * 1024

COL_DQ, COL_DK, COL_DV, COL_SQ, COL_IQ, COL_GA, COL_GB = (i * 1024 for i in range(7))
COL_CKV = 7168
COL_IK = 7424
COL_IW = 7552

LOG2E = math.log2(math.e)
MASK_VALUE = -1e30
M_INIT = -1e29
INT_MIN = -2 ** 31


def _cparams(sem):
    return pltpu.CompilerParams(dimension_semantics=sem, vmem_limit_bytes=VMEM_LIMIT)


def _inproj_kernel(x_ref, g_ref, w_ref, o_ref, xn_ref):
    @pl.when(pl.program_id(1) == 0)
    def _():
        x = x_ref[...]
        ms = jnp.mean(x * x, axis=-1, keepdims=True)
        xn_ref[...] = (x * lax.rsqrt(ms + EPS) * g_ref[...]).astype(BF16)

    o_ref[...] = jnp.dot(xn_ref[...], w_ref[...], preferred_element_type=F32).astype(o_ref.dtype)


def _inproj(x2, g, w, tm, tn):
    n, d = x2.shape
    width = w.shape[1]
    return pl.pallas_call(
        _inproj_kernel,
        out_shape=jax.ShapeDtypeStruct((n, width), BF16),
        grid=(n // tm, width // tn),
        in_specs=[pl.BlockSpec((tm, d), lambda i, j: (i, 0)),
                  pl.BlockSpec((1, d), lambda i, j: (0, 0)),
                  pl.BlockSpec((d, tn), lambda i, j: (0, j))],
        out_specs=pl.BlockSpec((tm, tn), lambda i, j: (i, j)),
        scratch_shapes=[pltpu.VMEM((tm, d), BF16)],
        compiler_params=_cparams(("parallel", "arbitrary")),
        name="inproj",
    )(x2, g, w)


def _lane_chunk_max(s):
    smax = s[:, 0:LANES]
    for j in range(1, s.shape[1] // LANES):
        smax = jnp.maximum(smax, s[:, j * LANES:(j + 1) * LANES])
    return smax


def _softmax_step(s_ref, tk, v, m_ref, l_ref, acc_ref, smax=None, split=True):
    nl = tk // LANES
    if smax is None:
        smax = _lane_chunk_max(s_ref[:, 0:tk])
    m_prev = m_ref[...]
    m_new = jnp.maximum(m_prev, jnp.max(smax, axis=-1, keepdims=True))
    alpha = jnp.exp2(m_prev - m_new)
    psum = None
    ps = []
    for j in range(nl):
        pj = jnp.exp2(s_ref[:, j * LANES:(j + 1) * LANES] - m_new)
        psum = pj if psum is None else psum + pj
        ps.append(pj.astype(BF16))
    l_ref[...] = alpha * l_ref[...] + psum
    pv = _dot_split(jnp.concatenate(ps, axis=1), v, split)
    e = acc_ref.shape[1]
    a = alpha if e == LANES else jnp.concatenate([alpha] * (e // LANES), axis=1)
    acc_ref[...] = a * acc_ref[...] + pv
    m_ref[...] = m_new


def _softmax_init(m_ref, l_ref, acc_ref):
    m_ref[...] = jnp.full(m_ref.shape, M_INIT, F32)
    l_ref[...] = jnp.zeros(l_ref.shape, F32)
    acc_ref[...] = jnp.zeros(acc_ref.shape, F32)


def _softmax_result(l_ref, acc_ref):
    return acc_ref[...] * (1.0 / jnp.sum(l_ref[...], axis=-1, keepdims=True))


def _near_bias(s_ref, d0, d1, delta, groups, t, tk):
    for g in range(groups):
        for rb in range(t // LANES):
            for cb in range(tk // LANES):
                bd = delta + rb - cb
                rows = slice(g * t + rb * LANES, g * t + (rb + 1) * LANES)
                cols = slice(cb * LANES, (cb + 1) * LANES)
                if bd == 0:
                    s_ref[rows, cols] = s_ref[rows, cols] + d0(g)
                elif bd == 1:
                    s_ref[rows, cols] = s_ref[rows, cols] + d1(g)
                elif bd < 0:
                    s_ref[rows, cols] = jnp.full((LANES, LANES), MASK_VALUE, F32)


def _dot_nt(a, b, split=True):
    dn = (((1,), (1,)), ((), ()))
    if not split:
        return lax.dot_general(a, b, dn, preferred_element_type=F32)
    h = a.shape[0] // 2
    return jnp.concatenate([lax.dot_general(a[:h], b, dn, preferred_element_type=F32),
                            lax.dot_general(a[h:], b, dn, preferred_element_type=F32)], axis=0)


def _dot_split(a, b, split=True):
    if not split:
        return jnp.dot(a, b, preferred_element_type=F32)
    h = a.shape[0] // 2
    return jnp.concatenate([jnp.dot(a[:h], b, preferred_element_type=F32),
                            jnp.dot(a[h:], b, preferred_element_type=F32)], axis=0)


def _diff_kernel(lam_ref, q_ref, k_ref, v_ref, bias_ref, g_ref, o_ref, q2_ref, s_ref, m_ref, l_ref, acc_ref,
                 *, out_scale, heads):
    t = q_ref.shape[0]
    e = 2 * HEAD_DIM_DIFF
    qi = pl.program_id(2)
    cols = [slice(hh * e, (hh + 1) * e) for hh in range(heads)]

    lane = lax.broadcasted_iota(I32, (t, e), 1)
    for hh in range(heads):
        q = q_ref[:, cols[hh]]
        zero = jnp.zeros_like(q)
        q2_ref[hh, 0:t, :] = jnp.where(lane < HEAD_DIM_DIFF, q, zero)
        q2_ref[hh, t:2 * t, :] = jnp.where(lane >= HEAD_DIM_DIFF, q, zero)
        _softmax_init(m_ref.at[hh], l_ref.at[hh], acc_ref.at[hh])

    def chunk(kc, delta):
        off = pl.multiple_of(kc * t, t)
        for hh in range(heads):
            s = _dot_nt(q2_ref[hh], k_ref[pl.ds(off, t), cols[hh]])
            s_ref[hh] = s
            smax = None
            if delta is None:
                smax = _lane_chunk_max(s)
            else:
                _near_bias(s_ref.at[hh], lambda g: bias_ref[0, hh], lambda g: bias_ref[1, hh], delta, 2, t, t)
            _softmax_step(s_ref.at[hh], t, v_ref[pl.ds(off, t), cols[hh]], m_ref.at[hh], l_ref.at[hh],
                          acc_ref.at[hh], smax)

    def far_body(kc, carry):
        chunk(kc, None)
        return carry

    lax.fori_loop(0, jnp.maximum(qi - 1, 0), far_body, 0)

    @pl.when(qi >= 1)
    def _():
        chunk(qi - 1, t // LANES)

    chunk(qi, 0)

    for hh in range(heads):
        o = _softmax_result(l_ref.at[hh], acc_ref.at[hh])
        o = o[0:t, :] - lam_ref[0, 0] * o[t:2 * t, :]
        ms = jnp.mean(o * o, axis=-1, keepdims=True)
        o_ref[:, cols[hh]] = (o * lax.rsqrt(ms + EPS) * g_ref[...] * out_scale).astype(o_ref.dtype)


def _diff_attention(proj, lam, bias_blocks, subln_g, batch, seq, out_scale):
    t = DIFF_BLOCK
    nq = seq // t
    h = N_HEADS_DIFF
    e = 2 * HEAD_DIM_DIFF
    hp = DIFF_HEADS_PER_STEP
    w = hp * e
    kernel = functools.partial(_diff_kernel, out_scale=out_scale, heads=hp)
    return pl.pallas_call(
        kernel,
        out_shape=jax.ShapeDtypeStruct((batch * seq, h * e), BF16),
        grid=(batch, h // hp, nq),
        in_specs=[pl.BlockSpec(memory_space=pltpu.SMEM),
                  pl.BlockSpec((t, w), lambda b, hh, qi: (b * nq + qi, COL_DQ // w + hh)),
                  pl.BlockSpec((seq, w), lambda b, hh, qi: (b, COL_DK // w + hh)),
                  pl.BlockSpec((seq, w), lambda b, hh, qi: (b, COL_DV // w + hh)),
                  pl.BlockSpec((2, hp, LANES, LANES), lambda b, hh, qi: (0, hh, 0, 0)),
                  pl.BlockSpec((1, e), lambda b, hh, qi: (0, 0))],
        out_specs=pl.BlockSpec((t, w), lambda b, hh, qi: (b * nq + qi, hh)),
        scratch_shapes=[pltpu.VMEM((hp, 2 * t, e), BF16),
                        pltpu.VMEM((hp, 2 * t, t), F32),
                        pltpu.VMEM((hp, 2 * t, LANES), F32),
                        pltpu.VMEM((hp, 2 * t, LANES), F32),
                        pltpu.VMEM((hp, 2 * t, e), F32)],
        compiler_params=_cparams(("parallel", "parallel", "arbitrary")),
        name="diff_attn",
    )(lam, proj, proj, proj, bias_blocks, subln_g)


def _sortable_key(x):
    bits = lax.bitcast_convert_type(x, I32)
    return bits ^ ((bits >> 31) & jnp.int32(0x7FFFFFFF))


def _dsa_kernel(iq_ref, sq_ref, iw_ref, ik_ref, ckv_ref, kvg_ref, wuk_ref, wuv_ref, bias_ref, o_ref,
                c_ref, key_ref, keyt_ref, keyh_ref, qi_ref, wb_ref, ql_ref, s_ref, m_ref, l_ref, acc_ref, *, topk, scale):
    t = iq_ref.shape[0]
    tk = KEY_CHUNK
    qi = pl.program_id(1)
    n_chunks = qi + 1
    hi, hb = N_HEADS_IDX, N_HEADS_DSA

    @pl.when(qi == 0)
    def _():
        ckv = ckv_ref[...].astype(F32)
        ms = jnp.mean(ckv * ckv, axis=-1, keepdims=True)
        c_ref[...] = (ckv * lax.rsqrt(ms + EPS) * kvg_ref[...]).astype(BF16)

    rg = qi_ref.shape[2]
    lane = lax.broadcasted_iota(I32, (t, LANES), 1)
    for h in range(hi):
        blk = iq_ref[:, (h // 2) * LANES:(h // 2 + 1) * LANES]
        keep = (lane < HEAD_DIM_IDX) if h % 2 == 0 else (lane >= HEAD_DIM_IDX)
        qi_ref[:, h, :, :] = jnp.where(keep, blk, jnp.zeros_like(blk)).reshape(t // rg, rg, LANES)
        wb_ref[:, h, :, :] = jnp.broadcast_to(iw_ref[:, h:h + 1].astype(F32), (t, LANES)).reshape(t // rg, rg, LANES)

    def score_chunk(kc, diag):
        off = pl.multiple_of(kc * t, t)
        d = _dot_nt(qi_ref[...].reshape(hi * t, LANES), ik_ref[pl.ds(off, t), :]).reshape(t // rg, hi, rg, t)
        w = jnp.concatenate([wb_ref[...]] * (t // LANES), axis=-1)
        sc = jnp.sum(jnp.maximum(d, 0.0) * w, axis=1).reshape(t, t)
        key = _sortable_key(sc + 0.0)
        if diag:
            row = lax.broadcasted_iota(I32, (t, t), 0)
            col = lax.broadcasted_iota(I32, (t, t), 1)
            key = jnp.where(col <= row, key, jnp.int32(INT_MIN))
        key_ref[:, pl.ds(off, t)] = key
        key_t = key.T
        keyt_ref[pl.ds(off, t), :] = key_t
        keyh_ref[pl.ds(off, t), :] = (key_t >> 16).astype(jnp.int16)

    def score_body(j, carry):
        score_chunk(2 * j, False)
        score_chunk(2 * j + 1, False)
        return carry

    lax.fori_loop(0, qi // 2, score_body, 0)

    @pl.when(qi % 2 == 1)
    def _():
        score_chunk(qi - 1, False)

    score_chunk(qi, True)

    @pl.when(qi % 2 == 0)
    def _():
        off = pl.multiple_of((qi + 1) * t, t)
        key_ref[:, pl.ds(off, t)] = jnp.full((t, t), INT_MIN, I32)
        keyt_ref[pl.ds(off, t), :] = jnp.full((t, t), INT_MIN, I32)
        keyh_ref[pl.ds(off, t), :] = jnp.full((t, t), INT_MIN >> 16, jnp.int16)

    n_steps = (qi + 2) // 2

    def count_ge(cand_s):
        def body(kc, cnt):
            off = pl.multiple_of(kc * tk, tk)
            k = keyt_ref[pl.ds(off, tk), :].reshape(tk // 8, 8, t)
            return cnt + jnp.sum(jnp.where(k >= cand_s[None], 1, 0), axis=0)

        cnt = lax.fori_loop(0, n_steps, body, jnp.zeros((8, t), I32))
        return jnp.sum(cnt, axis=0, keepdims=True)

    def count_packed(c_row):
        c16 = jnp.broadcast_to(c_row.astype(jnp.int16), (16, t))

        def body(kc, cnt):
            off = pl.multiple_of(kc * tk, tk)
            k = keyh_ref[pl.ds(off, tk), :].reshape(tk // 16, 16, t)
            hit = jnp.where(k >= c16[None], jnp.int16(1), jnp.int16(0))
            for j in range(tk // 16):
                cnt = cnt + hit[j]
            return cnt

        cnt = lax.fori_loop(0, n_steps, body, jnp.zeros((16, t), jnp.int16))
        return jnp.sum(cnt.astype(I32), axis=0, keepdims=True)

    def make_bit_body(count):
        def bit_body(i, carry):
            cur, n_ge = carry
            bit = lax.shift_left(jnp.int32(1), 31 - i)
            cand = cur | bit
            total = count(cand ^ jnp.int32(INT_MIN))
            accept = total >= topk
            return jnp.where(accept, cand, cur), jnp.where(accept, total, n_ge)
        return bit_body

    zeros8 = jnp.zeros((8, t), I32)
    low_bias = jnp.int32(1 << 15)
    cur, n_ge = lax.fori_loop(
        0, 16, make_bit_body(lambda cand_s: count_packed(cand_s[0:1, :] >> 16)), (zeros8, zeros8))
    high = (cur ^ jnp.int32(INT_MIN)) >> 16
    above_high = jnp.where(high[0:1, :] == 2 ** 15 - 1, 0, count_packed(jnp.minimum(high[0:1, :], 2 ** 15 - 2) + 1))

    def repack_body(kc, carry):
        off = pl.multiple_of(kc * tk, tk)
        k = keyt_ref[pl.ds(off, tk), :].reshape(tk // 8, 8, t)
        low = jnp.where((k >> 16) == high[None], (k & jnp.int32(0xFFFF)) - low_bias, -low_bias)
        keyh_ref[pl.ds(off, tk), :] = low.reshape(tk, t).astype(jnp.int16)
        return carry

    lax.fori_loop(0, n_steps, repack_body, 0)
    cur, n_ge = lax.fori_loop(
        16, 32, make_bit_body(lambda cand_s: above_high + count_packed((cand_s[0:1, :] & jnp.int32(0xFFFF)) - low_bias)),
        (cur, n_ge))
    thr = jnp.maximum(cur ^ jnp.int32(INT_MIN), jnp.int32(INT_MIN + 1))

    def lanes_to_rows(v):
        b = jnp.broadcast_to(v[0:1, :], (LANES, t)).T
        return jnp.concatenate([b] * (tk // LANES), axis=1)

    thr_w = lanes_to_rows(thr)
    ties = jnp.max(n_ge) > topk

    @pl.when(jnp.logical_not(ties))
    def _():
        def mask_body(kc, carry):
            off = pl.multiple_of(kc * tk, tk)
            am = jnp.where(key_ref[:, pl.ds(off, tk)] >= thr_w, 0.0, MASK_VALUE).astype(F32)
            key_ref[:, pl.ds(off, tk)] = lax.bitcast_convert_type(am, I32)
            return carry

        lax.fori_loop(0, n_steps, mask_body, 0)

    @pl.when(ties)
    def _():
        above = jnp.where(thr == jnp.int32(2 ** 31 - 1), 0, count_ge(jnp.minimum(thr, jnp.int32(2 ** 31 - 2)) + 1))
        need_w = lanes_to_rows(topk - above).astype(F32)
        r = lax.broadcasted_iota(I32, (tk, tk), 0)
        c = lax.broadcasted_iota(I32, (tk, tk), 1)
        before = jnp.where(r < c, 1.0, 0.0).astype(BF16)
        ones_w = jnp.ones((tk, tk), BF16)

        def tie_body(kc, seen):
            off = pl.multiple_of(kc * tk, tk)
            k = key_ref[:, pl.ds(off, tk)]
            eq = k == thr_w
            eq_b = jnp.where(eq, 1.0, 0.0).astype(BF16)
            rank = seen + jnp.dot(eq_b, before, preferred_element_type=F32)
            keep = (k > thr_w) | (eq & (rank < need_w))
            key_ref[:, pl.ds(off, tk)] = lax.bitcast_convert_type(jnp.where(keep, 0.0, MASK_VALUE).astype(F32), I32)
            return seen + jnp.dot(eq_b, ones_w, preferred_element_type=F32)

        lax.fori_loop(0, n_steps, tie_body, jnp.zeros((t, tk), F32))

    streams = ql_ref.shape[0]
    split = streams == 1
    hs = hb // streams
    for h in range(hb):
        qh = sq_ref[:, h * HEAD_DIM_DSA:(h + 1) * HEAD_DIM_DSA]
        ql = jnp.dot(qh, wuk_ref[h], preferred_element_type=F32) * scale
        ql_ref[h // hs, (h % hs) * t:(h % hs + 1) * t, :] = ql.astype(BF16)
    for g in range(streams):
        _softmax_init(m_ref.at[g], l_ref.at[g], acc_ref.at[g])

    def chunk(kc, width, delta):
        off = pl.multiple_of(kc * tk, tk)
        c = c_ref[pl.ds(off, width), :]
        am = lax.bitcast_convert_type(key_ref[:, pl.ds(off, width)], F32)
        for g in range(streams):
            s = (_dot_nt(ql_ref[g], c, split).reshape(hs, t, width) + am[None]).reshape(hs * t, width)
            s_ref[g, :, 0:width] = s
            smax = None
            if delta is None:
                smax = _lane_chunk_max(s)
            else:
                _near_bias(s_ref.at[g], lambda j: bias_ref[0, g * hs + j], lambda j: bias_ref[1, g * hs + j],
                           delta, hs, t, width)
            _softmax_step(s_ref.at[g], width, c, m_ref.at[g], l_ref.at[g], acc_ref.at[g], smax, split)

    def far_body(kc, carry):
        chunk(kc, tk, None)
        return carry

    lax.fori_loop(0, jnp.maximum((qi - 1) // 2, 0), far_body, 0)
    half = qi // 2

    @pl.when(qi % 2 == 1)
    def _():
        chunk(half, tk, t // LANES)

    @pl.when(qi % 2 == 0)
    def _():
        @pl.when(half >= 1)
        def _():
            chunk(half - 1, tk, tk // LANES)
        chunk(half, t, 0)

    for h in range(hb):
        g, j = h // hs, h % hs
        inv_l = 1.0 / jnp.sum(l_ref[g, j * t:(j + 1) * t, :], axis=-1, keepdims=True)
        ol = (acc_ref[g, j * t:(j + 1) * t, :] * inv_l).astype(BF16)
        o = jnp.dot(ol, wuv_ref[h], preferred_element_type=F32)
        o_ref[:, h * HEAD_DIM_DSA:(h + 1) * HEAD_DIM_DSA] = o.astype(o_ref.dtype)


def _dsa_attention(proj, kv_g, w_ukt, w_uv, bias_blocks, batch, seq, topk):
    t = DSA_BLOCK
    nq = seq // t
    hb, hi = N_HEADS_DSA, N_HEADS_IDX
    width = hb * HEAD_DIM_DSA
    ns = DSA_STREAMS
    rs = hb // ns * t
    kernel = functools.partial(_dsa_kernel, topk=topk, scale=HEAD_DIM_DSA ** -0.5 * LOG2E)
    return pl.pallas_call(
        kernel,
        out_shape=jax.ShapeDtypeStruct((batch * seq, width), BF16),
        grid=(batch, nq),
        in_specs=[pl.BlockSpec((t, 1024), lambda b, qi: (b * nq + qi, COL_IQ // 1024)),
                  pl.BlockSpec((t, 1024), lambda b, qi: (b * nq + qi, COL_SQ // 1024)),
                  pl.BlockSpec((t, LANES), lambda b, qi: (b * nq + qi, COL_IW // LANES)),
                  pl.BlockSpec((seq, LANES), lambda b, qi: (b, COL_IK // LANES)),
                  pl.BlockSpec((seq, KV_LATENT), lambda b, qi: (b, COL_CKV // KV_LATENT)),
                  pl.BlockSpec((1, KV_LATENT), lambda b, qi: (0, 0)),
                  pl.BlockSpec((hb, HEAD_DIM_DSA, KV_LATENT), lambda b, qi: (0, 0, 0)),
                  pl.BlockSpec((hb, KV_LATENT, HEAD_DIM_DSA), lambda b, qi: (0, 0, 0)),
                  pl.BlockSpec((2, hb, LANES, LANES), lambda b, qi: (0, 0, 0, 0))],
        out_specs=pl.BlockSpec((t, width), lambda b, qi: (b * nq + qi, 0)),
        scratch_shapes=[pltpu.VMEM((seq, KV_LATENT), BF16),
                        pltpu.VMEM((t, seq), I32),
                        pltpu.VMEM((seq, t), I32),
                        pltpu.VMEM((seq, t), jnp.int16),
                        pltpu.VMEM((t // BF16_ROWS, hi, BF16_ROWS, LANES), BF16),
                        pltpu.VMEM((t // BF16_ROWS, hi, BF16_ROWS, LANES), F32),
                        pltpu.VMEM((ns, rs, KV_LATENT), BF16),
                        pltpu.VMEM((ns, rs, KEY_CHUNK), F32),
                        pltpu.VMEM((ns, rs, LANES), F32),
                        pltpu.VMEM((ns, rs, LANES), F32),
                        pltpu.VMEM((ns, rs, KV_LATENT), F32)],
        compiler_params=_cparams(("parallel", "arbitrary")),
        name="dsa_attn",
    )(proj, proj, proj, proj, proj, kv_g, w_ukt, w_uv, bias_blocks)


def _pack_halves(x):
    c = x.shape[1] // 2
    lo = lax.bitcast_convert_type(x[:, :c].astype(BF16).astype(F32), I32)
    hi = lax.bitcast_convert_type(x[:, c:].astype(BF16).astype(F32), I32)
    return lax.shift_right_logical(lo, 16) | (hi & jnp.int32(-65536))


def _unpack_halves(w):
    lo = lax.bitcast_convert_type(lax.shift_left(w, 16), F32)
    hi = lax.bitcast_convert_type(w & jnp.int32(-65536), F32)
    return lo, hi


def _merge_kernel(x_ref, yd_ref, ys_ref, ga_ref, gb_ref, wd_ref, ws_ref, wo_ref, g_ref, wrh_ref, wrl_ref, br_ref,
                  h_ref, hn_ref, route_ref):
    bd = _dot_split(yd_ref[...], wd_ref[...])
    bs = _dot_split(ys_ref[...], ws_ref[...])
    merged = (jax.nn.sigmoid(ga_ref[...].astype(F32)) * bd + jax.nn.sigmoid(gb_ref[...].astype(F32)) * bs)
    h = x_ref[...] + _dot_split(merged.astype(BF16), wo_ref[...])
    h_ref[...] = h
    ms = jnp.mean(h * h, axis=-1, keepdims=True)
    hn = h * lax.rsqrt(ms + EPS) * g_ref[...]
    hn_ref[...] = _pack_halves(hn)

    hn_hi = hn.astype(BF16)
    hn_lo = (hn - hn_hi.astype(F32)).astype(BF16)
    logits = (_dot_split(hn_hi, wrh_ref[...]) + _dot_split(hn_lo, wrh_ref[...])
              + _dot_split(hn_hi, wrl_ref[...]))
    logits = logits + br_ref[...]
    lane = lax.broadcasted_iota(I32, logits.shape, 1)
    vals, ids = [], []
    for _ in range(TOP_K_EXPERTS):
        mx = jnp.max(logits, axis=-1, keepdims=True)
        ix = jnp.min(jnp.where(logits == mx, lane, LANES), axis=-1, keepdims=True)
        vals.append(mx)
        ids.append(ix)
        logits = jnp.where(lane == ix, -jnp.inf, logits)
    es = [jnp.exp(v - vals[0]) for v in vals]
    inv = 1.0 / (es[0] + es[1] + es[2] + es[3])
    route = jnp.zeros(logits.shape, F32)
    for k in range(TOP_K_EXPERTS):
        route = jnp.where(lane == k, es[k] * inv, route)
        route = jnp.where(lane == TOP_K_EXPERTS + k, ids[k].astype(F32), route)
    route_ref[...] = route


def _merge(x2, y_diff, y_dsa, proj, w_bd, w_bs, w_out, g_ffn, w_router_hi, w_router_lo, b_router, tm):
    n, d = x2.shape
    row = lambda i: (i, 0)
    const = lambda i: (0, 0)
    return pl.pallas_call(
        _merge_kernel,
        out_shape=(jax.ShapeDtypeStruct((n, d), F32),
                   jax.ShapeDtypeStruct((n, d // 2), I32),
                   jax.ShapeDtypeStruct((n, LANES), F32)),
        grid=(n // tm,),
        in_specs=[pl.BlockSpec((tm, d), row),
                  pl.BlockSpec((tm, d), row),
                  pl.BlockSpec((tm, d), row),
                  pl.BlockSpec((tm, d), lambda i: (i, COL_GA // 1024)),
                  pl.BlockSpec((tm, d), lambda i: (i, COL_GB // 1024)),
                  pl.BlockSpec((d, d), const),
                  pl.BlockSpec((d, d), const),
                  pl.BlockSpec((d, d), const),
                  pl.BlockSpec((1, d), const),
                  pl.BlockSpec((d, LANES), const),
                  pl.BlockSpec((d, LANES), const),
                  pl.BlockSpec((1, LANES), const)],
        out_specs=(pl.BlockSpec((tm, d), row),
                   pl.BlockSpec((tm, d // 2), row),
                   pl.BlockSpec((tm, LANES), row)),
        compiler_params=_cparams(("parallel",)),
        name="merge_router",
    )(x2, y_diff, y_dsa, proj, proj, w_bd, w_bs, w_out, g_ffn, w_router_hi, w_router_lo, b_router)


def _regroup_kernel(w_ref, p_ref, o_ref):
    pw = p_ref.shape[0]
    for j in range(w_ref.shape[2] // pw):
        w = w_ref[0, :, j * pw:(j + 1) * pw].astype(BF16)
        o_ref[0, :, j * pw:(j + 1) * pw] = jnp.dot(w, p_ref[...], preferred_element_type=F32).astype(BF16)


def _regroup_gate_up(w_gu, rows):
    e, d, f2 = w_gu.shape
    pw = 2 * LANES
    src = jnp.arange(pw, dtype=I32)
    dst = (src % 2) * LANES + src // 2
    perm = (dst[:, None] == jnp.arange(pw, dtype=I32)[None, :]).astype(BF16)
    return pl.pallas_call(
        _regroup_kernel,
        out_shape=jax.ShapeDtypeStruct((e, d, f2), BF16),
        grid=(e, d // rows),
        in_specs=[pl.BlockSpec((1, rows, f2), lambda i, j: (i, j, 0)),
                  pl.BlockSpec((pw, pw), lambda i, j: (0, 0))],
        out_specs=pl.BlockSpec((1, rows, f2), lambda i, j: (i, j, 0)),
        compiler_params=_cparams(("parallel", "parallel")),
        name="regroup_gate_up",
    )(w_gu, perm)


def _ffn_kernel(be_ref, nu_ref, x_ref, wgu_ref, wd_ref, bgu_ref, bd_ref, o_ref):
    @pl.when(pl.program_id(0) < nu_ref[0])
    def _():
        x_lo, x_hi = _unpack_halves(x_ref[...])
        half = x_lo.shape[1]
        gu = (jnp.dot(x_lo.astype(BF16), wgu_ref[0, 0:half, :], preferred_element_type=F32)
              + jnp.dot(x_hi.astype(BF16), wgu_ref[0, half:2 * half, :], preferred_element_type=F32)
              + bgu_ref[0])
        acts = []
        for j in range(gu.shape[1] // (2 * LANES)):
            gate = jnp.minimum(gu[:, 2 * j * LANES:(2 * j + 1) * LANES], SWIGLU_LIMIT)
            up = jnp.clip(gu[:, (2 * j + 1) * LANES:(2 * j + 2) * LANES], -SWIGLU_LIMIT, SWIGLU_LIMIT)
            glu = gate * jax.nn.sigmoid(gate * SWIGLU_ALPHA)
            acts.append(((up + 1.0) * glu).astype(BF16))
        a = jnp.concatenate(acts, axis=1)
        y = jnp.dot(a, wd_ref[0].astype(BF16), preferred_element_type=F32) + bd_ref[0]
        o_ref[...] = _pack_halves(y)

    @pl.when(pl.program_id(0) >= nu_ref[0])
    def _():
        o_ref[...] = jnp.zeros(o_ref.shape, o_ref.dtype)


def _expert_ffn(blk_exp, n_used, xs, wgu, wd, bgu, bd):
    p, dw = xs.shape
    f, d = wd.shape[1], wd.shape[2]
    nblk = p // MOE_ROWS
    wmap = lambda i, be, nu: (be[i], 0, 0)
    grid_spec = pltpu.PrefetchScalarGridSpec(
        num_scalar_prefetch=2,
        grid=(nblk,),
        in_specs=[pl.BlockSpec((MOE_ROWS, dw), lambda i, be, nu: (i, 0)),
                  pl.BlockSpec((1, d, 2 * f), wmap),
                  pl.BlockSpec((1, f, d), wmap),
                  pl.BlockSpec((1, 1, 2 * f), wmap),
                  pl.BlockSpec((1, 1, d), wmap)],
        out_specs=pl.BlockSpec((MOE_ROWS, dw), lambda i, be, nu: (i, 0)),
    )
    return pl.pallas_call(
        _ffn_kernel,
        out_shape=jax.ShapeDtypeStruct((p, dw), I32),
        grid_spec=grid_spec,
        compiler_params=_cparams(("arbitrary",)),
        name="expert_ffn",
    )(blk_exp, n_used, xs, wgu, wd, bgu, bd)


def _route_kernel(route_ref, dest_ref, cnt_ref, u_ref, carry_ref, pstart_ref, *, block_rows):
    ph, i = pl.program_id(0), pl.program_id(1)
    tm = route_ref.shape[0]

    @pl.when((ph == 0) & (i == 0))
    def _():
        r = lax.broadcasted_iota(I32, (tm, tm), 0)
        c = lax.broadcasted_iota(I32, (tm, tm), 1)
        u_ref[...] = jnp.where(r < c, 1.0, 0.0).astype(BF16)
        carry_ref[...] = jnp.zeros(carry_ref.shape, F32)

    @pl.when((ph == 1) & (i == 0))
    def _():
        counts = carry_ref[...]
        cnt_ref[...] = counts
        padded = jnp.ceil(counts * (1.0 / block_rows)) * block_rows
        r = lax.broadcasted_iota(I32, (LANES, LANES), 0)
        c = lax.broadcasted_iota(I32, (LANES, LANES), 1)
        lower = jnp.where(c < r, 1.0, 0.0).astype(F32)
        pstart_ref[...] = jnp.dot(lower, padded, preferred_element_type=F32, precision=lax.Precision.HIGHEST)
        carry_ref[...] = jnp.zeros(carry_ref.shape, F32)

    rt = route_ref[...].T
    sub = lax.broadcasted_iota(I32, (LANES, tm), 0)
    hits = [sub == rt[TOP_K_EXPERTS + k:TOP_K_EXPERTS + k + 1, :].astype(I32) for k in range(TOP_K_EXPERTS)]
    m = jnp.zeros((LANES, tm), F32)
    for hit in hits:
        m = m + jnp.where(hit, 1.0, 0.0)
    tile_counts = jnp.broadcast_to(jnp.sum(m, axis=1, keepdims=True), (LANES, LANES))

    @pl.when(ph == 0)
    def _():
        dest_ref[...] = jnp.zeros(dest_ref.shape, I32)

    @pl.when(ph == 1)
    def _():
        prefix = jnp.dot(m.astype(BF16), u_ref[...], preferred_element_type=F32)
        rank = prefix + (pstart_ref[:, 0:1] + carry_ref[:, 0:1])
        rows = [jnp.sum(jnp.where(hit, rank, 0.0), axis=0, keepdims=True) for hit in hits]
        rows.append(jnp.zeros((dest_ref.shape[0] - TOP_K_EXPERTS, tm), F32))
        dest_ref[...] = jnp.concatenate(rows, axis=0).astype(I32)

    carry_ref[...] = carry_ref[...] + tile_counts


def _route_rows(route, tm):
    n = route.shape[0]
    nt = n // tm
    kernel = functools.partial(_route_kernel, block_rows=MOE_ROWS)
    return pl.pallas_call(
        kernel,
        out_shape=(jax.ShapeDtypeStruct((8, n + tm), I32), jax.ShapeDtypeStruct((LANES, LANES), F32)),
        grid=(2, nt),
        in_specs=[pl.BlockSpec((tm, LANES), lambda ph, i: (i, 0))],
        out_specs=(pl.BlockSpec((8, tm), lambda ph, i: (0, ph * i + (1 - ph) * nt)),
                   pl.BlockSpec((LANES, LANES), lambda ph, i: (0, 0))),
        scratch_shapes=[pltpu.VMEM((tm, tm), BF16),
                        pltpu.VMEM((LANES, LANES), F32),
                        pltpu.VMEM((LANES, LANES), F32)],
        compiler_params=_cparams(("arbitrary", "arbitrary")),
        name="route_rows",
    )(route)


SC_WINDOW = 128
SC_WORKERS = 32


def _sc_mesh():
    return plsc.VectorSubcoreMesh(core_axis_name="c", subcore_axis_name="s")


def _sc_scatter_rows(src, dest, p):
    n, d = src.shape
    assert n % (SC_WINDOW * SC_WORKERS) == 0, "token count must split evenly over the vector subcores"
    per = n // (SC_WINDOW * SC_WORKERS)

    @pl.kernel(out_type=jax.ShapeDtypeStruct((p, d), src.dtype), mesh=_sc_mesh(),
               scratch_types=[pltpu.VMEM((dest.shape[0], SC_WINDOW), I32), pltpu.VMEM((SC_WINDOW, d), src.dtype)])
    def scatter(src_hbm, idx_hbm, out_hbm, idx_vmem, buf):
        wid = lax.axis_index("c") * (SC_WORKERS // 2) + lax.axis_index("s")

        @pl.loop(0, per)
        def _(j):
            off = (wid * per + j) * SC_WINDOW
            pltpu.sync_copy(idx_hbm.at[:, pl.ds(off, SC_WINDOW)], idx_vmem)
            pltpu.sync_copy(src_hbm.at[pl.ds(off, SC_WINDOW), :], buf)
            for k in range(TOP_K_EXPERTS):
                pltpu.sync_copy(buf, out_hbm.at[idx_vmem.at[k]])

    return scatter(src, dest)


def _sc_gather_rows(src, dest):
    n = dest.shape[1]
    d = src.shape[1]
    assert n % (SC_WINDOW * SC_WORKERS) == 0, "token count must split evenly over the vector subcores"
    per = n // (SC_WINDOW * SC_WORKERS)

    @pl.kernel(out_type=jax.ShapeDtypeStruct((TOP_K_EXPERTS * n, d), src.dtype), mesh=_sc_mesh(),
               scratch_types=[pltpu.VMEM((dest.shape[0], SC_WINDOW), I32), pltpu.VMEM((SC_WINDOW, d), src.dtype)])
    def gather(src_hbm, idx_hbm, out_hbm, idx_vmem, buf):
        wid = lax.axis_index("c") * (SC_WORKERS // 2) + lax.axis_index("s")

        @pl.loop(0, per)
        def _(j):
            off = (wid * per + j) * SC_WINDOW
            pltpu.sync_copy(idx_hbm.at[:, pl.ds(off, SC_WINDOW)], idx_vmem)
            for k in range(TOP_K_EXPERTS):
                pltpu.sync_copy(src_hbm.at[idx_vmem.at[k]], buf)
                pltpu.sync_copy(buf, out_hbm.at[pl.ds(k * n + off, SC_WINDOW), :])

    return gather(src, dest)


def _combine_kernel(h_ref, y_ref, route_ref, g_ref, o_ref):
    half = h_ref.shape[1] // 2
    h_lo, h_hi = h_ref[:, 0:half], h_ref[:, half:2 * half]
    route = route_ref[...]
    for k in range(TOP_K_EXPERTS):
        y_lo, y_hi = _unpack_halves(y_ref[k])
        gate = route[:, k:k + 1]
        h_lo = h_lo + gate * y_lo
        h_hi = h_hi + gate * y_hi
    ms = (jnp.sum(h_lo * h_lo, axis=-1, keepdims=True)
          + jnp.sum(h_hi * h_hi, axis=-1, keepdims=True)) * (1.0 / (2 * half))
    inv = lax.rsqrt(ms + EPS)
    o_ref[:, 0:half] = h_lo * inv * g_ref[:, 0:half]
    o_ref[:, half:2 * half] = h_hi * inv * g_ref[:, half:2 * half]


def _combine(h1, yg, route, g_final, tm):
    n, d = h1.shape
    return pl.pallas_call(
        _combine_kernel,
        out_shape=jax.ShapeDtypeStruct((n, d), F32),
        grid=(n // tm,),
        in_specs=[pl.BlockSpec((tm, d), lambda i: (i, 0)),
                  pl.BlockSpec((TOP_K_EXPERTS, tm, d // 2), lambda i: (0, i, 0)),
                  pl.BlockSpec((tm, LANES), lambda i: (i, 0)),
                  pl.BlockSpec((1, d), lambda i: (0, 0))],
        out_specs=pl.BlockSpec((tm, d), lambda i: (i, 0)),
        compiler_params=_cparams(("parallel",)),
        name="combine_norm",
    )(h1, yg, route, g_final)


def _t5_bucket(dist):
    n = jnp.maximum(dist, 0)
    max_exact = N_BUCKETS // 2
    nf = jnp.maximum(n, 1).astype(F32)
    large = max_exact + (jnp.log(nf / max_exact) / math.log(MAX_DISTANCE / max_exact)
                         * (N_BUCKETS - max_exact)).astype(I32)
    large = jnp.minimum(large, N_BUCKETS - 1)
    return jnp.where(n < max_exact, n, large)


def _bias_blocks(bias_tab):
    t = LANES
    assert MAX_DISTANCE <= LANES
    r = jnp.arange(t, dtype=I32)[:, None]
    c = jnp.arange(t, dtype=I32)[None, :]
    rel = (bias_tab - bias_tab[N_BUCKETS - 1][None, :]).astype(F32)
    tiles = []
    buckets = jnp.arange(N_BUCKETS, dtype=I32)[:, None, None]
    for delta in (0, t):
        dist = r - c + delta
        hit = _t5_bucket(dist)[None] == buckets
        b = jnp.sum(jnp.where(hit[:, None], rel[:, :, None, None], 0.0), axis=0)
        tiles.append(jnp.where((dist >= 0)[None], b * LOG2E, MASK_VALUE))
    return jnp.stack(tiles)


def _regroup_w_in(w_in):
    sizes = (1024, 1024, 1024, 1024, KV_LATENT, 1024, HEAD_DIM_IDX, N_HEADS_IDX, D_MODEL, D_MODEL)
    parts, off = [], 0
    for sz in sizes:
        parts.append(w_in[:, off:off + sz])
        off += sz
    dq, dk, dv, sq, ckv, iq, ik, iw, ga, gb = parts
    dq = dq * (HEAD_DIM_DIFF ** -0.5 * LOG2E)
    iw = iw * ((N_HEADS_IDX ** -0.5) * (HEAD_DIM_IDX ** -0.5))
    pad = jnp.zeros((w_in.shape[0], PROJ_WIDTH - COL_IW - N_HEADS_IDX), w_in.dtype)
    w = jnp.concatenate([dq, dk, dv, sq, iq, ga, gb, ckv, ik, ik, iw, pad], axis=1)
    return w.astype(BF16)


def _block_tables(counts, n_assign):
    e, bm = N_EXPERTS, MOE_ROWS
    padded = (counts + bm - 1) // bm * bm
    pends = jnp.cumsum(padded)
    nblk = -(-(n_assign + e * (bm - 1)) // bm)
    first_row = jnp.arange(nblk, dtype=I32) * bm
    blk_exp = jnp.minimum(jnp.sum((pends[None, :] <= first_row[:, None]).astype(I32), axis=1), e - 1)
    n_used = (pends[-1] // bm).astype(I32).reshape(1)
    return blk_exp, n_used, nblk


def kernel(x, norm_attn_g, w_in, rel_bias, lam_q1, lam_k1, lam_q2, lam_k2, diff_subln_g, kv_norm_g, w_uk, w_uv,
           w_branch_diff, w_branch_dsa, w_out, norm_ffn_g, w_router, b_router, w_gate_up, b_gate_up, w_down,
           b_down, norm_final_g):
    batch, seq, d = x.shape
    n = batch * seq
    assert norm_attn_g.shape[0] == 1, "single-layer kernel"
    assert seq % DIFF_BLOCK == 0 and seq % KEY_CHUNK == 0 and d == D_MODEL
    assert seq <= (2 ** 15 - 1) * BF16_ROWS, "int16 per-element key counts"
    row_tile = math.gcd(n, ROW_TILE)
    token_tile = math.gcd(n, TOKEN_TILE)

    x2 = x.reshape(n, d)
    proj = _inproj(x2, norm_attn_g[0].reshape(1, d), _regroup_w_in(w_in[0]), row_tile, PROJ_COLS)

    lam_init = 0.8 - 0.6 * math.exp(-0.3 * 0)
    lam = (jnp.exp(jnp.sum(lam_q1[0].astype(F32) * lam_k1[0].astype(F32)))
           - jnp.exp(jnp.sum(lam_q2[0].astype(F32) * lam_k2[0].astype(F32))) + lam_init)
    y_diff = _diff_attention(proj, lam.reshape(1, 1).astype(F32), _bias_blocks(rel_bias[:, :N_HEADS_DIFF]),
                             diff_subln_g[0].reshape(1, -1).astype(F32), batch, seq, 1.0 - lam_init)

    y_dsa = _dsa_attention(proj, kv_norm_g[0].reshape(1, -1).astype(F32),
                           w_uk[0].transpose(0, 2, 1).astype(BF16), w_uv[0].astype(BF16),
                           _bias_blocks(rel_bias[:, N_HEADS_DIFF:]), batch, seq, min(TOPK_MAX, seq // 4))

    w_r = jnp.zeros((d, LANES), F32).at[:, :N_EXPERTS].set(w_router[0].astype(F32))
    b_r = jnp.full((1, LANES), MASK_VALUE, F32).at[0, :N_EXPERTS].set(b_router[0].astype(F32))
    h1, hn, route = _merge(x2, y_diff, y_dsa, proj, w_branch_diff[0].astype(BF16), w_branch_dsa[0].astype(BF16),
                           w_out[0].astype(BF16), norm_ffn_g[0].reshape(1, d).astype(F32),
                           w_r.astype(BF16), (w_r - w_r.astype(BF16).astype(F32)).astype(BF16), b_r,
                           token_tile)

    dest, counts = _route_rows(route, row_tile)
    dest = dest[:, :n]
    blk_exp, n_used, nblk = _block_tables(counts[:N_EXPERTS, 0].astype(I32), n * TOP_K_EXPERTS)
    xs = _sc_scatter_rows(hn, dest, nblk * MOE_ROWS)
    e, f = N_EXPERTS, D_EXPERT
    b_gu = b_gate_up[0].astype(F32).reshape(e, f // LANES, LANES, 2).transpose(0, 1, 3, 2).reshape(e, 1, 2 * f)
    ys = _expert_ffn(blk_exp, n_used, xs, _regroup_gate_up(w_gate_up[0], TOKEN_TILE), w_down[0],
                     b_gu, b_down[0][:, None, :].astype(F32))
    yg = _sc_gather_rows(ys, dest).reshape(TOP_K_EXPERTS, n, d // 2)
    out = _combine(h1, yg, route, norm_final_g.reshape(1, d).astype(F32), token_tile)
    return out.reshape(batch, seq, d)
```

```python
import functools
import math

import jax
import jax.numpy as jnp
from jax import lax
from jax.experimental import pallas as pl
from jax.experimental.pallas import tpu as pltpu
from jax.experimental.pallas import tpu_sc as plsc

F32 = jnp.float32
BF16 = jnp.bfloat16
I32 = jnp.int32

D_MODEL = 1024
N_HEADS_DIFF = 8
HEAD_DIM_DIFF = 64
N_HEADS_DSA = 8
HEAD_DIM_DSA = 128
KV_LATENT = 256
N_HEADS_IDX = 16
HEAD_DIM_IDX = 64
TOPK_MAX = 256
N_BUCKETS = 32
MAX_DISTANCE = 128
N_EXPERTS = 32
TOP_K_EXPERTS = 4
D_EXPERT = 1024
SWIGLU_LIMIT = 7.0
SWIGLU_ALPHA = 1.702
EPS = 1e-6

LANES = 128
BF16_ROWS = 16
ROW_TILE = 1024
PROJ_COLS = 1280
TOKEN_TILE = 512
DIFF_BLOCK = 512
DIFF_HEADS_PER_STEP = 4
DSA_BLOCK = 256
KEY_CHUNK = 512
DSA_STREAMS = 2
MOE_ROWS = 512
PROJ_WIDTH = 7680
VMEM_LIMIT = 56 * 1024 * 1024

COL_DQ, COL_DK, COL_DV, COL_SQ, COL_IQ, COL_GA, COL_GB = (i * 1024 for i in range(7))
COL_CKV = 7168
COL_IK = 7424
COL_IW = 7552

LOG2E = math.log2(math.e)
MASK_VALUE = -1e30
M_INIT = -1e29
INT_MIN = -2 ** 31


def _cparams(sem):
    return pltpu.CompilerParams(dimension_semantics=sem, vmem_limit_bytes=VMEM_LIMIT)


def _inproj_kernel(x_ref, g_ref, w_ref, o_ref, xn_ref):
    @pl.when(pl.program_id(1) == 0)
    def _():
        x = x_ref[...]
        ms = jnp.mean(x * x, axis=-1, keepdims=True)
        xn_ref[...] = (x * lax.rsqrt(ms + EPS) * g_ref[...]).astype(BF16)

    o_ref[...] = jnp.dot(xn_ref[...], w_ref[...], preferred_element_type=F32).astype(o_ref.dtype)


def _inproj(x2, g, w, tm, tn):
    n, d = x2.shape
    width = w.shape[1]
    return pl.pallas_call(
        _inproj_kernel,
        out_shape=jax.ShapeDtypeStruct((n, width), BF16),
        grid=(n // tm, width // tn),
        in_specs=[pl.BlockSpec((tm, d), lambda i, j: (i, 0)),
                  pl.BlockSpec((1, d), lambda i, j: (0, 0)),
                  pl.BlockSpec((d, tn), lambda i, j: (0, j))],
        out_specs=pl.BlockSpec((tm, tn), lambda i, j: (i, j)),
        scratch_shapes=[pltpu.VMEM((tm, d), BF16)],
        compiler_params=_cparams(("parallel", "arbitrary")),
        name="inproj",
    )(x2, g, w)


def _lane_chunk_max(s):
    smax = s[:, 0:LANES]
    for j in range(1, s.shape[1] // LANES):
        smax = jnp.maximum(smax, s[:, j * LANES:(j + 1) * LANES])
    return smax


def _softmax_step(s_ref, tk, v, m_ref, l_ref, acc_ref, smax=None, split=True):
    nl = tk // LANES
    if smax is None:
        smax = _lane_chunk_max(s_ref[:, 0:tk])
    m_prev = m_ref[...]
    m_new = jnp.maximum(m_prev, jnp.max(smax, axis=-1, keepdims=True))
    alpha = jnp.exp2(m_prev - m_new)
    psum = None
    ps = []
    for j in range(nl):
        pj = jnp.exp2(s_ref[:, j * LANES:(j + 1) * LANES] - m_new)
        psum = pj if psum is None else psum + pj
        ps.append(pj.astype(BF16))
    l_ref[...] = alpha * l_ref[...] + psum
    pv = _dot_split(jnp.concatenate(ps, axis=1), v, split)
    e = acc_ref.shape[1]
    a = alpha if e == LANES else jnp.concatenate([alpha] * (e // LANES), axis=1)
    acc_ref[...] = a * acc_ref[...] + pv
    m_ref[...] = m_new


def _softmax_init(m_ref, l_ref, acc_ref):
    m_ref[...] = jnp.full(m_ref.shape, M_INIT, F32)
    l_ref[...] = jnp.zeros(l_ref.shape, F32)
    acc_ref[...] = jnp.zeros(acc_ref.shape, F32)


def _softmax_result(l_ref, acc_ref):
    return acc_ref[...] * (1.0 / jnp.sum(l_ref[...], axis=-1, keepdims=True))


def _near_bias(s_ref, d0, d1, delta, groups, t, tk):
    for g in range(groups):
        for rb in range(t // LANES):
            for cb in range(tk // LANES):
                bd = delta + rb - cb
                rows = slice(g * t + rb * LANES, g * t + (rb + 1) * LANES)
                cols = slice(cb * LANES, (cb + 1) * LANES)
                if bd == 0:
                    s_ref[rows, cols] = s_ref[rows, cols] + d0(g)
                elif bd == 1:
                    s_ref[rows, cols] = s_ref[rows, cols] + d1(g)
                elif bd < 0:
                    s_ref[rows, cols] = jnp.full((LANES, LANES), MASK_VALUE, F32)


def _dot_nt(a, b, split=True):
    dn = (((1,), (1,)), ((), ()))
    if not split:
        return lax.dot_general(a, b, dn, preferred_element_type=F32)
    h = a.shape[0] // 2
    return jnp.concatenate([lax.dot_general(a[:h], b, dn, preferred_element_type=F32),
                            lax.dot_general(a[h:], b, dn, preferred_element_type=F32)], axis=0)


def _dot_split(a, b, split=True):
    if not split:
        return jnp.dot(a, b, preferred_element_type=F32)
    h = a.shape[0] // 2
    return jnp.concatenate([jnp.dot(a[:h], b, preferred_element_type=F32),
                            jnp.dot(a[h:], b, preferred_element_type=F32)], axis=0)


def _diff_kernel(lam_ref, q_ref, k_ref, v_ref, bias_ref, g_ref, o_ref, q2_ref, s_ref, m_ref, l_ref, acc_ref,
                 *, out_scale, heads):
    t = q_ref.shape[0]
    e = 2 * HEAD_DIM_DIFF
    qi = pl.program_id(2)
    cols = [slice(hh * e, (hh + 1) * e) for hh in range(heads)]

    lane = lax.broadcasted_iota(I32, (t, e), 1)
    for hh in range(heads):
        q = q_ref[:, cols[hh]]
        zero = jnp.zeros_like(q)
        q2_ref[hh, 0:t, :] = jnp.where(lane < HEAD_DIM_DIFF, q, zero)
        q2_ref[hh, t:2 * t, :] = jnp.where(lane >= HEAD_DIM_DIFF, q, zero)
        _softmax_init(m_ref.at[hh], l_ref.at[hh], acc_ref.at[hh])

    def chunk(kc, delta):
        off = pl.multiple_of(kc * t, t)
        for hh in range(heads):
            s = _dot_nt(q2_ref[hh], k_ref[pl.ds(off, t), cols[hh]])
            s_ref[hh] = s
            smax = None
            if delta is None:
                smax = _lane_chunk_max(s)
            else:
                _near_bias(s_ref.at[hh], lambda g: bias_ref[0, hh], lambda g: bias_ref[1, hh], delta, 2, t, t)
            _softmax_step(s_ref.at[hh], t, v_ref[pl.ds(off, t), cols[hh]], m_ref.at[hh], l_ref.at[hh],
                          acc_ref.at[hh], smax)

    def far_body(kc, carry):
        chunk(kc, None)
        return carry

    lax.fori_loop(0, jnp.maximum(qi - 1, 0), far_body, 0)

    @pl.when(qi >= 1)
    def _():
        chunk(qi - 1, t // LANES)

    chunk(qi, 0)

    for hh in range(heads):
        o = _softmax_result(l_ref.at[hh], acc_ref.at[hh])
        o = o[0:t, :] - lam_ref[0, 0] * o[t:2 * t, :]
        ms = jnp.mean(o * o, axis=-1, keepdims=True)
        o_ref[:, cols[hh]] = (o * lax.rsqrt(ms + EPS) * g_ref[...] * out_scale).astype(o_ref.dtype)


def _diff_attention(proj, lam, bias_blocks, subln_g, batch, seq, out_scale):
    t = DIFF_BLOCK
    nq = seq // t
    h = N_HEADS_DIFF
    e = 2 * HEAD_DIM_DIFF
    hp = DIFF_HEADS_PER_STEP
    w = hp * e
    kernel = functools.partial(_diff_kernel, out_scale=out_scale, heads=hp)
    return pl.pallas_call(
        kernel,
        out_shape=jax.ShapeDtypeStruct((batch * seq, h * e), BF16),
        grid=(batch, h // hp, nq),
        in_specs=[pl.BlockSpec(memory_space=pltpu.SMEM),
                  pl.BlockSpec((t, w), lambda b, hh, qi: (b * nq + qi, COL_DQ // w + hh)),
                  pl.BlockSpec((seq, w), lambda b, hh, qi: (b, COL_DK // w + hh)),
                  pl.BlockSpec((seq, w), lambda b, hh, qi: (b, COL_DV // w + hh)),
                  pl.BlockSpec((2, hp, LANES, LANES), lambda b, hh, qi: (0, hh, 0, 0)),
                  pl.BlockSpec((1, e), lambda b, hh, qi: (0, 0))],
        out_specs=pl.BlockSpec((t, w), lambda b, hh, qi: (b * nq + qi, hh)),
        scratch_shapes=[pltpu.VMEM((hp, 2 * t, e), BF16),
                        pltpu.VMEM((hp, 2 * t, t), F32),
                        pltpu.VMEM((hp, 2 * t, LANES), F32),
                        pltpu.VMEM((hp, 2 * t, LANES), F32),
                        pltpu.VMEM((hp, 2 * t, e), F32)],
        compiler_params=_cparams(("parallel", "parallel", "arbitrary")),
        name="diff_attn",
    )(lam, proj, proj, proj, bias_blocks, subln_g)


def _sortable_key(x):
    bits = lax.bitcast_convert_type(x, I32)
    return bits ^ ((bits >> 31) & jnp.int32(0x7FFFFFFF))


def _dsa_kernel(iq_ref, sq_ref, iw_ref, ik_ref, ckv_ref, kvg_ref, wuk_ref, wuv_ref, bias_ref, o_ref,
                c_ref, key_ref, keyt_ref, keyh_ref, qi_ref, wb_ref, ql_ref, s_ref, m_ref, l_ref, acc_ref, *, topk, scale):
    t = iq_ref.shape[0]
    tk = KEY_CHUNK
    qi = pl.program_id(1)
    n_chunks = qi + 1
    hi, hb = N_HEADS_IDX, N_HEADS_DSA

    @pl.when(qi == 0)
    def _():
        ckv = ckv_ref[...].astype(F32)
        ms = jnp.mean(ckv * ckv, axis=-1, keepdims=True)
        c_ref[...] = (ckv * lax.rsqrt(ms + EPS) * kvg_ref[...]).astype(BF16)

    rg = qi_ref.shape[2]
    lane = lax.broadcasted_iota(I32, (t, LANES), 1)
    for h in range(hi):
        blk = iq_ref[:, (h // 2) * LANES:(h // 2 + 1) * LANES]
        keep = (lane < HEAD_DIM_IDX) if h % 2 == 0 else (lane >= HEAD_DIM_IDX)
        qi_ref[:, h, :, :] = jnp.where(keep, blk, jnp.zeros_like(blk)).reshape(t // rg, rg, LANES)
        wb_ref[:, h, :, :] = jnp.broadcast_to(iw_ref[:, h:h + 1].astype(F32), (t, LANES)).reshape(t // rg, rg, LANES)

    def score_chunk(kc, diag):
        off = pl.multiple_of(kc * t, t)
        d = _dot_nt(qi_ref[...].reshape(hi * t, LANES), ik_ref[pl.ds(off, t), :]).reshape(t // rg, hi, rg, t)
        w = jnp.concatenate([wb_ref[...]] * (t // LANES), axis=-1)
        sc = jnp.sum(jnp.maximum(d, 0.0) * w, axis=1).reshape(t, t)
        key = _sortable_key(sc + 0.0)
        if diag:
            row = lax.broadcasted_iota(I32, (t, t), 0)
            col = lax.broadcasted_iota(I32, (t, t), 1)
            key = jnp.where(col <= row, key, jnp.int32(INT_MIN))
        key_ref[:, pl.ds(off, t)] = key
        key_t = key.T
        keyt_ref[pl.ds(off, t), :] = key_t
        keyh_ref[pl.ds(off, t), :] = (key_t >> 16).astype(jnp.int16)

    def score_body(j, carry):
        for u in range(4):
            score_chunk(4 * j + u, False)
        return carry

    lax.fori_loop(0, qi // 4, score_body, 0)
    done = qi // 4 * 4

    @pl.when(qi % 4 >= 2)
    def _():
        score_chunk(done, False)
        score_chunk(done + 1, False)

    @pl.when(qi % 2 == 1)
    def _():
        score_chunk(qi - 1, False)

    score_chunk(qi, True)

    @pl.when(qi % 2 == 0)
    def _():
        off = pl.multiple_of((qi + 1) * t, t)
        key_ref[:, pl.ds(off, t)] = jnp.full((t, t), INT_MIN, I32)
        keyt_ref[pl.ds(off, t), :] = jnp.full((t, t), INT_MIN, I32)
        keyh_ref[pl.ds(off, t), :] = jnp.full((t, t), INT_MIN >> 16, jnp.int16)

    n_steps = (qi + 2) // 2

    def count_ge(cand_s):
        def body(kc, cnt):
            off = pl.multiple_of(kc * tk, tk)
            k = keyt_ref[pl.ds(off, tk), :].reshape(tk // 8, 8, t)
            return cnt + jnp.sum(jnp.where(k >= cand_s[None], 1, 0), axis=0)

        cnt = lax.fori_loop(0, n_steps, body, jnp.zeros((8, t), I32))
        return jnp.sum(cnt, axis=0, keepdims=True)

    def count_packed(c_row):
        c16 = jnp.broadcast_to(c_row.astype(jnp.int16), (16, t))

        def body(kc, cnt):
            off = pl.multiple_of(kc * tk, tk)
            k = keyh_ref[pl.ds(off, tk), :].reshape(tk // 16, 16, t)
            hit = jnp.where(k >= c16[None], jnp.int16(1), jnp.int16(0))
            parts = [cnt, hit[0]]
            for j in range(1, tk // 16):
                parts[j % 2] = parts[j % 2] + hit[j]
            return parts[0] + parts[1]

        cnt = lax.fori_loop(0, n_steps, body, jnp.zeros((16, t), jnp.int16))
        return jnp.sum(cnt.astype(I32), axis=0, keepdims=True)

    def make_bit_body(count):
        def bit_body(i, carry):
            cur, n_ge = carry
            bit = lax.shift_left(jnp.int32(1), 31 - i)
            cand = cur | bit
            total = count(cand ^ jnp.int32(INT_MIN))
            accept = total >= topk
            return jnp.where(accept, cand, cur), jnp.where(accept, total, n_ge)
        return bit_body

    zeros8 = jnp.zeros((8, t), I32)
    low_bias = jnp.int32(1 << 15)
    cur, n_ge = lax.fori_loop(
        0, 16, make_bit_body(lambda cand_s: count_packed(cand_s[0:1, :] >> 16)), (zeros8, zeros8))
    high = (cur ^ jnp.int32(INT_MIN)) >> 16
    above_high = jnp.where(high[0:1, :] == 2 ** 15 - 1, 0, count_packed(jnp.minimum(high[0:1, :], 2 ** 15 - 2) + 1))

    def repack_body(kc, carry):
        off = pl.multiple_of(kc * tk, tk)
        k = keyt_ref[pl.ds(off, tk), :].reshape(tk // 8, 8, t)
        low = jnp.where((k >> 16) == high[None], (k & jnp.int32(0xFFFF)) - low_bias, -low_bias)
        keyh_ref[pl.ds(off, tk), :] = low.reshape(tk, t).astype(jnp.int16)
        return carry

    lax.fori_loop(0, n_steps, repack_body, 0)
    cur, n_ge = lax.fori_loop(
        16, 32, make_bit_body(lambda cand_s: above_high + count_packed((cand_s[0:1, :] & jnp.int32(0xFFFF)) - low_bias)),
        (cur, n_ge))
    thr = jnp.maximum(cur ^ jnp.int32(INT_MIN), jnp.int32(INT_MIN + 1))

    def lanes_to_rows(v):
        b = jnp.broadcast_to(v[0:1, :], (LANES, t)).T
        return jnp.concatenate([b] * (tk // LANES), axis=1)

    thr_w = lanes_to_rows(thr)
    ties = jnp.max(n_ge) > topk

    @pl.when(jnp.logical_not(ties))
    def _():
        def mask_body(kc, carry):
            off = pl.multiple_of(kc * tk, tk)
            am = jnp.where(key_ref[:, pl.ds(off, tk)] >= thr_w, 0.0, MASK_VALUE).astype(F32)
            key_ref[:, pl.ds(off, tk)] = lax.bitcast_convert_type(am, I32)
            return carry

        lax.fori_loop(0, n_steps, mask_body, 0)

    @pl.when(ties)
    def _():
        above = jnp.where(thr == jnp.int32(2 ** 31 - 1), 0, count_ge(jnp.minimum(thr, jnp.int32(2 ** 31 - 2)) + 1))
        need_w = lanes_to_rows(topk - above).astype(F32)
        r = lax.broadcasted_iota(I32, (tk, tk), 0)
        c = lax.broadcasted_iota(I32, (tk, tk), 1)
        before = jnp.where(r < c, 1.0, 0.0).astype(BF16)
        ones_w = jnp.ones((tk, tk), BF16)

        def tie_body(kc, seen):
            off = pl.multiple_of(kc * tk, tk)
            k = key_ref[:, pl.ds(off, tk)]
            eq = k == thr_w
            eq_b = jnp.where(eq, 1.0, 0.0).astype(BF16)
            rank = seen + jnp.dot(eq_b, before, preferred_element_type=F32)
            keep = (k > thr_w) | (eq & (rank < need_w))
            key_ref[:, pl.ds(off, tk)] = lax.bitcast_convert_type(jnp.where(keep, 0.0, MASK_VALUE).astype(F32), I32)
            return seen + jnp.dot(eq_b, ones_w, preferred_element_type=F32)

        lax.fori_loop(0, n_steps, tie_body, jnp.zeros((t, tk), F32))

    streams = ql_ref.shape[0]
    split = streams == 1
    hs = hb // streams
    for h in range(hb):
        qh = sq_ref[:, h * HEAD_DIM_DSA:(h + 1) * HEAD_DIM_DSA]
        ql = jnp.dot(qh, wuk_ref[h], preferred_element_type=F32) * scale
        ql_ref[h // hs, (h % hs) * t:(h % hs + 1) * t, :] = ql.astype(BF16)
    for g in range(streams):
        _softmax_init(m_ref.at[g], l_ref.at[g], acc_ref.at[g])

    def chunk(kc, width, delta):
        off = pl.multiple_of(kc * tk, tk)
        c = c_ref[pl.ds(off, width), :]
        am = lax.bitcast_convert_type(key_ref[:, pl.ds(off, width)], F32)
        for g in range(streams):
            s = (_dot_nt(ql_ref[g], c, split).reshape(hs, t, width) + am[None]).reshape(hs * t, width)
            s_ref[g, :, 0:width] = s
            smax = None
            if delta is None:
                smax = _lane_chunk_max(s)
            else:
                _near_bias(s_ref.at[g], lambda j: bias_ref[0, g * hs + j], lambda j: bias_ref[1, g * hs + j],
                           delta, hs, t, width)
            _softmax_step(s_ref.at[g], width, c, m_ref.at[g], l_ref.at[g], acc_ref.at[g], smax, split)

    def far_body(kc, carry):
        chunk(kc, tk, None)
        return carry

    lax.fori_loop(0, jnp.maximum((qi - 1) // 2, 0), far_body, 0)
    half = qi // 2

    @pl.when(qi % 2 == 1)
    def _():
        chunk(half, tk, t // LANES)

    @pl.when(qi % 2 == 0)
    def _():
        @pl.when(half >= 1)
        def _():
            chunk(half - 1, tk, tk // LANES)
        chunk(half, t, 0)

    for h in range(hb):
        g, j = h // hs, h % hs
        inv_l = 1.0 / jnp.sum(l_ref[g, j * t:(j + 1) * t, :], axis=-1, keepdims=True)
        ol = (acc_ref[g, j * t:(j + 1) * t, :] * inv_l).astype(BF16)
        o = jnp.dot(ol, wuv_ref[h], preferred_element_type=F32)
        o_ref[:, h * HEAD_DIM_DSA:(h + 1) * HEAD_DIM_DSA] = o.astype(o_ref.dtype)


def _dsa_attention(proj, kv_g, w_ukt, w_uv, bias_blocks, batch, seq, topk):
    t = DSA_BLOCK
    nq = seq // t
    hb, hi = N_HEADS_DSA, N_HEADS_IDX
    width = hb * HEAD_DIM_DSA
    ns = DSA_STREAMS
    rs = hb // ns * t
    kernel = functools.partial(_dsa_kernel, topk=topk, scale=HEAD_DIM_DSA ** -0.5 * LOG2E)
    return pl.pallas_call(
        kernel,
        out_shape=jax.ShapeDtypeStruct((batch * seq, width), BF16),
        grid=(batch, nq),
        in_specs=[pl.BlockSpec((t, 1024), lambda b, qi: (b * nq + qi, COL_IQ // 1024)),
                  pl.BlockSpec((t, 1024), lambda b, qi: (b * nq + qi, COL_SQ // 1024)),
                  pl.BlockSpec((t, LANES), lambda b, qi: (b * nq + qi, COL_IW // LANES)),
                  pl.BlockSpec((seq, LANES), lambda b, qi: (b, COL_IK // LANES)),
                  pl.BlockSpec((seq, KV_LATENT), lambda b, qi: (b, COL_CKV // KV_LATENT)),
                  pl.BlockSpec((1, KV_LATENT), lambda b, qi: (0, 0)),
                  pl.BlockSpec((hb, HEAD_DIM_DSA, KV_LATENT), lambda b, qi: (0, 0, 0)),
                  pl.BlockSpec((hb, KV_LATENT, HEAD_DIM_DSA), lambda b, qi: (0, 0, 0)),
                  pl.BlockSpec((2, hb, LANES, LANES), lambda b, qi: (0, 0, 0, 0))],
        out_specs=pl.BlockSpec((t, width), lambda b, qi: (b * nq + qi, 0)),
        scratch_shapes=[pltpu.VMEM((seq, KV_LATENT), BF16),
                        pltpu.VMEM((t, seq), I32),
                        pltpu.VMEM((seq, t), I32),
                        pltpu.VMEM((seq, t), jnp.int16),
                        pltpu.VMEM((t // BF16_ROWS, hi, BF16_ROWS, LANES), BF16),
                        pltpu.VMEM((t // BF16_ROWS, hi, BF16_ROWS, LANES), F32),
                        pltpu.VMEM((ns, rs, KV_LATENT), BF16),
                        pltpu.VMEM((ns, rs, KEY_CHUNK), F32),
                        pltpu.VMEM((ns, rs, LANES), F32),
                        pltpu.VMEM((ns, rs, LANES), F32),
                        pltpu.VMEM((ns, rs, KV_LATENT), F32)],
        compiler_params=_cparams(("parallel", "arbitrary")),
        name="dsa_attn",
    )(proj, proj, proj, proj, proj, kv_g, w_ukt, w_uv, bias_blocks)


def _pack_halves(x):
    c = x.shape[1] // 2
    lo = lax.bitcast_convert_type(x[:, :c].astype(BF16).astype(F32), I32)
    hi = lax.bitcast_convert_type(x[:, c:].astype(BF16).astype(F32), I32)
    return lax.shift_right_logical(lo, 16) | (hi & jnp.int32(-65536))


def _unpack_halves(w):
    lo = lax.bitcast_convert_type(lax.shift_left(w, 16), F32)
    hi = lax.bitcast_convert_type(w & jnp.int32(-65536), F32)
    return lo, hi


def _merge_kernel(x_ref, yd_ref, ys_ref, ga_ref, gb_ref, wd_ref, ws_ref, wo_ref, g_ref, wrh_ref, wrl_ref, br_ref,
                  h_ref, hn_ref, route_ref):
    bd = _dot_split(yd_ref[...], wd_ref[...])
    bs = _dot_split(ys_ref[...], ws_ref[...])
    merged = (jax.nn.sigmoid(ga_ref[...].astype(F32)) * bd + jax.nn.sigmoid(gb_ref[...].astype(F32)) * bs)
    h = x_ref[...] + _dot_split(merged.astype(BF16), wo_ref[...])
    h_ref[...] = h
    ms = jnp.mean(h * h, axis=-1, keepdims=True)
    hn = h * lax.rsqrt(ms + EPS) * g_ref[...]
    hn_ref[...] = _pack_halves(hn)

    hn_hi = hn.astype(BF16)
    hn_lo = (hn - hn_hi.astype(F32)).astype(BF16)
    logits = (_dot_split(hn_hi, wrh_ref[...]) + _dot_split(hn_lo, wrh_ref[...])
              + _dot_split(hn_hi, wrl_ref[...]))
    logits = logits + br_ref[...]
    lane = lax.broadcasted_iota(I32, logits.shape, 1)
    vals, ids = [], []
    for _ in range(TOP_K_EXPERTS):
        mx = jnp.max(logits, axis=-1, keepdims=True)
        ix = jnp.min(jnp.where(logits == mx, lane, LANES), axis=-1, keepdims=True)
        vals.append(mx)
        ids.append(ix)
        logits = jnp.where(lane == ix, -jnp.inf, logits)
    es = [jnp.exp(v - vals[0]) for v in vals]
    inv = 1.0 / (es[0] + es[1] + es[2] + es[3])
    route = jnp.zeros(logits.shape, F32)
    for k in range(TOP_K_EXPERTS):
        route = jnp.where(lane == k, es[k] * inv, route)
        route = jnp.where(lane == TOP_K_EXPERTS + k, ids[k].astype(F32), route)
    route_ref[...] = route


def _merge(x2, y_diff, y_dsa, proj, w_bd, w_bs, w_out, g_ffn, w_router_hi, w_router_lo, b_router, tm):
    n, d = x2.shape
    row = lambda i: (i, 0)
    const = lambda i: (0, 0)
    return pl.pallas_call(
        _merge_kernel,
        out_shape=(jax.ShapeDtypeStruct((n, d), F32),
                   jax.ShapeDtypeStruct((n, d // 2), I32),
                   jax.ShapeDtypeStruct((n, LANES), F32)),
        grid=(n // tm,),
        in_specs=[pl.BlockSpec((tm, d), row),
                  pl.BlockSpec((tm, d), row),
                  pl.BlockSpec((tm, d), row),
                  pl.BlockSpec((tm, d), lambda i: (i, COL_GA // 1024)),
                  pl.BlockSpec((tm, d), lambda i: (i, COL_GB // 1024)),
                  pl.BlockSpec((d, d), const),
                  pl.BlockSpec((d, d), const),
                  pl.BlockSpec((d, d), const),
                  pl.BlockSpec((1, d), const),
                  pl.BlockSpec((d, LANES), const),
                  pl.BlockSpec((d, LANES), const),
                  pl.BlockSpec((1, LANES), const)],
        out_specs=(pl.BlockSpec((tm, d), row),
                   pl.BlockSpec((tm, d // 2), row),
                   pl.BlockSpec((tm, LANES), row)),
        compiler_params=_cparams(("parallel",)),
        name="merge_router",
    )(x2, y_diff, y_dsa, proj, proj, w_bd, w_bs, w_out, g_ffn, w_router_hi, w_router_lo, b_router)


def _regroup_kernel(w_ref, p_ref, o_ref):
    pw = p_ref.shape[0]
    for j in range(w_ref.shape[2] // pw):
        w = w_ref[0, :, j * pw:(j + 1) * pw].astype(BF16)
        o_ref[0, :, j * pw:(j + 1) * pw] = jnp.dot(w, p_ref[...], preferred_element_type=F32).astype(BF16)


def _regroup_gate_up(w_gu, rows):
    e, d, f2 = w_gu.shape
    pw = 2 * LANES
    src = jnp.arange(pw, dtype=I32)
    dst = (src % 2) * LANES + src // 2
    perm = (dst[:, None] == jnp.arange(pw, dtype=I32)[None, :]).astype(BF16)
    return pl.pallas_call(
        _regroup_kernel,
        out_shape=jax.ShapeDtypeStruct((e, d, f2), BF16),
        grid=(e, d // rows),
        in_specs=[pl.BlockSpec((1, rows, f2), lambda i, j: (i, j, 0)),
                  pl.BlockSpec((pw, pw), lambda i, j: (0, 0))],
        out_specs=pl.BlockSpec((1, rows, f2), lambda i, j: (i, j, 0)),
        compiler_params=_cparams(("parallel", "parallel")),
        name="regroup_gate_up",
    )(w_gu, perm)


def _ffn_kernel(be_ref, nu_ref, x_ref, wgu_ref, wd_ref, bgu_ref, bd_ref, o_ref):
    @pl.when(pl.program_id(0) < nu_ref[0])
    def _():
        x_lo, x_hi = _unpack_halves(x_ref[...])
        half = x_lo.shape[1]
        gu = (jnp.dot(x_lo.astype(BF16), wgu_ref[0, 0:half, :], preferred_element_type=F32)
              + jnp.dot(x_hi.astype(BF16), wgu_ref[0, half:2 * half, :], preferred_element_type=F32)
              + bgu_ref[0])
        acts = []
        for j in range(gu.shape[1] // (2 * LANES)):
            gate = jnp.minimum(gu[:, 2 * j * LANES:(2 * j + 1) * LANES], SWIGLU_LIMIT)
            up = jnp.clip(gu[:, (2 * j + 1) * LANES:(2 * j + 2) * LANES], -SWIGLU_LIMIT, SWIGLU_LIMIT)
            glu = gate * jax.nn.sigmoid(gate * SWIGLU_ALPHA)
            acts.append(((up + 1.0) * glu).astype(BF16))
        a = jnp.concatenate(acts, axis=1)
        y = jnp.dot(a, wd_ref[0].astype(BF16), preferred_element_type=F32) + bd_ref[0]
        o_ref[...] = _pack_halves(y)

    @pl.when(pl.program_id(0) >= nu_ref[0])
    def _():
        o_ref[...] = jnp.zeros(o_ref.shape, o_ref.dtype)


def _expert_ffn(blk_exp, n_used, xs, wgu, wd, bgu, bd):
    p, dw = xs.shape
    f, d = wd.shape[1], wd.shape[2]
    nblk = p // MOE_ROWS
    wmap = lambda i, be, nu: (be[i], 0, 0)
    grid_spec = pltpu.PrefetchScalarGridSpec(
        num_scalar_prefetch=2,
        grid=(nblk,),
        in_specs=[pl.BlockSpec((MOE_ROWS, dw), lambda i, be, nu: (i, 0)),
                  pl.BlockSpec((1, d, 2 * f), wmap),
                  pl.BlockSpec((1, f, d), wmap),
                  pl.BlockSpec((1, 1, 2 * f), wmap),
                  pl.BlockSpec((1, 1, d), wmap)],
        out_specs=pl.BlockSpec((MOE_ROWS, dw), lambda i, be, nu: (i, 0)),
    )
    return pl.pallas_call(
        _ffn_kernel,
        out_shape=jax.ShapeDtypeStruct((p, dw), I32),
        grid_spec=grid_spec,
        compiler_params=_cparams(("arbitrary",)),
        name="expert_ffn",
    )(blk_exp, n_used, xs, wgu, wd, bgu, bd)


def _route_kernel(route_ref, dest_ref, cnt_ref, u_ref, carry_ref, pstart_ref, *, block_rows):
    ph, i = pl.program_id(0), pl.program_id(1)
    tm = route_ref.shape[0]

    @pl.when((ph == 0) & (i == 0))
    def _():
        r = lax.broadcasted_iota(I32, (tm, tm), 0)
        c = lax.broadcasted_iota(I32, (tm, tm), 1)
        u_ref[...] = jnp.where(r < c, 1.0, 0.0).astype(BF16)
        carry_ref[...] = jnp.zeros(carry_ref.shape, F32)

    @pl.when((ph == 1) & (i == 0))
    def _():
        counts = carry_ref[...]
        cnt_ref[...] = counts
        padded = jnp.ceil(counts * (1.0 / block_rows)) * block_rows
        r = lax.broadcasted_iota(I32, (LANES, LANES), 0)
        c = lax.broadcasted_iota(I32, (LANES, LANES), 1)
        lower = jnp.where(c < r, 1.0, 0.0).astype(F32)
        pstart_ref[...] = jnp.dot(lower, padded, preferred_element_type=F32, precision=lax.Precision.HIGHEST)
        carry_ref[...] = jnp.zeros(carry_ref.shape, F32)

    rt = route_ref[...].T
    sub = lax.broadcasted_iota(I32, (LANES, tm), 0)
    hits = [sub == rt[TOP_K_EXPERTS + k:TOP_K_EXPERTS + k + 1, :].astype(I32) for k in range(TOP_K_EXPERTS)]
    m = jnp.zeros((LANES, tm), F32)
    for hit in hits:
        m = m + jnp.where(hit, 1.0, 0.0)
    tile_counts = jnp.broadcast_to(jnp.sum(m, axis=1, keepdims=True), (LANES, LANES))

    @pl.when(ph == 0)
    def _():
        dest_ref[...] = jnp.zeros(dest_ref.shape, I32)

    @pl.when(ph == 1)
    def _():
        prefix = jnp.dot(m.astype(BF16), u_ref[...], preferred_element_type=F32)
        rank = prefix + (pstart_ref[:, 0:1] + carry_ref[:, 0:1])
        rows = [jnp.sum(jnp.where(hit, rank, 0.0), axis=0, keepdims=True) for hit in hits]
        rows.append(jnp.zeros((dest_ref.shape[0] - TOP_K_EXPERTS, tm), F32))
        dest_ref[...] = jnp.concatenate(rows, axis=0).astype(I32)

    carry_ref[...] = carry_ref[...] + tile_counts


def _route_rows(route, tm):
    n = route.shape[0]
    nt = n // tm
    kernel = functools.partial(_route_kernel, block_rows=MOE_ROWS)
    return pl.pallas_call(
        kernel,
        out_shape=(jax.ShapeDtypeStruct((8, n + tm), I32), jax.ShapeDtypeStruct((LANES, LANES), F32)),
        grid=(2, nt),
        in_specs=[pl.BlockSpec((tm, LANES), lambda ph, i: (i, 0))],
        out_specs=(pl.BlockSpec((8, tm), lambda ph, i: (0, ph * i + (1 - ph) * nt)),
                   pl.BlockSpec((LANES, LANES), lambda ph, i: (0, 0))),
        scratch_shapes=[pltpu.VMEM((tm, tm), BF16),
                        pltpu.VMEM((LANES, LANES), F32),
                        pltpu.VMEM((LANES, LANES), F32)],
        compiler_params=_cparams(("arbitrary", "arbitrary")),
        name="route_rows",
    )(route)


SC_WINDOW = 128
SC_WORKERS = 32


def _sc_mesh():
    return plsc.VectorSubcoreMesh(core_axis_name="c", subcore_axis_name="s")


def _sc_scatter_rows(src, dest, p):
    n, d = src.shape
    assert n % (SC_WINDOW * SC_WORKERS) == 0, "token count must split evenly over the vector subcores"
    per = n // (SC_WINDOW * SC_WORKERS)

    @pl.kernel(out_type=jax.ShapeDtypeStruct((p, d), src.dtype), mesh=_sc_mesh(),
               scratch_types=[pltpu.VMEM((dest.shape[0], SC_WINDOW), I32), pltpu.VMEM((SC_WINDOW, d), src.dtype)])
    def scatter(src_hbm, idx_hbm, out_hbm, idx_vmem, buf):
        wid = lax.axis_index("c") * (SC_WORKERS // 2) + lax.axis_index("s")

        @pl.loop(0, per)
        def _(j):
            off = (wid * per + j) * SC_WINDOW
            pltpu.sync_copy(idx_hbm.at[:, pl.ds(off, SC_WINDOW)], idx_vmem)
            pltpu.sync_copy(src_hbm.at[pl.ds(off, SC_WINDOW), :], buf)
            for k in range(TOP_K_EXPERTS):
                pltpu.sync_copy(buf, out_hbm.at[idx_vmem.at[k]])

    return scatter(src, dest)


def _sc_gather_rows(src, dest):
    n = dest.shape[1]
    d = src.shape[1]
    assert n % (SC_WINDOW * SC_WORKERS) == 0, "token count must split evenly over the vector subcores"
    per = n // (SC_WINDOW * SC_WORKERS)

    @pl.kernel(out_type=jax.ShapeDtypeStruct((TOP_K_EXPERTS * n, d), src.dtype), mesh=_sc_mesh(),
               scratch_types=[pltpu.VMEM((dest.shape[0], SC_WINDOW), I32), pltpu.VMEM((SC_WINDOW, d), src.dtype)])
    def gather(src_hbm, idx_hbm, out_hbm, idx_vmem, buf):
        wid = lax.axis_index("c") * (SC_WORKERS // 2) + lax.axis_index("s")

        @pl.loop(0, per)
        def _(j):
            off = (wid * per + j) * SC_WINDOW
            pltpu.sync_copy(idx_hbm.at[:, pl.ds(off, SC_WINDOW)], idx_vmem)
            for k in range(TOP_K_EXPERTS):
                pltpu.sync_copy(src_hbm.at[idx_vmem.at[k]], buf)
                pltpu.sync_copy(buf, out_hbm.at[pl.ds(k * n + off, SC_WINDOW), :])

    return gather(src, dest)


def _combine_kernel(h_ref, y_ref, route_ref, g_ref, o_ref):
    half = h_ref.shape[1] // 2
    h_lo, h_hi = h_ref[:, 0:half], h_ref[:, half:2 * half]
    route = route_ref[...]
    for k in range(TOP_K_EXPERTS):
        y_lo, y_hi = _unpack_halves(y_ref[k])
        gate = route[:, k:k + 1]
        h_lo = h_lo + gate * y_lo
        h_hi = h_hi + gate * y_hi
    ms = (jnp.sum(h_lo * h_lo, axis=-1, keepdims=True)
          + jnp.sum(h_hi * h_hi, axis=-1, keepdims=True)) * (1.0 / (2 * half))
    inv = lax.rsqrt(ms + EPS)
    o_ref[:, 0:half] = h_lo * inv * g_ref[:, 0:half]
    o_ref[:, half:2 * half] = h_hi * inv * g_ref[:, half:2 * half]


def _combine(h1, yg, route, g_final, tm):
    n, d = h1.shape
    return pl.pallas_call(
        _combine_kernel,
        out_shape=jax.ShapeDtypeStruct((n, d), F32),
        grid=(n // tm,),
        in_specs=[pl.BlockSpec((tm, d), lambda i: (i, 0)),
                  pl.BlockSpec((TOP_K_EXPERTS, tm, d // 2), lambda i: (0, i, 0)),
                  pl.BlockSpec((tm, LANES), lambda i: (i, 0)),
                  pl.BlockSpec((1, d), lambda i: (0, 0))],
        out_specs=pl.BlockSpec((tm, d), lambda i: (i, 0)),
        compiler_params=_cparams(("parallel",)),
        name="combine_norm",
    )(h1, yg, route, g_final)


def _t5_bucket(dist):
    n = jnp.maximum(dist, 0)
    max_exact = N_BUCKETS // 2
    nf = jnp.maximum(n, 1).astype(F32)
    large = max_exact + (jnp.log(nf / max_exact) / math.log(MAX_DISTANCE / max_exact)
                         * (N_BUCKETS - max_exact)).astype(I32)
    large = jnp.minimum(large, N_BUCKETS - 1)
    return jnp.where(n < max_exact, n, large)


def _bias_blocks(bias_tab):
    t = LANES
    assert MAX_DISTANCE <= LANES
    r = jnp.arange(t, dtype=I32)[:, None]
    c = jnp.arange(t, dtype=I32)[None, :]
    rel = (bias_tab - bias_tab[N_BUCKETS - 1][None, :]).astype(F32)
    tiles = []
    buckets = jnp.arange(N_BUCKETS, dtype=I32)[:, None, None]
    for delta in (0, t):
        dist = r - c + delta
        hit = _t5_bucket(dist)[None] == buckets
        b = jnp.sum(jnp.where(hit[:, None], rel[:, :, None, None], 0.0), axis=0)
        tiles.append(jnp.where((dist >= 0)[None], b * LOG2E, MASK_VALUE))
    return jnp.stack(tiles)


def _regroup_w_in(w_in):
    sizes = (1024, 1024, 1024, 1024, KV_LATENT, 1024, HEAD_DIM_IDX, N_HEADS_IDX, D_MODEL, D_MODEL)
    parts, off = [], 0
    for sz in sizes:
        parts.append(w_in[:, off:off + sz])
        off += sz
    dq, dk, dv, sq, ckv, iq, ik, iw, ga, gb = parts
    dq = dq * (HEAD_DIM_DIFF ** -0.5 * LOG2E)
    iw = iw * ((N_HEADS_IDX ** -0.5) * (HEAD_DIM_IDX ** -0.5))
    pad = jnp.zeros((w_in.shape[0], PROJ_WIDTH - COL_IW - N_HEADS_IDX), w_in.dtype)
    w = jnp.concatenate([dq, dk, dv, sq, iq, ga, gb, ckv, ik, ik, iw, pad], axis=1)
    return w.astype(BF16)


def _block_tables(counts, n_assign):
    e, bm = N_EXPERTS, MOE_ROWS
    padded = (counts + bm - 1) // bm * bm
    pends = jnp.cumsum(padded)
    nblk = -(-(n_assign + e * (bm - 1)) // bm)
    first_row = jnp.arange(nblk, dtype=I32) * bm
    blk_exp = jnp.minimum(jnp.sum((pends[None, :] <= first_row[:, None]).astype(I32), axis=1), e - 1)
    n_used = (pends[-1] // bm).astype(I32).reshape(1)
    return blk_exp, n_used, nblk


def kernel(x, norm_attn_g, w_in, rel_bias, lam_q1, lam_k1, lam_q2, lam_k2, diff_subln_g, kv_norm_g, w_uk, w_uv,
           w_branch_diff, w_branch_dsa, w_out, norm_ffn_g, w_router, b_router, w_gate_up, b_gate_up, w_down,
           b_down, norm_final_g):
    batch, seq, d = x.shape
    n = batch * seq
    assert norm_attn_g.shape[0] == 1, "single-layer kernel"
    assert seq % DIFF_BLOCK == 0 and seq % KEY_CHUNK == 0 and d == D_MODEL
    assert seq <= (2 ** 15 - 1) * BF16_ROWS, "int16 per-element key counts"
    row_tile = math.gcd(n, ROW_TILE)
    token_tile = math.gcd(n, TOKEN_TILE)

    x2 = x.reshape(n, d)
    proj = _inproj(x2, norm_attn_g[0].reshape(1, d), _regroup_w_in(w_in[0]), row_tile, PROJ_COLS)

    lam_init = 0.8 - 0.6 * math.exp(-0.3 * 0)
    lam = (jnp.exp(jnp.sum(lam_q1[0].astype(F32) * lam_k1[0].astype(F32)))
           - jnp.exp(jnp.sum(lam_q2[0].astype(F32) * lam_k2[0].astype(F32))) + lam_init)
    y_diff = _diff_attention(proj, lam.reshape(1, 1).astype(F32), _bias_blocks(rel_bias[:, :N_HEADS_DIFF]),
                             diff_subln_g[0].reshape(1, -1).astype(F32), batch, seq, 1.0 - lam_init)

    y_dsa = _dsa_attention(proj, kv_norm_g[0].reshape(1, -1).astype(F32),
                           w_uk[0].transpose(0, 2, 1).astype(BF16), w_uv[0].astype(BF16),
                           _bias_blocks(rel_bias[:, N_HEADS_DIFF:]), batch, seq, min(TOPK_MAX, seq // 4))

    w_r = jnp.zeros((d, LANES), F32).at[:, :N_EXPERTS].set(w_router[0].astype(F32))
    b_r = jnp.full((1, LANES), MASK_VALUE, F32).at[0, :N_EXPERTS].set(b_router[0].astype(F32))
    h1, hn, route = _merge(x2, y_diff, y_dsa, proj, w_branch_diff[0].astype(BF16), w_branch_dsa[0].astype(BF16),
                           w_out[0].astype(BF16), norm_ffn_g[0].reshape(1, d).astype(F32),
                           w_r.astype(BF16), (w_r - w_r.astype(BF16).astype(F32)).astype(BF16), b_r,
                           token_tile)

    dest, counts = _route_rows(route, row_tile)
    dest = dest[:, :n]
    blk_exp, n_used, nblk = _block_tables(counts[:N_EXPERTS, 0].astype(I32), n * TOP_K_EXPERTS)
    xs = _sc_scatter_rows(hn, dest, nblk * MOE_ROWS)
    e, f = N_EXPERTS, D_EXPERT
    b_gu = b_gate_up[0].astype(F32).reshape(e, f // LANES, LANES, 2).transpose(0, 1, 3, 2).reshape(e, 1, 2 * f)
    ys = _expert_ffn(blk_exp, n_used, xs, _regroup_gate_up(w_gate_up[0], TOKEN_TILE), w_down[0],
                     b_gu, b_down[0][:, None, :].astype(F32))
    yg = _sc_gather_rows(ys, dest).reshape(TOP_K_EXPERTS, n, d // 2)
    out = _combine(h1, yg, route, norm_final_g.reshape(1, d).astype(F32), token_tile)
    return out.reshape(batch, seq, d)
```

```python
import functools
import math

import jax
import jax.numpy as jnp
from jax import lax
from jax.experimental import pallas as pl
from jax.experimental.pallas import tpu as pltpu
from jax.experimental.pallas import tpu_sc as plsc

F32 = jnp.float32
BF16 = jnp.bfloat16
I32 = jnp.int32

D_MODEL = 1024
N_HEADS_DIFF = 8
HEAD_DIM_DIFF = 64
N_HEADS_DSA = 8
HEAD_DIM_DSA = 128
KV_LATENT = 256
N_HEADS_IDX = 16
HEAD_DIM_IDX = 64
TOPK_MAX = 256
N_BUCKETS = 32
MAX_DISTANCE = 128
N_EXPERTS = 32
TOP_K_EXPERTS = 4
D_EXPERT = 1024
SWIGLU_LIMIT = 7.0
SWIGLU_ALPHA = 1.702
EPS = 1e-6

LANES = 128
PACKED_ROWS = 16
ROW_TILE = 1024
PROJ_COLS = 1280
TOKEN_TILE = 512
DIFF_BLOCK = 512
DIFF_HEADS_PER_STEP = 4
DSA_BLOCK = 256
KEY_CHUNK = 512
DSA_STREAMS = 2
MOE_ROWS = 512
PROJ_WIDTH = 7680
VMEM_LIMIT = 56 * 1024 * 1024

COL_DQ, COL_DK, COL_DV, COL_SQ, COL_IQ, COL_GA, COL_GB = (i * 1024 for i in range(7))
COL_CKV = 7168
COL_IK = 7424
COL_IW = 7552

LOG2E = math.log2(math.e)
MASK_VALUE = -1e30
M_INIT = -1e29
INT_MIN = -2 ** 31


def _cparams(sem):
    return pltpu.CompilerParams(dimension_semantics=sem, vmem_limit_bytes=VMEM_LIMIT)


def _inproj_kernel(x_ref, g_ref, w_ref, o_ref, xn_ref):
    @pl.when(pl.program_id(1) == 0)
    def _():
        x = x_ref[...]
        ms = jnp.mean(x * x, axis=-1, keepdims=True)
        xn_ref[...] = (x * lax.rsqrt(ms + EPS) * g_ref[...]).astype(BF16)

    o_ref[...] = jnp.dot(xn_ref[...], w_ref[...], preferred_element_type=F32).astype(o_ref.dtype)


def _inproj(x2, g, w, tm, tn):
    n, d = x2.shape
    width = w.shape[1]
    return pl.pallas_call(
        _inproj_kernel,
        out_shape=jax.ShapeDtypeStruct((n, width), BF16),
        grid=(n // tm, width // tn),
        in_specs=[pl.BlockSpec((tm, d), lambda i, j: (i, 0)),
                  pl.BlockSpec((1, d), lambda i, j: (0, 0)),
                  pl.BlockSpec((d, tn), lambda i, j: (0, j))],
        out_specs=pl.BlockSpec((tm, tn), lambda i, j: (i, j)),
        scratch_shapes=[pltpu.VMEM((tm, d), BF16)],
        compiler_params=_cparams(("parallel", "arbitrary")),
        name="inproj",
    )(x2, g, w)


def _lane_chunk_max(s):
    smax = s[:, 0:LANES]
    for j in range(1, s.shape[1] // LANES):
        smax = jnp.maximum(smax, s[:, j * LANES:(j + 1) * LANES])
    return smax


def _softmax_step(s_ref, tk, v, m_ref, l_ref, acc_ref, smax=None, split=True):
    nl = tk // LANES
    if smax is None:
        smax = _lane_chunk_max(s_ref[:, 0:tk])
    m_prev = m_ref[...]
    m_new = jnp.maximum(m_prev, jnp.max(smax, axis=-1, keepdims=True))
    alpha = jnp.exp2(m_prev - m_new)
    psum = None
    ps = []
    for j in range(nl):
        pj = jnp.exp2(s_ref[:, j * LANES:(j + 1) * LANES] - m_new)
        psum = pj if psum is None else psum + pj
        ps.append(pj.astype(BF16))
    l_ref[...] = alpha * l_ref[...] + psum
    pv = _dot_split(jnp.concatenate(ps, axis=1), v, split)
    e = acc_ref.shape[1]
    a = alpha if e == LANES else jnp.concatenate([alpha] * (e // LANES), axis=1)
    acc_ref[...] = a * acc_ref[...] + pv
    m_ref[...] = m_new


def _softmax_init(m_ref, l_ref, acc_ref):
    m_ref[...] = jnp.full(m_ref.shape, M_INIT, F32)
    l_ref[...] = jnp.zeros(l_ref.shape, F32)
    acc_ref[...] = jnp.zeros(acc_ref.shape, F32)


def _softmax_result(l_ref, acc_ref):
    return acc_ref[...] * (1.0 / jnp.sum(l_ref[...], axis=-1, keepdims=True))


def _near_bias(s_ref, d0, d1, delta, groups, t, tk):
    for g in range(groups):
        for rb in range(t // LANES):
            for cb in range(tk // LANES):
                bd = delta + rb - cb
                rows = slice(g * t + rb * LANES, g * t + (rb + 1) * LANES)
                cols = slice(cb * LANES, (cb + 1) * LANES)
                if bd == 0:
                    s_ref[rows, cols] = s_ref[rows, cols] + d0(g)
                elif bd == 1:
                    s_ref[rows, cols] = s_ref[rows, cols] + d1(g)
                elif bd < 0:
                    s_ref[rows, cols] = jnp.full((LANES, LANES), MASK_VALUE, F32)


def _dot_nt(a, b, split=True):
    dn = (((1,), (1,)), ((), ()))
    if not split:
        return lax.dot_general(a, b, dn, preferred_element_type=F32)
    h = a.shape[0] // 2
    return jnp.concatenate([lax.dot_general(a[:h], b, dn, preferred_element_type=F32),
                            lax.dot_general(a[h:], b, dn, preferred_element_type=F32)], axis=0)


def _dot_split(a, b, split=True):
    if not split:
        return jnp.dot(a, b, preferred_element_type=F32)
    h = a.shape[0] // 2
    return jnp.concatenate([jnp.dot(a[:h], b, preferred_element_type=F32),
                            jnp.dot(a[h:], b, preferred_element_type=F32)], axis=0)


def _diff_kernel(lam_ref, q_ref, k_ref, v_ref, bias_ref, g_ref, o_ref, q2_ref, s_ref, m_ref, l_ref, acc_ref,
                 *, out_scale, heads):
    t = q_ref.shape[0]
    e = 2 * HEAD_DIM_DIFF
    qi = pl.program_id(2)
    cols = [slice(hh * e, (hh + 1) * e) for hh in range(heads)]

    lane = lax.broadcasted_iota(I32, (t, e), 1)
    for hh in range(heads):
        q = q_ref[:, cols[hh]]
        zero = jnp.zeros_like(q)
        q2_ref[hh, 0:t, :] = jnp.where(lane < HEAD_DIM_DIFF, q, zero)
        q2_ref[hh, t:2 * t, :] = jnp.where(lane >= HEAD_DIM_DIFF, q, zero)
        _softmax_init(m_ref.at[hh], l_ref.at[hh], acc_ref.at[hh])

    def chunk(kc, delta):
        off = pl.multiple_of(kc * t, t)
        for hh in range(heads):
            s = _dot_nt(q2_ref[hh], k_ref[pl.ds(off, t), cols[hh]])
            s_ref[hh] = s
            smax = None
            if delta is None:
                smax = _lane_chunk_max(s)
            else:
                _near_bias(s_ref.at[hh], lambda g: bias_ref[0, hh], lambda g: bias_ref[1, hh], delta, 2, t, t)
            _softmax_step(s_ref.at[hh], t, v_ref[pl.ds(off, t), cols[hh]], m_ref.at[hh], l_ref.at[hh],
                          acc_ref.at[hh], smax)

    def far_body(kc, carry):
        chunk(kc, None)
        return carry

    lax.fori_loop(0, jnp.maximum(qi - 1, 0), far_body, 0)

    @pl.when(qi >= 1)
    def _():
        chunk(qi - 1, t // LANES)

    chunk(qi, 0)

    for hh in range(heads):
        o = _softmax_result(l_ref.at[hh], acc_ref.at[hh])
        o = o[0:t, :] - lam_ref[0, 0] * o[t:2 * t, :]
        ms = jnp.mean(o * o, axis=-1, keepdims=True)
        o_ref[:, cols[hh]] = (o * lax.rsqrt(ms + EPS) * g_ref[...] * out_scale).astype(o_ref.dtype)


def _diff_attention(proj, lam, bias_blocks, subln_g, batch, seq, out_scale):
    t = DIFF_BLOCK
    nq = seq // t
    h = N_HEADS_DIFF
    e = 2 * HEAD_DIM_DIFF
    hp = DIFF_HEADS_PER_STEP
    w = hp * e
    kernel = functools.partial(_diff_kernel, out_scale=out_scale, heads=hp)
    return pl.pallas_call(
        kernel,
        out_shape=jax.ShapeDtypeStruct((batch * seq, h * e), BF16),
        grid=(batch, h // hp, nq),
        in_specs=[pl.BlockSpec(memory_space=pltpu.SMEM),
                  pl.BlockSpec((t, w), lambda b, hh, qi: (b * nq + qi, COL_DQ // w + hh)),
                  pl.BlockSpec((seq, w), lambda b, hh, qi: (b, COL_DK // w + hh)),
                  pl.BlockSpec((seq, w), lambda b, hh, qi: (b, COL_DV // w + hh)),
                  pl.BlockSpec((2, hp, LANES, LANES), lambda b, hh, qi: (0, hh, 0, 0)),
                  pl.BlockSpec((1, e), lambda b, hh, qi: (0, 0))],
        out_specs=pl.BlockSpec((t, w), lambda b, hh, qi: (b * nq + qi, hh)),
        scratch_shapes=[pltpu.VMEM((hp, 2 * t, e), BF16),
                        pltpu.VMEM((hp, 2 * t, t), F32),
                        pltpu.VMEM((hp, 2 * t, LANES), F32),
                        pltpu.VMEM((hp, 2 * t, LANES), F32),
                        pltpu.VMEM((hp, 2 * t, e), F32)],
        compiler_params=_cparams(("parallel", "parallel", "arbitrary")),
        name="diff_attn",
    )(lam, proj, proj, proj, bias_blocks, subln_g)


def _sortable_key(x):
    bits = lax.bitcast_convert_type(x, I32)
    return bits ^ ((bits >> 31) & jnp.int32(0x7FFFFFFF))


def _dsa_kernel(iq_ref, sq_ref, iw_ref, ik_ref, ckv_ref, kvg_ref, wuk_ref, wuv_ref, bias_ref, o_ref,
                c_ref, key_ref, keyt_ref, keyh_ref, qi_ref, wb_ref, ql_ref, s_ref, m_ref, l_ref, acc_ref, *, topk, scale):
    t = iq_ref.shape[0]
    tk = KEY_CHUNK
    qi = pl.program_id(1)
    n_chunks = qi + 1
    hi, hb = N_HEADS_IDX, N_HEADS_DSA

    @pl.when(qi == 0)
    def _():
        ckv = ckv_ref[...].astype(F32)
        ms = jnp.mean(ckv * ckv, axis=-1, keepdims=True)
        c_ref[...] = (ckv * lax.rsqrt(ms + EPS) * kvg_ref[...]).astype(BF16)

    rg = qi_ref.shape[2]
    lane = lax.broadcasted_iota(I32, (t, LANES), 1)
    for h in range(hi):
        blk = iq_ref[:, (h // 2) * LANES:(h // 2 + 1) * LANES]
        keep = (lane < HEAD_DIM_IDX) if h % 2 == 0 else (lane >= HEAD_DIM_IDX)
        qi_ref[:, h, :, :] = jnp.where(keep, blk, jnp.zeros_like(blk)).reshape(t // rg, rg, LANES)
        wb_ref[:, h, :, :] = jnp.broadcast_to(iw_ref[:, h:h + 1].astype(F32), (t, LANES)).reshape(t // rg, rg, LANES)

    def score_chunk(kc, diag):
        off = pl.multiple_of(kc * t, t)
        d = _dot_nt(qi_ref[...].reshape(hi * t, LANES), ik_ref[pl.ds(off, t), :]).reshape(t // rg, hi, rg, t)
        w = jnp.concatenate([wb_ref[...]] * (t // LANES), axis=-1)
        sc = jnp.sum(jnp.maximum(d, 0.0) * w, axis=1).reshape(t, t)
        key = _sortable_key(sc + 0.0)
        if diag:
            row = lax.broadcasted_iota(I32, (t, t), 0)
            col = lax.broadcasted_iota(I32, (t, t), 1)
            key = jnp.where(col <= row, key, jnp.int32(INT_MIN))
        key_ref[:, pl.ds(off, t)] = key
        key_t = key.T
        keyt_ref[pl.ds(off, t), :] = key_t
        keyh_ref[pl.ds(off, t), :] = (key_t >> 16).astype(jnp.int16)

    def score_body(j, carry):
        for u in range(4):
            score_chunk(4 * j + u, False)
        return carry

    lax.fori_loop(0, qi // 4, score_body, 0)
    done = qi // 4 * 4

    @pl.when(qi % 4 >= 2)
    def _():
        score_chunk(done, False)
        score_chunk(done + 1, False)

    @pl.when(qi % 2 == 1)
    def _():
        score_chunk(qi - 1, False)

    score_chunk(qi, True)

    @pl.when(qi % 2 == 0)
    def _():
        off = pl.multiple_of((qi + 1) * t, t)
        key_ref[:, pl.ds(off, t)] = jnp.full((t, t), INT_MIN, I32)
        keyt_ref[pl.ds(off, t), :] = jnp.full((t, t), INT_MIN, I32)
        keyh_ref[pl.ds(off, t), :] = jnp.full((t, t), INT_MIN >> 16, jnp.int16)

    n_steps = (qi + 2) // 2

    def count_ge(cand_s):
        def body(kc, cnt):
            off = pl.multiple_of(kc * tk, tk)
            k = keyt_ref[pl.ds(off, tk), :].reshape(tk // 8, 8, t)
            return cnt + jnp.sum(jnp.where(k >= cand_s[None], 1, 0), axis=0)

        cnt = lax.fori_loop(0, n_steps, body, jnp.zeros((8, t), I32))
        return jnp.sum(cnt, axis=0, keepdims=True)

    def count_packed(c_row):
        rows = PACKED_ROWS
        c16 = jnp.broadcast_to(c_row.astype(jnp.int16), (rows, t))

        def body(kc, cnt):
            off = pl.multiple_of(kc * tk, tk)
            k = keyh_ref[pl.ds(off, tk), :].reshape(tk // rows, rows, t)
            hit = jnp.where(k >= c16[None], jnp.int16(1), jnp.int16(0))
            parts = [cnt, hit[0]]
            for j in range(1, tk // rows):
                parts[j % 2] = parts[j % 2] + hit[j]
            return parts[0] + parts[1]

        cnt = lax.fori_loop(0, n_steps, body, jnp.zeros((rows, t), jnp.int16))
        return jnp.sum(cnt.astype(I32), axis=0, keepdims=True)

    def make_bit_body(count):
        def bit_body(i, carry):
            cur, n_ge = carry
            bit = lax.shift_left(jnp.int32(1), 31 - i)
            cand = cur | bit
            total = count(cand ^ jnp.int32(INT_MIN))
            accept = total >= topk
            return jnp.where(accept, cand, cur), jnp.where(accept, total, n_ge)
        return bit_body

    zeros8 = jnp.zeros((8, t), I32)
    low_bias = jnp.int32(1 << 15)
    cur, n_ge = lax.fori_loop(
        0, 16, make_bit_body(lambda cand_s: count_packed(cand_s[0:1, :] >> 16)), (zeros8, zeros8))
    high = (cur ^ jnp.int32(INT_MIN)) >> 16
    above_high = jnp.where(high[0:1, :] == 2 ** 15 - 1, 0, count_packed(jnp.minimum(high[0:1, :], 2 ** 15 - 2) + 1))

    def repack_body(kc, carry):
        off = pl.multiple_of(kc * tk, tk)
        k = keyt_ref[pl.ds(off, tk), :].reshape(tk // 8, 8, t)
        low = jnp.where((k >> 16) == high[None], (k & jnp.int32(0xFFFF)) - low_bias, -low_bias)
        keyh_ref[pl.ds(off, tk), :] = low.reshape(tk, t).astype(jnp.int16)
        return carry

    lax.fori_loop(0, n_steps, repack_body, 0)
    cur, n_ge = lax.fori_loop(
        16, 32, make_bit_body(lambda cand_s: above_high + count_packed((cand_s[0:1, :] & jnp.int32(0xFFFF)) - low_bias)),
        (cur, n_ge))
    thr = jnp.maximum(cur ^ jnp.int32(INT_MIN), jnp.int32(INT_MIN + 1))

    def lanes_to_rows(v):
        b = jnp.broadcast_to(v[0:1, :], (LANES, t)).T
        return jnp.concatenate([b] * (tk // LANES), axis=1)

    thr_w = lanes_to_rows(thr)
    ties = jnp.max(n_ge) > topk

    @pl.when(jnp.logical_not(ties))
    def _():
        def mask_body(kc, carry):
            off = pl.multiple_of(kc * tk, tk)
            am = jnp.where(key_ref[:, pl.ds(off, tk)] >= thr_w, 0.0, MASK_VALUE).astype(F32)
            key_ref[:, pl.ds(off, tk)] = lax.bitcast_convert_type(am, I32)
            return carry

        lax.fori_loop(0, n_steps, mask_body, 0)

    @pl.when(ties)
    def _():
        above = jnp.where(thr == jnp.int32(2 ** 31 - 1), 0, count_ge(jnp.minimum(thr, jnp.int32(2 ** 31 - 2)) + 1))
        need_w = lanes_to_rows(topk - above).astype(F32)
        r = lax.broadcasted_iota(I32, (tk, tk), 0)
        c = lax.broadcasted_iota(I32, (tk, tk), 1)
        before = jnp.where(r < c, 1.0, 0.0).astype(BF16)
        ones_w = jnp.ones((tk, tk), BF16)

        def tie_body(kc, seen):
            off = pl.multiple_of(kc * tk, tk)
            k = key_ref[:, pl.ds(off, tk)]
            eq = k == thr_w
            eq_b = jnp.where(eq, 1.0, 0.0).astype(BF16)
            rank = seen + jnp.dot(eq_b, before, preferred_element_type=F32)
            keep = (k > thr_w) | (eq & (rank < need_w))
            key_ref[:, pl.ds(off, tk)] = lax.bitcast_convert_type(jnp.where(keep, 0.0, MASK_VALUE).astype(F32), I32)
            return seen + jnp.dot(eq_b, ones_w, preferred_element_type=F32)

        lax.fori_loop(0, n_steps, tie_body, jnp.zeros((t, tk), F32))

    streams = ql_ref.shape[0]
    split = streams == 1
    hs = hb // streams
    for h in range(hb):
        qh = sq_ref[:, h * HEAD_DIM_DSA:(h + 1) * HEAD_DIM_DSA]
        ql = jnp.dot(qh, wuk_ref[h], preferred_element_type=F32) * scale
        ql_ref[h // hs, (h % hs) * t:(h % hs + 1) * t, :] = ql.astype(BF16)
    for g in range(streams):
        _softmax_init(m_ref.at[g], l_ref.at[g], acc_ref.at[g])

    def chunk(kc, width, delta):
        off = pl.multiple_of(kc * tk, tk)
        c = c_ref[pl.ds(off, width), :]
        am = lax.bitcast_convert_type(key_ref[:, pl.ds(off, width)], F32)
        for g in range(streams):
            s = (_dot_nt(ql_ref[g], c, split).reshape(hs, t, width) + am[None]).reshape(hs * t, width)
            s_ref[g, :, 0:width] = s
            smax = None
            if delta is None:
                smax = _lane_chunk_max(s)
            else:
                _near_bias(s_ref.at[g], lambda j: bias_ref[0, g * hs + j], lambda j: bias_ref[1, g * hs + j],
                           delta, hs, t, width)
            _softmax_step(s_ref.at[g], width, c, m_ref.at[g], l_ref.at[g], acc_ref.at[g], smax, split)

    def far_body(kc, carry):
        chunk(kc, tk, None)
        return carry

    lax.fori_loop(0, jnp.maximum((qi - 1) // 2, 0), far_body, 0)
    half = qi // 2

    @pl.when(qi % 2 == 1)
    def _():
        chunk(half, tk, t // LANES)

    @pl.when(qi % 2 == 0)
    def _():
        @pl.when(half >= 1)
        def _():
            chunk(half - 1, tk, tk // LANES)
        chunk(half, t, 0)

    for h in range(hb):
        g, j = h // hs, h % hs
        inv_l = 1.0 / jnp.sum(l_ref[g, j * t:(j + 1) * t, :], axis=-1, keepdims=True)
        ol = (acc_ref[g, j * t:(j + 1) * t, :] * inv_l).astype(BF16)
        o = jnp.dot(ol, wuv_ref[h], preferred_element_type=F32)
        o_ref[:, h * HEAD_DIM_DSA:(h + 1) * HEAD_DIM_DSA] = o.astype(o_ref.dtype)


def _dsa_attention(proj, kv_g, w_ukt, w_uv, bias_blocks, batch, seq, topk):
    t = DSA_BLOCK
    nq = seq // t
    hb, hi = N_HEADS_DSA, N_HEADS_IDX
    width = hb * HEAD_DIM_DSA
    ns = DSA_STREAMS
    rs = hb // ns * t
    kernel = functools.partial(_dsa_kernel, topk=topk, scale=HEAD_DIM_DSA ** -0.5 * LOG2E)
    return pl.pallas_call(
        kernel,
        out_shape=jax.ShapeDtypeStruct((batch * seq, width), BF16),
        grid=(batch, nq),
        in_specs=[pl.BlockSpec((t, 1024), lambda b, qi: (b * nq + qi, COL_IQ // 1024)),
                  pl.BlockSpec((t, 1024), lambda b, qi: (b * nq + qi, COL_SQ // 1024)),
                  pl.BlockSpec((t, LANES), lambda b, qi: (b * nq + qi, COL_IW // LANES)),
                  pl.BlockSpec((seq, LANES), lambda b, qi: (b, COL_IK // LANES)),
                  pl.BlockSpec((seq, KV_LATENT), lambda b, qi: (b, COL_CKV // KV_LATENT)),
                  pl.BlockSpec((1, KV_LATENT), lambda b, qi: (0, 0)),
                  pl.BlockSpec((hb, HEAD_DIM_DSA, KV_LATENT), lambda b, qi: (0, 0, 0)),
                  pl.BlockSpec((hb, KV_LATENT, HEAD_DIM_DSA), lambda b, qi: (0, 0, 0)),
                  pl.BlockSpec((2, hb, LANES, LANES), lambda b, qi: (0, 0, 0, 0))],
        out_specs=pl.BlockSpec((t, width), lambda b, qi: (b * nq + qi, 0)),
        scratch_shapes=[pltpu.VMEM((seq, KV_LATENT), BF16),
                        pltpu.VMEM((t, seq), I32),
                        pltpu.VMEM((seq, t), I32),
                        pltpu.VMEM((seq, t), jnp.int16),
                        pltpu.VMEM((t // PACKED_ROWS, hi, PACKED_ROWS, LANES), BF16),
                        pltpu.VMEM((t // PACKED_ROWS, hi, PACKED_ROWS, LANES), F32),
                        pltpu.VMEM((ns, rs, KV_LATENT), BF16),
                        pltpu.VMEM((ns, rs, KEY_CHUNK), F32),
                        pltpu.VMEM((ns, rs, LANES), F32),
                        pltpu.VMEM((ns, rs, LANES), F32),
                        pltpu.VMEM((ns, rs, KV_LATENT), F32)],
        compiler_params=_cparams(("parallel", "arbitrary")),
        name="dsa_attn",
    )(proj, proj, proj, proj, proj, kv_g, w_ukt, w_uv, bias_blocks)


def _pack_halves(x):
    c = x.shape[1] // 2
    lo = lax.bitcast_convert_type(x[:, :c].astype(BF16).astype(F32), I32)
    hi = lax.bitcast_convert_type(x[:, c:].astype(BF16).astype(F32), I32)
    return lax.shift_right_logical(lo, 16) | (hi & jnp.int32(-65536))


def _unpack_halves(w):
    lo = lax.bitcast_convert_type(lax.shift_left(w, 16), F32)
    hi = lax.bitcast_convert_type(w & jnp.int32(-65536), F32)
    return lo, hi


def _merge_kernel(x_ref, yd_ref, ys_ref, ga_ref, gb_ref, wd_ref, ws_ref, wo_ref, g_ref, wrh_ref, wrl_ref, br_ref,
                  h_ref, hn_ref, route_ref):
    bd = _dot_split(yd_ref[...], wd_ref[...])
    bs = _dot_split(ys_ref[...], ws_ref[...])
    merged = (jax.nn.sigmoid(ga_ref[...].astype(F32)) * bd + jax.nn.sigmoid(gb_ref[...].astype(F32)) * bs)
    h = x_ref[...] + _dot_split(merged.astype(BF16), wo_ref[...])
    h_ref[...] = h
    ms = jnp.mean(h * h, axis=-1, keepdims=True)
    hn = h * lax.rsqrt(ms + EPS) * g_ref[...]
    hn_ref[...] = _pack_halves(hn)

    hn_hi = hn.astype(BF16)
    hn_lo = (hn - hn_hi.astype(F32)).astype(BF16)
    logits = (_dot_split(hn_hi, wrh_ref[...]) + _dot_split(hn_lo, wrh_ref[...])
              + _dot_split(hn_hi, wrl_ref[...]))
    logits = logits + br_ref[...]
    lane = lax.broadcasted_iota(I32, logits.shape, 1)
    vals, ids = [], []
    for _ in range(TOP_K_EXPERTS):
        mx = jnp.max(logits, axis=-1, keepdims=True)
        ix = jnp.min(jnp.where(logits == mx, lane, LANES), axis=-1, keepdims=True)
        vals.append(mx)
        ids.append(ix)
        logits = jnp.where(lane == ix, -jnp.inf, logits)
    es = [jnp.exp(v - vals[0]) for v in vals]
    inv = 1.0 / (es[0] + es[1] + es[2] + es[3])
    route = jnp.zeros(logits.shape, F32)
    for k in range(TOP_K_EXPERTS):
        route = jnp.where(lane == k, es[k] * inv, route)
        route = jnp.where(lane == TOP_K_EXPERTS + k, ids[k].astype(F32), route)
    route_ref[...] = route


def _merge(x2, y_diff, y_dsa, proj, w_bd, w_bs, w_out, g_ffn, w_router_hi, w_router_lo, b_router, tm):
    n, d = x2.shape
    row = lambda i: (i, 0)
    const = lambda i: (0, 0)
    return pl.pallas_call(
        _merge_kernel,
        out_shape=(jax.ShapeDtypeStruct((n, d), F32),
                   jax.ShapeDtypeStruct((n, d // 2), I32),
                   jax.ShapeDtypeStruct((n, LANES), F32)),
        grid=(n // tm,),
        in_specs=[pl.BlockSpec((tm, d), row),
                  pl.BlockSpec((tm, d), row),
                  pl.BlockSpec((tm, d), row),
                  pl.BlockSpec((tm, d), lambda i: (i, COL_GA // 1024)),
                  pl.BlockSpec((tm, d), lambda i: (i, COL_GB // 1024)),
                  pl.BlockSpec((d, d), const),
                  pl.BlockSpec((d, d), const),
                  pl.BlockSpec((d, d), const),
                  pl.BlockSpec((1, d), const),
                  pl.BlockSpec((d, LANES), const),
                  pl.BlockSpec((d, LANES), const),
                  pl.BlockSpec((1, LANES), const)],
        out_specs=(pl.BlockSpec((tm, d), row),
                   pl.BlockSpec((tm, d // 2), row),
                   pl.BlockSpec((tm, LANES), row)),
        compiler_params=_cparams(("parallel",)),
        name="merge_router",
    )(x2, y_diff, y_dsa, proj, proj, w_bd, w_bs, w_out, g_ffn, w_router_hi, w_router_lo, b_router)


def _regroup_kernel(w_ref, p_ref, o_ref):
    pw = p_ref.shape[0]
    for j in range(w_ref.shape[2] // pw):
        w = w_ref[0, :, j * pw:(j + 1) * pw].astype(BF16)
        o_ref[0, :, j * pw:(j + 1) * pw] = jnp.dot(w, p_ref[...], preferred_element_type=F32).astype(BF16)


def _regroup_gate_up(w_gu, rows):
    e, d, f2 = w_gu.shape
    pw = 2 * LANES
    src = jnp.arange(pw, dtype=I32)
    dst = (src % 2) * LANES + src // 2
    perm = (dst[:, None] == jnp.arange(pw, dtype=I32)[None, :]).astype(BF16)
    return pl.pallas_call(
        _regroup_kernel,
        out_shape=jax.ShapeDtypeStruct((e, d, f2), BF16),
        grid=(e, d // rows),
        in_specs=[pl.BlockSpec((1, rows, f2), lambda i, j: (i, j, 0)),
                  pl.BlockSpec((pw, pw), lambda i, j: (0, 0))],
        out_specs=pl.BlockSpec((1, rows, f2), lambda i, j: (i, j, 0)),
        compiler_params=_cparams(("parallel", "parallel")),
        name="regroup_gate_up",
    )(w_gu, perm)


def _ffn_kernel(be_ref, nu_ref, x_ref, wgu_ref, wd_ref, bgu_ref, bd_ref, o_ref):
    @pl.when(pl.program_id(0) < nu_ref[0])
    def _():
        x_lo, x_hi = _unpack_halves(x_ref[...])
        half = x_lo.shape[1]
        gu = (jnp.dot(x_lo.astype(BF16), wgu_ref[0, 0:half, :], preferred_element_type=F32)
              + jnp.dot(x_hi.astype(BF16), wgu_ref[0, half:2 * half, :], preferred_element_type=F32)
              + bgu_ref[0])
        acts = []
        for j in range(gu.shape[1] // (2 * LANES)):
            gate = jnp.minimum(gu[:, 2 * j * LANES:(2 * j + 1) * LANES], SWIGLU_LIMIT)
            up = jnp.clip(gu[:, (2 * j + 1) * LANES:(2 * j + 2) * LANES], -SWIGLU_LIMIT, SWIGLU_LIMIT)
            glu = gate * jax.nn.sigmoid(gate * SWIGLU_ALPHA)
            acts.append(((up + 1.0) * glu).astype(BF16))
        a = jnp.concatenate(acts, axis=1)
        y = jnp.dot(a, wd_ref[0].astype(BF16), preferred_element_type=F32) + bd_ref[0]
        o_ref[...] = _pack_halves(y)

    @pl.when(pl.program_id(0) >= nu_ref[0])
    def _():
        o_ref[...] = jnp.zeros(o_ref.shape, o_ref.dtype)


def _expert_ffn(blk_exp, n_used, xs, wgu, wd, bgu, bd):
    p, dw = xs.shape
    f, d = wd.shape[1], wd.shape[2]
    nblk = p // MOE_ROWS
    wmap = lambda i, be, nu: (be[i], 0, 0)
    grid_spec = pltpu.PrefetchScalarGridSpec(
        num_scalar_prefetch=2,
        grid=(nblk,),
        in_specs=[pl.BlockSpec((MOE_ROWS, dw), lambda i, be, nu: (i, 0)),
                  pl.BlockSpec((1, d, 2 * f), wmap),
                  pl.BlockSpec((1, f, d), wmap),
                  pl.BlockSpec((1, 1, 2 * f), wmap),
                  pl.BlockSpec((1, 1, d), wmap)],
        out_specs=pl.BlockSpec((MOE_ROWS, dw), lambda i, be, nu: (i, 0)),
    )
    return pl.pallas_call(
        _ffn_kernel,
        out_shape=jax.ShapeDtypeStruct((p, dw), I32),
        grid_spec=grid_spec,
        compiler_params=_cparams(("arbitrary",)),
        name="expert_ffn",
    )(blk_exp, n_used, xs, wgu, wd, bgu, bd)


def _route_kernel(route_ref, dest_ref, cnt_ref, u_ref, carry_ref, pstart_ref, *, block_rows):
    ph, i = pl.program_id(0), pl.program_id(1)
    tm = route_ref.shape[0]

    @pl.when((ph == 0) & (i == 0))
    def _():
        r = lax.broadcasted_iota(I32, (tm, tm), 0)
        c = lax.broadcasted_iota(I32, (tm, tm), 1)
        u_ref[...] = jnp.where(r < c, 1.0, 0.0).astype(BF16)
        carry_ref[...] = jnp.zeros(carry_ref.shape, F32)

    @pl.when((ph == 1) & (i == 0))
    def _():
        counts = carry_ref[...]
        cnt_ref[...] = counts
        padded = jnp.ceil(counts * (1.0 / block_rows)) * block_rows
        r = lax.broadcasted_iota(I32, (LANES, LANES), 0)
        c = lax.broadcasted_iota(I32, (LANES, LANES), 1)
        lower = jnp.where(c < r, 1.0, 0.0).astype(F32)
        pstart_ref[...] = jnp.dot(lower, padded, preferred_element_type=F32, precision=lax.Precision.HIGHEST)
        carry_ref[...] = jnp.zeros(carry_ref.shape, F32)

    rt = route_ref[...].T
    sub = lax.broadcasted_iota(I32, (LANES, tm), 0)
    hits = [sub == rt[TOP_K_EXPERTS + k:TOP_K_EXPERTS + k + 1, :].astype(I32) for k in range(TOP_K_EXPERTS)]
    m = jnp.zeros((LANES, tm), F32)
    for hit in hits:
        m = m + jnp.where(hit, 1.0, 0.0)
    tile_counts = jnp.broadcast_to(jnp.sum(m, axis=1, keepdims=True), (LANES, LANES))

    @pl.when(ph == 0)
    def _():
        dest_ref[...] = jnp.zeros(dest_ref.shape, I32)

    @pl.when(ph == 1)
    def _():
        prefix = jnp.dot(m.astype(BF16), u_ref[...], preferred_element_type=F32)
        rank = prefix + (pstart_ref[:, 0:1] + carry_ref[:, 0:1])
        rows = [jnp.sum(jnp.where(hit, rank, 0.0), axis=0, keepdims=True) for hit in hits]
        rows.append(jnp.zeros((dest_ref.shape[0] - TOP_K_EXPERTS, tm), F32))
        dest_ref[...] = jnp.concatenate(rows, axis=0).astype(I32)

    carry_ref[...] = carry_ref[...] + tile_counts


def _route_rows(route, tm):
    n = route.shape[0]
    nt = n // tm
    kernel = functools.partial(_route_kernel, block_rows=MOE_ROWS)
    return pl.pallas_call(
        kernel,
        out_shape=(jax.ShapeDtypeStruct((8, n + tm), I32), jax.ShapeDtypeStruct((LANES, LANES), F32)),
        grid=(2, nt),
        in_specs=[pl.BlockSpec((tm, LANES), lambda ph, i: (i, 0))],
        out_specs=(pl.BlockSpec((8, tm), lambda ph, i: (0, ph * i + (1 - ph) * nt)),
                   pl.BlockSpec((LANES, LANES), lambda ph, i: (0, 0))),
        scratch_shapes=[pltpu.VMEM((tm, tm), BF16),
                        pltpu.VMEM((LANES, LANES), F32),
                        pltpu.VMEM((LANES, LANES), F32)],
        compiler_params=_cparams(("arbitrary", "arbitrary")),
        name="route_rows",
    )(route)


SC_WINDOW = 128
SC_WORKERS = 32


def _sc_mesh():
    return plsc.VectorSubcoreMesh(core_axis_name="c", subcore_axis_name="s")


def _sc_scatter_rows(src, dest, p):
    n, d = src.shape
    assert n % (SC_WINDOW * SC_WORKERS) == 0, "token count must split evenly over the vector subcores"
    per = n // (SC_WINDOW * SC_WORKERS)

    @pl.kernel(out_type=jax.ShapeDtypeStruct((p, d), src.dtype), mesh=_sc_mesh(),
               scratch_types=[pltpu.VMEM((dest.shape[0], SC_WINDOW), I32), pltpu.VMEM((SC_WINDOW, d), src.dtype)])
    def scatter(src_hbm, idx_hbm, out_hbm, idx_vmem, buf):
        wid = lax.axis_index("c") * (SC_WORKERS // 2) + lax.axis_index("s")

        @pl.loop(0, per)
        def _(j):
            off = (wid * per + j) * SC_WINDOW
            pltpu.sync_copy(idx_hbm.at[:, pl.ds(off, SC_WINDOW)], idx_vmem)
            pltpu.sync_copy(src_hbm.at[pl.ds(off, SC_WINDOW), :], buf)
            for k in range(TOP_K_EXPERTS):
                pltpu.sync_copy(buf, out_hbm.at[idx_vmem.at[k]])

    return scatter(src, dest)


def _sc_gather_rows(src, dest):
    n = dest.shape[1]
    d = src.shape[1]
    assert n % (SC_WINDOW * SC_WORKERS) == 0, "token count must split evenly over the vector subcores"
    per = n // (SC_WINDOW * SC_WORKERS)

    @pl.kernel(out_type=jax.ShapeDtypeStruct((TOP_K_EXPERTS * n, d), src.dtype), mesh=_sc_mesh(),
               scratch_types=[pltpu.VMEM((dest.shape[0], SC_WINDOW), I32), pltpu.VMEM((SC_WINDOW, d), src.dtype)])
    def gather(src_hbm, idx_hbm, out_hbm, idx_vmem, buf):
        wid = lax.axis_index("c") * (SC_WORKERS // 2) + lax.axis_index("s")

        @pl.loop(0, per)
        def _(j):
            off = (wid * per + j) * SC_WINDOW
            pltpu.sync_copy(idx_hbm.at[:, pl.ds(off, SC_WINDOW)], idx_vmem)
            for k in range(TOP_K_EXPERTS):
                pltpu.sync_copy(src_hbm.at[idx_vmem.at[k]], buf)
                pltpu.sync_copy(buf, out_hbm.at[pl.ds(k * n + off, SC_WINDOW), :])

    return gather(src, dest)


def _combine_kernel(h_ref, y_ref, route_ref, g_ref, o_ref):
    half = h_ref.shape[1] // 2
    h_lo, h_hi = h_ref[:, 0:half], h_ref[:, half:2 * half]
    route = route_ref[...]
    for k in range(TOP_K_EXPERTS):
        y_lo, y_hi = _unpack_halves(y_ref[k])
        gate = route[:, k:k + 1]
        h_lo = h_lo + gate * y_lo
        h_hi = h_hi + gate * y_hi
    ms = (jnp.sum(h_lo * h_lo, axis=-1, keepdims=True)
          + jnp.sum(h_hi * h_hi, axis=-1, keepdims=True)) * (1.0 / (2 * half))
    inv = lax.rsqrt(ms + EPS)
    o_ref[:, 0:half] = h_lo * inv * g_ref[:, 0:half]
    o_ref[:, half:2 * half] = h_hi * inv * g_ref[:, half:2 * half]


def _combine(h1, yg, route, g_final, tm):
    n, d = h1.shape
    return pl.pallas_call(
        _combine_kernel,
        out_shape=jax.ShapeDtypeStruct((n, d), F32),
        grid=(n // tm,),
        in_specs=[pl.BlockSpec((tm, d), lambda i: (i, 0)),
                  pl.BlockSpec((TOP_K_EXPERTS, tm, d // 2), lambda i: (0, i, 0)),
                  pl.BlockSpec((tm, LANES), lambda i: (i, 0)),
                  pl.BlockSpec((1, d), lambda i: (0, 0))],
        out_specs=pl.BlockSpec((tm, d), lambda i: (i, 0)),
        compiler_params=_cparams(("parallel",)),
        name="combine_norm",
    )(h1, yg, route, g_final)


def _t5_bucket(dist):
    n = jnp.maximum(dist, 0)
    max_exact = N_BUCKETS // 2
    nf = jnp.maximum(n, 1).astype(F32)
    large = max_exact + (jnp.log(nf / max_exact) / math.log(MAX_DISTANCE / max_exact)
                         * (N_BUCKETS - max_exact)).astype(I32)
    large = jnp.minimum(large, N_BUCKETS - 1)
    return jnp.where(n < max_exact, n, large)


def _bias_blocks(bias_tab):
    t = LANES
    assert MAX_DISTANCE <= LANES
    r = jnp.arange(t, dtype=I32)[:, None]
    c = jnp.arange(t, dtype=I32)[None, :]
    rel = (bias_tab - bias_tab[N_BUCKETS - 1][None, :]).astype(F32)
    tiles = []
    buckets = jnp.arange(N_BUCKETS, dtype=I32)[:, None, None]
    for delta in (0, t):
        dist = r - c + delta
        hit = _t5_bucket(dist)[None] == buckets
        b = jnp.sum(jnp.where(hit[:, None], rel[:, :, None, None], 0.0), axis=0)
        tiles.append(jnp.where((dist >= 0)[None], b * LOG2E, MASK_VALUE))
    return jnp.stack(tiles)


def _regroup_w_in(w_in):
    sizes = (1024, 1024, 1024, 1024, KV_LATENT, 1024, HEAD_DIM_IDX, N_HEADS_IDX, D_MODEL, D_MODEL)
    parts, off = [], 0
    for sz in sizes:
        parts.append(w_in[:, off:off + sz])
        off += sz
    dq, dk, dv, sq, ckv, iq, ik, iw, ga, gb = parts
    dq = dq * (HEAD_DIM_DIFF ** -0.5 * LOG2E)
    iw = iw * ((N_HEADS_IDX ** -0.5) * (HEAD_DIM_IDX ** -0.5))
    pad = jnp.zeros((w_in.shape[0], PROJ_WIDTH - COL_IW - N_HEADS_IDX), w_in.dtype)
    w = jnp.concatenate([dq, dk, dv, sq, iq, ga, gb, ckv, ik, ik, iw, pad], axis=1)
    return w.astype(BF16)


def _block_tables(counts, n_assign):
    e, bm = N_EXPERTS, MOE_ROWS
    padded = (counts + bm - 1) // bm * bm
    pends = jnp.cumsum(padded)
    nblk = -(-(n_assign + e * (bm - 1)) // bm)
    first_row = jnp.arange(nblk, dtype=I32) * bm
    blk_exp = jnp.minimum(jnp.sum((pends[None, :] <= first_row[:, None]).astype(I32), axis=1), e - 1)
    n_used = (pends[-1] // bm).astype(I32).reshape(1)
    return blk_exp, n_used, nblk


def kernel(x, norm_attn_g, w_in, rel_bias, lam_q1, lam_k1, lam_q2, lam_k2, diff_subln_g, kv_norm_g, w_uk, w_uv,
           w_branch_diff, w_branch_dsa, w_out, norm_ffn_g, w_router, b_router, w_gate_up, b_gate_up, w_down,
           b_down, norm_final_g):
    batch, seq, d = x.shape
    n = batch * seq
    assert norm_attn_g.shape[0] == 1, "single-layer kernel"
    assert seq % DIFF_BLOCK == 0 and seq % KEY_CHUNK == 0 and d == D_MODEL
    assert seq <= (2 ** 15 - 1) * PACKED_ROWS, "int16 per-element key counts"
    row_tile = math.gcd(n, ROW_TILE)
    token_tile = math.gcd(n, TOKEN_TILE)

    x2 = x.reshape(n, d)
    proj = _inproj(x2, norm_attn_g[0].reshape(1, d), _regroup_w_in(w_in[0]), row_tile, PROJ_COLS)

    lam_init = 0.8 - 0.6 * math.exp(-0.3 * 0)
    lam = (jnp.exp(jnp.sum(lam_q1[0].astype(F32) * lam_k1[0].astype(F32)))
           - jnp.exp(jnp.sum(lam_q2[0].astype(F32) * lam_k2[0].astype(F32))) + lam_init)
    y_diff = _diff_attention(proj, lam.reshape(1, 1).astype(F32), _bias_blocks(rel_bias[:, :N_HEADS_DIFF]),
                             diff_subln_g[0].reshape(1, -1).astype(F32), batch, seq, 1.0 - lam_init)

    y_dsa = _dsa_attention(proj, kv_norm_g[0].reshape(1, -1).astype(F32),
                           w_uk[0].transpose(0, 2, 1).astype(BF16), w_uv[0].astype(BF16),
                           _bias_blocks(rel_bias[:, N_HEADS_DIFF:]), batch, seq, min(TOPK_MAX, seq // 4))

    w_r = jnp.zeros((d, LANES), F32).at[:, :N_EXPERTS].set(w_router[0].astype(F32))
    b_r = jnp.full((1, LANES), MASK_VALUE, F32).at[0, :N_EXPERTS].set(b_router[0].astype(F32))
    h1, hn, route = _merge(x2, y_diff, y_dsa, proj, w_branch_diff[0].astype(BF16), w_branch_dsa[0].astype(BF16),
                           w_out[0].astype(BF16), norm_ffn_g[0].reshape(1, d).astype(F32),
                           w_r.astype(BF16), (w_r - w_r.astype(BF16).astype(F32)).astype(BF16), b_r,
                           row_tile)

    dest, counts = _route_rows(route, row_tile)
    dest = dest[:, :n]
    blk_exp, n_used, nblk = _block_tables(counts[:N_EXPERTS, 0].astype(I32), n * TOP_K_EXPERTS)
    xs = _sc_scatter_rows(hn, dest, nblk * MOE_ROWS)
    e, f = N_EXPERTS, D_EXPERT
    b_gu = b_gate_up[0].astype(F32).reshape(e, f // LANES, LANES, 2).transpose(0, 1, 3, 2).reshape(e, 1, 2 * f)
    ys = _expert_ffn(blk_exp, n_used, xs, _regroup_gate_up(w_gate_up[0], TOKEN_TILE), w_down[0],
                     b_gu, b_down[0][:, None, :].astype(F32))
    yg = _sc_gather_rows(ys, dest).reshape(TOP_K_EXPERTS, n, d // 2)
    out = _combine(h1, yg, route, norm_final_g.reshape(1, d).astype(F32), token_tile)
    return out.reshape(batch, seq, d)
```

```python
import functools
import math

import jax
import jax.numpy as jnp
from jax import lax
from jax.experimental import pallas as pl
from jax.experimental.pallas import tpu as pltpu
from jax.experimental.pallas import tpu_sc as plsc

F32 = jnp.float32
BF16 = jnp.bfloat16
I32 = jnp.int32

D_MODEL = 1024
N_HEADS_DIFF = 8
HEAD_DIM_DIFF = 64
N_HEADS_DSA = 8
HEAD_DIM_DSA = 128
KV_LATENT = 256
N_HEADS_IDX = 16
HEAD_DIM_IDX = 64
TOPK_MAX = 256
N_BUCKETS = 32
MAX_DISTANCE = 128
N_EXPERTS = 32
TOP_K_EXPERTS = 4
D_EXPERT = 1024
SWIGLU_LIMIT = 7.0
SWIGLU_ALPHA = 1.702
EPS = 1e-6

LANES = 128
PACKED_ROWS = 16
ROW_TILE = 1024
PROJ_COLS = 1280
TOKEN_TILE = 512
DIFF_BLOCK = 512
DIFF_HEADS_PER_STEP = 4
DSA_BLOCK = 256
KEY_CHUNK = 512
DSA_STREAMS = 2
MOE_ROWS = 512
FFN_STREAMS = 2
PROJ_WIDTH = 7680
VMEM_LIMIT = 56 * 1024 * 1024

COL_DQ, COL_DK, COL_DV, COL_SQ, COL_IQ, COL_GA, COL_GB = (i * 1024 for i in range(7))
COL_CKV = 7168
COL_IK = 7424
COL_IW = 7552

LOG2E = math.log2(math.e)
MASK_VALUE = -1e30
M_INIT = -1e29
INT_MIN = -2 ** 31


def _cparams(sem):
    return pltpu.CompilerParams(dimension_semantics=sem, vmem_limit_bytes=VMEM_LIMIT)


def _inproj_kernel(x_ref, g_ref, w_ref, o_ref, xn_ref):
    @pl.when(pl.program_id(1) == 0)
    def _():
        x = x_ref[...]
        ms = jnp.mean(x * x, axis=-1, keepdims=True)
        xn_ref[...] = (x * lax.rsqrt(ms + EPS) * g_ref[...]).astype(BF16)

    o_ref[...] = jnp.dot(xn_ref[...], w_ref[...], preferred_element_type=F32).astype(o_ref.dtype)


def _inproj(x2, g, w, tm, tn):
    n, d = x2.shape
    width = w.shape[1]
    return pl.pallas_call(
        _inproj_kernel,
        out_shape=jax.ShapeDtypeStruct((n, width), BF16),
        grid=(n // tm, width // tn),
        in_specs=[pl.BlockSpec((tm, d), lambda i, j: (i, 0)),
                  pl.BlockSpec((1, d), lambda i, j: (0, 0)),
                  pl.BlockSpec((d, tn), lambda i, j: (0, j))],
        out_specs=pl.BlockSpec((tm, tn), lambda i, j: (i, j)),
        scratch_shapes=[pltpu.VMEM((tm, d), BF16)],
        compiler_params=_cparams(("parallel", "arbitrary")),
        name="inproj",
    )(x2, g, w)


def _lane_chunk_max(s):
    smax = s[:, 0:LANES]
    for j in range(1, s.shape[1] // LANES):
        smax = jnp.maximum(smax, s[:, j * LANES:(j + 1) * LANES])
    return smax


def _softmax_step(s_ref, tk, v, m_ref, l_ref, acc_ref, smax=None, split=True):
    nl = tk // LANES
    if smax is None:
        smax = _lane_chunk_max(s_ref[:, 0:tk])
    m_prev = m_ref[...]
    m_new = jnp.maximum(m_prev, jnp.max(smax, axis=-1, keepdims=True))
    alpha = jnp.exp2(m_prev - m_new)
    psum = None
    ps = []
    for j in range(nl):
        pj = jnp.exp2(s_ref[:, j * LANES:(j + 1) * LANES] - m_new)
        psum = pj if psum is None else psum + pj
        ps.append(pj.astype(BF16))
    l_ref[...] = alpha * l_ref[...] + psum
    pv = _dot_split(jnp.concatenate(ps, axis=1), v, split)
    e = acc_ref.shape[1]
    a = alpha if e == LANES else jnp.concatenate([alpha] * (e // LANES), axis=1)
    acc_ref[...] = a * acc_ref[...] + pv
    m_ref[...] = m_new


def _softmax_init(m_ref, l_ref, acc_ref):
    m_ref[...] = jnp.full(m_ref.shape, M_INIT, F32)
    l_ref[...] = jnp.zeros(l_ref.shape, F32)
    acc_ref[...] = jnp.zeros(acc_ref.shape, F32)


def _softmax_result(l_ref, acc_ref):
    return acc_ref[...] * (1.0 / jnp.sum(l_ref[...], axis=-1, keepdims=True))


def _near_bias(s_ref, d0, d1, delta, groups, t, tk):
    for g in range(groups):
        for rb in range(t // LANES):
            for cb in range(tk // LANES):
                bd = delta + rb - cb
                rows = slice(g * t + rb * LANES, g * t + (rb + 1) * LANES)
                cols = slice(cb * LANES, (cb + 1) * LANES)
                if bd == 0:
                    s_ref[rows, cols] = s_ref[rows, cols] + d0(g)
                elif bd == 1:
                    s_ref[rows, cols] = s_ref[rows, cols] + d1(g)
                elif bd < 0:
                    s_ref[rows, cols] = jnp.full((LANES, LANES), MASK_VALUE, F32)


def _dot_nt(a, b, split=True):
    dn = (((1,), (1,)), ((), ()))
    if not split:
        return lax.dot_general(a, b, dn, preferred_element_type=F32)
    h = a.shape[0] // 2
    return jnp.concatenate([lax.dot_general(a[:h], b, dn, preferred_element_type=F32),
                            lax.dot_general(a[h:], b, dn, preferred_element_type=F32)], axis=0)


def _dot_split(a, b, split=True):
    if not split:
        return jnp.dot(a, b, preferred_element_type=F32)
    h = a.shape[0] // 2
    return jnp.concatenate([jnp.dot(a[:h], b, preferred_element_type=F32),
                            jnp.dot(a[h:], b, preferred_element_type=F32)], axis=0)


def _diff_kernel(lam_ref, q_ref, k_ref, v_ref, bias_ref, g_ref, o_ref, q2_ref, s_ref, m_ref, l_ref, acc_ref,
                 *, out_scale, heads):
    t = q_ref.shape[0]
    e = 2 * HEAD_DIM_DIFF
    qi = pl.program_id(2)
    cols = [slice(hh * e, (hh + 1) * e) for hh in range(heads)]

    lane = lax.broadcasted_iota(I32, (t, e), 1)
    for hh in range(heads):
        q = q_ref[:, cols[hh]]
        zero = jnp.zeros_like(q)
        q2_ref[hh, 0:t, :] = jnp.where(lane < HEAD_DIM_DIFF, q, zero)
        q2_ref[hh, t:2 * t, :] = jnp.where(lane >= HEAD_DIM_DIFF, q, zero)
        _softmax_init(m_ref.at[hh], l_ref.at[hh], acc_ref.at[hh])

    def chunk(kc, delta):
        off = pl.multiple_of(kc * t, t)
        for hh in range(heads):
            s = _dot_nt(q2_ref[hh], k_ref[pl.ds(off, t), cols[hh]])
            s_ref[hh] = s
            smax = None
            if delta is None:
                smax = _lane_chunk_max(s)
            else:
                _near_bias(s_ref.at[hh], lambda g: bias_ref[0, hh], lambda g: bias_ref[1, hh], delta, 2, t, t)
            _softmax_step(s_ref.at[hh], t, v_ref[pl.ds(off, t), cols[hh]], m_ref.at[hh], l_ref.at[hh],
                          acc_ref.at[hh], smax)

    def far_body(kc, carry):
        chunk(kc, None)
        return carry

    lax.fori_loop(0, jnp.maximum(qi - 1, 0), far_body, 0)

    @pl.when(qi >= 1)
    def _():
        chunk(qi - 1, t // LANES)

    chunk(qi, 0)

    for hh in range(heads):
        o = _softmax_result(l_ref.at[hh], acc_ref.at[hh])
        o = o[0:t, :] - lam_ref[0, 0] * o[t:2 * t, :]
        ms = jnp.mean(o * o, axis=-1, keepdims=True)
        o_ref[:, cols[hh]] = (o * lax.rsqrt(ms + EPS) * g_ref[...] * out_scale).astype(o_ref.dtype)


def _diff_attention(proj, lam, bias_blocks, subln_g, batch, seq, out_scale):
    t = DIFF_BLOCK
    nq = seq // t
    h = N_HEADS_DIFF
    e = 2 * HEAD_DIM_DIFF
    hp = DIFF_HEADS_PER_STEP
    w = hp * e
    kernel = functools.partial(_diff_kernel, out_scale=out_scale, heads=hp)
    return pl.pallas_call(
        kernel,
        out_shape=jax.ShapeDtypeStruct((batch * seq, h * e), BF16),
        grid=(batch, h // hp, nq),
        in_specs=[pl.BlockSpec(memory_space=pltpu.SMEM),
                  pl.BlockSpec((t, w), lambda b, hh, qi: (b * nq + qi, COL_DQ // w + hh)),
                  pl.BlockSpec((seq, w), lambda b, hh, qi: (b, COL_DK // w + hh)),
                  pl.BlockSpec((seq, w), lambda b, hh, qi: (b, COL_DV // w + hh)),
                  pl.BlockSpec((2, hp, LANES, LANES), lambda b, hh, qi: (0, hh, 0, 0)),
                  pl.BlockSpec((1, e), lambda b, hh, qi: (0, 0))],
        out_specs=pl.BlockSpec((t, w), lambda b, hh, qi: (b * nq + qi, hh)),
        scratch_shapes=[pltpu.VMEM((hp, 2 * t, e), BF16),
                        pltpu.VMEM((hp, 2 * t, t), F32),
                        pltpu.VMEM((hp, 2 * t, LANES), F32),
                        pltpu.VMEM((hp, 2 * t, LANES), F32),
                        pltpu.VMEM((hp, 2 * t, e), F32)],
        compiler_params=_cparams(("parallel", "parallel", "arbitrary")),
        name="diff_attn",
    )(lam, proj, proj, proj, bias_blocks, subln_g)


def _sortable_key(x):
    bits = lax.bitcast_convert_type(x, I32)
    return bits ^ ((bits >> 31) & jnp.int32(0x7FFFFFFF))


def _dsa_kernel(iq_ref, sq_ref, iw_ref, ik_ref, ckv_ref, kvg_ref, wuk_ref, wuv_ref, bias_ref, o_ref,
                c_ref, key_ref, keyt_ref, keyh_ref, qi_ref, wb_ref, ql_ref, s_ref, m_ref, l_ref, acc_ref, *, topk, scale):
    t = iq_ref.shape[0]
    tk = KEY_CHUNK
    qi = pl.program_id(1)
    n_chunks = qi + 1
    hi, hb = N_HEADS_IDX, N_HEADS_DSA

    @pl.when(qi == 0)
    def _():
        ckv = ckv_ref[...].astype(F32)
        ms = jnp.mean(ckv * ckv, axis=-1, keepdims=True)
        c_ref[...] = (ckv * lax.rsqrt(ms + EPS) * kvg_ref[...]).astype(BF16)

    rg = qi_ref.shape[2]
    lane = lax.broadcasted_iota(I32, (t, LANES), 1)
    for h in range(hi):
        blk = iq_ref[:, (h // 2) * LANES:(h // 2 + 1) * LANES]
        keep = (lane < HEAD_DIM_IDX) if h % 2 == 0 else (lane >= HEAD_DIM_IDX)
        qi_ref[:, h, :, :] = jnp.where(keep, blk, jnp.zeros_like(blk)).reshape(t // rg, rg, LANES)
        wb_ref[:, h, :, :] = jnp.broadcast_to(iw_ref[:, h:h + 1].astype(F32), (t, LANES)).reshape(t // rg, rg, LANES)

    def score_chunk(kc, diag):
        off = pl.multiple_of(kc * t, t)
        d = _dot_nt(qi_ref[...].reshape(hi * t, LANES), ik_ref[pl.ds(off, t), :]).reshape(t // rg, hi, rg, t)
        w = jnp.concatenate([wb_ref[...]] * (t // LANES), axis=-1)
        sc = jnp.sum(jnp.maximum(d, 0.0) * w, axis=1).reshape(t, t)
        key = _sortable_key(sc + 0.0)
        if diag:
            row = lax.broadcasted_iota(I32, (t, t), 0)
            col = lax.broadcasted_iota(I32, (t, t), 1)
            key = jnp.where(col <= row, key, jnp.int32(INT_MIN))
        key_ref[:, pl.ds(off, t)] = key
        key_t = key.T
        keyt_ref[pl.ds(off, t), :] = key_t
        keyh_ref[pl.ds(off, t), :] = (key_t >> 16).astype(jnp.int16)

    def score_body(j, carry):
        for u in range(4):
            score_chunk(4 * j + u, False)
        return carry

    lax.fori_loop(0, qi // 4, score_body, 0)
    done = qi // 4 * 4

    @pl.when(qi % 4 >= 2)
    def _():
        score_chunk(done, False)
        score_chunk(done + 1, False)

    @pl.when(qi % 2 == 1)
    def _():
        score_chunk(qi - 1, False)

    score_chunk(qi, True)

    @pl.when(qi % 2 == 0)
    def _():
        off = pl.multiple_of((qi + 1) * t, t)
        key_ref[:, pl.ds(off, t)] = jnp.full((t, t), INT_MIN, I32)
        keyt_ref[pl.ds(off, t), :] = jnp.full((t, t), INT_MIN, I32)
        keyh_ref[pl.ds(off, t), :] = jnp.full((t, t), INT_MIN >> 16, jnp.int16)

    n_steps = (qi + 2) // 2

    def count_ge(cand_s):
        def body(kc, cnt):
            off = pl.multiple_of(kc * tk, tk)
            k = keyt_ref[pl.ds(off, tk), :].reshape(tk // 8, 8, t)
            return cnt + jnp.sum(jnp.where(k >= cand_s[None], 1, 0), axis=0)

        cnt = lax.fori_loop(0, n_steps, body, jnp.zeros((8, t), I32))
        return jnp.sum(cnt, axis=0, keepdims=True)

    def count_packed(c_row):
        rows = PACKED_ROWS
        c16 = jnp.broadcast_to(c_row.astype(jnp.int16), (rows, t))

        def body(kc, cnt):
            off = pl.multiple_of(kc * tk, tk)
            k = keyh_ref[pl.ds(off, tk), :].reshape(tk // rows, rows, t)
            hit = jnp.where(k >= c16[None], jnp.int16(1), jnp.int16(0))
            parts = [cnt, hit[0]]
            for j in range(1, tk // rows):
                parts[j % 2] = parts[j % 2] + hit[j]
            return parts[0] + parts[1]

        cnt = lax.fori_loop(0, n_steps, body, jnp.zeros((rows, t), jnp.int16))
        return jnp.sum(cnt.astype(I32), axis=0, keepdims=True)

    def make_bit_body(count):
        def bit_body(i, carry):
            cur, n_ge = carry
            bit = lax.shift_left(jnp.int32(1), 31 - i)
            cand = cur | bit
            total = count(cand ^ jnp.int32(INT_MIN))
            accept = total >= topk
            return jnp.where(accept, cand, cur), jnp.where(accept, total, n_ge)
        return bit_body

    zeros8 = jnp.zeros((8, t), I32)
    low_bias = jnp.int32(1 << 15)
    cur, n_ge = lax.fori_loop(
        0, 16, make_bit_body(lambda cand_s: count_packed(cand_s[0:1, :] >> 16)), (zeros8, zeros8))
    high = (cur ^ jnp.int32(INT_MIN)) >> 16
    above_high = jnp.where(high[0:1, :] == 2 ** 15 - 1, 0, count_packed(jnp.minimum(high[0:1, :], 2 ** 15 - 2) + 1))

    def repack_body(kc, carry):
        off = pl.multiple_of(kc * tk, tk)
        k = keyt_ref[pl.ds(off, tk), :].reshape(tk // 8, 8, t)
        low = jnp.where((k >> 16) == high[None], (k & jnp.int32(0xFFFF)) - low_bias, -low_bias)
        keyh_ref[pl.ds(off, tk), :] = low.reshape(tk, t).astype(jnp.int16)
        return carry

    lax.fori_loop(0, n_steps, repack_body, 0)
    cur, n_ge = lax.fori_loop(
        16, 32, make_bit_body(lambda cand_s: above_high + count_packed((cand_s[0:1, :] & jnp.int32(0xFFFF)) - low_bias)),
        (cur, n_ge))
    thr = jnp.maximum(cur ^ jnp.int32(INT_MIN), jnp.int32(INT_MIN + 1))

    def lanes_to_rows(v):
        b = jnp.broadcast_to(v[0:1, :], (LANES, t)).T
        return jnp.concatenate([b] * (tk // LANES), axis=1)

    thr_w = lanes_to_rows(thr)
    ties = jnp.max(n_ge) > topk

    @pl.when(jnp.logical_not(ties))
    def _():
        def mask_body(kc, carry):
            off = pl.multiple_of(kc * tk, tk)
            am = jnp.where(key_ref[:, pl.ds(off, tk)] >= thr_w, 0.0, MASK_VALUE).astype(F32)
            key_ref[:, pl.ds(off, tk)] = lax.bitcast_convert_type(am, I32)
            return carry

        lax.fori_loop(0, n_steps, mask_body, 0)

    @pl.when(ties)
    def _():
        above = jnp.where(thr == jnp.int32(2 ** 31 - 1), 0, count_ge(jnp.minimum(thr, jnp.int32(2 ** 31 - 2)) + 1))
        need_w = lanes_to_rows(topk - above).astype(F32)
        r = lax.broadcasted_iota(I32, (tk, tk), 0)
        c = lax.broadcasted_iota(I32, (tk, tk), 1)
        before = jnp.where(r < c, 1.0, 0.0).astype(BF16)
        ones_w = jnp.ones((tk, tk), BF16)

        def tie_body(kc, seen):
            off = pl.multiple_of(kc * tk, tk)
            k = key_ref[:, pl.ds(off, tk)]
            eq = k == thr_w
            eq_b = jnp.where(eq, 1.0, 0.0).astype(BF16)
            rank = seen + jnp.dot(eq_b, before, preferred_element_type=F32)
            keep = (k > thr_w) | (eq & (rank < need_w))
            key_ref[:, pl.ds(off, tk)] = lax.bitcast_convert_type(jnp.where(keep, 0.0, MASK_VALUE).astype(F32), I32)
            return seen + jnp.dot(eq_b, ones_w, preferred_element_type=F32)

        lax.fori_loop(0, n_steps, tie_body, jnp.zeros((t, tk), F32))

    streams = ql_ref.shape[0]
    split = streams == 1
    hs = hb // streams
    for h in range(hb):
        qh = sq_ref[:, h * HEAD_DIM_DSA:(h + 1) * HEAD_DIM_DSA]
        ql = jnp.dot(qh, wuk_ref[h], preferred_element_type=F32) * scale
        ql_ref[h // hs, (h % hs) * t:(h % hs + 1) * t, :] = ql.astype(BF16)
    for g in range(streams):
        _softmax_init(m_ref.at[g], l_ref.at[g], acc_ref.at[g])

    def chunk(kc, width, delta):
        off = pl.multiple_of(kc * tk, tk)
        c = c_ref[pl.ds(off, width), :]
        am = lax.bitcast_convert_type(key_ref[:, pl.ds(off, width)], F32)
        for g in range(streams):
            s = (_dot_nt(ql_ref[g], c, split).reshape(hs, t, width) + am[None]).reshape(hs * t, width)
            s_ref[g, :, 0:width] = s
            smax = None
            if delta is None:
                smax = _lane_chunk_max(s)
            else:
                _near_bias(s_ref.at[g], lambda j: bias_ref[0, g * hs + j], lambda j: bias_ref[1, g * hs + j],
                           delta, hs, t, width)
            _softmax_step(s_ref.at[g], width, c, m_ref.at[g], l_ref.at[g], acc_ref.at[g], smax, split)

    def far_body(kc, carry):
        chunk(kc, tk, None)
        return carry

    lax.fori_loop(0, jnp.maximum((qi - 1) // 2, 0), far_body, 0)
    half = qi // 2

    @pl.when(qi % 2 == 1)
    def _():
        chunk(half, tk, t // LANES)

    @pl.when(qi % 2 == 0)
    def _():
        @pl.when(half >= 1)
        def _():
            chunk(half - 1, tk, tk // LANES)
        chunk(half, t, 0)

    for h in range(hb):
        g, j = h // hs, h % hs
        inv_l = 1.0 / jnp.sum(l_ref[g, j * t:(j + 1) * t, :], axis=-1, keepdims=True)
        ol = (acc_ref[g, j * t:(j + 1) * t, :] * inv_l).astype(BF16)
        o = jnp.dot(ol, wuv_ref[h], preferred_element_type=F32)
        o_ref[:, h * HEAD_DIM_DSA:(h + 1) * HEAD_DIM_DSA] = o.astype(o_ref.dtype)


def _dsa_attention(proj, kv_g, w_ukt, w_uv, bias_blocks, batch, seq, topk):
    t = DSA_BLOCK
    nq = seq // t
    hb, hi = N_HEADS_DSA, N_HEADS_IDX
    width = hb * HEAD_DIM_DSA
    ns = DSA_STREAMS
    rs = hb // ns * t
    kernel = functools.partial(_dsa_kernel, topk=topk, scale=HEAD_DIM_DSA ** -0.5 * LOG2E)
    return pl.pallas_call(
        kernel,
        out_shape=jax.ShapeDtypeStruct((batch * seq, width), BF16),
        grid=(batch, nq),
        in_specs=[pl.BlockSpec((t, 1024), lambda b, qi: (b * nq + qi, COL_IQ // 1024)),
                  pl.BlockSpec((t, 1024), lambda b, qi: (b * nq + qi, COL_SQ // 1024)),
                  pl.BlockSpec((t, LANES), lambda b, qi: (b * nq + qi, COL_IW // LANES)),
                  pl.BlockSpec((seq, LANES), lambda b, qi: (b, COL_IK // LANES)),
                  pl.BlockSpec((seq, KV_LATENT), lambda b, qi: (b, COL_CKV // KV_LATENT)),
                  pl.BlockSpec((1, KV_LATENT), lambda b, qi: (0, 0)),
                  pl.BlockSpec((hb, HEAD_DIM_DSA, KV_LATENT), lambda b, qi: (0, 0, 0)),
                  pl.BlockSpec((hb, KV_LATENT, HEAD_DIM_DSA), lambda b, qi: (0, 0, 0)),
                  pl.BlockSpec((2, hb, LANES, LANES), lambda b, qi: (0, 0, 0, 0))],
        out_specs=pl.BlockSpec((t, width), lambda b, qi: (b * nq + qi, 0)),
        scratch_shapes=[pltpu.VMEM((seq, KV_LATENT), BF16),
                        pltpu.VMEM((t, seq), I32),
                        pltpu.VMEM((seq, t), I32),
                        pltpu.VMEM((seq, t), jnp.int16),
                        pltpu.VMEM((t // PACKED_ROWS, hi, PACKED_ROWS, LANES), BF16),
                        pltpu.VMEM((t // PACKED_ROWS, hi, PACKED_ROWS, LANES), F32),
                        pltpu.VMEM((ns, rs, KV_LATENT), BF16),
                        pltpu.VMEM((ns, rs, KEY_CHUNK), F32),
                        pltpu.VMEM((ns, rs, LANES), F32),
                        pltpu.VMEM((ns, rs, LANES), F32),
                        pltpu.VMEM((ns, rs, KV_LATENT), F32)],
        compiler_params=_cparams(("parallel", "arbitrary")),
        name="dsa_attn",
    )(proj, proj, proj, proj, proj, kv_g, w_ukt, w_uv, bias_blocks)


def _pack_halves(x):
    c = x.shape[1] // 2
    lo = lax.bitcast_convert_type(x[:, :c].astype(BF16).astype(F32), I32)
    hi = lax.bitcast_convert_type(x[:, c:].astype(BF16).astype(F32), I32)
    return lax.shift_right_logical(lo, 16) | (hi & jnp.int32(-65536))


def _unpack_halves(w):
    lo = lax.bitcast_convert_type(lax.shift_left(w, 16), F32)
    hi = lax.bitcast_convert_type(w & jnp.int32(-65536), F32)
    return lo, hi


def _merge_kernel(x_ref, yd_ref, ys_ref, ga_ref, gb_ref, wd_ref, ws_ref, wo_ref, g_ref, wrh_ref, wrl_ref, br_ref,
                  h_ref, hn_ref, route_ref):
    bd = _dot_split(yd_ref[...], wd_ref[...])
    bs = _dot_split(ys_ref[...], ws_ref[...])
    merged = (jax.nn.sigmoid(ga_ref[...].astype(F32)) * bd + jax.nn.sigmoid(gb_ref[...].astype(F32)) * bs)
    h = x_ref[...] + _dot_split(merged.astype(BF16), wo_ref[...])
    h_ref[...] = h
    ms = jnp.mean(h * h, axis=-1, keepdims=True)
    hn = h * lax.rsqrt(ms + EPS) * g_ref[...]
    hn_ref[...] = _pack_halves(hn)

    hn_hi = hn.astype(BF16)
    hn_lo = (hn - hn_hi.astype(F32)).astype(BF16)
    logits = (_dot_split(hn_hi, wrh_ref[...]) + _dot_split(hn_lo, wrh_ref[...])
              + _dot_split(hn_hi, wrl_ref[...]))
    logits = logits + br_ref[...]
    lane = lax.broadcasted_iota(I32, logits.shape, 1)
    vals, ids = [], []
    for _ in range(TOP_K_EXPERTS):
        mx = jnp.max(logits, axis=-1, keepdims=True)
        ix = jnp.min(jnp.where(logits == mx, lane, LANES), axis=-1, keepdims=True)
        vals.append(mx)
        ids.append(ix)
        logits = jnp.where(lane == ix, -jnp.inf, logits)
    es = [jnp.exp(v - vals[0]) for v in vals]
    inv = 1.0 / (es[0] + es[1] + es[2] + es[3])
    route = jnp.zeros(logits.shape, F32)
    for k in range(TOP_K_EXPERTS):
        route = jnp.where(lane == k, es[k] * inv, route)
        route = jnp.where(lane == TOP_K_EXPERTS + k, ids[k].astype(F32), route)
    route_ref[...] = route


def _merge(x2, y_diff, y_dsa, proj, w_bd, w_bs, w_out, g_ffn, w_router_hi, w_router_lo, b_router, tm):
    n, d = x2.shape
    row = lambda i: (i, 0)
    const = lambda i: (0, 0)
    return pl.pallas_call(
        _merge_kernel,
        out_shape=(jax.ShapeDtypeStruct((n, d), F32),
                   jax.ShapeDtypeStruct((n, d // 2), I32),
                   jax.ShapeDtypeStruct((n, LANES), F32)),
        grid=(n // tm,),
        in_specs=[pl.BlockSpec((tm, d), row),
                  pl.BlockSpec((tm, d), row),
                  pl.BlockSpec((tm, d), row),
                  pl.BlockSpec((tm, d), lambda i: (i, COL_GA // 1024)),
                  pl.BlockSpec((tm, d), lambda i: (i, COL_GB // 1024)),
                  pl.BlockSpec((d, d), const),
                  pl.BlockSpec((d, d), const),
                  pl.BlockSpec((d, d), const),
                  pl.BlockSpec((1, d), const),
                  pl.BlockSpec((d, LANES), const),
                  pl.BlockSpec((d, LANES), const),
                  pl.BlockSpec((1, LANES), const)],
        out_specs=(pl.BlockSpec((tm, d), row),
                   pl.BlockSpec((tm, d // 2), row),
                   pl.BlockSpec((tm, LANES), row)),
        compiler_params=_cparams(("parallel",)),
        name="merge_router",
    )(x2, y_diff, y_dsa, proj, proj, w_bd, w_bs, w_out, g_ffn, w_router_hi, w_router_lo, b_router)


def _regroup_kernel(w_ref, p_ref, o_ref):
    pw = p_ref.shape[0]
    for j in range(w_ref.shape[2] // pw):
        w = w_ref[0, :, j * pw:(j + 1) * pw].astype(BF16)
        o_ref[0, :, j * pw:(j + 1) * pw] = jnp.dot(w, p_ref[...], preferred_element_type=F32).astype(BF16)


def _regroup_gate_up(w_gu, rows):
    e, d, f2 = w_gu.shape
    pw = 2 * LANES
    src = jnp.arange(pw, dtype=I32)
    dst = (src % 2) * LANES + src // 2
    perm = (dst[:, None] == jnp.arange(pw, dtype=I32)[None, :]).astype(BF16)
    return pl.pallas_call(
        _regroup_kernel,
        out_shape=jax.ShapeDtypeStruct((e, d, f2), BF16),
        grid=(e, d // rows),
        in_specs=[pl.BlockSpec((1, rows, f2), lambda i, j: (i, j, 0)),
                  pl.BlockSpec((pw, pw), lambda i, j: (0, 0))],
        out_specs=pl.BlockSpec((1, rows, f2), lambda i, j: (i, j, 0)),
        compiler_params=_cparams(("parallel", "parallel")),
        name="regroup_gate_up",
    )(w_gu, perm)


def _ffn_kernel(be_ref, nu_ref, x_ref, wgu_ref, wd_ref, bgu_ref, bd_ref, o_ref):
    @pl.when(pl.program_id(0) < nu_ref[0])
    def _():
        wd = wd_ref[0].astype(BF16)
        rows = x_ref.shape[0] // FFN_STREAMS
        for r in range(FFN_STREAMS):
            x_lo, x_hi = _unpack_halves(x_ref[r * rows:(r + 1) * rows, :])
            half = x_lo.shape[1]
            gu = (jnp.dot(x_lo.astype(BF16), wgu_ref[0, 0:half, :], preferred_element_type=F32)
                  + jnp.dot(x_hi.astype(BF16), wgu_ref[0, half:2 * half, :], preferred_element_type=F32)
                  + bgu_ref[0])
            acts = []
            for j in range(gu.shape[1] // (2 * LANES)):
                gate = jnp.minimum(gu[:, 2 * j * LANES:(2 * j + 1) * LANES], SWIGLU_LIMIT)
                up = jnp.clip(gu[:, (2 * j + 1) * LANES:(2 * j + 2) * LANES], -SWIGLU_LIMIT, SWIGLU_LIMIT)
                glu = gate * jax.nn.sigmoid(gate * SWIGLU_ALPHA)
                acts.append(((up + 1.0) * glu).astype(BF16))
            a = jnp.concatenate(acts, axis=1)
            y = jnp.dot(a, wd, preferred_element_type=F32) + bd_ref[0]
            o_ref[r * rows:(r + 1) * rows, :] = _pack_halves(y)

    @pl.when(pl.program_id(0) >= nu_ref[0])
    def _():
        o_ref[...] = jnp.zeros(o_ref.shape, o_ref.dtype)


def _expert_ffn(blk_exp, n_used, xs, wgu, wd, bgu, bd):
    p, dw = xs.shape
    f, d = wd.shape[1], wd.shape[2]
    nblk = p // MOE_ROWS
    wmap = lambda i, be, nu: (be[i], 0, 0)
    grid_spec = pltpu.PrefetchScalarGridSpec(
        num_scalar_prefetch=2,
        grid=(nblk,),
        in_specs=[pl.BlockSpec((MOE_ROWS, dw), lambda i, be, nu: (i, 0)),
                  pl.BlockSpec((1, d, 2 * f), wmap),
                  pl.BlockSpec((1, f, d), wmap),
                  pl.BlockSpec((1, 1, 2 * f), wmap),
                  pl.BlockSpec((1, 1, d), wmap)],
        out_specs=pl.BlockSpec((MOE_ROWS, dw), lambda i, be, nu: (i, 0)),
    )
    return pl.pallas_call(
        _ffn_kernel,
        out_shape=jax.ShapeDtypeStruct((p, dw), I32),
        grid_spec=grid_spec,
        compiler_params=_cparams(("arbitrary",)),
        name="expert_ffn",
    )(blk_exp, n_used, xs, wgu, wd, bgu, bd)


def _route_kernel(route_ref, dest_ref, cnt_ref, u_ref, carry_ref, pstart_ref, *, block_rows):
    ph, i = pl.program_id(0), pl.program_id(1)
    tm = route_ref.shape[0]

    @pl.when((ph == 0) & (i == 0))
    def _():
        r = lax.broadcasted_iota(I32, (tm, tm), 0)
        c = lax.broadcasted_iota(I32, (tm, tm), 1)
        u_ref[...] = jnp.where(r < c, 1.0, 0.0).astype(BF16)
        carry_ref[...] = jnp.zeros(carry_ref.shape, F32)

    @pl.when((ph == 1) & (i == 0))
    def _():
        counts = carry_ref[...]
        cnt_ref[...] = counts
        padded = jnp.ceil(counts * (1.0 / block_rows)) * block_rows
        r = lax.broadcasted_iota(I32, (LANES, LANES), 0)
        c = lax.broadcasted_iota(I32, (LANES, LANES), 1)
        lower = jnp.where(c < r, 1.0, 0.0).astype(F32)
        pstart_ref[...] = jnp.dot(lower, padded, preferred_element_type=F32, precision=lax.Precision.HIGHEST)
        carry_ref[...] = jnp.zeros(carry_ref.shape, F32)

    rt = route_ref[...].T
    sub = lax.broadcasted_iota(I32, (LANES, tm), 0)
    hits = [sub == rt[TOP_K_EXPERTS + k:TOP_K_EXPERTS + k + 1, :].astype(I32) for k in range(TOP_K_EXPERTS)]
    m = jnp.zeros((LANES, tm), F32)
    for hit in hits:
        m = m + jnp.where(hit, 1.0, 0.0)
    tile_counts = jnp.broadcast_to(jnp.sum(m, axis=1, keepdims=True), (LANES, LANES))

    @pl.when(ph == 0)
    def _():
        dest_ref[...] = jnp.zeros(dest_ref.shape, I32)

    @pl.when(ph == 1)
    def _():
        prefix = jnp.dot(m.astype(BF16), u_ref[...], preferred_element_type=F32)
        rank = prefix + (pstart_ref[:, 0:1] + carry_ref[:, 0:1])
        rows = [jnp.sum(jnp.where(hit, rank, 0.0), axis=0, keepdims=True) for hit in hits]
        rows.append(jnp.zeros((dest_ref.shape[0] - TOP_K_EXPERTS, tm), F32))
        dest_ref[...] = jnp.concatenate(rows, axis=0).astype(I32)

    carry_ref[...] = carry_ref[...] + tile_counts


def _route_rows(route, tm):
    n = route.shape[0]
    nt = n // tm
    kernel = functools.partial(_route_kernel, block_rows=MOE_ROWS)
    return pl.pallas_call(
        kernel,
        out_shape=(jax.ShapeDtypeStruct((8, n + tm), I32), jax.ShapeDtypeStruct((LANES, LANES), F32)),
        grid=(2, nt),
        in_specs=[pl.BlockSpec((tm, LANES), lambda ph, i: (i, 0))],
        out_specs=(pl.BlockSpec((8, tm), lambda ph, i: (0, ph * i + (1 - ph) * nt)),
                   pl.BlockSpec((LANES, LANES), lambda ph, i: (0, 0))),
        scratch_shapes=[pltpu.VMEM((tm, tm), BF16),
                        pltpu.VMEM((LANES, LANES), F32),
                        pltpu.VMEM((LANES, LANES), F32)],
        compiler_params=_cparams(("arbitrary", "arbitrary")),
        name="route_rows",
    )(route)


SC_WINDOW = 128
SC_WORKERS = 32


def _sc_mesh():
    return plsc.VectorSubcoreMesh(core_axis_name="c", subcore_axis_name="s")


def _sc_scatter_rows(src, dest, p):
    n, d = src.shape
    assert n % (SC_WINDOW * SC_WORKERS) == 0, "token count must split evenly over the vector subcores"
    per = n // (SC_WINDOW * SC_WORKERS)

    @pl.kernel(out_type=jax.ShapeDtypeStruct((p, d), src.dtype), mesh=_sc_mesh(),
               scratch_types=[pltpu.VMEM((dest.shape[0], SC_WINDOW), I32), pltpu.VMEM((SC_WINDOW, d), src.dtype)])
    def scatter(src_hbm, idx_hbm, out_hbm, idx_vmem, buf):
        wid = lax.axis_index("c") * (SC_WORKERS // 2) + lax.axis_index("s")

        @pl.loop(0, per)
        def _(j):
            off = (wid * per + j) * SC_WINDOW
            pltpu.sync_copy(idx_hbm.at[:, pl.ds(off, SC_WINDOW)], idx_vmem)
            pltpu.sync_copy(src_hbm.at[pl.ds(off, SC_WINDOW), :], buf)
            for k in range(TOP_K_EXPERTS):
                pltpu.sync_copy(buf, out_hbm.at[idx_vmem.at[k]])

    return scatter(src, dest)


def _sc_gather_rows(src, dest):
    n = dest.shape[1]
    d = src.shape[1]
    assert n % (SC_WINDOW * SC_WORKERS) == 0, "token count must split evenly over the vector subcores"
    per = n // (SC_WINDOW * SC_WORKERS)

    @pl.kernel(out_type=jax.ShapeDtypeStruct((TOP_K_EXPERTS * n, d), src.dtype), mesh=_sc_mesh(),
               scratch_types=[pltpu.VMEM((dest.shape[0], SC_WINDOW), I32), pltpu.VMEM((SC_WINDOW, d), src.dtype)])
    def gather(src_hbm, idx_hbm, out_hbm, idx_vmem, buf):
        wid = lax.axis_index("c") * (SC_WORKERS // 2) + lax.axis_index("s")

        @pl.loop(0, per)
        def _(j):
            off = (wid * per + j) * SC_WINDOW
            pltpu.sync_copy(idx_hbm.at[:, pl.ds(off, SC_WINDOW)], idx_vmem)
            for k in range(TOP_K_EXPERTS):
                pltpu.sync_copy(src_hbm.at[idx_vmem.at[k]], buf)
                pltpu.sync_copy(buf, out_hbm.at[pl.ds(k * n + off, SC_WINDOW), :])

    return gather(src, dest)


def _combine_kernel(h_ref, y_ref, route_ref, g_ref, o_ref):
    half = h_ref.shape[1] // 2
    h_lo, h_hi = h_ref[:, 0:half], h_ref[:, half:2 * half]
    route = route_ref[...]
    for k in range(TOP_K_EXPERTS):
        y_lo, y_hi = _unpack_halves(y_ref[k])
        gate = route[:, k:k + 1]
        h_lo = h_lo + gate * y_lo
        h_hi = h_hi + gate * y_hi
    ms = (jnp.sum(h_lo * h_lo, axis=-1, keepdims=True)
          + jnp.sum(h_hi * h_hi, axis=-1, keepdims=True)) * (1.0 / (2 * half))
    inv = lax.rsqrt(ms + EPS)
    o_ref[:, 0:half] = h_lo * inv * g_ref[:, 0:half]
    o_ref[:, half:2 * half] = h_hi * inv * g_ref[:, half:2 * half]


def _combine(h1, yg, route, g_final, tm):
    n, d = h1.shape
    return pl.pallas_call(
        _combine_kernel,
        out_shape=jax.ShapeDtypeStruct((n, d), F32),
        grid=(n // tm,),
        in_specs=[pl.BlockSpec((tm, d), lambda i: (i, 0)),
                  pl.BlockSpec((TOP_K_EXPERTS, tm, d // 2), lambda i: (0, i, 0)),
                  pl.BlockSpec((tm, LANES), lambda i: (i, 0)),
                  pl.BlockSpec((1, d), lambda i: (0, 0))],
        out_specs=pl.BlockSpec((tm, d), lambda i: (i, 0)),
        compiler_params=_cparams(("parallel",)),
        name="combine_norm",
    )(h1, yg, route, g_final)


def _t5_bucket(dist):
    n = jnp.maximum(dist, 0)
    max_exact = N_BUCKETS // 2
    nf = jnp.maximum(n, 1).astype(F32)
    large = max_exact + (jnp.log(nf / max_exact) / math.log(MAX_DISTANCE / max_exact)
                         * (N_BUCKETS - max_exact)).astype(I32)
    large = jnp.minimum(large, N_BUCKETS - 1)
    return jnp.where(n < max_exact, n, large)


def _bias_blocks(bias_tab):
    t = LANES
    assert MAX_DISTANCE <= LANES
    r = jnp.arange(t, dtype=I32)[:, None]
    c = jnp.arange(t, dtype=I32)[None, :]
    rel = (bias_tab - bias_tab[N_BUCKETS - 1][None, :]).astype(F32)
    tiles = []
    buckets = jnp.arange(N_BUCKETS, dtype=I32)[:, None, None]
    for delta in (0, t):
        dist = r - c + delta
        hit = _t5_bucket(dist)[None] == buckets
        b = jnp.sum(jnp.where(hit[:, None], rel[:, :, None, None], 0.0), axis=0)
        tiles.append(jnp.where((dist >= 0)[None], b * LOG2E, MASK_VALUE))
    return jnp.stack(tiles)


def _regroup_w_in(w_in):
    sizes = (1024, 1024, 1024, 1024, KV_LATENT, 1024, HEAD_DIM_IDX, N_HEADS_IDX, D_MODEL, D_MODEL)
    parts, off = [], 0
    for sz in sizes:
        parts.append(w_in[:, off:off + sz])
        off += sz
    dq, dk, dv, sq, ckv, iq, ik, iw, ga, gb = parts
    dq = dq * (HEAD_DIM_DIFF ** -0.5 * LOG2E)
    iw = iw * ((N_HEADS_IDX ** -0.5) * (HEAD_DIM_IDX ** -0.5))
    pad = jnp.zeros((w_in.shape[0], PROJ_WIDTH - COL_IW - N_HEADS_IDX), w_in.dtype)
    w = jnp.concatenate([dq, dk, dv, sq, iq, ga, gb, ckv, ik, ik, iw, pad], axis=1)
    return w.astype(BF16)


def _block_tables(counts, n_assign):
    e, bm = N_EXPERTS, MOE_ROWS
    padded = (counts + bm - 1) // bm * bm
    pends = jnp.cumsum(padded)
    nblk = -(-(n_assign + e * (bm - 1)) // bm)
    first_row = jnp.arange(nblk, dtype=I32) * bm
    blk_exp = jnp.minimum(jnp.sum((pends[None, :] <= first_row[:, None]).astype(I32), axis=1), e - 1)
    n_used = (pends[-1] // bm).astype(I32).reshape(1)
    return blk_exp, n_used, nblk


def kernel(x, norm_attn_g, w_in, rel_bias, lam_q1, lam_k1, lam_q2, lam_k2, diff_subln_g, kv_norm_g, w_uk, w_uv,
           w_branch_diff, w_branch_dsa, w_out, norm_ffn_g, w_router, b_router, w_gate_up, b_gate_up, w_down,
           b_down, norm_final_g):
    batch, seq, d = x.shape
    n = batch * seq
    assert norm_attn_g.shape[0] == 1, "single-layer kernel"
    assert seq % DIFF_BLOCK == 0 and seq % KEY_CHUNK == 0 and d == D_MODEL
    assert seq <= (2 ** 15 - 1) * PACKED_ROWS, "int16 per-element key counts"
    row_tile = math.gcd(n, ROW_TILE)
    token_tile = math.gcd(n, TOKEN_TILE)

    x2 = x.reshape(n, d)
    proj = _inproj(x2, norm_attn_g[0].reshape(1, d), _regroup_w_in(w_in[0]), row_tile, PROJ_COLS)

    lam_init = 0.8 - 0.6 * math.exp(-0.3 * 0)
    lam = (jnp.exp(jnp.sum(lam_q1[0].astype(F32) * lam_k1[0].astype(F32)))
           - jnp.exp(jnp.sum(lam_q2[0].astype(F32) * lam_k2[0].astype(F32))) + lam_init)
    y_diff = _diff_attention(proj, lam.reshape(1, 1).astype(F32), _bias_blocks(rel_bias[:, :N_HEADS_DIFF]),
                             diff_subln_g[0].reshape(1, -1).astype(F32), batch, seq, 1.0 - lam_init)

    y_dsa = _dsa_attention(proj, kv_norm_g[0].reshape(1, -1).astype(F32),
                           w_uk[0].transpose(0, 2, 1).astype(BF16), w_uv[0].astype(BF16),
                           _bias_blocks(rel_bias[:, N_HEADS_DIFF:]), batch, seq, min(TOPK_MAX, seq // 4))

    w_r = jnp.zeros((d, LANES), F32).at[:, :N_EXPERTS].set(w_router[0].astype(F32))
    b_r = jnp.full((1, LANES), MASK_VALUE, F32).at[0, :N_EXPERTS].set(b_router[0].astype(F32))
    h1, hn, route = _merge(x2, y_diff, y_dsa, proj, w_branch_diff[0].astype(BF16), w_branch_dsa[0].astype(BF16),
                           w_out[0].astype(BF16), norm_ffn_g[0].reshape(1, d).astype(F32),
                           w_r.astype(BF16), (w_r - w_r.astype(BF16).astype(F32)).astype(BF16), b_r,
                           row_tile)

    dest, counts = _route_rows(route, row_tile)
    dest = dest[:, :n]
    blk_exp, n_used, nblk = _block_tables(counts[:N_EXPERTS, 0].astype(I32), n * TOP_K_EXPERTS)
    xs = _sc_scatter_rows(hn, dest, nblk * MOE_ROWS)
    e, f = N_EXPERTS, D_EXPERT
    b_gu = b_gate_up[0].astype(F32).reshape(e, f // LANES, LANES, 2).transpose(0, 1, 3, 2).reshape(e, 1, 2 * f)
    ys = _expert_ffn(blk_exp, n_used, xs, _regroup_gate_up(w_gate_up[0], TOKEN_TILE), w_down[0],
                     b_gu, b_down[0][:, None, :].astype(F32))
    yg = _sc_gather_rows(ys, dest).reshape(TOP_K_EXPERTS, n, d // 2)
    out = _combine(h1, yg, route, norm_final_g.reshape(1, d).astype(F32), token_tile)
    return out.reshape(batch, seq, d)
```

```python
import functools
import math

import jax
import jax.numpy as jnp
from jax import lax
from jax.experimental import pallas as pl
from jax.experimental.pallas import tpu as pltpu
from jax.experimental.pallas import tpu_sc as plsc

F32 = jnp.float32
BF16 = jnp.bfloat16
I32 = jnp.int32

D_MODEL = 1024
N_HEADS_DIFF = 8
HEAD_DIM_DIFF = 64
N_HEADS_DSA = 8
HEAD_DIM_DSA = 128
KV_LATENT = 256
N_HEADS_IDX = 16
HEAD_DIM_IDX = 64
TOPK_MAX = 256
N_BUCKETS = 32
MAX_DISTANCE = 128
N_EXPERTS = 32
TOP_K_EXPERTS = 4
D_EXPERT = 1024
SWIGLU_LIMIT = 7.0
SWIGLU_ALPHA = 1.702
EPS = 1e-6

LANES = 128
PACKED_ROWS = 16
ROW_TILE = 1024
PROJ_COLS = 2560
TOKEN_TILE = 512
DIFF_BLOCK = 512
DIFF_HEADS_PER_STEP = 4
DSA_BLOCK = 256
KEY_CHUNK = 512
DSA_STREAMS = 2
MOE_ROWS = 512
FFN_STREAMS = 2
PROJ_WIDTH = 7680
VMEM_LIMIT = 56 * 1024 * 1024

COL_DQ, COL_DK, COL_DV, COL_SQ, COL_IQ, COL_GA, COL_GB = (i * 1024 for i in range(7))
COL_CKV = 7168
COL_IK = 7424
COL_IW = 7552

LOG2E = math.log2(math.e)
MASK_VALUE = -1e30
M_INIT = -1e29
INT_MIN = -2 ** 31


def _cparams(sem):
    return pltpu.CompilerParams(dimension_semantics=sem, vmem_limit_bytes=VMEM_LIMIT)


def _inproj_kernel(x_ref, g_ref, w_ref, o_ref, xn_ref):
    @pl.when(pl.program_id(1) == 0)
    def _():
        x = x_ref[...]
        ms = jnp.mean(x * x, axis=-1, keepdims=True)
        xn_ref[...] = (x * lax.rsqrt(ms + EPS) * g_ref[...]).astype(BF16)

    o_ref[...] = jnp.dot(xn_ref[...], w_ref[...], preferred_element_type=F32).astype(o_ref.dtype)


def _inproj(x2, g, w, tm, tn):
    n, d = x2.shape
    width = w.shape[1]
    return pl.pallas_call(
        _inproj_kernel,
        out_shape=jax.ShapeDtypeStruct((n, width), BF16),
        grid=(n // tm, width // tn),
        in_specs=[pl.BlockSpec((tm, d), lambda i, j: (i, 0)),
                  pl.BlockSpec((1, d), lambda i, j: (0, 0)),
                  pl.BlockSpec((d, tn), lambda i, j: (0, j))],
        out_specs=pl.BlockSpec((tm, tn), lambda i, j: (i, j)),
        scratch_shapes=[pltpu.VMEM((tm, d), BF16)],
        compiler_params=_cparams(("parallel", "arbitrary")),
        name="inproj",
    )(x2, g, w)


def _lane_chunk_max(s):
    smax = s[:, 0:LANES]
    for j in range(1, s.shape[1] // LANES):
        smax = jnp.maximum(smax, s[:, j * LANES:(j + 1) * LANES])
    return smax


def _softmax_step(s_ref, tk, v, m_ref, l_ref, acc_ref, smax=None, split=True):
    nl = tk // LANES
    if smax is None:
        smax = _lane_chunk_max(s_ref[:, 0:tk])
    m_prev = m_ref[...]
    m_new = jnp.maximum(m_prev, jnp.max(smax, axis=-1, keepdims=True))
    alpha = jnp.exp2(m_prev - m_new)
    psum = None
    ps = []
    for j in range(nl):
        pj = jnp.exp2(s_ref[:, j * LANES:(j + 1) * LANES] - m_new)
        psum = pj if psum is None else psum + pj
        ps.append(pj.astype(BF16))
    l_ref[...] = alpha * l_ref[...] + psum
    pv = _dot_split(jnp.concatenate(ps, axis=1), v, split)
    e = acc_ref.shape[1]
    a = alpha if e == LANES else jnp.concatenate([alpha] * (e // LANES), axis=1)
    acc_ref[...] = a * acc_ref[...] + pv
    m_ref[...] = m_new


def _softmax_init(m_ref, l_ref, acc_ref):
    m_ref[...] = jnp.full(m_ref.shape, M_INIT, F32)
    l_ref[...] = jnp.zeros(l_ref.shape, F32)
    acc_ref[...] = jnp.zeros(acc_ref.shape, F32)


def _softmax_result(l_ref, acc_ref):
    return acc_ref[...] * (1.0 / jnp.sum(l_ref[...], axis=-1, keepdims=True))


def _near_bias(s_ref, d0, d1, delta, groups, t, tk):
    for g in range(groups):
        for rb in range(t // LANES):
            for cb in range(tk // LANES):
                bd = delta + rb - cb
                rows = slice(g * t + rb * LANES, g * t + (rb + 1) * LANES)
                cols = slice(cb * LANES, (cb + 1) * LANES)
                if bd == 0:
                    s_ref[rows, cols] = s_ref[rows, cols] + d0(g)
                elif bd == 1:
                    s_ref[rows, cols] = s_ref[rows, cols] + d1(g)
                elif bd < 0:
                    s_ref[rows, cols] = jnp.full((LANES, LANES), MASK_VALUE, F32)


def _dot_nt(a, b, split=True):
    dn = (((1,), (1,)), ((), ()))
    if not split:
        return lax.dot_general(a, b, dn, preferred_element_type=F32)
    h = a.shape[0] // 2
    return jnp.concatenate([lax.dot_general(a[:h], b, dn, preferred_element_type=F32),
                            lax.dot_general(a[h:], b, dn, preferred_element_type=F32)], axis=0)


def _dot_split(a, b, split=True):
    if not split:
        return jnp.dot(a, b, preferred_element_type=F32)
    h = a.shape[0] // 2
    return jnp.concatenate([jnp.dot(a[:h], b, preferred_element_type=F32),
                            jnp.dot(a[h:], b, preferred_element_type=F32)], axis=0)


def _diff_kernel(lam_ref, q_ref, k_ref, v_ref, bias_ref, g_ref, o_ref, q2_ref, s_ref, m_ref, l_ref, acc_ref,
                 *, out_scale, heads):
    t = q_ref.shape[0]
    e = 2 * HEAD_DIM_DIFF
    qi = pl.program_id(2)
    cols = [slice(hh * e, (hh + 1) * e) for hh in range(heads)]

    lane = lax.broadcasted_iota(I32, (t, e), 1)
    for hh in range(heads):
        q = q_ref[:, cols[hh]]
        zero = jnp.zeros_like(q)
        q2_ref[hh, 0:t, :] = jnp.where(lane < HEAD_DIM_DIFF, q, zero)
        q2_ref[hh, t:2 * t, :] = jnp.where(lane >= HEAD_DIM_DIFF, q, zero)
        _softmax_init(m_ref.at[hh], l_ref.at[hh], acc_ref.at[hh])

    def chunk(kc, delta):
        off = pl.multiple_of(kc * t, t)
        for hh in range(heads):
            s = _dot_nt(q2_ref[hh], k_ref[pl.ds(off, t), cols[hh]])
            s_ref[hh] = s
            smax = None
            if delta is None:
                smax = _lane_chunk_max(s)
            else:
                _near_bias(s_ref.at[hh], lambda g: bias_ref[0, hh], lambda g: bias_ref[1, hh], delta, 2, t, t)
            _softmax_step(s_ref.at[hh], t, v_ref[pl.ds(off, t), cols[hh]], m_ref.at[hh], l_ref.at[hh],
                          acc_ref.at[hh], smax)

    def far_body(kc, carry):
        chunk(kc, None)
        return carry

    lax.fori_loop(0, jnp.maximum(qi - 1, 0), far_body, 0)

    @pl.when(qi >= 1)
    def _():
        chunk(qi - 1, t // LANES)

    chunk(qi, 0)

    for hh in range(heads):
        o = _softmax_result(l_ref.at[hh], acc_ref.at[hh])
        o = o[0:t, :] - lam_ref[0, 0] * o[t:2 * t, :]
        ms = jnp.mean(o * o, axis=-1, keepdims=True)
        o_ref[:, cols[hh]] = (o * lax.rsqrt(ms + EPS) * g_ref[...] * out_scale).astype(o_ref.dtype)


def _diff_attention(proj, lam, bias_blocks, subln_g, batch, seq, out_scale):
    t = DIFF_BLOCK
    nq = seq // t
    h = N_HEADS_DIFF
    e = 2 * HEAD_DIM_DIFF
    hp = DIFF_HEADS_PER_STEP
    w = hp * e
    kernel = functools.partial(_diff_kernel, out_scale=out_scale, heads=hp)
    return pl.pallas_call(
        kernel,
        out_shape=jax.ShapeDtypeStruct((batch * seq, h * e), BF16),
        grid=(batch, h // hp, nq),
        in_specs=[pl.BlockSpec(memory_space=pltpu.SMEM),
                  pl.BlockSpec((t, w), lambda b, hh, qi: (b * nq + qi, COL_DQ // w + hh)),
                  pl.BlockSpec((seq, w), lambda b, hh, qi: (b, COL_DK // w + hh)),
                  pl.BlockSpec((seq, w), lambda b, hh, qi: (b, COL_DV // w + hh)),
                  pl.BlockSpec((2, hp, LANES, LANES), lambda b, hh, qi: (0, hh, 0, 0)),
                  pl.BlockSpec((1, e), lambda b, hh, qi: (0, 0))],
        out_specs=pl.BlockSpec((t, w), lambda b, hh, qi: (b * nq + qi, hh)),
        scratch_shapes=[pltpu.VMEM((hp, 2 * t, e), BF16),
                        pltpu.VMEM((hp, 2 * t, t), F32),
                        pltpu.VMEM((hp, 2 * t, LANES), F32),
                        pltpu.VMEM((hp, 2 * t, LANES), F32),
                        pltpu.VMEM((hp, 2 * t, e), F32)],
        compiler_params=_cparams(("parallel", "parallel", "arbitrary")),
        name="diff_attn",
    )(lam, proj, proj, proj, bias_blocks, subln_g)


def _sortable_key(x):
    bits = lax.bitcast_convert_type(x, I32)
    return bits ^ ((bits >> 31) & jnp.int32(0x7FFFFFFF))


def _dsa_kernel(iq_ref, sq_ref, iw_ref, ik_ref, ckv_ref, kvg_ref, wuk_ref, wuv_ref, bias_ref, o_ref,
                c_ref, key_ref, keyt_ref, keyh_ref, qi_ref, wb_ref, ql_ref, s_ref, m_ref, l_ref, acc_ref, *, topk, scale):
    t = iq_ref.shape[0]
    tk = KEY_CHUNK
    qi = pl.program_id(1)
    n_chunks = qi + 1
    hi, hb = N_HEADS_IDX, N_HEADS_DSA

    @pl.when(qi == 0)
    def _():
        ckv = ckv_ref[...].astype(F32)
        ms = jnp.mean(ckv * ckv, axis=-1, keepdims=True)
        c_ref[...] = (ckv * lax.rsqrt(ms + EPS) * kvg_ref[...]).astype(BF16)

    rg = qi_ref.shape[2]
    lane = lax.broadcasted_iota(I32, (t, LANES), 1)
    for h in range(hi):
        blk = iq_ref[:, (h // 2) * LANES:(h // 2 + 1) * LANES]
        keep = (lane < HEAD_DIM_IDX) if h % 2 == 0 else (lane >= HEAD_DIM_IDX)
        qi_ref[:, h, :, :] = jnp.where(keep, blk, jnp.zeros_like(blk)).reshape(t // rg, rg, LANES)
        wb_ref[:, h, :, :] = jnp.broadcast_to(iw_ref[:, h:h + 1].astype(F32), (t, LANES)).reshape(t // rg, rg, LANES)

    def score_chunk(kc, diag):
        off = pl.multiple_of(kc * t, t)
        d = _dot_nt(qi_ref[...].reshape(hi * t, LANES), ik_ref[pl.ds(off, t), :]).reshape(t // rg, hi, rg, t)
        w = jnp.concatenate([wb_ref[...]] * (t // LANES), axis=-1)
        sc = jnp.sum(jnp.maximum(d, 0.0) * w, axis=1).reshape(t, t)
        key = _sortable_key(sc + 0.0)
        if diag:
            row = lax.broadcasted_iota(I32, (t, t), 0)
            col = lax.broadcasted_iota(I32, (t, t), 1)
            key = jnp.where(col <= row, key, jnp.int32(INT_MIN))
        key_ref[:, pl.ds(off, t)] = key
        key_t = key.T
        keyt_ref[pl.ds(off, t), :] = key_t
        keyh_ref[pl.ds(off, t), :] = (key_t >> 16).astype(jnp.int16)

    def score_body(j, carry):
        for u in range(4):
            score_chunk(4 * j + u, False)
        return carry

    lax.fori_loop(0, qi // 4, score_body, 0)
    done = qi // 4 * 4

    @pl.when(qi % 4 >= 2)
    def _():
        score_chunk(done, False)
        score_chunk(done + 1, False)

    @pl.when(qi % 2 == 1)
    def _():
        score_chunk(qi - 1, False)

    score_chunk(qi, True)

    @pl.when(qi % 2 == 0)
    def _():
        off = pl.multiple_of((qi + 1) * t, t)
        key_ref[:, pl.ds(off, t)] = jnp.full((t, t), INT_MIN, I32)
        keyt_ref[pl.ds(off, t), :] = jnp.full((t, t), INT_MIN, I32)
        keyh_ref[pl.ds(off, t), :] = jnp.full((t, t), INT_MIN >> 16, jnp.int16)

    n_steps = (qi + 2) // 2

    def count_ge(cand_s):
        def body(kc, cnt):
            off = pl.multiple_of(kc * tk, tk)
            k = keyt_ref[pl.ds(off, tk), :].reshape(tk // 8, 8, t)
            return cnt + jnp.sum(jnp.where(k >= cand_s[None], 1, 0), axis=0)

        cnt = lax.fori_loop(0, n_steps, body, jnp.zeros((8, t), I32))
        return jnp.sum(cnt, axis=0, keepdims=True)

    def count_packed(c_row):
        rows = PACKED_ROWS
        c16 = jnp.broadcast_to(c_row.astype(jnp.int16), (rows, t))

        def body(kc, cnt):
            off = pl.multiple_of(kc * tk, tk)
            k = keyh_ref[pl.ds(off, tk), :].reshape(tk // rows, rows, t)
            hit = jnp.where(k >= c16[None], jnp.int16(1), jnp.int16(0))
            parts = [cnt, hit[0]]
            for j in range(1, tk // rows):
                parts[j % 2] = parts[j % 2] + hit[j]
            return parts[0] + parts[1]

        cnt = lax.fori_loop(0, n_steps, body, jnp.zeros((rows, t), jnp.int16))
        return jnp.sum(cnt.astype(I32), axis=0, keepdims=True)

    def make_bit_body(count):
        def bit_body(i, carry):
            cur, n_ge = carry
            bit = lax.shift_left(jnp.int32(1), 31 - i)
            cand = cur | bit
            total = count(cand ^ jnp.int32(INT_MIN))
            accept = total >= topk
            return jnp.where(accept, cand, cur), jnp.where(accept, total, n_ge)
        return bit_body

    zeros8 = jnp.zeros((8, t), I32)
    low_bias = jnp.int32(1 << 15)
    cur, n_ge = lax.fori_loop(
        0, 16, make_bit_body(lambda cand_s: count_packed(cand_s[0:1, :] >> 16)), (zeros8, zeros8))
    high = (cur ^ jnp.int32(INT_MIN)) >> 16
    above_high = jnp.where(high[0:1, :] == 2 ** 15 - 1, 0, count_packed(jnp.minimum(high[0:1, :], 2 ** 15 - 2) + 1))

    def repack_body(kc, carry):
        off = pl.multiple_of(kc * tk, tk)
        k = keyt_ref[pl.ds(off, tk), :].reshape(tk // 8, 8, t)
        low = jnp.where((k >> 16) == high[None], (k & jnp.int32(0xFFFF)) - low_bias, -low_bias)
        keyh_ref[pl.ds(off, tk), :] = low.reshape(tk, t).astype(jnp.int16)
        return carry

    lax.fori_loop(0, n_steps, repack_body, 0)
    cur, n_ge = lax.fori_loop(
        16, 32, make_bit_body(lambda cand_s: above_high + count_packed((cand_s[0:1, :] & jnp.int32(0xFFFF)) - low_bias)),
        (cur, n_ge))
    thr = jnp.maximum(cur ^ jnp.int32(INT_MIN), jnp.int32(INT_MIN + 1))

    def lanes_to_rows(v):
        b = jnp.broadcast_to(v[0:1, :], (LANES, t)).T
        return jnp.concatenate([b] * (tk // LANES), axis=1)

    thr_w = lanes_to_rows(thr)
    ties = jnp.max(n_ge) > topk

    @pl.when(jnp.logical_not(ties))
    def _():
        def mask_body(kc, carry):
            off = pl.multiple_of(kc * tk, tk)
            am = jnp.where(key_ref[:, pl.ds(off, tk)] >= thr_w, 0.0, MASK_VALUE).astype(F32)
            key_ref[:, pl.ds(off, tk)] = lax.bitcast_convert_type(am, I32)
            return carry

        lax.fori_loop(0, n_steps, mask_body, 0)

    @pl.when(ties)
    def _():
        above = jnp.where(thr == jnp.int32(2 ** 31 - 1), 0, count_ge(jnp.minimum(thr, jnp.int32(2 ** 31 - 2)) + 1))
        need_w = lanes_to_rows(topk - above).astype(F32)
        r = lax.broadcasted_iota(I32, (tk, tk), 0)
        c = lax.broadcasted_iota(I32, (tk, tk), 1)
        before = jnp.where(r < c, 1.0, 0.0).astype(BF16)
        ones_w = jnp.ones((tk, tk), BF16)

        def tie_body(kc, seen):
            off = pl.multiple_of(kc * tk, tk)
            k = key_ref[:, pl.ds(off, tk)]
            eq = k == thr_w
            eq_b = jnp.where(eq, 1.0, 0.0).astype(BF16)
            rank = seen + jnp.dot(eq_b, before, preferred_element_type=F32)
            keep = (k > thr_w) | (eq & (rank < need_w))
            key_ref[:, pl.ds(off, tk)] = lax.bitcast_convert_type(jnp.where(keep, 0.0, MASK_VALUE).astype(F32), I32)
            return seen + jnp.dot(eq_b, ones_w, preferred_element_type=F32)

        lax.fori_loop(0, n_steps, tie_body, jnp.zeros((t, tk), F32))

    streams = ql_ref.shape[0]
    split = streams == 1
    hs = hb // streams
    for h in range(hb):
        qh = sq_ref[:, h * HEAD_DIM_DSA:(h + 1) * HEAD_DIM_DSA]
        ql = jnp.dot(qh, wuk_ref[h], preferred_element_type=F32) * scale
        ql_ref[h // hs, (h % hs) * t:(h % hs + 1) * t, :] = ql.astype(BF16)
    for g in range(streams):
        _softmax_init(m_ref.at[g], l_ref.at[g], acc_ref.at[g])

    def chunk(kc, width, delta):
        off = pl.multiple_of(kc * tk, tk)
        c = c_ref[pl.ds(off, width), :]
        am = lax.bitcast_convert_type(key_ref[:, pl.ds(off, width)], F32)
        for g in range(streams):
            s = (_dot_nt(ql_ref[g], c, split).reshape(hs, t, width) + am[None]).reshape(hs * t, width)
            s_ref[g, :, 0:width] = s
            smax = None
            if delta is None:
                smax = _lane_chunk_max(s)
            else:
                _near_bias(s_ref.at[g], lambda j: bias_ref[0, g * hs + j], lambda j: bias_ref[1, g * hs + j],
                           delta, hs, t, width)
            _softmax_step(s_ref.at[g], width, c, m_ref.at[g], l_ref.at[g], acc_ref.at[g], smax, split)

    def far_body(kc, carry):
        chunk(kc, tk, None)
        return carry

    lax.fori_loop(0, jnp.maximum((qi - 1) // 2, 0), far_body, 0)
    half = qi // 2

    @pl.when(qi % 2 == 1)
    def _():
        chunk(half, tk, t // LANES)

    @pl.when(qi % 2 == 0)
    def _():
        @pl.when(half >= 1)
        def _():
            chunk(half - 1, tk, tk // LANES)
        chunk(half, t, 0)

    for h in range(hb):
        g, j = h // hs, h % hs
        inv_l = 1.0 / jnp.sum(l_ref[g, j * t:(j + 1) * t, :], axis=-1, keepdims=True)
        ol = (acc_ref[g, j * t:(j + 1) * t, :] * inv_l).astype(BF16)
        o = jnp.dot(ol, wuv_ref[h], preferred_element_type=F32)
        o_ref[:, h * HEAD_DIM_DSA:(h + 1) * HEAD_DIM_DSA] = o.astype(o_ref.dtype)


def _dsa_attention(proj, kv_g, w_ukt, w_uv, bias_blocks, batch, seq, topk):
    t = DSA_BLOCK
    nq = seq // t
    hb, hi = N_HEADS_DSA, N_HEADS_IDX
    width = hb * HEAD_DIM_DSA
    ns = DSA_STREAMS
    rs = hb // ns * t
    kernel = functools.partial(_dsa_kernel, topk=topk, scale=HEAD_DIM_DSA ** -0.5 * LOG2E)
    return pl.pallas_call(
        kernel,
        out_shape=jax.ShapeDtypeStruct((batch * seq, width), BF16),
        grid=(batch, nq),
        in_specs=[pl.BlockSpec((t, 1024), lambda b, qi: (b * nq + qi, COL_IQ // 1024)),
                  pl.BlockSpec((t, 1024), lambda b, qi: (b * nq + qi, COL_SQ // 1024)),
                  pl.BlockSpec((t, LANES), lambda b, qi: (b * nq + qi, COL_IW // LANES)),
                  pl.BlockSpec((seq, LANES), lambda b, qi: (b, COL_IK // LANES)),
                  pl.BlockSpec((seq, KV_LATENT), lambda b, qi: (b, COL_CKV // KV_LATENT)),
                  pl.BlockSpec((1, KV_LATENT), lambda b, qi: (0, 0)),
                  pl.BlockSpec((hb, HEAD_DIM_DSA, KV_LATENT), lambda b, qi: (0, 0, 0)),
                  pl.BlockSpec((hb, KV_LATENT, HEAD_DIM_DSA), lambda b, qi: (0, 0, 0)),
                  pl.BlockSpec((2, hb, LANES, LANES), lambda b, qi: (0, 0, 0, 0))],
        out_specs=pl.BlockSpec((t, width), lambda b, qi: (b * nq + qi, 0)),
        scratch_shapes=[pltpu.VMEM((seq, KV_LATENT), BF16),
                        pltpu.VMEM((t, seq), I32),
                        pltpu.VMEM((seq, t), I32),
                        pltpu.VMEM((seq, t), jnp.int16),
                        pltpu.VMEM((t // PACKED_ROWS, hi, PACKED_ROWS, LANES), BF16),
                        pltpu.VMEM((t // PACKED_ROWS, hi, PACKED_ROWS, LANES), F32),
                        pltpu.VMEM((ns, rs, KV_LATENT), BF16),
                        pltpu.VMEM((ns, rs, KEY_CHUNK), F32),
                        pltpu.VMEM((ns, rs, LANES), F32),
                        pltpu.VMEM((ns, rs, LANES), F32),
                        pltpu.VMEM((ns, rs, KV_LATENT), F32)],
        compiler_params=_cparams(("parallel", "arbitrary")),
        name="dsa_attn",
    )(proj, proj, proj, proj, proj, kv_g, w_ukt, w_uv, bias_blocks)


def _pack_halves(x):
    c = x.shape[1] // 2
    lo = lax.bitcast_convert_type(x[:, :c].astype(BF16).astype(F32), I32)
    hi = lax.bitcast_convert_type(x[:, c:].astype(BF16).astype(F32), I32)
    return lax.shift_right_logical(lo, 16) | (hi & jnp.int32(-65536))


def _unpack_halves(w):
    lo = lax.bitcast_convert_type(lax.shift_left(w, 16), F32)
    hi = lax.bitcast_convert_type(w & jnp.int32(-65536), F32)
    return lo, hi


def _merge_kernel(x_ref, yd_ref, ys_ref, ga_ref, gb_ref, wd_ref, ws_ref, wo_ref, g_ref, wrh_ref, wrl_ref, br_ref,
                  h_ref, hn_ref, route_ref):
    bd = _dot_split(yd_ref[...], wd_ref[...])
    bs = _dot_split(ys_ref[...], ws_ref[...])
    merged = (jax.nn.sigmoid(ga_ref[...].astype(F32)) * bd + jax.nn.sigmoid(gb_ref[...].astype(F32)) * bs)
    h = x_ref[...] + _dot_split(merged.astype(BF16), wo_ref[...])
    h_ref[...] = h
    ms = jnp.mean(h * h, axis=-1, keepdims=True)
    hn = h * lax.rsqrt(ms + EPS) * g_ref[...]
    hn_ref[...] = _pack_halves(hn)

    hn_hi = hn.astype(BF16)
    hn_lo = (hn - hn_hi.astype(F32)).astype(BF16)
    logits = (_dot_split(hn_hi, wrh_ref[...]) + _dot_split(hn_lo, wrh_ref[...])
              + _dot_split(hn_hi, wrl_ref[...]))
    logits = logits + br_ref[...]
    lane = lax.broadcasted_iota(I32, logits.shape, 1)
    vals, ids = [], []
    for _ in range(TOP_K_EXPERTS):
        mx = jnp.max(logits, axis=-1, keepdims=True)
        ix = jnp.min(jnp.where(logits == mx, lane, LANES), axis=-1, keepdims=True)
        vals.append(mx)
        ids.append(ix)
        logits = jnp.where(lane == ix, -jnp.inf, logits)
    es = [jnp.exp(v - vals[0]) for v in vals]
    inv = 1.0 / (es[0] + es[1] + es[2] + es[3])
    route = jnp.zeros(logits.shape, F32)
    for k in range(TOP_K_EXPERTS):
        route = jnp.where(lane == k, es[k] * inv, route)
        route = jnp.where(lane == TOP_K_EXPERTS + k, ids[k].astype(F32), route)
    route_ref[...] = route


def _merge(x2, y_diff, y_dsa, proj, w_bd, w_bs, w_out, g_ffn, w_router_hi, w_router_lo, b_router, tm):
    n, d = x2.shape
    row = lambda i: (i, 0)
    const = lambda i: (0, 0)
    return pl.pallas_call(
        _merge_kernel,
        out_shape=(jax.ShapeDtypeStruct((n, d), F32),
                   jax.ShapeDtypeStruct((n, d // 2), I32),
                   jax.ShapeDtypeStruct((n, LANES), F32)),
        grid=(n // tm,),
        in_specs=[pl.BlockSpec((tm, d), row),
                  pl.BlockSpec((tm, d), row),
                  pl.BlockSpec((tm, d), row),
                  pl.BlockSpec((tm, d), lambda i: (i, COL_GA // 1024)),
                  pl.BlockSpec((tm, d), lambda i: (i, COL_GB // 1024)),
                  pl.BlockSpec((d, d), const),
                  pl.BlockSpec((d, d), const),
                  pl.BlockSpec((d, d), const),
                  pl.BlockSpec((1, d), const),
                  pl.BlockSpec((d, LANES), const),
                  pl.BlockSpec((d, LANES), const),
                  pl.BlockSpec((1, LANES), const)],
        out_specs=(pl.BlockSpec((tm, d), row),
                   pl.BlockSpec((tm, d // 2), row),
                   pl.BlockSpec((tm, LANES), row)),
        compiler_params=_cparams(("parallel",)),
        name="merge_router",
    )(x2, y_diff, y_dsa, proj, proj, w_bd, w_bs, w_out, g_ffn, w_router_hi, w_router_lo, b_router)


def _regroup_kernel(w_ref, p_ref, o_ref):
    pw = p_ref.shape[0]
    for j in range(w_ref.shape[2] // pw):
        w = w_ref[0, :, j * pw:(j + 1) * pw].astype(BF16)
        o_ref[0, :, j * pw:(j + 1) * pw] = jnp.dot(w, p_ref[...], preferred_element_type=F32).astype(BF16)


def _regroup_gate_up(w_gu, rows):
    e, d, f2 = w_gu.shape
    pw = 2 * LANES
    src = jnp.arange(pw, dtype=I32)
    dst = (src % 2) * LANES + src // 2
    perm = (dst[:, None] == jnp.arange(pw, dtype=I32)[None, :]).astype(BF16)
    return pl.pallas_call(
        _regroup_kernel,
        out_shape=jax.ShapeDtypeStruct((e, d, f2), BF16),
        grid=(e, d // rows),
        in_specs=[pl.BlockSpec((1, rows, f2), lambda i, j: (i, j, 0)),
                  pl.BlockSpec((pw, pw), lambda i, j: (0, 0))],
        out_specs=pl.BlockSpec((1, rows, f2), lambda i, j: (i, j, 0)),
        compiler_params=_cparams(("parallel", "parallel")),
        name="regroup_gate_up",
    )(w_gu, perm)


def _ffn_kernel(be_ref, nu_ref, x_ref, wgu_ref, wd_ref, bgu_ref, bd_ref, o_ref):
    @pl.when(pl.program_id(0) < nu_ref[0])
    def _():
        wd = wd_ref[0].astype(BF16)
        rows = x_ref.shape[0] // FFN_STREAMS
        for r in range(FFN_STREAMS):
            x_lo, x_hi = _unpack_halves(x_ref[r * rows:(r + 1) * rows, :])
            half = x_lo.shape[1]
            gu = (jnp.dot(x_lo.astype(BF16), wgu_ref[0, 0:half, :], preferred_element_type=F32)
                  + jnp.dot(x_hi.astype(BF16), wgu_ref[0, half:2 * half, :], preferred_element_type=F32)
                  + bgu_ref[0])
            acts = []
            for j in range(gu.shape[1] // (2 * LANES)):
                gate = jnp.minimum(gu[:, 2 * j * LANES:(2 * j + 1) * LANES], SWIGLU_LIMIT)
                up = jnp.clip(gu[:, (2 * j + 1) * LANES:(2 * j + 2) * LANES], -SWIGLU_LIMIT, SWIGLU_LIMIT)
                glu = gate * jax.nn.sigmoid(gate * SWIGLU_ALPHA)
                acts.append(((up + 1.0) * glu).astype(BF16))
            a = jnp.concatenate(acts, axis=1)
            y = jnp.dot(a, wd, preferred_element_type=F32) + bd_ref[0]
            o_ref[r * rows:(r + 1) * rows, :] = _pack_halves(y)

    @pl.when(pl.program_id(0) >= nu_ref[0])
    def _():
        o_ref[...] = jnp.zeros(o_ref.shape, o_ref.dtype)


def _expert_ffn(blk_exp, n_used, xs, wgu, wd, bgu, bd):
    p, dw = xs.shape
    f, d = wd.shape[1], wd.shape[2]
    nblk = p // MOE_ROWS
    wmap = lambda i, be, nu: (be[i], 0, 0)
    grid_spec = pltpu.PrefetchScalarGridSpec(
        num_scalar_prefetch=2,
        grid=(nblk,),
        in_specs=[pl.BlockSpec((MOE_ROWS, dw), lambda i, be, nu: (i, 0)),
                  pl.BlockSpec((1, d, 2 * f), wmap),
                  pl.BlockSpec((1, f, d), wmap),
                  pl.BlockSpec((1, 1, 2 * f), wmap),
                  pl.BlockSpec((1, 1, d), wmap)],
        out_specs=pl.BlockSpec((MOE_ROWS, dw), lambda i, be, nu: (i, 0)),
    )
    return pl.pallas_call(
        _ffn_kernel,
        out_shape=jax.ShapeDtypeStruct((p, dw), I32),
        grid_spec=grid_spec,
        compiler_params=_cparams(("arbitrary",)),
        name="expert_ffn",
    )(blk_exp, n_used, xs, wgu, wd, bgu, bd)


def _route_kernel(route_ref, dest_ref, cnt_ref, u_ref, carry_ref, pstart_ref, *, block_rows):
    ph, i = pl.program_id(0), pl.program_id(1)
    tm = route_ref.shape[0]

    @pl.when((ph == 0) & (i == 0))
    def _():
        r = lax.broadcasted_iota(I32, (tm, tm), 0)
        c = lax.broadcasted_iota(I32, (tm, tm), 1)
        u_ref[...] = jnp.where(r < c, 1.0, 0.0).astype(BF16)
        carry_ref[...] = jnp.zeros(carry_ref.shape, F32)

    @pl.when((ph == 1) & (i == 0))
    def _():
        counts = carry_ref[...]
        cnt_ref[...] = counts
        padded = jnp.ceil(counts * (1.0 / block_rows)) * block_rows
        r = lax.broadcasted_iota(I32, (LANES, LANES), 0)
        c = lax.broadcasted_iota(I32, (LANES, LANES), 1)
        lower = jnp.where(c < r, 1.0, 0.0).astype(F32)
        pstart_ref[...] = jnp.dot(lower, padded, preferred_element_type=F32, precision=lax.Precision.HIGHEST)
        carry_ref[...] = jnp.zeros(carry_ref.shape, F32)

    rt = route_ref[...].T
    sub = lax.broadcasted_iota(I32, (LANES, tm), 0)
    hits = [sub == rt[TOP_K_EXPERTS + k:TOP_K_EXPERTS + k + 1, :].astype(I32) for k in range(TOP_K_EXPERTS)]
    m = jnp.zeros((LANES, tm), F32)
    for hit in hits:
        m = m + jnp.where(hit, 1.0, 0.0)
    tile_counts = jnp.broadcast_to(jnp.sum(m, axis=1, keepdims=True), (LANES, LANES))

    @pl.when(ph == 0)
    def _():
        dest_ref[...] = jnp.zeros(dest_ref.shape, I32)

    @pl.when(ph == 1)
    def _():
        prefix = jnp.dot(m.astype(BF16), u_ref[...], preferred_element_type=F32)
        rank = prefix + (pstart_ref[:, 0:1] + carry_ref[:, 0:1])
        rows = [jnp.sum(jnp.where(hit, rank, 0.0), axis=0, keepdims=True) for hit in hits]
        rows.append(jnp.zeros((dest_ref.shape[0] - TOP_K_EXPERTS, tm), F32))
        dest_ref[...] = jnp.concatenate(rows, axis=0).astype(I32)

    carry_ref[...] = carry_ref[...] + tile_counts


def _route_rows(route, tm):
    n = route.shape[0]
    nt = n // tm
    kernel = functools.partial(_route_kernel, block_rows=MOE_ROWS)
    return pl.pallas_call(
        kernel,
        out_shape=(jax.ShapeDtypeStruct((8, n + tm), I32), jax.ShapeDtypeStruct((LANES, LANES), F32)),
        grid=(2, nt),
        in_specs=[pl.BlockSpec((tm, LANES), lambda ph, i: (i, 0))],
        out_specs=(pl.BlockSpec((8, tm), lambda ph, i: (0, ph * i + (1 - ph) * nt)),
                   pl.BlockSpec((LANES, LANES), lambda ph, i: (0, 0))),
        scratch_shapes=[pltpu.VMEM((tm, tm), BF16),
                        pltpu.VMEM((LANES, LANES), F32),
                        pltpu.VMEM((LANES, LANES), F32)],
        compiler_params=_cparams(("arbitrary", "arbitrary")),
        name="route_rows",
    )(route)


SC_WINDOW = 128
SC_WORKERS = 32


def _sc_mesh():
    return plsc.VectorSubcoreMesh(core_axis_name="c", subcore_axis_name="s")


def _sc_scatter_rows(src, dest, p):
    n, d = src.shape
    assert n % (SC_WINDOW * SC_WORKERS) == 0, "token count must split evenly over the vector subcores"
    per = n // (SC_WINDOW * SC_WORKERS)

    @pl.kernel(out_type=jax.ShapeDtypeStruct((p, d), src.dtype), mesh=_sc_mesh(),
               scratch_types=[pltpu.VMEM((dest.shape[0], SC_WINDOW), I32), pltpu.VMEM((SC_WINDOW, d), src.dtype)])
    def scatter(src_hbm, idx_hbm, out_hbm, idx_vmem, buf):
        wid = lax.axis_index("c") * (SC_WORKERS // 2) + lax.axis_index("s")

        @pl.loop(0, per)
        def _(j):
            off = (wid * per + j) * SC_WINDOW
            pltpu.sync_copy(idx_hbm.at[:, pl.ds(off, SC_WINDOW)], idx_vmem)
            pltpu.sync_copy(src_hbm.at[pl.ds(off, SC_WINDOW), :], buf)
            for k in range(TOP_K_EXPERTS):
                pltpu.sync_copy(buf, out_hbm.at[idx_vmem.at[k]])

    return scatter(src, dest)


def _sc_gather_rows(src, dest):
    n = dest.shape[1]
    d = src.shape[1]
    assert n % (SC_WINDOW * SC_WORKERS) == 0, "token count must split evenly over the vector subcores"
    per = n // (SC_WINDOW * SC_WORKERS)

    @pl.kernel(out_type=jax.ShapeDtypeStruct((TOP_K_EXPERTS * n, d), src.dtype), mesh=_sc_mesh(),
               scratch_types=[pltpu.VMEM((dest.shape[0], SC_WINDOW), I32), pltpu.VMEM((SC_WINDOW, d), src.dtype)])
    def gather(src_hbm, idx_hbm, out_hbm, idx_vmem, buf):
        wid = lax.axis_index("c") * (SC_WORKERS // 2) + lax.axis_index("s")

        @pl.loop(0, per)
        def _(j):
            off = (wid * per + j) * SC_WINDOW
            pltpu.sync_copy(idx_hbm.at[:, pl.ds(off, SC_WINDOW)], idx_vmem)
            for k in range(TOP_K_EXPERTS):
                pltpu.sync_copy(src_hbm.at[idx_vmem.at[k]], buf)
                pltpu.sync_copy(buf, out_hbm.at[pl.ds(k * n + off, SC_WINDOW), :])

    return gather(src, dest)


def _combine_kernel(h_ref, y_ref, route_ref, g_ref, o_ref):
    half = h_ref.shape[1] // 2
    h_lo, h_hi = h_ref[:, 0:half], h_ref[:, half:2 * half]
    route = route_ref[...]
    for k in range(TOP_K_EXPERTS):
        y_lo, y_hi = _unpack_halves(y_ref[k])
        gate = route[:, k:k + 1]
        h_lo = h_lo + gate * y_lo
        h_hi = h_hi + gate * y_hi
    ms = (jnp.sum(h_lo * h_lo, axis=-1, keepdims=True)
          + jnp.sum(h_hi * h_hi, axis=-1, keepdims=True)) * (1.0 / (2 * half))
    inv = lax.rsqrt(ms + EPS)
    o_ref[:, 0:half] = h_lo * inv * g_ref[:, 0:half]
    o_ref[:, half:2 * half] = h_hi * inv * g_ref[:, half:2 * half]


def _combine(h1, yg, route, g_final, tm):
    n, d = h1.shape
    return pl.pallas_call(
        _combine_kernel,
        out_shape=jax.ShapeDtypeStruct((n, d), F32),
        grid=(n // tm,),
        in_specs=[pl.BlockSpec((tm, d), lambda i: (i, 0)),
                  pl.BlockSpec((TOP_K_EXPERTS, tm, d // 2), lambda i: (0, i, 0)),
                  pl.BlockSpec((tm, LANES), lambda i: (i, 0)),
                  pl.BlockSpec((1, d), lambda i: (0, 0))],
        out_specs=pl.BlockSpec((tm, d), lambda i: (i, 0)),
        compiler_params=_cparams(("parallel",)),
        name="combine_norm",
    )(h1, yg, route, g_final)


def _t5_bucket(dist):
    n = jnp.maximum(dist, 0)
    max_exact = N_BUCKETS // 2
    nf = jnp.maximum(n, 1).astype(F32)
    large = max_exact + (jnp.log(nf / max_exact) / math.log(MAX_DISTANCE / max_exact)
                         * (N_BUCKETS - max_exact)).astype(I32)
    large = jnp.minimum(large, N_BUCKETS - 1)
    return jnp.where(n < max_exact, n, large)


def _bias_blocks(bias_tab):
    t = LANES
    assert MAX_DISTANCE <= LANES
    r = jnp.arange(t, dtype=I32)[:, None]
    c = jnp.arange(t, dtype=I32)[None, :]
    rel = (bias_tab - bias_tab[N_BUCKETS - 1][None, :]).astype(F32)
    tiles = []
    buckets = jnp.arange(N_BUCKETS, dtype=I32)[:, None, None]
    for delta in (0, t):
        dist = r - c + delta
        hit = _t5_bucket(dist)[None] == buckets
        b = jnp.sum(jnp.where(hit[:, None], rel[:, :, None, None], 0.0), axis=0)
        tiles.append(jnp.where((dist >= 0)[None], b * LOG2E, MASK_VALUE))
    return jnp.stack(tiles)


def _regroup_w_in(w_in):
    sizes = (1024, 1024, 1024, 1024, KV_LATENT, 1024, HEAD_DIM_IDX, N_HEADS_IDX, D_MODEL, D_MODEL)
    parts, off = [], 0
    for sz in sizes:
        parts.append(w_in[:, off:off + sz])
        off += sz
    dq, dk, dv, sq, ckv, iq, ik, iw, ga, gb = parts
    dq = dq * (HEAD_DIM_DIFF ** -0.5 * LOG2E)
    iw = iw * ((N_HEADS_IDX ** -0.5) * (HEAD_DIM_IDX ** -0.5))
    pad = jnp.zeros((w_in.shape[0], PROJ_WIDTH - COL_IW - N_HEADS_IDX), w_in.dtype)
    w = jnp.concatenate([dq, dk, dv, sq, iq, ga, gb, ckv, ik, ik, iw, pad], axis=1)
    return w.astype(BF16)


def _block_tables(counts, n_assign):
    e, bm = N_EXPERTS, MOE_ROWS
    padded = (counts + bm - 1) // bm * bm
    pends = jnp.cumsum(padded)
    nblk = -(-(n_assign + e * (bm - 1)) // bm)
    first_row = jnp.arange(nblk, dtype=I32) * bm
    blk_exp = jnp.minimum(jnp.sum((pends[None, :] <= first_row[:, None]).astype(I32), axis=1), e - 1)
    n_used = (pends[-1] // bm).astype(I32).reshape(1)
    return blk_exp, n_used, nblk


def kernel(x, norm_attn_g, w_in, rel_bias, lam_q1, lam_k1, lam_q2, lam_k2, diff_subln_g, kv_norm_g, w_uk, w_uv,
           w_branch_diff, w_branch_dsa, w_out, norm_ffn_g, w_router, b_router, w_gate_up, b_gate_up, w_down,
           b_down, norm_final_g):
    batch, seq, d = x.shape
    n = batch * seq
    assert norm_attn_g.shape[0] == 1, "single-layer kernel"
    assert seq % DIFF_BLOCK == 0 and seq % KEY_CHUNK == 0 and d == D_MODEL
    assert seq <= (2 ** 15 - 1) * PACKED_ROWS, "int16 per-element key counts"
    row_tile = math.gcd(n, ROW_TILE)
    token_tile = math.gcd(n, TOKEN_TILE)

    x2 = x.reshape(n, d)
    proj = _inproj(x2, norm_attn_g[0].reshape(1, d), _regroup_w_in(w_in[0]), row_tile, PROJ_COLS)

    lam_init = 0.8 - 0.6 * math.exp(-0.3 * 0)
    lam = (jnp.exp(jnp.sum(lam_q1[0].astype(F32) * lam_k1[0].astype(F32)))
           - jnp.exp(jnp.sum(lam_q2[0].astype(F32) * lam_k2[0].astype(F32))) + lam_init)
    y_diff = _diff_attention(proj, lam.reshape(1, 1).astype(F32), _bias_blocks(rel_bias[:, :N_HEADS_DIFF]),
                             diff_subln_g[0].reshape(1, -1).astype(F32), batch, seq, 1.0 - lam_init)

    y_dsa = _dsa_attention(proj, kv_norm_g[0].reshape(1, -1).astype(F32),
                           w_uk[0].transpose(0, 2, 1).astype(BF16), w_uv[0].astype(BF16),
                           _bias_blocks(rel_bias[:, N_HEADS_DIFF:]), batch, seq, min(TOPK_MAX, seq // 4))

    w_r = jnp.zeros((d, LANES), F32).at[:, :N_EXPERTS].set(w_router[0].astype(F32))
    b_r = jnp.full((1, LANES), MASK_VALUE, F32).at[0, :N_EXPERTS].set(b_router[0].astype(F32))
    h1, hn, route = _merge(x2, y_diff, y_dsa, proj, w_branch_diff[0].astype(BF16), w_branch_dsa[0].astype(BF16),
                           w_out[0].astype(BF16), norm_ffn_g[0].reshape(1, d).astype(F32),
                           w_r.astype(BF16), (w_r - w_r.astype(BF16).astype(F32)).astype(BF16), b_r,
                           row_tile)

    dest, counts = _route_rows(route, row_tile)
    dest = dest[:, :n]
    blk_exp, n_used, nblk = _block_tables(counts[:N_EXPERTS, 0].astype(I32), n * TOP_K_EXPERTS)
    xs = _sc_scatter_rows(hn, dest, nblk * MOE_ROWS)
    e, f = N_EXPERTS, D_EXPERT
    b_gu = b_gate_up[0].astype(F32).reshape(e, f // LANES, LANES, 2).transpose(0, 1, 3, 2).reshape(e, 1, 2 * f)
    ys = _expert_ffn(blk_exp, n_used, xs, _regroup_gate_up(w_gate_up[0], TOKEN_TILE), w_down[0],
                     b_gu, b_down[0][:, None, :].astype(F32))
    yg = _sc_gather_rows(ys, dest).reshape(TOP_K_EXPERTS, n, d // 2)
    out = _combine(h1, yg, route, norm_final_g.reshape(1, d).astype(F32), token_tile)
    return out.reshape(batch, seq, d)
```

```python
import functools
import math

import jax
import jax.numpy as jnp
from jax import lax
from jax.experimental import pallas as pl
from jax.experimental.pallas import tpu as pltpu
from jax.experimental.pallas import tpu_sc as plsc

F32 = jnp.float32
BF16 = jnp.bfloat16
I32 = jnp.int32

D_MODEL = 1024
N_HEADS_DIFF = 8
HEAD_DIM_DIFF = 64
N_HEADS_DSA = 8
HEAD_DIM_DSA = 128
KV_LATENT = 256
N_HEADS_IDX = 16
HEAD_DIM_IDX = 64
TOPK_MAX = 256
N_BUCKETS = 32
MAX_DISTANCE = 128
N_EXPERTS = 32
TOP_K_EXPERTS = 4
D_EXPERT = 1024
SWIGLU_LIMIT = 7.0
SWIGLU_ALPHA = 1.702
EPS = 1e-6

LANES = 128
PACKED_ROWS = 16
ROW_TILE = 1024
PROJ_COLS = 2560
TOKEN_TILE = 1024
DIFF_BLOCK = 512
DIFF_HEADS_PER_STEP = 4
DSA_BLOCK = 256
KEY_CHUNK = 512
DSA_STREAMS = 2
MOE_ROWS = 512
FFN_STREAMS = 2
PROJ_WIDTH = 7680
VMEM_LIMIT = 56 * 1024 * 1024

COL_DQ, COL_DK, COL_DV, COL_SQ, COL_IQ, COL_GA, COL_GB = (i * 1024 for i in range(7))
COL_CKV = 7168
COL_IK = 7424
COL_IW = 7552

LOG2E = math.log2(math.e)
MASK_VALUE = -1e30
M_INIT = -1e29
INT_MIN = -2 ** 31


def _cparams(sem):
    return pltpu.CompilerParams(dimension_semantics=sem, vmem_limit_bytes=VMEM_LIMIT)


def _inproj_kernel(x_ref, g_ref, w_ref, o_ref, xn_ref):
    @pl.when(pl.program_id(1) == 0)
    def _():
        x = x_ref[...]
        ms = jnp.mean(x * x, axis=-1, keepdims=True)
        xn_ref[...] = (x * lax.rsqrt(ms + EPS) * g_ref[...]).astype(BF16)

    o_ref[...] = jnp.dot(xn_ref[...], w_ref[...], preferred_element_type=F32).astype(o_ref.dtype)


def _inproj(x2, g, w, tm, tn):
    n, d = x2.shape
    width = w.shape[1]
    return pl.pallas_call(
        _inproj_kernel,
        out_shape=jax.ShapeDtypeStruct((n, width), BF16),
        grid=(n // tm, width // tn),
        in_specs=[pl.BlockSpec((tm, d), lambda i, j: (i, 0)),
                  pl.BlockSpec((1, d), lambda i, j: (0, 0)),
                  pl.BlockSpec((d, tn), lambda i, j: (0, j))],
        out_specs=pl.BlockSpec((tm, tn), lambda i, j: (i, j)),
        scratch_shapes=[pltpu.VMEM((tm, d), BF16)],
        compiler_params=_cparams(("parallel", "arbitrary")),
        name="inproj",
    )(x2, g, w)


def _lane_chunk_max(s):
    smax = s[:, 0:LANES]
    for j in range(1, s.shape[1] // LANES):
        smax = jnp.maximum(smax, s[:, j * LANES:(j + 1) * LANES])
    return smax


def _softmax_step(s_ref, tk, v, m_ref, l_ref, acc_ref, smax=None, split=True):
    nl = tk // LANES
    if smax is None:
        smax = _lane_chunk_max(s_ref[:, 0:tk])
    m_prev = m_ref[...]
    m_new = jnp.maximum(m_prev, jnp.max(smax, axis=-1, keepdims=True))
    alpha = jnp.exp2(m_prev - m_new)
    psum = None
    ps = []
    for j in range(nl):
        pj = jnp.exp2(s_ref[:, j * LANES:(j + 1) * LANES] - m_new)
        psum = pj if psum is None else psum + pj
        ps.append(pj.astype(BF16))
    l_ref[...] = alpha * l_ref[...] + psum
    pv = _dot_split(jnp.concatenate(ps, axis=1), v, split)
    e = acc_ref.shape[1]
    a = alpha if e == LANES else jnp.concatenate([alpha] * (e // LANES), axis=1)
    acc_ref[...] = a * acc_ref[...] + pv
    m_ref[...] = m_new


def _softmax_init(m_ref, l_ref, acc_ref):
    m_ref[...] = jnp.full(m_ref.shape, M_INIT, F32)
    l_ref[...] = jnp.zeros(l_ref.shape, F32)
    acc_ref[...] = jnp.zeros(acc_ref.shape, F32)


def _softmax_result(l_ref, acc_ref):
    return acc_ref[...] * (1.0 / jnp.sum(l_ref[...], axis=-1, keepdims=True))


def _near_bias(s_ref, d0, d1, delta, groups, t, tk):
    for g in range(groups):
        for rb in range(t // LANES):
            for cb in range(tk // LANES):
                bd = delta + rb - cb
                rows = slice(g * t + rb * LANES, g * t + (rb + 1) * LANES)
                cols = slice(cb * LANES, (cb + 1) * LANES)
                if bd == 0:
                    s_ref[rows, cols] = s_ref[rows, cols] + d0(g)
                elif bd == 1:
                    s_ref[rows, cols] = s_ref[rows, cols] + d1(g)
                elif bd < 0:
                    s_ref[rows, cols] = jnp.full((LANES, LANES), MASK_VALUE, F32)


def _dot_nt(a, b, split=True):
    dn = (((1,), (1,)), ((), ()))
    if not split:
        return lax.dot_general(a, b, dn, preferred_element_type=F32)
    h = a.shape[0] // 2
    return jnp.concatenate([lax.dot_general(a[:h], b, dn, preferred_element_type=F32),
                            lax.dot_general(a[h:], b, dn, preferred_element_type=F32)], axis=0)


def _dot_split(a, b, split=True):
    if not split:
        return jnp.dot(a, b, preferred_element_type=F32)
    h = a.shape[0] // 2
    return jnp.concatenate([jnp.dot(a[:h], b, preferred_element_type=F32),
                            jnp.dot(a[h:], b, preferred_element_type=F32)], axis=0)


def _diff_kernel(lam_ref, q_ref, k_ref, v_ref, bias_ref, g_ref, o_ref, q2_ref, s_ref, m_ref, l_ref, acc_ref,
                 *, out_scale, heads):
    t = q_ref.shape[0]
    e = 2 * HEAD_DIM_DIFF
    qi = pl.program_id(2)
    cols = [slice(hh * e, (hh + 1) * e) for hh in range(heads)]

    lane = lax.broadcasted_iota(I32, (t, e), 1)
    for hh in range(heads):
        q = q_ref[:, cols[hh]]
        zero = jnp.zeros_like(q)
        q2_ref[hh, 0:t, :] = jnp.where(lane < HEAD_DIM_DIFF, q, zero)
        q2_ref[hh, t:2 * t, :] = jnp.where(lane >= HEAD_DIM_DIFF, q, zero)
        _softmax_init(m_ref.at[hh], l_ref.at[hh], acc_ref.at[hh])

    def chunk(kc, delta):
        off = pl.multiple_of(kc * t, t)
        for hh in range(heads):
            s = _dot_nt(q2_ref[hh], k_ref[pl.ds(off, t), cols[hh]])
            s_ref[hh] = s
            smax = None
            if delta is None:
                smax = _lane_chunk_max(s)
            else:
                _near_bias(s_ref.at[hh], lambda g: bias_ref[0, hh], lambda g: bias_ref[1, hh], delta, 2, t, t)
            _softmax_step(s_ref.at[hh], t, v_ref[pl.ds(off, t), cols[hh]], m_ref.at[hh], l_ref.at[hh],
                          acc_ref.at[hh], smax)

    def far_body(kc, carry):
        chunk(kc, None)
        return carry

    lax.fori_loop(0, jnp.maximum(qi - 1, 0), far_body, 0)

    @pl.when(qi >= 1)
    def _():
        chunk(qi - 1, t // LANES)

    chunk(qi, 0)

    for hh in range(heads):
        o = _softmax_result(l_ref.at[hh], acc_ref.at[hh])
        o = o[0:t, :] - lam_ref[0, 0] * o[t:2 * t, :]
        ms = jnp.mean(o * o, axis=-1, keepdims=True)
        o_ref[:, cols[hh]] = (o * lax.rsqrt(ms + EPS) * g_ref[...] * out_scale).astype(o_ref.dtype)


def _diff_attention(proj, lam, bias_blocks, subln_g, batch, seq, out_scale):
    t = DIFF_BLOCK
    nq = seq // t
    h = N_HEADS_DIFF
    e = 2 * HEAD_DIM_DIFF
    hp = DIFF_HEADS_PER_STEP
    w = hp * e
    kernel = functools.partial(_diff_kernel, out_scale=out_scale, heads=hp)
    return pl.pallas_call(
        kernel,
        out_shape=jax.ShapeDtypeStruct((batch * seq, h * e), BF16),
        grid=(batch, h // hp, nq),
        in_specs=[pl.BlockSpec(memory_space=pltpu.SMEM),
                  pl.BlockSpec((t, w), lambda b, hh, qi: (b * nq + qi, COL_DQ // w + hh)),
                  pl.BlockSpec((seq, w), lambda b, hh, qi: (b, COL_DK // w + hh)),
                  pl.BlockSpec((seq, w), lambda b, hh, qi: (b, COL_DV // w + hh)),
                  pl.BlockSpec((2, hp, LANES, LANES), lambda b, hh, qi: (0, hh, 0, 0)),
                  pl.BlockSpec((1, e), lambda b, hh, qi: (0, 0))],
        out_specs=pl.BlockSpec((t, w), lambda b, hh, qi: (b * nq + qi, hh)),
        scratch_shapes=[pltpu.VMEM((hp, 2 * t, e), BF16),
                        pltpu.VMEM((hp, 2 * t, t), F32),
                        pltpu.VMEM((hp, 2 * t, LANES), F32),
                        pltpu.VMEM((hp, 2 * t, LANES), F32),
                        pltpu.VMEM((hp, 2 * t, e), F32)],
        compiler_params=_cparams(("parallel", "parallel", "arbitrary")),
        name="diff_attn",
    )(lam, proj, proj, proj, bias_blocks, subln_g)


def _sortable_key(x):
    bits = lax.bitcast_convert_type(x, I32)
    return bits ^ ((bits >> 31) & jnp.int32(0x7FFFFFFF))


def _dsa_kernel(iq_ref, sq_ref, iw_ref, ik_ref, ckv_ref, kvg_ref, wuk_ref, wuv_ref, bias_ref, o_ref,
                c_ref, key_ref, keyt_ref, keyh_ref, qi_ref, wb_ref, ql_ref, s_ref, m_ref, l_ref, acc_ref, *, topk, scale):
    t = iq_ref.shape[0]
    tk = KEY_CHUNK
    qi = pl.program_id(1)
    n_chunks = qi + 1
    hi, hb = N_HEADS_IDX, N_HEADS_DSA

    @pl.when(qi == 0)
    def _():
        ckv = ckv_ref[...].astype(F32)
        ms = jnp.mean(ckv * ckv, axis=-1, keepdims=True)
        c_ref[...] = (ckv * lax.rsqrt(ms + EPS) * kvg_ref[...]).astype(BF16)

    rg = qi_ref.shape[2]
    lane = lax.broadcasted_iota(I32, (t, LANES), 1)
    for h in range(hi):
        blk = iq_ref[:, (h // 2) * LANES:(h // 2 + 1) * LANES]
        keep = (lane < HEAD_DIM_IDX) if h % 2 == 0 else (lane >= HEAD_DIM_IDX)
        qi_ref[:, h, :, :] = jnp.where(keep, blk, jnp.zeros_like(blk)).reshape(t // rg, rg, LANES)
        wb_ref[:, h, :, :] = jnp.broadcast_to(iw_ref[:, h:h + 1].astype(F32), (t, LANES)).reshape(t // rg, rg, LANES)

    def score_chunk(kc, diag):
        off = pl.multiple_of(kc * t, t)
        d = _dot_nt(qi_ref[...].reshape(hi * t, LANES), ik_ref[pl.ds(off, t), :]).reshape(t // rg, hi, rg, t)
        w = jnp.concatenate([wb_ref[...]] * (t // LANES), axis=-1)
        sc = jnp.sum(jnp.maximum(d, 0.0) * w, axis=1).reshape(t, t)
        key = _sortable_key(sc + 0.0)
        if diag:
            row = lax.broadcasted_iota(I32, (t, t), 0)
            col = lax.broadcasted_iota(I32, (t, t), 1)
            key = jnp.where(col <= row, key, jnp.int32(INT_MIN))
        key_ref[:, pl.ds(off, t)] = key
        key_t = key.T
        keyt_ref[pl.ds(off, t), :] = key_t
        keyh_ref[pl.ds(off, t), :] = (key_t >> 16).astype(jnp.int16)

    def score_body(j, carry):
        for u in range(4):
            score_chunk(4 * j + u, False)
        return carry

    lax.fori_loop(0, qi // 4, score_body, 0)
    done = qi // 4 * 4

    @pl.when(qi % 4 >= 2)
    def _():
        score_chunk(done, False)
        score_chunk(done + 1, False)

    @pl.when(qi % 2 == 1)
    def _():
        score_chunk(qi - 1, False)

    score_chunk(qi, True)

    @pl.when(qi % 2 == 0)
    def _():
        off = pl.multiple_of((qi + 1) * t, t)
        key_ref[:, pl.ds(off, t)] = jnp.full((t, t), INT_MIN, I32)
        keyt_ref[pl.ds(off, t), :] = jnp.full((t, t), INT_MIN, I32)
        keyh_ref[pl.ds(off, t), :] = jnp.full((t, t), INT_MIN >> 16, jnp.int16)

    n_steps = (qi + 2) // 2

    def count_ge(cand_s):
        def body(kc, cnt):
            off = pl.multiple_of(kc * tk, tk)
            k = keyt_ref[pl.ds(off, tk), :].reshape(tk // 8, 8, t)
            return cnt + jnp.sum(jnp.where(k >= cand_s[None], 1, 0), axis=0)

        cnt = lax.fori_loop(0, n_steps, body, jnp.zeros((8, t), I32))
        return jnp.sum(cnt, axis=0, keepdims=True)

    def count_packed(c_row):
        rows = PACKED_ROWS
        c16 = jnp.broadcast_to(c_row.astype(jnp.int16), (rows, t))

        def body(kc, cnt):
            off = pl.multiple_of(kc * tk, tk)
            k = keyh_ref[pl.ds(off, tk), :].reshape(tk // rows, rows, t)
            hit = jnp.where(k >= c16[None], jnp.int16(1), jnp.int16(0))
            parts = [cnt, hit[0]]
            for j in range(1, tk // rows):
                parts[j % 2] = parts[j % 2] + hit[j]
            return parts[0] + parts[1]

        cnt = lax.fori_loop(0, n_steps, body, jnp.zeros((rows, t), jnp.int16))
        return jnp.sum(cnt.astype(I32), axis=0, keepdims=True)

    def make_bit_body(count):
        def bit_body(i, carry):
            cur, n_ge = carry
            bit = lax.shift_left(jnp.int32(1), 31 - i)
            cand = cur | bit
            total = count(cand ^ jnp.int32(INT_MIN))
            accept = total >= topk
            return jnp.where(accept, cand, cur), jnp.where(accept, total, n_ge)
        return bit_body

    zeros8 = jnp.zeros((8, t), I32)
    low_bias = jnp.int32(1 << 15)
    cur, n_ge = lax.fori_loop(
        0, 16, make_bit_body(lambda cand_s: count_packed(cand_s[0:1, :] >> 16)), (zeros8, zeros8))
    high = (cur ^ jnp.int32(INT_MIN)) >> 16
    above_high = jnp.where(high[0:1, :] == 2 ** 15 - 1, 0, count_packed(jnp.minimum(high[0:1, :], 2 ** 15 - 2) + 1))

    def repack_body(kc, carry):
        off = pl.multiple_of(kc * tk, tk)
        k = keyt_ref[pl.ds(off, tk), :].reshape(tk // 8, 8, t)
        low = jnp.where((k >> 16) == high[None], (k & jnp.int32(0xFFFF)) - low_bias, -low_bias)
        keyh_ref[pl.ds(off, tk), :] = low.reshape(tk, t).astype(jnp.int16)
        return carry

    lax.fori_loop(0, n_steps, repack_body, 0)
    cur, n_ge = lax.fori_loop(
        16, 32, make_bit_body(lambda cand_s: above_high + count_packed((cand_s[0:1, :] & jnp.int32(0xFFFF)) - low_bias)),
        (cur, n_ge))
    thr = jnp.maximum(cur ^ jnp.int32(INT_MIN), jnp.int32(INT_MIN + 1))

    def lanes_to_rows(v):
        b = jnp.broadcast_to(v[0:1, :], (LANES, t)).T
        return jnp.concatenate([b] * (tk // LANES), axis=1)

    thr_w = lanes_to_rows(thr)
    ties = jnp.max(n_ge) > topk

    @pl.when(jnp.logical_not(ties))
    def _():
        def mask_body(kc, carry):
            off = pl.multiple_of(kc * tk, tk)
            am = jnp.where(key_ref[:, pl.ds(off, tk)] >= thr_w, 0.0, MASK_VALUE).astype(F32)
            key_ref[:, pl.ds(off, tk)] = lax.bitcast_convert_type(am, I32)
            return carry

        lax.fori_loop(0, n_steps, mask_body, 0)

    @pl.when(ties)
    def _():
        above = jnp.where(thr == jnp.int32(2 ** 31 - 1), 0, count_ge(jnp.minimum(thr, jnp.int32(2 ** 31 - 2)) + 1))
        need_w = lanes_to_rows(topk - above).astype(F32)
        r = lax.broadcasted_iota(I32, (tk, tk), 0)
        c = lax.broadcasted_iota(I32, (tk, tk), 1)
        before = jnp.where(r < c, 1.0, 0.0).astype(BF16)
        ones_w = jnp.ones((tk, tk), BF16)

        def tie_body(kc, seen):
            off = pl.multiple_of(kc * tk, tk)
            k = key_ref[:, pl.ds(off, tk)]
            eq = k == thr_w
            eq_b = jnp.where(eq, 1.0, 0.0).astype(BF16)
            rank = seen + jnp.dot(eq_b, before, preferred_element_type=F32)
            keep = (k > thr_w) | (eq & (rank < need_w))
            key_ref[:, pl.ds(off, tk)] = lax.bitcast_convert_type(jnp.where(keep, 0.0, MASK_VALUE).astype(F32), I32)
            return seen + jnp.dot(eq_b, ones_w, preferred_element_type=F32)

        lax.fori_loop(0, n_steps, tie_body, jnp.zeros((t, tk), F32))

    streams = ql_ref.shape[0]
    split = streams == 1
    hs = hb // streams
    for h in range(hb):
        qh = sq_ref[:, h * HEAD_DIM_DSA:(h + 1) * HEAD_DIM_DSA]
        ql = jnp.dot(qh, wuk_ref[h], preferred_element_type=F32) * scale
        ql_ref[h // hs, (h % hs) * t:(h % hs + 1) * t, :] = ql.astype(BF16)
    for g in range(streams):
        _softmax_init(m_ref.at[g], l_ref.at[g], acc_ref.at[g])

    def chunk(kc, width, delta):
        off = pl.multiple_of(kc * tk, tk)
        c = c_ref[pl.ds(off, width), :]
        am = lax.bitcast_convert_type(key_ref[:, pl.ds(off, width)], F32)
        for g in range(streams):
            s = (_dot_nt(ql_ref[g], c, split).reshape(hs, t, width) + am[None]).reshape(hs * t, width)
            s_ref[g, :, 0:width] = s
            smax = None
            if delta is None:
                smax = _lane_chunk_max(s)
            else:
                _near_bias(s_ref.at[g], lambda j: bias_ref[0, g * hs + j], lambda j: bias_ref[1, g * hs + j],
                           delta, hs, t, width)
            _softmax_step(s_ref.at[g], width, c, m_ref.at[g], l_ref.at[g], acc_ref.at[g], smax, split)

    def far_body(kc, carry):
        chunk(kc, tk, None)
        return carry

    lax.fori_loop(0, jnp.maximum((qi - 1) // 2, 0), far_body, 0)
    half = qi // 2

    @pl.when(qi % 2 == 1)
    def _():
        chunk(half, tk, t // LANES)

    @pl.when(qi % 2 == 0)
    def _():
        @pl.when(half >= 1)
        def _():
            chunk(half - 1, tk, tk // LANES)
        chunk(half, t, 0)

    for h in range(hb):
        g, j = h // hs, h % hs
        inv_l = 1.0 / jnp.sum(l_ref[g, j * t:(j + 1) * t, :], axis=-1, keepdims=True)
        ol = (acc_ref[g, j * t:(j + 1) * t, :] * inv_l).astype(BF16)
        o = jnp.dot(ol, wuv_ref[h], preferred_element_type=F32)
        o_ref[:, h * HEAD_DIM_DSA:(h + 1) * HEAD_DIM_DSA] = o.astype(o_ref.dtype)


def _dsa_attention(proj, kv_g, w_ukt, w_uv, bias_blocks, batch, seq, topk):
    t = DSA_BLOCK
    nq = seq // t
    hb, hi = N_HEADS_DSA, N_HEADS_IDX
    width = hb * HEAD_DIM_DSA
    ns = DSA_STREAMS
    rs = hb // ns * t
    kernel = functools.partial(_dsa_kernel, topk=topk, scale=HEAD_DIM_DSA ** -0.5 * LOG2E)
    return pl.pallas_call(
        kernel,
        out_shape=jax.ShapeDtypeStruct((batch * seq, width), BF16),
        grid=(batch, nq),
        in_specs=[pl.BlockSpec((t, 1024), lambda b, qi: (b * nq + qi, COL_IQ // 1024)),
                  pl.BlockSpec((t, 1024), lambda b, qi: (b * nq + qi, COL_SQ // 1024)),
                  pl.BlockSpec((t, LANES), lambda b, qi: (b * nq + qi, COL_IW // LANES)),
                  pl.BlockSpec((seq, LANES), lambda b, qi: (b, COL_IK // LANES)),
                  pl.BlockSpec((seq, KV_LATENT), lambda b, qi: (b, COL_CKV // KV_LATENT)),
                  pl.BlockSpec((1, KV_LATENT), lambda b, qi: (0, 0)),
                  pl.BlockSpec((hb, HEAD_DIM_DSA, KV_LATENT), lambda b, qi: (0, 0, 0)),
                  pl.BlockSpec((hb, KV_LATENT, HEAD_DIM_DSA), lambda b, qi: (0, 0, 0)),
                  pl.BlockSpec((2, hb, LANES, LANES), lambda b, qi: (0, 0, 0, 0))],
        out_specs=pl.BlockSpec((t, width), lambda b, qi: (b * nq + qi, 0)),
        scratch_shapes=[pltpu.VMEM((seq, KV_LATENT), BF16),
                        pltpu.VMEM((t, seq), I32),
                        pltpu.VMEM((seq, t), I32),
                        pltpu.VMEM((seq, t), jnp.int16),
                        pltpu.VMEM((t // PACKED_ROWS, hi, PACKED_ROWS, LANES), BF16),
                        pltpu.VMEM((t // PACKED_ROWS, hi, PACKED_ROWS, LANES), F32),
                        pltpu.VMEM((ns, rs, KV_LATENT), BF16),
                        pltpu.VMEM((ns, rs, KEY_CHUNK), F32),
                        pltpu.VMEM((ns, rs, LANES), F32),
                        pltpu.VMEM((ns, rs, LANES), F32),
                        pltpu.VMEM((ns, rs, KV_LATENT), F32)],
        compiler_params=_cparams(("parallel", "arbitrary")),
        name="dsa_attn",
    )(proj, proj, proj, proj, proj, kv_g, w_ukt, w_uv, bias_blocks)


def _pack_halves(x):
    c = x.shape[1] // 2
    lo = lax.bitcast_convert_type(x[:, :c].astype(BF16).astype(F32), I32)
    hi = lax.bitcast_convert_type(x[:, c:].astype(BF16).astype(F32), I32)
    return lax.shift_right_logical(lo, 16) | (hi & jnp.int32(-65536))


def _unpack_halves(w):
    lo = lax.bitcast_convert_type(lax.shift_left(w, 16), F32)
    hi = lax.bitcast_convert_type(w & jnp.int32(-65536), F32)
    return lo, hi


def _merge_kernel(x_ref, yd_ref, ys_ref, ga_ref, gb_ref, wd_ref, ws_ref, wo_ref, g_ref, wrh_ref, wrl_ref, br_ref,
                  h_ref, hn_ref, route_ref):
    bd = _dot_split(yd_ref[...], wd_ref[...])
    bs = _dot_split(ys_ref[...], ws_ref[...])
    merged = (jax.nn.sigmoid(ga_ref[...].astype(F32)) * bd + jax.nn.sigmoid(gb_ref[...].astype(F32)) * bs)
    h = x_ref[...] + _dot_split(merged.astype(BF16), wo_ref[...])
    h_ref[...] = h
    ms = jnp.mean(h * h, axis=-1, keepdims=True)
    hn = h * lax.rsqrt(ms + EPS) * g_ref[...]
    hn_ref[...] = _pack_halves(hn)

    hn_hi = hn.astype(BF16)
    hn_lo = (hn - hn_hi.astype(F32)).astype(BF16)
    logits = (_dot_split(hn_hi, wrh_ref[...]) + _dot_split(hn_lo, wrh_ref[...])
              + _dot_split(hn_hi, wrl_ref[...]))
    logits = logits + br_ref[...]
    lane = lax.broadcasted_iota(I32, logits.shape, 1)
    vals, ids = [], []
    for _ in range(TOP_K_EXPERTS):
        mx = jnp.max(logits, axis=-1, keepdims=True)
        ix = jnp.min(jnp.where(logits == mx, lane, LANES), axis=-1, keepdims=True)
        vals.append(mx)
        ids.append(ix)
        logits = jnp.where(lane == ix, -jnp.inf, logits)
    es = [jnp.exp(v - vals[0]) for v in vals]
    inv = 1.0 / (es[0] + es[1] + es[2] + es[3])
    route = jnp.zeros(logits.shape, F32)
    for k in range(TOP_K_EXPERTS):
        route = jnp.where(lane == k, es[k] * inv, route)
        route = jnp.where(lane == TOP_K_EXPERTS + k, ids[k].astype(F32), route)
    route_ref[...] = route


def _merge(x2, y_diff, y_dsa, proj, w_bd, w_bs, w_out, g_ffn, w_router_hi, w_router_lo, b_router, tm):
    n, d = x2.shape
    row = lambda i: (i, 0)
    const = lambda i: (0, 0)
    return pl.pallas_call(
        _merge_kernel,
        out_shape=(jax.ShapeDtypeStruct((n, d), F32),
                   jax.ShapeDtypeStruct((n, d // 2), I32),
                   jax.ShapeDtypeStruct((n, LANES), F32)),
        grid=(n // tm,),
        in_specs=[pl.BlockSpec((tm, d), row),
                  pl.BlockSpec((tm, d), row),
                  pl.BlockSpec((tm, d), row),
                  pl.BlockSpec((tm, d), lambda i: (i, COL_GA // 1024)),
                  pl.BlockSpec((tm, d), lambda i: (i, COL_GB // 1024)),
                  pl.BlockSpec((d, d), const),
                  pl.BlockSpec((d, d), const),
                  pl.BlockSpec((d, d), const),
                  pl.BlockSpec((1, d), const),
                  pl.BlockSpec((d, LANES), const),
                  pl.BlockSpec((d, LANES), const),
                  pl.BlockSpec((1, LANES), const)],
        out_specs=(pl.BlockSpec((tm, d), row),
                   pl.BlockSpec((tm, d // 2), row),
                   pl.BlockSpec((tm, LANES), row)),
        compiler_params=_cparams(("parallel",)),
        name="merge_router",
    )(x2, y_diff, y_dsa, proj, proj, w_bd, w_bs, w_out, g_ffn, w_router_hi, w_router_lo, b_router)


def _regroup_kernel(w_ref, p_ref, o_ref):
    pw = p_ref.shape[0]
    for j in range(w_ref.shape[2] // pw):
        w = w_ref[0, :, j * pw:(j + 1) * pw].astype(BF16)
        o_ref[0, :, j * pw:(j + 1) * pw] = jnp.dot(w, p_ref[...], preferred_element_type=F32).astype(BF16)


def _regroup_gate_up(w_gu, rows):
    e, d, f2 = w_gu.shape
    pw = 2 * LANES
    src = jnp.arange(pw, dtype=I32)
    dst = (src % 2) * LANES + src // 2
    perm = (dst[:, None] == jnp.arange(pw, dtype=I32)[None, :]).astype(BF16)
    return pl.pallas_call(
        _regroup_kernel,
        out_shape=jax.ShapeDtypeStruct((e, d, f2), BF16),
        grid=(e, d // rows),
        in_specs=[pl.BlockSpec((1, rows, f2), lambda i, j: (i, j, 0)),
                  pl.BlockSpec((pw, pw), lambda i, j: (0, 0))],
        out_specs=pl.BlockSpec((1, rows, f2), lambda i, j: (i, j, 0)),
        compiler_params=_cparams(("parallel", "parallel")),
        name="regroup_gate_up",
    )(w_gu, perm)


def _ffn_kernel(be_ref, nu_ref, x_ref, wgu_ref, wd_ref, bgu_ref, bd_ref, o_ref):
    @pl.when(pl.program_id(0) < nu_ref[0])
    def _():
        wd = wd_ref[0].astype(BF16)
        rows = x_ref.shape[0] // FFN_STREAMS
        for r in range(FFN_STREAMS):
            x_lo, x_hi = _unpack_halves(x_ref[r * rows:(r + 1) * rows, :])
            half = x_lo.shape[1]
            gu = (jnp.dot(x_lo.astype(BF16), wgu_ref[0, 0:half, :], preferred_element_type=F32)
                  + jnp.dot(x_hi.astype(BF16), wgu_ref[0, half:2 * half, :], preferred_element_type=F32)
                  + bgu_ref[0])
            acts = []
            for j in range(gu.shape[1] // (2 * LANES)):
                gate = jnp.minimum(gu[:, 2 * j * LANES:(2 * j + 1) * LANES], SWIGLU_LIMIT)
                up = jnp.clip(gu[:, (2 * j + 1) * LANES:(2 * j + 2) * LANES], -SWIGLU_LIMIT, SWIGLU_LIMIT)
                glu = gate * jax.nn.sigmoid(gate * SWIGLU_ALPHA)
                acts.append(((up + 1.0) * glu).astype(BF16))
            a = jnp.concatenate(acts, axis=1)
            y = jnp.dot(a, wd, preferred_element_type=F32) + bd_ref[0]
            o_ref[r * rows:(r + 1) * rows, :] = _pack_halves(y)

    @pl.when(pl.program_id(0) >= nu_ref[0])
    def _():
        o_ref[...] = jnp.zeros(o_ref.shape, o_ref.dtype)


def _expert_ffn(blk_exp, n_used, xs, wgu, wd, bgu, bd):
    p, dw = xs.shape
    f, d = wd.shape[1], wd.shape[2]
    nblk = p // MOE_ROWS
    wmap = lambda i, be, nu: (be[i], 0, 0)
    grid_spec = pltpu.PrefetchScalarGridSpec(
        num_scalar_prefetch=2,
        grid=(nblk,),
        in_specs=[pl.BlockSpec((MOE_ROWS, dw), lambda i, be, nu: (i, 0)),
                  pl.BlockSpec((1, d, 2 * f), wmap),
                  pl.BlockSpec((1, f, d), wmap),
                  pl.BlockSpec((1, 1, 2 * f), wmap),
                  pl.BlockSpec((1, 1, d), wmap)],
        out_specs=pl.BlockSpec((MOE_ROWS, dw), lambda i, be, nu: (i, 0)),
    )
    return pl.pallas_call(
        _ffn_kernel,
        out_shape=jax.ShapeDtypeStruct((p, dw), I32),
        grid_spec=grid_spec,
        compiler_params=_cparams(("arbitrary",)),
        name="expert_ffn",
    )(blk_exp, n_used, xs, wgu, wd, bgu, bd)


def _route_kernel(route_ref, dest_ref, cnt_ref, u_ref, carry_ref, pstart_ref, *, block_rows):
    ph, i = pl.program_id(0), pl.program_id(1)
    tm = route_ref.shape[0]

    @pl.when((ph == 0) & (i == 0))
    def _():
        r = lax.broadcasted_iota(I32, (tm, tm), 0)
        c = lax.broadcasted_iota(I32, (tm, tm), 1)
        u_ref[...] = jnp.where(r < c, 1.0, 0.0).astype(BF16)
        carry_ref[...] = jnp.zeros(carry_ref.shape, F32)

    @pl.when((ph == 1) & (i == 0))
    def _():
        counts = carry_ref[...]
        cnt_ref[...] = counts
        padded = jnp.ceil(counts * (1.0 / block_rows)) * block_rows
        r = lax.broadcasted_iota(I32, (LANES, LANES), 0)
        c = lax.broadcasted_iota(I32, (LANES, LANES), 1)
        lower = jnp.where(c < r, 1.0, 0.0).astype(F32)
        pstart_ref[...] = jnp.dot(lower, padded, preferred_element_type=F32, precision=lax.Precision.HIGHEST)
        carry_ref[...] = jnp.zeros(carry_ref.shape, F32)

    rt = route_ref[...].T
    sub = lax.broadcasted_iota(I32, (LANES, tm), 0)
    hits = [sub == rt[TOP_K_EXPERTS + k:TOP_K_EXPERTS + k + 1, :].astype(I32) for k in range(TOP_K_EXPERTS)]
    m = jnp.zeros((LANES, tm), F32)
    for hit in hits:
        m = m + jnp.where(hit, 1.0, 0.0)
    tile_counts = jnp.broadcast_to(jnp.sum(m, axis=1, keepdims=True), (LANES, LANES))

    @pl.when(ph == 0)
    def _():
        dest_ref[...] = jnp.zeros(dest_ref.shape, I32)

    @pl.when(ph == 1)
    def _():
        prefix = jnp.dot(m.astype(BF16), u_ref[...], preferred_element_type=F32)
        rank = prefix + (pstart_ref[:, 0:1] + carry_ref[:, 0:1])
        rows = [jnp.sum(jnp.where(hit, rank, 0.0), axis=0, keepdims=True) for hit in hits]
        rows.append(jnp.zeros((dest_ref.shape[0] - TOP_K_EXPERTS, tm), F32))
        dest_ref[...] = jnp.concatenate(rows, axis=0).astype(I32)

    carry_ref[...] = carry_ref[...] + tile_counts


def _route_rows(route, tm):
    n = route.shape[0]
    nt = n // tm
    kernel = functools.partial(_route_kernel, block_rows=MOE_ROWS)
    return pl.pallas_call(
        kernel,
        out_shape=(jax.ShapeDtypeStruct((8, n + tm), I32), jax.ShapeDtypeStruct((LANES, LANES), F32)),
        grid=(2, nt),
        in_specs=[pl.BlockSpec((tm, LANES), lambda ph, i: (i, 0))],
        out_specs=(pl.BlockSpec((8, tm), lambda ph, i: (0, ph * i + (1 - ph) * nt)),
                   pl.BlockSpec((LANES, LANES), lambda ph, i: (0, 0))),
        scratch_shapes=[pltpu.VMEM((tm, tm), BF16),
                        pltpu.VMEM((LANES, LANES), F32),
                        pltpu.VMEM((LANES, LANES), F32)],
        compiler_params=_cparams(("arbitrary", "arbitrary")),
        name="route_rows",
    )(route)


SC_WINDOW = 128
SC_WORKERS = 32


def _sc_mesh():
    return plsc.VectorSubcoreMesh(core_axis_name="c", subcore_axis_name="s")


def _sc_scatter_rows(src, dest, p):
    n, d = src.shape
    assert n % (SC_WINDOW * SC_WORKERS) == 0, "token count must split evenly over the vector subcores"
    per = n // (SC_WINDOW * SC_WORKERS)

    @pl.kernel(out_type=jax.ShapeDtypeStruct((p, d), src.dtype), mesh=_sc_mesh(),
               scratch_types=[pltpu.VMEM((dest.shape[0], SC_WINDOW), I32), pltpu.VMEM((SC_WINDOW, d), src.dtype)])
    def scatter(src_hbm, idx_hbm, out_hbm, idx_vmem, buf):
        wid = lax.axis_index("c") * (SC_WORKERS // 2) + lax.axis_index("s")

        @pl.loop(0, per)
        def _(j):
            off = (wid * per + j) * SC_WINDOW
            pltpu.sync_copy(idx_hbm.at[:, pl.ds(off, SC_WINDOW)], idx_vmem)
            pltpu.sync_copy(src_hbm.at[pl.ds(off, SC_WINDOW), :], buf)
            for k in range(TOP_K_EXPERTS):
                pltpu.sync_copy(buf, out_hbm.at[idx_vmem.at[k]])

    return scatter(src, dest)


def _sc_gather_rows(src, dest):
    n = dest.shape[1]
    d = src.shape[1]
    assert n % (SC_WINDOW * SC_WORKERS) == 0, "token count must split evenly over the vector subcores"
    per = n // (SC_WINDOW * SC_WORKERS)

    @pl.kernel(out_type=jax.ShapeDtypeStruct((TOP_K_EXPERTS * n, d), src.dtype), mesh=_sc_mesh(),
               scratch_types=[pltpu.VMEM((dest.shape[0], SC_WINDOW), I32), pltpu.VMEM((SC_WINDOW, d), src.dtype)])
    def gather(src_hbm, idx_hbm, out_hbm, idx_vmem, buf):
        wid = lax.axis_index("c") * (SC_WORKERS // 2) + lax.axis_index("s")

        @pl.loop(0, per)
        def _(j):
            off = (wid * per + j) * SC_WINDOW
            pltpu.sync_copy(idx_hbm.at[:, pl.ds(off, SC_WINDOW)], idx_vmem)
            for k in range(TOP_K_EXPERTS):
                pltpu.sync_copy(src_hbm.at[idx_vmem.at[k]], buf)
                pltpu.sync_copy(buf, out_hbm.at[pl.ds(k * n + off, SC_WINDOW), :])

    return gather(src, dest)


def _combine_kernel(h_ref, y_ref, route_ref, g_ref, o_ref):
    half = h_ref.shape[1] // 2
    h_lo, h_hi = h_ref[:, 0:half], h_ref[:, half:2 * half]
    route = route_ref[...]
    for k in range(TOP_K_EXPERTS):
        y_lo, y_hi = _unpack_halves(y_ref[k])
        gate = route[:, k:k + 1]
        h_lo = h_lo + gate * y_lo
        h_hi = h_hi + gate * y_hi
    ms = (jnp.sum(h_lo * h_lo, axis=-1, keepdims=True)
          + jnp.sum(h_hi * h_hi, axis=-1, keepdims=True)) * (1.0 / (2 * half))
    inv = lax.rsqrt(ms + EPS)
    o_ref[:, 0:half] = h_lo * inv * g_ref[:, 0:half]
    o_ref[:, half:2 * half] = h_hi * inv * g_ref[:, half:2 * half]


def _combine(h1, yg, route, g_final, tm):
    n, d = h1.shape
    return pl.pallas_call(
        _combine_kernel,
        out_shape=jax.ShapeDtypeStruct((n, d), F32),
        grid=(n // tm,),
        in_specs=[pl.BlockSpec((tm, d), lambda i: (i, 0)),
                  pl.BlockSpec((TOP_K_EXPERTS, tm, d // 2), lambda i: (0, i, 0)),
                  pl.BlockSpec((tm, LANES), lambda i: (i, 0)),
                  pl.BlockSpec((1, d), lambda i: (0, 0))],
        out_specs=pl.BlockSpec((tm, d), lambda i: (i, 0)),
        compiler_params=_cparams(("parallel",)),
        name="combine_norm",
    )(h1, yg, route, g_final)


def _t5_bucket(dist):
    n = jnp.maximum(dist, 0)
    max_exact = N_BUCKETS // 2
    nf = jnp.maximum(n, 1).astype(F32)
    large = max_exact + (jnp.log(nf / max_exact) / math.log(MAX_DISTANCE / max_exact)
                         * (N_BUCKETS - max_exact)).astype(I32)
    large = jnp.minimum(large, N_BUCKETS - 1)
    return jnp.where(n < max_exact, n, large)


def _bias_blocks(bias_tab):
    t = LANES
    assert MAX_DISTANCE <= LANES
    r = jnp.arange(t, dtype=I32)[:, None]
    c = jnp.arange(t, dtype=I32)[None, :]
    rel = (bias_tab - bias_tab[N_BUCKETS - 1][None, :]).astype(F32)
    tiles = []
    buckets = jnp.arange(N_BUCKETS, dtype=I32)[:, None, None]
    for delta in (0, t):
        dist = r - c + delta
        hit = _t5_bucket(dist)[None] == buckets
        b = jnp.sum(jnp.where(hit[:, None], rel[:, :, None, None], 0.0), axis=0)
        tiles.append(jnp.where((dist >= 0)[None], b * LOG2E, MASK_VALUE))
    return jnp.stack(tiles)


def _regroup_w_in(w_in):
    sizes = (1024, 1024, 1024, 1024, KV_LATENT, 1024, HEAD_DIM_IDX, N_HEADS_IDX, D_MODEL, D_MODEL)
    parts, off = [], 0
    for sz in sizes:
        parts.append(w_in[:, off:off + sz])
        off += sz
    dq, dk, dv, sq, ckv, iq, ik, iw, ga, gb = parts
    dq = dq * (HEAD_DIM_DIFF ** -0.5 * LOG2E)
    iw = iw * ((N_HEADS_IDX ** -0.5) * (HEAD_DIM_IDX ** -0.5))
    pad = jnp.zeros((w_in.shape[0], PROJ_WIDTH - COL_IW - N_HEADS_IDX), w_in.dtype)
    w = jnp.concatenate([dq, dk, dv, sq, iq, ga, gb, ckv, ik, ik, iw, pad], axis=1)
    return w.astype(BF16)


def _block_tables(counts, n_assign):
    e, bm = N_EXPERTS, MOE_ROWS
    padded = (counts + bm - 1) // bm * bm
    pends = jnp.cumsum(padded)
    nblk = -(-(n_assign + e * (bm - 1)) // bm)
    first_row = jnp.arange(nblk, dtype=I32) * bm
    blk_exp = jnp.minimum(jnp.sum((pends[None, :] <= first_row[:, None]).astype(I32), axis=1), e - 1)
    n_used = (pends[-1] // bm).astype(I32).reshape(1)
    return blk_exp, n_used, nblk


def kernel(x, norm_attn_g, w_in, rel_bias, lam_q1, lam_k1, lam_q2, lam_k2, diff_subln_g, kv_norm_g, w_uk, w_uv,
           w_branch_diff, w_branch_dsa, w_out, norm_ffn_g, w_router, b_router, w_gate_up, b_gate_up, w_down,
           b_down, norm_final_g):
    batch, seq, d = x.shape
    n = batch * seq
    assert norm_attn_g.shape[0] == 1, "single-layer kernel"
    assert seq % DIFF_BLOCK == 0 and seq % KEY_CHUNK == 0 and d == D_MODEL
    assert seq <= (2 ** 15 - 1) * PACKED_ROWS, "int16 per-element key counts"
    row_tile = math.gcd(n, ROW_TILE)
    token_tile = math.gcd(n, TOKEN_TILE)

    x2 = x.reshape(n, d)
    proj = _inproj(x2, norm_attn_g[0].reshape(1, d), _regroup_w_in(w_in[0]), row_tile, PROJ_COLS)

    lam_init = 0.8 - 0.6 * math.exp(-0.3 * 0)
    lam = (jnp.exp(jnp.sum(lam_q1[0].astype(F32) * lam_k1[0].astype(F32)))
           - jnp.exp(jnp.sum(lam_q2[0].astype(F32) * lam_k2[0].astype(F32))) + lam_init)
    y_diff = _diff_attention(proj, lam.reshape(1, 1).astype(F32), _bias_blocks(rel_bias[:, :N_HEADS_DIFF]),
                             diff_subln_g[0].reshape(1, -1).astype(F32), batch, seq, 1.0 - lam_init)

    y_dsa = _dsa_attention(proj, kv_norm_g[0].reshape(1, -1).astype(F32),
                           w_uk[0].transpose(0, 2, 1).astype(BF16), w_uv[0].astype(BF16),
                           _bias_blocks(rel_bias[:, N_HEADS_DIFF:]), batch, seq, min(TOPK_MAX, seq // 4))

    w_r = jnp.zeros((d, LANES), F32).at[:, :N_EXPERTS].set(w_router[0].astype(F32))
    b_r = jnp.full((1, LANES), MASK_VALUE, F32).at[0, :N_EXPERTS].set(b_router[0].astype(F32))
    h1, hn, route = _merge(x2, y_diff, y_dsa, proj, w_branch_diff[0].astype(BF16), w_branch_dsa[0].astype(BF16),
                           w_out[0].astype(BF16), norm_ffn_g[0].reshape(1, d).astype(F32),
                           w_r.astype(BF16), (w_r - w_r.astype(BF16).astype(F32)).astype(BF16), b_r,
                           row_tile)

    dest, counts = _route_rows(route, row_tile)
    dest = dest[:, :n]
    blk_exp, n_used, nblk = _block_tables(counts[:N_EXPERTS, 0].astype(I32), n * TOP_K_EXPERTS)
    xs = _sc_scatter_rows(hn, dest, nblk * MOE_ROWS)
    e, f = N_EXPERTS, D_EXPERT
    b_gu = b_gate_up[0].astype(F32).reshape(e, f // LANES, LANES, 2).transpose(0, 1, 3, 2).reshape(e, 1, 2 * f)
    ys = _expert_ffn(blk_exp, n_used, xs, _regroup_gate_up(w_gate_up[0], TOKEN_TILE), w_down[0],
                     b_gu, b_down[0][:, None, :].astype(F32))
    yg = _sc_gather_rows(ys, dest).reshape(TOP_K_EXPERTS, n, d // 2)
    out = _combine(h1, yg, route, norm_final_g.reshape(1, d).astype(F32), token_tile)
    return out.reshape(batch, seq, d)
```

```python
import functools
import math

import jax
import jax.numpy as jnp
from jax import lax
from jax.experimental import pallas as pl
from jax.experimental.pallas import tpu as pltpu
from jax.experimental.pallas import tpu_sc as plsc

F32 = jnp.float32
BF16 = jnp.bfloat16
I32 = jnp.int32

D_MODEL = 1024
N_HEADS_DIFF = 8
HEAD_DIM_DIFF = 64
N_HEADS_DSA = 8
HEAD_DIM_DSA = 128
KV_LATENT = 256
N_HEADS_IDX = 16
HEAD_DIM_IDX = 64
TOPK_MAX = 256
N_BUCKETS = 32
MAX_DISTANCE = 128
N_EXPERTS = 32
TOP_K_EXPERTS = 4
D_EXPERT = 1024
SWIGLU_LIMIT = 7.0
SWIGLU_ALPHA = 1.702
EPS = 1e-6

LANES = 128
PACKED_ROWS = 16
ROW_TILE = 1024
PROJ_COLS = 2560
TOKEN_TILE = 1024
DIFF_BLOCK = 512
DIFF_HEADS_PER_STEP = 4
DSA_BLOCK = 256
KEY_CHUNK = 512
DSA_STREAMS = 2
MOE_ROWS = 512
COMBINE_PARTS = 4
FFN_STREAMS = 2
PROJ_WIDTH = 7680
VMEM_LIMIT = 56 * 1024 * 1024

COL_DQ, COL_DK, COL_DV, COL_SQ, COL_IQ, COL_GA, COL_GB = (i * 1024 for i in range(7))
COL_CKV = 7168
COL_IK = 7424
COL_IW = 7552

LOG2E = math.log2(math.e)
MASK_VALUE = -1e30
M_INIT = -1e29
INT_MIN = -2 ** 31


def _cparams(sem):
    return pltpu.CompilerParams(dimension_semantics=sem, vmem_limit_bytes=VMEM_LIMIT)


def _inproj_kernel(x_ref, g_ref, w_ref, o_ref, xn_ref):
    @pl.when(pl.program_id(1) == 0)
    def _():
        x = x_ref[...]
        ms = jnp.mean(x * x, axis=-1, keepdims=True)
        xn_ref[...] = (x * lax.rsqrt(ms + EPS) * g_ref[...]).astype(BF16)

    o_ref[...] = jnp.dot(xn_ref[...], w_ref[...], preferred_element_type=F32).astype(o_ref.dtype)


def _inproj(x2, g, w, tm, tn):
    n, d = x2.shape
    width = w.shape[1]
    return pl.pallas_call(
        _inproj_kernel,
        out_shape=jax.ShapeDtypeStruct((n, width), BF16),
        grid=(n // tm, width // tn),
        in_specs=[pl.BlockSpec((tm, d), lambda i, j: (i, 0)),
                  pl.BlockSpec((1, d), lambda i, j: (0, 0)),
                  pl.BlockSpec((d, tn), lambda i, j: (0, j))],
        out_specs=pl.BlockSpec((tm, tn), lambda i, j: (i, j)),
        scratch_shapes=[pltpu.VMEM((tm, d), BF16)],
        compiler_params=_cparams(("parallel", "arbitrary")),
        name="inproj",
    )(x2, g, w)


def _lane_chunk_max(s):
    smax = s[:, 0:LANES]
    for j in range(1, s.shape[1] // LANES):
        smax = jnp.maximum(smax, s[:, j * LANES:(j + 1) * LANES])
    return smax


def _softmax_step(s_ref, tk, v, m_ref, l_ref, acc_ref, smax=None, split=True):
    nl = tk // LANES
    if smax is None:
        smax = _lane_chunk_max(s_ref[:, 0:tk])
    m_prev = m_ref[...]
    m_new = jnp.maximum(m_prev, jnp.max(smax, axis=-1, keepdims=True))
    alpha = jnp.exp2(m_prev - m_new)
    psum = None
    ps = []
    for j in range(nl):
        pj = jnp.exp2(s_ref[:, j * LANES:(j + 1) * LANES] - m_new)
        psum = pj if psum is None else psum + pj
        ps.append(pj.astype(BF16))
    l_ref[...] = alpha * l_ref[...] + psum
    pv = _dot_split(jnp.concatenate(ps, axis=1), v, split)
    e = acc_ref.shape[1]
    a = alpha if e == LANES else jnp.concatenate([alpha] * (e // LANES), axis=1)
    acc_ref[...] = a * acc_ref[...] + pv
    m_ref[...] = m_new


def _softmax_init(m_ref, l_ref, acc_ref):
    m_ref[...] = jnp.full(m_ref.shape, M_INIT, F32)
    l_ref[...] = jnp.zeros(l_ref.shape, F32)
    acc_ref[...] = jnp.zeros(acc_ref.shape, F32)


def _softmax_result(l_ref, acc_ref):
    return acc_ref[...] * (1.0 / jnp.sum(l_ref[...], axis=-1, keepdims=True))


def _near_bias(s_ref, d0, d1, delta, groups, t, tk):
    for g in range(groups):
        for rb in range(t // LANES):
            for cb in range(tk // LANES):
                bd = delta + rb - cb
                rows = slice(g * t + rb * LANES, g * t + (rb + 1) * LANES)
                cols = slice(cb * LANES, (cb + 1) * LANES)
                if bd == 0:
                    s_ref[rows, cols] = s_ref[rows, cols] + d0(g)
                elif bd == 1:
                    s_ref[rows, cols] = s_ref[rows, cols] + d1(g)
                elif bd < 0:
                    s_ref[rows, cols] = jnp.full((LANES, LANES), MASK_VALUE, F32)


def _dot_nt(a, b, split=True):
    dn = (((1,), (1,)), ((), ()))
    if not split:
        return lax.dot_general(a, b, dn, preferred_element_type=F32)
    h = a.shape[0] // 2
    return jnp.concatenate([lax.dot_general(a[:h], b, dn, preferred_element_type=F32),
                            lax.dot_general(a[h:], b, dn, preferred_element_type=F32)], axis=0)


def _dot_split(a, b, split=True):
    if not split:
        return jnp.dot(a, b, preferred_element_type=F32)
    h = a.shape[0] // 2
    return jnp.concatenate([jnp.dot(a[:h], b, preferred_element_type=F32),
                            jnp.dot(a[h:], b, preferred_element_type=F32)], axis=0)


def _diff_kernel(lam_ref, q_ref, k_ref, v_ref, bias_ref, g_ref, o_ref, q2_ref, s_ref, m_ref, l_ref, acc_ref,
                 *, out_scale, heads):
    t = q_ref.shape[0]
    e = 2 * HEAD_DIM_DIFF
    qi = pl.program_id(2)
    cols = [slice(hh * e, (hh + 1) * e) for hh in range(heads)]

    lane = lax.broadcasted_iota(I32, (t, e), 1)
    for hh in range(heads):
        q = q_ref[:, cols[hh]]
        zero = jnp.zeros_like(q)
        q2_ref[hh, 0:t, :] = jnp.where(lane < HEAD_DIM_DIFF, q, zero)
        q2_ref[hh, t:2 * t, :] = jnp.where(lane >= HEAD_DIM_DIFF, q, zero)
        _softmax_init(m_ref.at[hh], l_ref.at[hh], acc_ref.at[hh])

    def chunk(kc, delta):
        off = pl.multiple_of(kc * t, t)
        for hh in range(heads):
            s = _dot_nt(q2_ref[hh], k_ref[pl.ds(off, t), cols[hh]])
            s_ref[hh] = s
            smax = None
            if delta is None:
                smax = _lane_chunk_max(s)
            else:
                _near_bias(s_ref.at[hh], lambda g: bias_ref[0, hh], lambda g: bias_ref[1, hh], delta, 2, t, t)
            _softmax_step(s_ref.at[hh], t, v_ref[pl.ds(off, t), cols[hh]], m_ref.at[hh], l_ref.at[hh],
                          acc_ref.at[hh], smax)

    def far_body(kc, carry):
        chunk(kc, None)
        return carry

    lax.fori_loop(0, jnp.maximum(qi - 1, 0), far_body, 0)

    @pl.when(qi >= 1)
    def _():
        chunk(qi - 1, t // LANES)

    chunk(qi, 0)

    for hh in range(heads):
        o = _softmax_result(l_ref.at[hh], acc_ref.at[hh])
        o = o[0:t, :] - lam_ref[0, 0] * o[t:2 * t, :]
        ms = jnp.mean(o * o, axis=-1, keepdims=True)
        o_ref[:, cols[hh]] = (o * lax.rsqrt(ms + EPS) * g_ref[...] * out_scale).astype(o_ref.dtype)


def _diff_attention(proj, lam, bias_blocks, subln_g, batch, seq, out_scale):
    t = DIFF_BLOCK
    nq = seq // t
    h = N_HEADS_DIFF
    e = 2 * HEAD_DIM_DIFF
    hp = DIFF_HEADS_PER_STEP
    w = hp * e
    kernel = functools.partial(_diff_kernel, out_scale=out_scale, heads=hp)
    return pl.pallas_call(
        kernel,
        out_shape=jax.ShapeDtypeStruct((batch * seq, h * e), BF16),
        grid=(batch, h // hp, nq),
        in_specs=[pl.BlockSpec(memory_space=pltpu.SMEM),
                  pl.BlockSpec((t, w), lambda b, hh, qi: (b * nq + qi, COL_DQ // w + hh)),
                  pl.BlockSpec((seq, w), lambda b, hh, qi: (b, COL_DK // w + hh)),
                  pl.BlockSpec((seq, w), lambda b, hh, qi: (b, COL_DV // w + hh)),
                  pl.BlockSpec((2, hp, LANES, LANES), lambda b, hh, qi: (0, hh, 0, 0)),
                  pl.BlockSpec((1, e), lambda b, hh, qi: (0, 0))],
        out_specs=pl.BlockSpec((t, w), lambda b, hh, qi: (b * nq + qi, hh)),
        scratch_shapes=[pltpu.VMEM((hp, 2 * t, e), BF16),
                        pltpu.VMEM((hp, 2 * t, t), F32),
                        pltpu.VMEM((hp, 2 * t, LANES), F32),
                        pltpu.VMEM((hp, 2 * t, LANES), F32),
                        pltpu.VMEM((hp, 2 * t, e), F32)],
        compiler_params=_cparams(("parallel", "parallel", "arbitrary")),
        name="diff_attn",
    )(lam, proj, proj, proj, bias_blocks, subln_g)


def _sortable_key(x):
    bits = lax.bitcast_convert_type(x, I32)
    return bits ^ ((bits >> 31) & jnp.int32(0x7FFFFFFF))


def _dsa_kernel(iq_ref, sq_ref, iw_ref, ik_ref, ckv_ref, kvg_ref, wuk_ref, wuv_ref, bias_ref, o_ref,
                c_ref, key_ref, keyt_ref, keyh_ref, qi_ref, wb_ref, ql_ref, s_ref, m_ref, l_ref, acc_ref, *, topk, scale):
    t = iq_ref.shape[0]
    tk = KEY_CHUNK
    qi = pl.program_id(1)
    n_chunks = qi + 1
    hi, hb = N_HEADS_IDX, N_HEADS_DSA

    @pl.when(qi == 0)
    def _():
        ckv = ckv_ref[...].astype(F32)
        ms = jnp.mean(ckv * ckv, axis=-1, keepdims=True)
        c_ref[...] = (ckv * lax.rsqrt(ms + EPS) * kvg_ref[...]).astype(BF16)

    rg = qi_ref.shape[2]
    lane = lax.broadcasted_iota(I32, (t, LANES), 1)
    for h in range(hi):
        blk = iq_ref[:, (h // 2) * LANES:(h // 2 + 1) * LANES]
        keep = (lane < HEAD_DIM_IDX) if h % 2 == 0 else (lane >= HEAD_DIM_IDX)
        qi_ref[:, h, :, :] = jnp.where(keep, blk, jnp.zeros_like(blk)).reshape(t // rg, rg, LANES)
        wb_ref[:, h, :, :] = jnp.broadcast_to(iw_ref[:, h:h + 1].astype(F32), (t, LANES)).reshape(t // rg, rg, LANES)

    def score_chunk(kc, diag):
        off = pl.multiple_of(kc * t, t)
        d = _dot_nt(qi_ref[...].reshape(hi * t, LANES), ik_ref[pl.ds(off, t), :]).reshape(t // rg, hi, rg, t)
        w = jnp.concatenate([wb_ref[...]] * (t // LANES), axis=-1)
        sc = jnp.sum(jnp.maximum(d, 0.0) * w, axis=1).reshape(t, t)
        key = _sortable_key(sc + 0.0)
        if diag:
            row = lax.broadcasted_iota(I32, (t, t), 0)
            col = lax.broadcasted_iota(I32, (t, t), 1)
            key = jnp.where(col <= row, key, jnp.int32(INT_MIN))
        key_ref[:, pl.ds(off, t)] = key
        key_t = key.T
        keyt_ref[pl.ds(off, t), :] = key_t
        keyh_ref[pl.ds(off, t), :] = (key_t >> 16).astype(jnp.int16)

    def score_body(j, carry):
        for u in range(4):
            score_chunk(4 * j + u, False)
        return carry

    lax.fori_loop(0, qi // 4, score_body, 0)
    done = qi // 4 * 4

    @pl.when(qi % 4 >= 2)
    def _():
        score_chunk(done, False)
        score_chunk(done + 1, False)

    @pl.when(qi % 2 == 1)
    def _():
        score_chunk(qi - 1, False)

    score_chunk(qi, True)

    @pl.when(qi % 2 == 0)
    def _():
        off = pl.multiple_of((qi + 1) * t, t)
        key_ref[:, pl.ds(off, t)] = jnp.full((t, t), INT_MIN, I32)
        keyt_ref[pl.ds(off, t), :] = jnp.full((t, t), INT_MIN, I32)
        keyh_ref[pl.ds(off, t), :] = jnp.full((t, t), INT_MIN >> 16, jnp.int16)

    n_steps = (qi + 2) // 2

    def count_ge(cand_s):
        def body(kc, cnt):
            off = pl.multiple_of(kc * tk, tk)
            k = keyt_ref[pl.ds(off, tk), :].reshape(tk // 8, 8, t)
            return cnt + jnp.sum(jnp.where(k >= cand_s[None], 1, 0), axis=0)

        cnt = lax.fori_loop(0, n_steps, body, jnp.zeros((8, t), I32))
        return jnp.sum(cnt, axis=0, keepdims=True)

    def count_packed(c_row):
        rows = PACKED_ROWS
        c16 = jnp.broadcast_to(c_row.astype(jnp.int16), (rows, t))

        def body(kc, cnt):
            off = pl.multiple_of(kc * tk, tk)
            k = keyh_ref[pl.ds(off, tk), :].reshape(tk // rows, rows, t)
            hit = jnp.where(k >= c16[None], jnp.int16(1), jnp.int16(0))
            parts = [cnt, hit[0]]
            for j in range(1, tk // rows):
                parts[j % 2] = parts[j % 2] + hit[j]
            return parts[0] + parts[1]

        cnt = lax.fori_loop(0, n_steps, body, jnp.zeros((rows, t), jnp.int16))
        return jnp.sum(cnt.astype(I32), axis=0, keepdims=True)

    def make_bit_body(count):
        def bit_body(i, carry):
            cur, n_ge = carry
            bit = lax.shift_left(jnp.int32(1), 31 - i)
            cand = cur | bit
            total = count(cand ^ jnp.int32(INT_MIN))
            accept = total >= topk
            return jnp.where(accept, cand, cur), jnp.where(accept, total, n_ge)
        return bit_body

    zeros8 = jnp.zeros((8, t), I32)
    low_bias = jnp.int32(1 << 15)
    cur, n_ge = lax.fori_loop(
        0, 16, make_bit_body(lambda cand_s: count_packed(cand_s[0:1, :] >> 16)), (zeros8, zeros8))
    high = (cur ^ jnp.int32(INT_MIN)) >> 16
    above_high = jnp.where(high[0:1, :] == 2 ** 15 - 1, 0, count_packed(jnp.minimum(high[0:1, :], 2 ** 15 - 2) + 1))

    def repack_body(kc, carry):
        off = pl.multiple_of(kc * tk, tk)
        k = keyt_ref[pl.ds(off, tk), :].reshape(tk // 8, 8, t)
        low = jnp.where((k >> 16) == high[None], (k & jnp.int32(0xFFFF)) - low_bias, -low_bias)
        keyh_ref[pl.ds(off, tk), :] = low.reshape(tk, t).astype(jnp.int16)
        return carry

    lax.fori_loop(0, n_steps, repack_body, 0)
    cur, n_ge = lax.fori_loop(
        16, 32, make_bit_body(lambda cand_s: above_high + count_packed((cand_s[0:1, :] & jnp.int32(0xFFFF)) - low_bias)),
        (cur, n_ge))
    thr = jnp.maximum(cur ^ jnp.int32(INT_MIN), jnp.int32(INT_MIN + 1))

    def lanes_to_rows(v):
        b = jnp.broadcast_to(v[0:1, :], (LANES, t)).T
        return jnp.concatenate([b] * (tk // LANES), axis=1)

    thr_w = lanes_to_rows(thr)
    ties = jnp.max(n_ge) > topk

    @pl.when(jnp.logical_not(ties))
    def _():
        def mask_body(kc, carry):
            off = pl.multiple_of(kc * tk, tk)
            am = jnp.where(key_ref[:, pl.ds(off, tk)] >= thr_w, 0.0, MASK_VALUE).astype(F32)
            key_ref[:, pl.ds(off, tk)] = lax.bitcast_convert_type(am, I32)
            return carry

        lax.fori_loop(0, n_steps, mask_body, 0)

    @pl.when(ties)
    def _():
        above = jnp.where(thr == jnp.int32(2 ** 31 - 1), 0, count_ge(jnp.minimum(thr, jnp.int32(2 ** 31 - 2)) + 1))
        need_w = lanes_to_rows(topk - above).astype(F32)
        r = lax.broadcasted_iota(I32, (tk, tk), 0)
        c = lax.broadcasted_iota(I32, (tk, tk), 1)
        before = jnp.where(r < c, 1.0, 0.0).astype(BF16)
        ones_w = jnp.ones((tk, tk), BF16)

        def tie_body(kc, seen):
            off = pl.multiple_of(kc * tk, tk)
            k = key_ref[:, pl.ds(off, tk)]
            eq = k == thr_w
            eq_b = jnp.where(eq, 1.0, 0.0).astype(BF16)
            rank = seen + jnp.dot(eq_b, before, preferred_element_type=F32)
            keep = (k > thr_w) | (eq & (rank < need_w))
            key_ref[:, pl.ds(off, tk)] = lax.bitcast_convert_type(jnp.where(keep, 0.0, MASK_VALUE).astype(F32), I32)
            return seen + jnp.dot(eq_b, ones_w, preferred_element_type=F32)

        lax.fori_loop(0, n_steps, tie_body, jnp.zeros((t, tk), F32))

    streams = ql_ref.shape[0]
    split = streams == 1
    hs = hb // streams
    for h in range(hb):
        qh = sq_ref[:, h * HEAD_DIM_DSA:(h + 1) * HEAD_DIM_DSA]
        ql = jnp.dot(qh, wuk_ref[h], preferred_element_type=F32) * scale
        ql_ref[h // hs, (h % hs) * t:(h % hs + 1) * t, :] = ql.astype(BF16)
    for g in range(streams):
        _softmax_init(m_ref.at[g], l_ref.at[g], acc_ref.at[g])

    def chunk(kc, width, delta):
        off = pl.multiple_of(kc * tk, tk)
        c = c_ref[pl.ds(off, width), :]
        am = lax.bitcast_convert_type(key_ref[:, pl.ds(off, width)], F32)
        for g in range(streams):
            s = (_dot_nt(ql_ref[g], c, split).reshape(hs, t, width) + am[None]).reshape(hs * t, width)
            s_ref[g, :, 0:width] = s
            smax = None
            if delta is None:
                smax = _lane_chunk_max(s)
            else:
                _near_bias(s_ref.at[g], lambda j: bias_ref[0, g * hs + j], lambda j: bias_ref[1, g * hs + j],
                           delta, hs, t, width)
            _softmax_step(s_ref.at[g], width, c, m_ref.at[g], l_ref.at[g], acc_ref.at[g], smax, split)

    def far_body(kc, carry):
        chunk(kc, tk, None)
        return carry

    lax.fori_loop(0, jnp.maximum((qi - 1) // 2, 0), far_body, 0)
    half = qi // 2

    @pl.when(qi % 2 == 1)
    def _():
        chunk(half, tk, t // LANES)

    @pl.when(qi % 2 == 0)
    def _():
        @pl.when(half >= 1)
        def _():
            chunk(half - 1, tk, tk // LANES)
        chunk(half, t, 0)

    for h in range(hb):
        g, j = h // hs, h % hs
        inv_l = 1.0 / jnp.sum(l_ref[g, j * t:(j + 1) * t, :], axis=-1, keepdims=True)
        ol = (acc_ref[g, j * t:(j + 1) * t, :] * inv_l).astype(BF16)
        o = jnp.dot(ol, wuv_ref[h], preferred_element_type=F32)
        o_ref[:, h * HEAD_DIM_DSA:(h + 1) * HEAD_DIM_DSA] = o.astype(o_ref.dtype)


def _dsa_attention(proj, kv_g, w_ukt, w_uv, bias_blocks, batch, seq, topk):
    t = DSA_BLOCK
    nq = seq // t
    hb, hi = N_HEADS_DSA, N_HEADS_IDX
    width = hb * HEAD_DIM_DSA
    ns = DSA_STREAMS
    rs = hb // ns * t
    kernel = functools.partial(_dsa_kernel, topk=topk, scale=HEAD_DIM_DSA ** -0.5 * LOG2E)
    return pl.pallas_call(
        kernel,
        out_shape=jax.ShapeDtypeStruct((batch * seq, width), BF16),
        grid=(batch, nq),
        in_specs=[pl.BlockSpec((t, 1024), lambda b, qi: (b * nq + qi, COL_IQ // 1024)),
                  pl.BlockSpec((t, 1024), lambda b, qi: (b * nq + qi, COL_SQ // 1024)),
                  pl.BlockSpec((t, LANES), lambda b, qi: (b * nq + qi, COL_IW // LANES)),
                  pl.BlockSpec((seq, LANES), lambda b, qi: (b, COL_IK // LANES)),
                  pl.BlockSpec((seq, KV_LATENT), lambda b, qi: (b, COL_CKV // KV_LATENT)),
                  pl.BlockSpec((1, KV_LATENT), lambda b, qi: (0, 0)),
                  pl.BlockSpec((hb, HEAD_DIM_DSA, KV_LATENT), lambda b, qi: (0, 0, 0)),
                  pl.BlockSpec((hb, KV_LATENT, HEAD_DIM_DSA), lambda b, qi: (0, 0, 0)),
                  pl.BlockSpec((2, hb, LANES, LANES), lambda b, qi: (0, 0, 0, 0))],
        out_specs=pl.BlockSpec((t, width), lambda b, qi: (b * nq + qi, 0)),
        scratch_shapes=[pltpu.VMEM((seq, KV_LATENT), BF16),
                        pltpu.VMEM((t, seq), I32),
                        pltpu.VMEM((seq, t), I32),
                        pltpu.VMEM((seq, t), jnp.int16),
                        pltpu.VMEM((t // PACKED_ROWS, hi, PACKED_ROWS, LANES), BF16),
                        pltpu.VMEM((t // PACKED_ROWS, hi, PACKED_ROWS, LANES), F32),
                        pltpu.VMEM((ns, rs, KV_LATENT), BF16),
                        pltpu.VMEM((ns, rs, KEY_CHUNK), F32),
                        pltpu.VMEM((ns, rs, LANES), F32),
                        pltpu.VMEM((ns, rs, LANES), F32),
                        pltpu.VMEM((ns, rs, KV_LATENT), F32)],
        compiler_params=_cparams(("parallel", "arbitrary")),
        name="dsa_attn",
    )(proj, proj, proj, proj, proj, kv_g, w_ukt, w_uv, bias_blocks)


def _pack_halves(x):
    c = x.shape[1] // 2
    lo = lax.bitcast_convert_type(x[:, :c].astype(BF16).astype(F32), I32)
    hi = lax.bitcast_convert_type(x[:, c:].astype(BF16).astype(F32), I32)
    return lax.shift_right_logical(lo, 16) | (hi & jnp.int32(-65536))


def _unpack_halves(w):
    lo = lax.bitcast_convert_type(lax.shift_left(w, 16), F32)
    hi = lax.bitcast_convert_type(w & jnp.int32(-65536), F32)
    return lo, hi


def _merge_kernel(x_ref, yd_ref, ys_ref, ga_ref, gb_ref, wd_ref, ws_ref, wo_ref, g_ref, wrh_ref, wrl_ref, br_ref,
                  h_ref, hn_ref, route_ref):
    bd = _dot_split(yd_ref[...], wd_ref[...])
    bs = _dot_split(ys_ref[...], ws_ref[...])
    merged = (jax.nn.sigmoid(ga_ref[...].astype(F32)) * bd + jax.nn.sigmoid(gb_ref[...].astype(F32)) * bs)
    h = x_ref[...] + _dot_split(merged.astype(BF16), wo_ref[...])
    h_ref[...] = h
    ms = jnp.mean(h * h, axis=-1, keepdims=True)
    hn = h * lax.rsqrt(ms + EPS) * g_ref[...]
    hn_ref[...] = _pack_halves(hn)

    hn_hi = hn.astype(BF16)
    hn_lo = (hn - hn_hi.astype(F32)).astype(BF16)
    logits = (_dot_split(hn_hi, wrh_ref[...]) + _dot_split(hn_lo, wrh_ref[...])
              + _dot_split(hn_hi, wrl_ref[...]))
    logits = logits + br_ref[...]
    lane = lax.broadcasted_iota(I32, logits.shape, 1)
    vals, ids = [], []
    for _ in range(TOP_K_EXPERTS):
        mx = jnp.max(logits, axis=-1, keepdims=True)
        ix = jnp.min(jnp.where(logits == mx, lane, LANES), axis=-1, keepdims=True)
        vals.append(mx)
        ids.append(ix)
        logits = jnp.where(lane == ix, -jnp.inf, logits)
    es = [jnp.exp(v - vals[0]) for v in vals]
    inv = 1.0 / (es[0] + es[1] + es[2] + es[3])
    route = jnp.zeros(logits.shape, F32)
    for k in range(TOP_K_EXPERTS):
        route = jnp.where(lane == k, es[k] * inv, route)
        route = jnp.where(lane == TOP_K_EXPERTS + k, ids[k].astype(F32), route)
    route_ref[...] = route


def _merge(x2, y_diff, y_dsa, proj, w_bd, w_bs, w_out, g_ffn, w_router_hi, w_router_lo, b_router, tm):
    n, d = x2.shape
    row = lambda i: (i, 0)
    const = lambda i: (0, 0)
    return pl.pallas_call(
        _merge_kernel,
        out_shape=(jax.ShapeDtypeStruct((n, d), F32),
                   jax.ShapeDtypeStruct((n, d // 2), I32),
                   jax.ShapeDtypeStruct((n, LANES), F32)),
        grid=(n // tm,),
        in_specs=[pl.BlockSpec((tm, d), row),
                  pl.BlockSpec((tm, d), row),
                  pl.BlockSpec((tm, d), row),
                  pl.BlockSpec((tm, d), lambda i: (i, COL_GA // 1024)),
                  pl.BlockSpec((tm, d), lambda i: (i, COL_GB // 1024)),
                  pl.BlockSpec((d, d), const),
                  pl.BlockSpec((d, d), const),
                  pl.BlockSpec((d, d), const),
                  pl.BlockSpec((1, d), const),
                  pl.BlockSpec((d, LANES), const),
                  pl.BlockSpec((d, LANES), const),
                  pl.BlockSpec((1, LANES), const)],
        out_specs=(pl.BlockSpec((tm, d), row),
                   pl.BlockSpec((tm, d // 2), row),
                   pl.BlockSpec((tm, LANES), row)),
        compiler_params=_cparams(("parallel",)),
        name="merge_router",
    )(x2, y_diff, y_dsa, proj, proj, w_bd, w_bs, w_out, g_ffn, w_router_hi, w_router_lo, b_router)


def _regroup_kernel(w_ref, p_ref, o_ref):
    pw = p_ref.shape[0]
    for j in range(w_ref.shape[2] // pw):
        w = w_ref[0, :, j * pw:(j + 1) * pw].astype(BF16)
        o_ref[0, :, j * pw:(j + 1) * pw] = jnp.dot(w, p_ref[...], preferred_element_type=F32).astype(BF16)


def _regroup_gate_up(w_gu, rows):
    e, d, f2 = w_gu.shape
    pw = 2 * LANES
    src = jnp.arange(pw, dtype=I32)
    dst = (src % 2) * LANES + src // 2
    perm = (dst[:, None] == jnp.arange(pw, dtype=I32)[None, :]).astype(BF16)
    return pl.pallas_call(
        _regroup_kernel,
        out_shape=jax.ShapeDtypeStruct((e, d, f2), BF16),
        grid=(e, d // rows),
        in_specs=[pl.BlockSpec((1, rows, f2), lambda i, j: (i, j, 0)),
                  pl.BlockSpec((pw, pw), lambda i, j: (0, 0))],
        out_specs=pl.BlockSpec((1, rows, f2), lambda i, j: (i, j, 0)),
        compiler_params=_cparams(("parallel", "parallel")),
        name="regroup_gate_up",
    )(w_gu, perm)


def _ffn_kernel(be_ref, nu_ref, x_ref, wgu_ref, wd_ref, bgu_ref, bd_ref, o_ref):
    @pl.when(pl.program_id(0) < nu_ref[0])
    def _():
        wd = wd_ref[0].astype(BF16)
        rows = x_ref.shape[0] // FFN_STREAMS
        for r in range(FFN_STREAMS):
            x_lo, x_hi = _unpack_halves(x_ref[r * rows:(r + 1) * rows, :])
            half = x_lo.shape[1]
            gu = (jnp.dot(x_lo.astype(BF16), wgu_ref[0, 0:half, :], preferred_element_type=F32)
                  + jnp.dot(x_hi.astype(BF16), wgu_ref[0, half:2 * half, :], preferred_element_type=F32)
                  + bgu_ref[0])
            acts = []
            for j in range(gu.shape[1] // (2 * LANES)):
                gate = jnp.minimum(gu[:, 2 * j * LANES:(2 * j + 1) * LANES], SWIGLU_LIMIT)
                up = jnp.clip(gu[:, (2 * j + 1) * LANES:(2 * j + 2) * LANES], -SWIGLU_LIMIT, SWIGLU_LIMIT)
                glu = gate * jax.nn.sigmoid(gate * SWIGLU_ALPHA)
                acts.append(((up + 1.0) * glu).astype(BF16))
            a = jnp.concatenate(acts, axis=1)
            y = jnp.dot(a, wd, preferred_element_type=F32) + bd_ref[0]
            o_ref[r * rows:(r + 1) * rows, :] = _pack_halves(y)

    @pl.when(pl.program_id(0) >= nu_ref[0])
    def _():
        o_ref[...] = jnp.zeros(o_ref.shape, o_ref.dtype)


def _expert_ffn(blk_exp, n_used, xs, wgu, wd, bgu, bd):
    p, dw = xs.shape
    f, d = wd.shape[1], wd.shape[2]
    nblk = p // MOE_ROWS
    wmap = lambda i, be, nu: (be[i], 0, 0)
    grid_spec = pltpu.PrefetchScalarGridSpec(
        num_scalar_prefetch=2,
        grid=(nblk,),
        in_specs=[pl.BlockSpec((MOE_ROWS, dw), lambda i, be, nu: (i, 0)),
                  pl.BlockSpec((1, d, 2 * f), wmap),
                  pl.BlockSpec((1, f, d), wmap),
                  pl.BlockSpec((1, 1, 2 * f), wmap),
                  pl.BlockSpec((1, 1, d), wmap)],
        out_specs=pl.BlockSpec((MOE_ROWS, dw), lambda i, be, nu: (i, 0)),
    )
    return pl.pallas_call(
        _ffn_kernel,
        out_shape=jax.ShapeDtypeStruct((p, dw), I32),
        grid_spec=grid_spec,
        compiler_params=_cparams(("arbitrary",)),
        name="expert_ffn",
    )(blk_exp, n_used, xs, wgu, wd, bgu, bd)


def _route_kernel(route_ref, dest_ref, cnt_ref, u_ref, carry_ref, pstart_ref, *, block_rows):
    ph, i = pl.program_id(0), pl.program_id(1)
    tm = route_ref.shape[0]

    @pl.when((ph == 0) & (i == 0))
    def _():
        r = lax.broadcasted_iota(I32, (tm, tm), 0)
        c = lax.broadcasted_iota(I32, (tm, tm), 1)
        u_ref[...] = jnp.where(r < c, 1.0, 0.0).astype(BF16)
        carry_ref[...] = jnp.zeros(carry_ref.shape, F32)

    @pl.when((ph == 1) & (i == 0))
    def _():
        counts = carry_ref[...]
        cnt_ref[...] = counts
        padded = jnp.ceil(counts * (1.0 / block_rows)) * block_rows
        r = lax.broadcasted_iota(I32, (LANES, LANES), 0)
        c = lax.broadcasted_iota(I32, (LANES, LANES), 1)
        lower = jnp.where(c < r, 1.0, 0.0).astype(F32)
        pstart_ref[...] = jnp.dot(lower, padded, preferred_element_type=F32, precision=lax.Precision.HIGHEST)
        carry_ref[...] = jnp.zeros(carry_ref.shape, F32)

    rt = route_ref[...].T
    sub = lax.broadcasted_iota(I32, (LANES, tm), 0)
    hits = [sub == rt[TOP_K_EXPERTS + k:TOP_K_EXPERTS + k + 1, :].astype(I32) for k in range(TOP_K_EXPERTS)]
    m = jnp.zeros((LANES, tm), F32)
    for hit in hits:
        m = m + jnp.where(hit, 1.0, 0.0)
    tile_counts = jnp.broadcast_to(jnp.sum(m, axis=1, keepdims=True), (LANES, LANES))

    @pl.when(ph == 0)
    def _():
        dest_ref[...] = jnp.zeros(dest_ref.shape, I32)

    @pl.when(ph == 1)
    def _():
        prefix = jnp.dot(m.astype(BF16), u_ref[...], preferred_element_type=F32)
        rank = prefix + (pstart_ref[:, 0:1] + carry_ref[:, 0:1])
        rows = [jnp.sum(jnp.where(hit, rank, 0.0), axis=0, keepdims=True) for hit in hits]
        rows.append(jnp.zeros((dest_ref.shape[0] - TOP_K_EXPERTS, tm), F32))
        dest_ref[...] = jnp.concatenate(rows, axis=0).astype(I32)

    carry_ref[...] = carry_ref[...] + tile_counts


def _route_rows(route, tm):
    n = route.shape[0]
    nt = n // tm
    kernel = functools.partial(_route_kernel, block_rows=MOE_ROWS)
    return pl.pallas_call(
        kernel,
        out_shape=(jax.ShapeDtypeStruct((8, n + tm), I32), jax.ShapeDtypeStruct((LANES, LANES), F32)),
        grid=(2, nt),
        in_specs=[pl.BlockSpec((tm, LANES), lambda ph, i: (i, 0))],
        out_specs=(pl.BlockSpec((8, tm), lambda ph, i: (0, ph * i + (1 - ph) * nt)),
                   pl.BlockSpec((LANES, LANES), lambda ph, i: (0, 0))),
        scratch_shapes=[pltpu.VMEM((tm, tm), BF16),
                        pltpu.VMEM((LANES, LANES), F32),
                        pltpu.VMEM((LANES, LANES), F32)],
        compiler_params=_cparams(("arbitrary", "arbitrary")),
        name="route_rows",
    )(route)


SC_WINDOW = 128
SC_WORKERS = 32


def _sc_mesh():
    return plsc.VectorSubcoreMesh(core_axis_name="c", subcore_axis_name="s")


def _sc_scatter_rows(src, dest, p):
    n, d = src.shape
    assert n % (SC_WINDOW * SC_WORKERS) == 0, "token count must split evenly over the vector subcores"
    per = n // (SC_WINDOW * SC_WORKERS)

    @pl.kernel(out_type=jax.ShapeDtypeStruct((p, d), src.dtype), mesh=_sc_mesh(),
               scratch_types=[pltpu.VMEM((dest.shape[0], SC_WINDOW), I32), pltpu.VMEM((SC_WINDOW, d), src.dtype)])
    def scatter(src_hbm, idx_hbm, out_hbm, idx_vmem, buf):
        wid = lax.axis_index("c") * (SC_WORKERS // 2) + lax.axis_index("s")

        @pl.loop(0, per)
        def _(j):
            off = (wid * per + j) * SC_WINDOW
            pltpu.sync_copy(idx_hbm.at[:, pl.ds(off, SC_WINDOW)], idx_vmem)
            pltpu.sync_copy(src_hbm.at[pl.ds(off, SC_WINDOW), :], buf)
            for k in range(TOP_K_EXPERTS):
                pltpu.sync_copy(buf, out_hbm.at[idx_vmem.at[k]])

    return scatter(src, dest)


def _sc_gather_rows(src, dest, start, count):
    d = src.shape[1]
    assert count % (SC_WINDOW * SC_WORKERS) == 0, "token count must split evenly over the vector subcores"
    per = count // (SC_WINDOW * SC_WORKERS)

    @pl.kernel(out_type=jax.ShapeDtypeStruct((TOP_K_EXPERTS * count, d), src.dtype), mesh=_sc_mesh(),
               scratch_types=[pltpu.VMEM((dest.shape[0], SC_WINDOW), I32), pltpu.VMEM((SC_WINDOW, d), src.dtype)])
    def gather(src_hbm, idx_hbm, out_hbm, idx_vmem, buf):
        wid = lax.axis_index("c") * (SC_WORKERS // 2) + lax.axis_index("s")

        @pl.loop(0, per)
        def _(j):
            off = (wid * per + j) * SC_WINDOW
            pltpu.sync_copy(idx_hbm.at[:, pl.ds(start + off, SC_WINDOW)], idx_vmem)
            for k in range(TOP_K_EXPERTS):
                pltpu.sync_copy(src_hbm.at[idx_vmem.at[k]], buf)
                pltpu.sync_copy(buf, out_hbm.at[pl.ds(k * count + off, SC_WINDOW), :])

    return gather(src, dest)


def _combine_kernel(h_ref, y_ref, route_ref, g_ref, o_ref):
    half = h_ref.shape[1] // 2
    h_lo, h_hi = h_ref[:, 0:half], h_ref[:, half:2 * half]
    route = route_ref[...]
    for k in range(TOP_K_EXPERTS):
        y_lo, y_hi = _unpack_halves(y_ref[k])
        gate = route[:, k:k + 1]
        h_lo = h_lo + gate * y_lo
        h_hi = h_hi + gate * y_hi
    ms = (jnp.sum(h_lo * h_lo, axis=-1, keepdims=True)
          + jnp.sum(h_hi * h_hi, axis=-1, keepdims=True)) * (1.0 / (2 * half))
    inv = lax.rsqrt(ms + EPS)
    o_ref[:, 0:half] = h_lo * inv * g_ref[:, 0:half]
    o_ref[:, half:2 * half] = h_hi * inv * g_ref[:, half:2 * half]


def _combine_part_kernel(h_ref, y_ref, route_ref, g_ref, prev_ref, o_ref):
    del prev_ref
    _combine_kernel(h_ref, y_ref, route_ref, g_ref, o_ref)


def _combine(h1, yg, route, g_final, tm, start, out_prev=None):
    n, d = h1.shape
    b0 = start // tm
    row = lambda i: (b0 + i, 0)
    in_specs = [pl.BlockSpec((tm, d), row),
                pl.BlockSpec((TOP_K_EXPERTS, tm, d // 2), lambda i: (0, i, 0)),
                pl.BlockSpec((tm, LANES), row),
                pl.BlockSpec((1, d), lambda i: (0, 0))]
    args = [h1, yg, route, g_final]
    aliases = {}
    kernel = _combine_kernel
    if out_prev is not None:
        in_specs.append(pl.BlockSpec(memory_space=pl.ANY))
        args.append(out_prev)
        aliases = {len(args) - 1: 0}
        kernel = _combine_part_kernel
    return pl.pallas_call(
        kernel,
        out_shape=jax.ShapeDtypeStruct((n, d), F32),
        grid=(yg.shape[1] // tm,),
        in_specs=in_specs,
        out_specs=pl.BlockSpec((tm, d), row),
        input_output_aliases=aliases,
        compiler_params=_cparams(("parallel",)),
        name="combine_norm",
    )(*args)


def _t5_bucket(dist):
    n = jnp.maximum(dist, 0)
    max_exact = N_BUCKETS // 2
    nf = jnp.maximum(n, 1).astype(F32)
    large = max_exact + (jnp.log(nf / max_exact) / math.log(MAX_DISTANCE / max_exact)
                         * (N_BUCKETS - max_exact)).astype(I32)
    large = jnp.minimum(large, N_BUCKETS - 1)
    return jnp.where(n < max_exact, n, large)


def _bias_blocks(bias_tab):
    t = LANES
    assert MAX_DISTANCE <= LANES
    r = jnp.arange(t, dtype=I32)[:, None]
    c = jnp.arange(t, dtype=I32)[None, :]
    rel = (bias_tab - bias_tab[N_BUCKETS - 1][None, :]).astype(F32)
    tiles = []
    buckets = jnp.arange(N_BUCKETS, dtype=I32)[:, None, None]
    for delta in (0, t):
        dist = r - c + delta
        hit = _t5_bucket(dist)[None] == buckets
        b = jnp.sum(jnp.where(hit[:, None], rel[:, :, None, None], 0.0), axis=0)
        tiles.append(jnp.where((dist >= 0)[None], b * LOG2E, MASK_VALUE))
    return jnp.stack(tiles)


def _regroup_w_in(w_in):
    sizes = (1024, 1024, 1024, 1024, KV_LATENT, 1024, HEAD_DIM_IDX, N_HEADS_IDX, D_MODEL, D_MODEL)
    parts, off = [], 0
    for sz in sizes:
        parts.append(w_in[:, off:off + sz])
        off += sz
    dq, dk, dv, sq, ckv, iq, ik, iw, ga, gb = parts
    dq = dq * (HEAD_DIM_DIFF ** -0.5 * LOG2E)
    iw = iw * ((N_HEADS_IDX ** -0.5) * (HEAD_DIM_IDX ** -0.5))
    pad = jnp.zeros((w_in.shape[0], PROJ_WIDTH - COL_IW - N_HEADS_IDX), w_in.dtype)
    w = jnp.concatenate([dq, dk, dv, sq, iq, ga, gb, ckv, ik, ik, iw, pad], axis=1)
    return w.astype(BF16)


def _block_tables(counts, n_assign):
    e, bm = N_EXPERTS, MOE_ROWS
    padded = (counts + bm - 1) // bm * bm
    pends = jnp.cumsum(padded)
    nblk = -(-(n_assign + e * (bm - 1)) // bm)
    first_row = jnp.arange(nblk, dtype=I32) * bm
    blk_exp = jnp.minimum(jnp.sum((pends[None, :] <= first_row[:, None]).astype(I32), axis=1), e - 1)
    n_used = (pends[-1] // bm).astype(I32).reshape(1)
    return blk_exp, n_used, nblk


def kernel(x, norm_attn_g, w_in, rel_bias, lam_q1, lam_k1, lam_q2, lam_k2, diff_subln_g, kv_norm_g, w_uk, w_uv,
           w_branch_diff, w_branch_dsa, w_out, norm_ffn_g, w_router, b_router, w_gate_up, b_gate_up, w_down,
           b_down, norm_final_g):
    batch, seq, d = x.shape
    n = batch * seq
    assert norm_attn_g.shape[0] == 1, "single-layer kernel"
    assert seq % DIFF_BLOCK == 0 and seq % KEY_CHUNK == 0 and d == D_MODEL
    assert seq <= (2 ** 15 - 1) * PACKED_ROWS, "int16 per-element key counts"
    row_tile = math.gcd(n, ROW_TILE)
    token_tile = math.gcd(n, TOKEN_TILE)

    x2 = x.reshape(n, d)
    proj = _inproj(x2, norm_attn_g[0].reshape(1, d), _regroup_w_in(w_in[0]), row_tile, PROJ_COLS)

    lam_init = 0.8 - 0.6 * math.exp(-0.3 * 0)
    lam = (jnp.exp(jnp.sum(lam_q1[0].astype(F32) * lam_k1[0].astype(F32)))
           - jnp.exp(jnp.sum(lam_q2[0].astype(F32) * lam_k2[0].astype(F32))) + lam_init)
    y_diff = _diff_attention(proj, lam.reshape(1, 1).astype(F32), _bias_blocks(rel_bias[:, :N_HEADS_DIFF]),
                             diff_subln_g[0].reshape(1, -1).astype(F32), batch, seq, 1.0 - lam_init)

    y_dsa = _dsa_attention(proj, kv_norm_g[0].reshape(1, -1).astype(F32),
                           w_uk[0].transpose(0, 2, 1).astype(BF16), w_uv[0].astype(BF16),
                           _bias_blocks(rel_bias[:, N_HEADS_DIFF:]), batch, seq, min(TOPK_MAX, seq // 4))

    w_r = jnp.zeros((d, LANES), F32).at[:, :N_EXPERTS].set(w_router[0].astype(F32))
    b_r = jnp.full((1, LANES), MASK_VALUE, F32).at[0, :N_EXPERTS].set(b_router[0].astype(F32))
    h1, hn, route = _merge(x2, y_diff, y_dsa, proj, w_branch_diff[0].astype(BF16), w_branch_dsa[0].astype(BF16),
                           w_out[0].astype(BF16), norm_ffn_g[0].reshape(1, d).astype(F32),
                           w_r.astype(BF16), (w_r - w_r.astype(BF16).astype(F32)).astype(BF16), b_r,
                           row_tile)

    dest, counts = _route_rows(route, row_tile)
    dest = dest[:, :n]
    blk_exp, n_used, nblk = _block_tables(counts[:N_EXPERTS, 0].astype(I32), n * TOP_K_EXPERTS)
    xs = _sc_scatter_rows(hn, dest, nblk * MOE_ROWS)
    e, f = N_EXPERTS, D_EXPERT
    b_gu = b_gate_up[0].astype(F32).reshape(e, f // LANES, LANES, 2).transpose(0, 1, 3, 2).reshape(e, 1, 2 * f)
    ys = _expert_ffn(blk_exp, n_used, xs, _regroup_gate_up(w_gate_up[0], TOKEN_TILE), w_down[0],
                     b_gu, b_down[0][:, None, :].astype(F32))
    g_final = norm_final_g.reshape(1, d).astype(F32)
    part = n // COMBINE_PARTS
    out = None
    for p in range(COMBINE_PARTS):
        yg = _sc_gather_rows(ys, dest, p * part, part).reshape(TOP_K_EXPERTS, part, d // 2)
        out = _combine(h1, yg, route, g_final, math.gcd(part, token_tile), p * part, out)
    return out.reshape(batch, seq, d)
```
